```python
import math
import jax, jax.numpy as jnp
from jax import lax
import numpy as np

D_MODEL = 2048
BATCH = 2
SEQ = 4096
DEPTH = 1

CHUNK = 64
PLE_DIM = 256
D_MIX = D_MODEL
RW_WIDTH = D_MIX // 2
RW_HEAD = 64
RW_HEADS = RW_WIDTH // RW_HEAD
RW_DECAY_LORA = 64
RW_ICLR_LORA = 64
DS_WIDTH = D_MIX - RW_WIDTH
DS_HEAD = 64
DS_HEADS = DS_WIDTH // DS_HEAD
DS_Q_RANK = 384
DS_KV_RANK = 256
IDX_HEADS = 16
IDX_DIM = 64
TOPK_MAX = 256
Q_BLOCK = 128
NUM_BUCKETS = 32
MAX_DISTANCE = 128
NORM_EPS = 1e-6
GN_EPS = 64e-5
SHIFTED_COLS = 3 * RW_WIDTH + RW_DECAY_LORA + RW_ICLR_LORA
D_IN = SHIFTED_COLS + RW_WIDTH + DS_Q_RANK + DS_KV_RANK + IDX_DIM + IDX_HEADS + DS_WIDTH

kernel_name = "hybrid_rwkv7_dsa_parallel_heads"


def _split_points(sizes):
    pts, acc = [], 0
    for s in sizes[:-1]:
        acc += s
        pts.append(acc)
    return pts


def rms_norm(x, g, eps=NORM_EPS):
    xf = x.astype(jnp.float32)
    y = xf * lax.rsqrt(jnp.mean(xf * xf, axis=-1, keepdims=True) + eps)
    return (y * g.astype(jnp.float32)).astype(x.dtype)


def t5_bucket(rel):
    nb = NUM_BUCKETS // 2
    max_exact = nb // 2
    ret = jnp.where(rel > 0, nb, 0)
    n = jnp.abs(rel)
    nf = jnp.maximum(n, 1).astype(jnp.float32)
    large = max_exact + (jnp.log(nf / max_exact) / math.log(MAX_DISTANCE / max_exact)
                         * (nb - max_exact)).astype(jnp.int32)
    large = jnp.minimum(large, nb - 1)
    return ret + jnp.where(n < max_exact, n, large)


def rwkv7_mixer(z, mu, w0, w_up, a0, a_up, k_k, k_a, r_k, ln_g, ln_b):
    B, T, _ = z.shape
    H, N = RW_HEADS, RW_HEAD
    f32 = jnp.float32
    z_prev = jnp.pad(z, ((0, 0), (1, 0), (0, 0)))[:, :-1]
    z = z + mu * (z_prev - z)
    r, k, v, wd, ad = jnp.split(z, [RW_WIDTH, 2 * RW_WIDTH, 3 * RW_WIDTH,
                                    3 * RW_WIDTH + RW_DECAY_LORA], axis=-1)
    w_log = -jax.nn.softplus(-(w0 + jnp.tanh(wd) @ w_up)) - 0.5
    decay = jnp.exp(-jnp.exp(w_log.astype(f32)))
    a = jax.nn.sigmoid(a0 + ad @ a_up).astype(f32)
    heads = lambda t: t.astype(f32).reshape(B, T, H, N)
    kk = heads(k * k_k)
    kk = kk / jnp.maximum(jnp.sqrt(jnp.sum(kk * kk, axis=-1, keepdims=True)), 1e-12)
    k = k.astype(f32) * (1.0 + (a - 1.0) * k_a.astype(f32))
    rh, kh, vh, ah, wh = heads(r), heads(k), heads(v), heads(a), heads(decay)
    a_vec = -kk
    b_vec = kk * ah

    def step(S, inp):
        r_t, w_t, k_t, v_t, a_t, b_t = inp
        Sa = jnp.einsum('bhvk,bhk->bhv', S, a_t)
        S = (S * w_t[:, :, None, :] + Sa[..., None] * b_t[:, :, None, :]
             + v_t[..., None] * k_t[:, :, None, :])
        y = jnp.einsum('bhvk,bhk->bhv', S, r_t)
        return S, y

    tm = lambda t: jnp.moveaxis(t, 1, 0)
    S0 = jnp.zeros((B, H, N, N), f32)
    _, y = lax.scan(step, S0, (tm(rh), tm(wh), tm(kh), tm(vh), tm(a_vec), tm(b_vec)))
    y = jnp.moveaxis(y, 0, 1)
    mean = jnp.mean(y, axis=-1, keepdims=True)
    var = jnp.mean(jnp.square(y - mean), axis=-1, keepdims=True)
    y = (y - mean) * lax.rsqrt(var + GN_EPS)
    y = y * ln_g.astype(f32).reshape(H, N) + ln_b.astype(f32).reshape(H, N)
    bonus = jnp.sum(rh * kh * r_k.astype(f32), axis=-1, keepdims=True) * vh
    return (y + bonus).reshape(B, T, RW_WIDTH).astype(z.dtype)


def dsa_mixer(q_lat, kv_lat, k_idx, w_idx, q_norm_g, kv_norm_g, ik_norm_g,
              w_uq, w_uk, w_uv, iw_q, rel_bias):
    B, T, _ = q_lat.shape
    H, DH, R = DS_HEADS, DS_HEAD, DS_KV_RANK
    f32 = jnp.float32
    q_lat = rms_norm(q_lat, q_norm_g)
    c_kv = rms_norm(kv_lat, kv_norm_g)
    q = (q_lat @ w_uq).reshape(B, T, H, DH)
    q_abs = jnp.einsum('bthd,rhd->bthr', q, w_uk) * (DH ** -0.5)
    q_idx = (q_lat @ iw_q).reshape(B, T, IDX_HEADS, IDX_DIM)
    k_idx = rms_norm(k_idx, ik_norm_g)
    w_idx = w_idx * (IDX_HEADS ** -0.5 * IDX_DIM ** -0.5)
    top_k = min(TOPK_MAX, T // 4)
    nb = T // Q_BLOCK
    blocks = lambda t: jnp.moveaxis(t.reshape(B, nb, Q_BLOCK, *t.shape[2:]), 1, 0)
    key_pos = jnp.arange(T, dtype=jnp.int32)
    bias_tab = rel_bias.astype(f32)

    def attend_block(args):
        qa, qi, wi, q0 = args
        q_pos = q0 + jnp.arange(Q_BLOCK, dtype=jnp.int32)
        limit = (q_pos // CHUNK + 1) * CHUNK
        admissible = key_pos[None, :] < limit[:, None]
        logit_i = jnp.einsum('bqhd,bsd->bqhs', qi, k_idx)
        score = jnp.einsum('bqh,bqhs->bqs', wi, jax.nn.relu(logit_i)).astype(f32)
        score = jnp.where(admissible[None], score, -jnp.inf)
        _, sel = lax.top_k(score, top_k)
        valid = sel < limit[None, :, None]
        kv_sel = jax.vmap(lambda c, idx: c[idx])(c_kv, sel)
        bias = bias_tab[t5_bucket(sel - q_pos[None, :, None])]
        logits = (jnp.einsum('bqhr,bqkr->bqhk', qa, kv_sel).astype(f32)
                  + jnp.moveaxis(bias, -1, 2))
        logits = jnp.where(valid[:, :, None, :], logits, -jnp.inf)
        probs = jax.nn.softmax(logits, axis=-1).astype(c_kv.dtype)
        o_lat = jnp.einsum('bqhk,bqkr->bqhr', probs, kv_sel)
        return jnp.einsum('bqhr,rhd->bqhd', o_lat, w_uv)

    starts = jnp.arange(nb, dtype=jnp.int32) * Q_BLOCK
    out = lax.map(attend_block, (blocks(q_abs), blocks(q_idx), blocks(w_idx), starts))
    return jnp.moveaxis(out, 0, 1).reshape(B, T, DS_WIDTH)


def setup_inputs(seed: int = 0) -> dict:
    key = jax.random.key(seed)
    ks = jax.random.split(key, 26)
    f32 = jnp.float32
    nrm = lambda k, shape, scale: scale * jax.random.normal(k, shape, f32)
    L = DEPTH
    return {
        "x": nrm(ks[0], (BATCH, SEQ, D_MODEL), 1.0),
        "p": nrm(ks[1], (DEPTH, BATCH, SEQ, PLE_DIM), 1.0),
        "w_in": nrm(ks[2], (L, D_MODEL, D_IN), D_MODEL ** -0.5),
        "norm_g": 1.0 + nrm(ks[3], (L, D_MODEL), 0.02),
        "rw_mu": jax.random.uniform(ks[4], (L, SHIFTED_COLS), f32),
        "rw_w0": nrm(ks[5], (L, RW_WIDTH), 0.5),
        "rw_w_up": nrm(ks[6], (L, RW_DECAY_LORA, RW_WIDTH), 0.5 * RW_DECAY_LORA ** -0.5),
        "rw_a0": nrm(ks[7], (L, RW_WIDTH), 0.5),
        "rw_a_up": nrm(ks[8], (L, RW_ICLR_LORA, RW_WIDTH), 0.5 * RW_ICLR_LORA ** -0.5),
        "rw_k_k": 1.0 + nrm(ks[9], (L, RW_WIDTH), 0.1),
        "rw_k_a": 1.0 + nrm(ks[10], (L, RW_WIDTH), 0.1),
        "rw_r_k": nrm(ks[11], (L, RW_HEADS, RW_HEAD), 0.1),
        "rw_ln_g": 1.0 + nrm(ks[12], (L, RW_WIDTH), 0.02),
        "rw_ln_b": nrm(ks[13], (L, RW_WIDTH), 0.02),
        "ds_q_norm_g": 1.0 + nrm(ks[14], (L, DS_Q_RANK), 0.02),
        "ds_kv_norm_g": 1.0 + nrm(ks[15], (L, DS_KV_RANK), 0.02),
        "idx_k_norm_g": 1.0 + nrm(ks[16], (L, IDX_DIM), 0.02),
        "ds_w_uq": nrm(ks[17], (L, DS_Q_RANK, DS_HEADS * DS_HEAD), DS_Q_RANK ** -0.5),
        "ds_w_uk": nrm(ks[18], (L, DS_KV_RANK, DS_HEADS, DS_HEAD), DS_KV_RANK ** -0.5),
        "ds_w_uv": nrm(ks[19], (L, DS_KV_RANK, DS_HEADS, DS_HEAD), DS_KV_RANK ** -0.5),
        "idx_w_q": nrm(ks[20], (L, DS_Q_RANK, IDX_HEADS * IDX_DIM), DS_Q_RANK ** -0.5),
        "rel_bias": nrm(ks[21], (NUM_BUCKETS, DS_HEADS), 0.5),
        "w_out": nrm(ks[22], (L, D_MIX, D_MODEL), D_MIX ** -0.5),
        "ple_w": nrm(ks[23], (L, PLE_DIM, D_MODEL), PLE_DIM ** -0.5),
        "ple_gate_w": nrm(ks[24], (L, D_MODEL, D_MODEL), D_MODEL ** -0.5),
        "final_g": 1.0 + nrm(ks[25], (D_MODEL,), 0.02),
    }


def reference(x, p, w_in, norm_g, rw_mu, rw_w0, rw_w_up, rw_a0, rw_a_up, rw_k_k, rw_k_a,
              rw_r_k, rw_ln_g, rw_ln_b, ds_q_norm_g, ds_kv_norm_g, idx_k_norm_g, ds_w_uq,
              ds_w_uk, ds_w_uv, idx_w_q, rel_bias, w_out, ple_w, ple_gate_w, final_g):
    splits = _split_points([SHIFTED_COLS, RW_WIDTH, DS_Q_RANK, DS_KV_RANK,
                            IDX_DIM, IDX_HEADS, DS_WIDTH])
    h = x
    for i in range(DEPTH):
        xn = rms_norm(h, norm_g[i])
        z = xn @ w_in[i]
        z_rw, g_rw, q_lat, kv_lat, k_idx, w_idx, g_ds = jnp.split(z, splits, axis=-1)
        o_rw = rwkv7_mixer(z_rw, rw_mu[i], rw_w0[i], rw_w_up[i], rw_a0[i], rw_a_up[i],
                           rw_k_k[i], rw_k_a[i], rw_r_k[i], rw_ln_g[i], rw_ln_b[i])
        o_ds = dsa_mixer(q_lat, kv_lat, k_idx, w_idx, ds_q_norm_g[i], ds_kv_norm_g[i],
                         idx_k_norm_g[i], ds_w_uq[i], ds_w_uk[i], ds_w_uv[i], idx_w_q[i], rel_bias)
        mixed = jnp.concatenate([o_rw * jax.nn.silu(g_rw), o_ds * jax.nn.silu(g_ds)], axis=-1)
        h = h + mixed @ w_out[i]
        h = h + (p[i] @ ple_w[i]) * jax.nn.sigmoid(h @ ple_gate_w[i])
    return rms_norm(h, final_g)
```

```python
import functools
import math

import jax
import jax.numpy as jnp
from jax import lax
from jax.experimental import pallas as pl
from jax.experimental.pallas import tpu as pltpu

F32 = jnp.float32
BF16 = jnp.bfloat16
I32 = jnp.int32

RW_WIDTH = 1024
RW_HEAD = 64
RW_HEADS = 16
RW_LORA = 64
DS_WIDTH = 1024
DS_HEAD = 64
DS_HEADS = 16
DS_Q_RANK = 384
DS_KV_RANK = 256
IDX_HEADS = 16
IDX_DIM = 64
TOPK_MAX = 256
CHUNK = 64
NUM_BUCKETS = 32
MAX_DISTANCE = 128
NORM_EPS = 1e-6
GN_EPS = 64e-5

COL_R, COL_K, COL_V, COL_GRW, COL_GDS = 0, 1024, 2048, 3072, 4096
COL_KV = 5120
COL_Q = 5376
COL_WA = 5760
COL_KX = 5888
Z_WIDTH = 6144

LANES = 128
QB = 128
SK = 512
RW_CHUNK = 64
INT_MIN = -2 ** 31
MASK_NEG = -1e30
VMEM_LIMIT = 52 * 1024 * 1024


def _sigmoid(x):
    return 1.0 / (1.0 + jnp.exp(-x))


def _dot(a, b):
    return jnp.dot(a, b, preferred_element_type=F32)


def _dot_nt(a, b):
    return lax.dot_general(a, b, (((1,), (1,)), ((), ())), preferred_element_type=F32)


def _dot_tn(a, b):
    return lax.dot_general(a, b, (((0,), (0,)), ((), ())), preferred_element_type=F32)


def _inproj_kernel(x_ref, g_ref, w_ref, o_ref, xn_ref):
    @pl.when(pl.program_id(1) == 0)
    def _():
        x = x_ref[...]
        ms = jnp.mean(x * x, axis=-1, keepdims=True)
        xn_ref[...] = (x * lax.rsqrt(ms + NORM_EPS) * g_ref[...]).astype(BF16)

    o_ref[...] = _dot(xn_ref[...], w_ref[...])


def _inproj(x2, g, w, tm=512, tn=768):
    m, d = x2.shape
    n = w.shape[1]
    return pl.pallas_call(
        _inproj_kernel,
        grid=(m // tm, n // tn),
        in_specs=[pl.BlockSpec((tm, d), lambda i, j: (i, 0)),
                  pl.BlockSpec((1, d), lambda i, j: (0, 0)),
                  pl.BlockSpec((d, tn), lambda i, j: (0, j))],
        out_specs=pl.BlockSpec((tm, tn), lambda i, j: (i, j)),
        out_shape=jax.ShapeDtypeStruct((m, n), F32),
        scratch_shapes=[pltpu.VMEM((tm, d), BF16)],
        compiler_params=pltpu.CompilerParams(dimension_semantics=("parallel", "arbitrary"),
                                             vmem_limit_bytes=VMEM_LIMIT),
        name="inproj",
    )(x2, g, w)


def _kvprep_kernel(kv_ref, kx_ref, gkv_ref, gik_ref, ckv_ref, kid_ref):
    kv = kv_ref[...]
    ms = jnp.mean(kv * kv, axis=-1, keepdims=True)
    ckv_ref[...] = (kv * lax.rsqrt(ms + NORM_EPS) * gkv_ref[...]).astype(BF16)
    ki = kx_ref[:, 0:IDX_DIM]
    ms2 = jnp.mean(ki * ki, axis=-1, keepdims=True)
    kid_ref[...] = (ki * lax.rsqrt(ms2 + NORM_EPS) * gik_ref[...]).astype(BF16)


def _kvprep(z, gkv, gik, tm=512):
    m = z.shape[0]
    return pl.pallas_call(
        _kvprep_kernel,
        grid=(m // tm,),
        in_specs=[pl.BlockSpec((tm, DS_KV_RANK), lambda i: (i, COL_KV // DS_KV_RANK)),
                  pl.BlockSpec((tm, LANES), lambda i: (i, COL_KX // LANES)),
                  pl.BlockSpec((1, DS_KV_RANK), lambda i: (0, 0)),
                  pl.BlockSpec((1, IDX_DIM), lambda i: (0, 0))],
        out_specs=[pl.BlockSpec((tm, DS_KV_RANK), lambda i: (i, 0)),
                   pl.BlockSpec((tm, IDX_DIM), lambda i: (i, 0))],
        out_shape=[jax.ShapeDtypeStruct((m, DS_KV_RANK), BF16),
                   jax.ShapeDtypeStruct((m, IDX_DIM), BF16)],
        compiler_params=pltpu.CompilerParams(dimension_semantics=("parallel",)),
        name="kvprep",
    )(z, z, gkv, gik)


def _rwkv_kernel(r_ref, k_ref, v_ref, g_ref, wa_ref,
                 mur_ref, muk_ref, muv_ref, muwa_ref,
                 w0_ref, a0_ref, kk_ref, ka_ref, rk_ref, lng_ref, lnb_ref,
                 wup_ref, aup_ref,
                 o_ref,
                 pr_ref, pk_ref, pv_ref, pwa_ref, st_ref):
    C = RW_CHUNK
    N = RW_HEAD

    @pl.when(pl.program_id(1) == 0)
    def _():
        pr_ref[...] = jnp.zeros_like(pr_ref)
        pk_ref[...] = jnp.zeros_like(pk_ref)
        pv_ref[...] = jnp.zeros_like(pv_ref)
        pwa_ref[...] = jnp.zeros_like(pwa_ref)
        st_ref[...] = jnp.zeros_like(st_ref)

    row = lax.broadcasted_iota(I32, (C, 1), 0)

    def shift(ref, prev_ref, mu_ref):
        z = ref[...]
        zp = pltpu.roll(z, 1, 0)
        zp = jnp.where(row == 0, prev_ref[...], zp)
        prev_ref[...] = z[C - 1:C, :]
        return z + mu_ref[...] * (zp - z)

    r = shift(r_ref, pr_ref, mur_ref)
    k = shift(k_ref, pk_ref, muk_ref)
    v = shift(v_ref, pv_ref, muv_ref)
    wa = shift(wa_ref, pwa_ref, muwa_ref)
    wd = wa[:, 0:RW_LORA]
    ad = wa[:, RW_LORA:2 * RW_LORA]

    wl = w0_ref[...] + _dot(jnp.tanh(wd).astype(BF16), wup_ref[...])
    nwl = -wl
    softplus = jnp.maximum(nwl, 0.0) + jnp.log1p(jnp.exp(-jnp.abs(nwl)))
    w_log = -softplus - 0.5
    lw = -jnp.exp(w_log)
    a = _sigmoid(a0_ref[...] + _dot(ad.astype(BF16), aup_ref[...]))
    kk = k * kk_ref[...]
    k2 = k * (1.0 + (a - 1.0) * ka_ref[...])

    ti = lax.broadcasted_iota(I32, (C, C), 0)
    tj = lax.broadcasted_iota(I32, (C, C), 1)
    incl = ti >= tj
    strict = ti > tj
    tri = jnp.where(incl, 1.0, 0.0).astype(F32)
    eye = jnp.where(ti == tj, 1.0, 0.0).astype(F32)
    cum = jnp.dot(tri, lw, preferred_element_type=F32, precision=lax.Precision.HIGHEST)
    p = jnp.exp(cum)
    pinv = jnp.exp(-cum)
    pprev = jnp.exp(cum - lw)

    g = g_ref[...]
    gate = g * _sigmoid(g)
    bonus_rk = r * k2 * rk_ref[...]
    lng = lng_ref[...]
    lnb = lnb_ref[...]

    for h in range(RW_HEADS):
        cs = slice(h * N, (h + 1) * N)
        kkh = kk[:, cs]
        nrm = jnp.sqrt(jnp.sum(kkh * kkh, axis=-1, keepdims=True))
        kkn = kkh / jnp.maximum(nrm, 1e-12)
        vh = v[:, cs]
        at = (-kkn) * pprev[:, cs]
        bt = (kkn * a[:, cs]) * pinv[:, cs]
        kt = k2[:, cs] * pinv[:, cs]
        rt = r[:, cs] * p[:, cs]
        ar = jnp.concatenate([at, rt], axis=0).astype(BF16)
        bk = jnp.concatenate([bt, kt], axis=0).astype(BF16)
        gram = _dot_nt(ar, bk)
        a_ab = jnp.where(strict, gram[0:C, 0:C], 0.0)
        a_ak = jnp.where(strict, gram[0:C, C:2 * C], 0.0)
        a_rb = jnp.where(incl, gram[C:2 * C, 0:C], 0.0)
        a_rk = jnp.where(incl, gram[C:2 * C, C:2 * C], 0.0)
        tm = eye + a_ab
        pw = a_ab
        n = 1
        while 2 * n < C:
            pwb = pw.astype(BF16)
            pw = _dot(pwb, pwb)
            tm = tm + _dot(pw.astype(BF16), tm.astype(BF16))
            n *= 2
        g0 = st_ref[h]
        g0b = g0.astype(BF16)
        vb = vh.astype(BF16)
        arg = _dot_nt(ar, g0b)
        x = arg[0:C] + _dot(a_ak.astype(BF16), vb)
        u = _dot(tm.astype(BF16), x.astype(BF16))
        ub = u.astype(BF16)
        y = arg[C:2 * C] + _dot(a_rb.astype(BF16), ub) + _dot(a_rk.astype(BF16), vb)
        plast = p[C - 1:C, cs]
        uv = jnp.concatenate([ub, vb], axis=0)
        st_ref[h] = (g0 + _dot_tn(uv, bk)) * plast
        mean = jnp.mean(y, axis=-1, keepdims=True)
        yc = y - mean
        var = jnp.mean(yc * yc, axis=-1, keepdims=True)
        yn = yc * lax.rsqrt(var + GN_EPS) * lng[:, cs] + lnb[:, cs]
        bonus = jnp.sum(bonus_rk[:, cs], axis=-1, keepdims=True) * vh
        o_ref[:, cs] = ((yn + bonus) * gate[:, cs]).astype(BF16)


def _rwkv(z, B, T, mu, w0, a0, k_k, k_a, r_k, ln_g, ln_b, w_up, a_up):
    C = RW_CHUNK
    nc = T // C
    W = RW_WIDTH
    row = lambda a: a.reshape(1, -1).astype(F32)
    mu_r, mu_k, mu_v = mu[0:W], mu[W:2 * W], mu[2 * W:3 * W]
    mu_wa = mu[3 * W:3 * W + 2 * RW_LORA]
    zspec = lambda col: pl.BlockSpec((C, W), lambda b, c: (b * nc + c, col // W))
    pspec = lambda width: pl.BlockSpec((1, width), lambda b, c: (0, 0))
    wspec = pl.BlockSpec((RW_LORA, W), lambda b, c: (0, 0))
    return pl.pallas_call(
        _rwkv_kernel,
        grid=(B, nc),
        in_specs=[zspec(COL_R), zspec(COL_K), zspec(COL_V), zspec(COL_GRW),
                  pl.BlockSpec((C, LANES), lambda b, c: (b * nc + c, COL_WA // LANES)),
                  pspec(W), pspec(W), pspec(W), pspec(LANES),
                  pspec(W), pspec(W), pspec(W), pspec(W), pspec(W), pspec(W), pspec(W),
                  wspec, wspec],
        out_specs=pl.BlockSpec((C, W), lambda b, c: (b * nc + c, 0)),
        out_shape=jax.ShapeDtypeStruct((B * T, W), BF16),
        scratch_shapes=[pltpu.VMEM((1, W), F32), pltpu.VMEM((1, W), F32), pltpu.VMEM((1, W), F32),
                        pltpu.VMEM((1, LANES), F32),
                        pltpu.VMEM((RW_HEADS, RW_HEAD, RW_HEAD), F32)],
        compiler_params=pltpu.CompilerParams(dimension_semantics=("parallel", "arbitrary"),
                                             vmem_limit_bytes=VMEM_LIMIT),
        name="rwkv",
    )(z, z, z, z, z,
      row(mu_r), row(mu_k), row(mu_v), row(mu_wa),
      row(w0), row(a0), row(k_k), row(k_a), row(r_k), row(ln_g), row(ln_b),
      w_up.astype(BF16), a_up.astype(BF16))


def _biastab_kernel(rb_ref, o_ref):
    nb = NUM_BUCKETS // 2
    max_exact = nb // 2
    r = lax.broadcasted_iota(I32, (QB, 2 * QB), 0)
    c = lax.broadcasted_iota(I32, (QB, 2 * QB), 1)
    rel = c - QB - r
    ret = jnp.where(rel > 0, nb, 0)
    n = jnp.abs(rel)
    nf = jnp.maximum(n, 1).astype(F32)
    large = max_exact + (jnp.log(nf / max_exact) / math.log(MAX_DISTANCE / max_exact)
                         * (nb - max_exact)).astype(I32)
    large = jnp.minimum(large, nb - 1)
    bucket = ret + jnp.where(n < max_exact, n, large)
    for h in range(DS_HEADS):
        far = rb_ref[nb - 1, h]
        acc = jnp.zeros((QB, 2 * QB), F32)
        for b in range(NUM_BUCKETS):
            acc = jnp.where(bucket == b, rb_ref[b, h] - far, acc)
        o_ref[h] = acc


def _biastab(rel_bias):
    return pl.pallas_call(
        _biastab_kernel,
        in_specs=[pl.BlockSpec(memory_space=pltpu.SMEM)],
        out_specs=pl.BlockSpec(memory_space=pltpu.VMEM),
        out_shape=jax.ShapeDtypeStruct((DS_HEADS, QB, 2 * QB), F32),
        name="biastab",
    )(rel_bias.astype(F32))


def _fold_lanes(x):
    acc = x[:, 0:LANES]
    for j in range(1, x.shape[1] // LANES):
        acc = acc + x[:, j * LANES:(j + 1) * LANES]
    return acc


def _dsa_kernel(ql_ref, kx_ref, gds_ref, kid_ref, ckv_ref, qg_ref, wq_ref, wuk_ref, wuv_ref, tab_ref,
                o_ref,
                key_ref, msk_ref, lg_ref, qa_ref, qi_ref, oh_ref, *, topk):
    i = pl.program_id(1)
    q0 = i * QB
    ntile = jnp.right_shift(q0 + (QB + SK - 1), SK.bit_length() - 1)

    ql = ql_ref[...]
    ms = jnp.mean(ql * ql, axis=-1, keepdims=True)
    qn = (ql * lax.rsqrt(ms + NORM_EPS) * qg_ref[...]).astype(BF16)
    q = _dot(qn, wq_ref[...])
    for h in range(DS_HEADS):
        qh = q[:, h * DS_HEAD:(h + 1) * DS_HEAD].astype(BF16)
        qa_ref[h] = (_dot(qh, wuk_ref[h]) * (DS_HEAD ** -0.5)).astype(BF16)
        qi_ref[h] = q[:, DS_WIDTH + h * IDX_DIM:DS_WIDTH + (h + 1) * IDX_DIM].astype(BF16)
    wi = kx_ref[:, IDX_DIM:IDX_DIM + IDX_HEADS] * (IDX_HEADS ** -0.5 * IDX_DIM ** -0.5)

    rowi = lax.broadcasted_iota(I32, (QB, 1), 0)
    csh = CHUNK.bit_length() - 1
    limit = jnp.left_shift(jnp.right_shift(q0 + rowi, csh) + 1, csh)
    coli = lax.broadcasted_iota(I32, (1, SK), 1)

    def score_tile(kt, carry):
        off = pl.multiple_of(kt * SK, SK)
        kid = kid_ref[pl.ds(off, SK), :]
        s = jnp.zeros((QB, SK), F32)
        for h in range(IDX_HEADS):
            lg = _dot_nt(qi_ref[h], kid)
            s = s + wi[:, h:h + 1] * jnp.maximum(lg, 0.0)
        bits = pltpu.bitcast(s, I32)
        key = jnp.where(bits < 0, bits ^ 0x7FFFFFFF, bits)
        adm = (off + coli) < limit
        key_ref[:, pl.ds(off, SK)] = jnp.where(adm, key, INT_MIN)
        return carry

    lax.fori_loop(0, ntile, score_tile, 0)

    def count(pred):
        def body(kt, acc):
            off = pl.multiple_of(kt * SK, SK)
            keys = key_ref[:, pl.ds(off, SK)]
            return acc + _fold_lanes(jnp.where(pred(keys, off), 1, 0).astype(I32))
        acc = lax.fori_loop(0, ntile, body, jnp.zeros((QB, LANES), I32))
        return jnp.sum(acc, axis=1, keepdims=True)

    def bit_step(it, lo):
        inc = jnp.left_shift(jnp.int32(1), 31 - it)
        cand = lo + inc
        cnt = count(lambda keys, off: keys >= cand)
        return jnp.where(cnt >= topk, cand, lo)

    thr = lax.fori_loop(0, 32, bit_step, jnp.full((QB, 1), INT_MIN, I32))
    cnt_gt = count(lambda keys, off: keys > thr)
    nbits = max(1, (key_ref.shape[1] - 1).bit_length())

    def idx_step(it, m):
        cand = m + jnp.left_shift(jnp.int32(1), nbits - 1 - it)
        cnt = cnt_gt + count(lambda keys, off: (keys == thr) & ((off + coli) < cand))
        return jnp.where(cnt < topk, cand, m)

    cut = lax.fori_loop(0, nbits, idx_step, jnp.zeros((QB, 1), I32))

    def mask_tile(kt, carry):
        off = pl.multiple_of(kt * SK, SK)
        keys = key_ref[:, pl.ds(off, SK)]
        sel = (keys > thr) | ((keys == thr) & ((off + coli) <= cut))
        sel = sel & (keys != INT_MIN)
        msk_ref[:, pl.ds(off, SK)] = jnp.where(sel, 0.0, MASK_NEG).astype(F32)
        return carry

    lax.fori_loop(0, ntile, mask_tile, 0)

    def head(h, carry):
        qa = qa_ref[h]

        def logits_tile(kt, c):
            off = pl.multiple_of(kt * SK, SK)
            ckv = ckv_ref[pl.ds(off, SK), :]
            lg_ref[:, pl.ds(off, SK)] = _dot_nt(qa, ckv) + msk_ref[:, pl.ds(off, SK)]
            return c

        lax.fori_loop(0, ntile, logits_tile, 0)
        d0 = pl.multiple_of(q0, QB)
        lg_ref[:, pl.ds(d0, QB)] = (_dot_nt(qa, ckv_ref[pl.ds(d0, QB), :])
                                    + tab_ref[h, :, pl.ds(QB, QB)] + msk_ref[:, pl.ds(d0, QB)])

        @pl.when(i > 0)
        def _():
            d1 = pl.multiple_of(q0 - QB, QB)
            lg_ref[:, pl.ds(d1, QB)] = (_dot_nt(qa, ckv_ref[pl.ds(d1, QB), :])
                                        + tab_ref[h, :, pl.ds(0, QB)] + msk_ref[:, pl.ds(d1, QB)])

        def max_tile(kt, m):
            off = pl.multiple_of(kt * SK, SK)
            t = lg_ref[:, pl.ds(off, SK)]
            acc = t[:, 0:LANES]
            for j in range(1, SK // LANES):
                acc = jnp.maximum(acc, t[:, j * LANES:(j + 1) * LANES])
            return jnp.maximum(m, acc)

        mx = lax.fori_loop(0, ntile, max_tile, jnp.full((QB, LANES), MASK_NEG, F32))
        mx = jnp.max(mx, axis=1, keepdims=True)

        def pv_tile(kt, c):
            acc, ls = c
            off = pl.multiple_of(kt * SK, SK)
            pr = jnp.exp(lg_ref[:, pl.ds(off, SK)] - mx)
            ls = ls + _fold_lanes(pr)
            acc = acc + _dot(pr.astype(BF16), ckv_ref[pl.ds(off, SK), :])
            return acc, ls

        acc, ls = lax.fori_loop(0, ntile, pv_tile,
                                (jnp.zeros((QB, DS_KV_RANK), F32), jnp.zeros((QB, LANES), F32)))
        o_lat = acc / jnp.sum(ls, axis=1, keepdims=True)
        oh_ref[h] = _dot(o_lat.astype(BF16), wuv_ref[h])
        return carry

    lax.fori_loop(0, DS_HEADS, head, 0)

    g = gds_ref[...]
    gate = g * _sigmoid(g)
    for h in range(DS_HEADS):
        cs = slice(h * DS_HEAD, (h + 1) * DS_HEAD)
        o_ref[:, cs] = (oh_ref[h] * gate[:, cs]).astype(BF16)


def _dsa(z, ckv, kid, B, T, q_norm_g, w_uq, w_uk, w_uv, iw_q, tab):
    nq = T // QB
    topk = min(TOPK_MAX, T // 4)
    wq = jnp.concatenate([w_uq, iw_q], axis=1).astype(BF16)
    wuk_t = jnp.transpose(w_uk, (1, 2, 0)).astype(BF16)
    wuv_h = jnp.transpose(w_uv, (1, 0, 2)).astype(BF16)
    const2 = lambda b, i: (0, 0)
    const3 = lambda b, i: (0, 0, 0)
    return pl.pallas_call(
        functools.partial(_dsa_kernel, topk=topk),
        grid=(B, nq),
        in_specs=[pl.BlockSpec((QB, DS_Q_RANK), lambda b, i: (b * nq + i, COL_Q // DS_Q_RANK)),
                  pl.BlockSpec((QB, LANES), lambda b, i: (b * nq + i, COL_KX // LANES)),
                  pl.BlockSpec((QB, DS_WIDTH), lambda b, i: (b * nq + i, COL_GDS // DS_WIDTH)),
                  pl.BlockSpec((T, IDX_DIM), lambda b, i: (b, 0)),
                  pl.BlockSpec((T, DS_KV_RANK), lambda b, i: (b, 0)),
                  pl.BlockSpec((1, DS_Q_RANK), const2),
                  pl.BlockSpec((DS_Q_RANK, 2 * DS_WIDTH), const2),
                  pl.BlockSpec((DS_HEADS, DS_HEAD, DS_KV_RANK), const3),
                  pl.BlockSpec((DS_HEADS, DS_KV_RANK, DS_HEAD), const3),
                  pl.BlockSpec((DS_HEADS, QB, 2 * QB), const3)],
        out_specs=pl.BlockSpec((QB, DS_WIDTH), lambda b, i: (b * nq + i, 0)),
        out_shape=jax.ShapeDtypeStruct((B * T, DS_WIDTH), BF16),
        scratch_shapes=[pltpu.VMEM((QB, T), I32),
                        pltpu.VMEM((QB, T), F32),
                        pltpu.VMEM((QB, T), F32),
                        pltpu.VMEM((DS_HEADS, QB, DS_KV_RANK), BF16),
                        pltpu.VMEM((IDX_HEADS, QB, IDX_DIM), BF16),
                        pltpu.VMEM((DS_HEADS, QB, DS_HEAD), F32)],
        compiler_params=pltpu.CompilerParams(dimension_semantics=("parallel", "arbitrary"),
                                             vmem_limit_bytes=VMEM_LIMIT),
        name="dsa",
    )(z, z, z, kid, ckv, q_norm_g.reshape(1, -1).astype(F32), wq, wuk_t, wuv_h, tab)


def _outproj_kernel(x_ref, a1_ref, a2_ref, w1_ref, w2_ref, o_ref):
    o_ref[...] = x_ref[...] + _dot(a1_ref[...], w1_ref[...]) + _dot(a2_ref[...], w2_ref[...])


def _outproj(x2, o_rw, o_ds, w_out, tm=512, tn=1024):
    m, d = x2.shape
    kh = o_rw.shape[1]
    return pl.pallas_call(
        _outproj_kernel,
        grid=(m // tm, d // tn),
        in_specs=[pl.BlockSpec((tm, tn), lambda i, j: (i, j)),
                  pl.BlockSpec((tm, kh), lambda i, j: (i, 0)),
                  pl.BlockSpec((tm, kh), lambda i, j: (i, 0)),
                  pl.BlockSpec((kh, tn), lambda i, j: (0, j)),
                  pl.BlockSpec((kh, tn), lambda i, j: (1, j))],
        out_specs=pl.BlockSpec((tm, tn), lambda i, j: (i, j)),
        out_shape=jax.ShapeDtypeStruct((m, d), F32),
        compiler_params=pltpu.CompilerParams(dimension_semantics=("parallel", "arbitrary"),
                                             vmem_limit_bytes=VMEM_LIMIT),
        name="outproj",
    )(x2, o_rw, o_ds, w_out, w_out)


def _tail_kernel(h_ref, p_ref, pw_ref, gw_ref, fg_ref, o_ref):
    h = h_ref[...]
    e = _dot(p_ref[...].astype(BF16), pw_ref[...])
    gate = _sigmoid(_dot(h.astype(BF16), gw_ref[...]))
    h2 = h + e * gate
    ms = jnp.mean(h2 * h2, axis=-1, keepdims=True)
    o_ref[...] = h2 * lax.rsqrt(ms + NORM_EPS) * fg_ref[...]


def _tail(h, p2, ple_w, gate_w, final_g, tm=256):
    m, d = h.shape
    pd = p2.shape[1]
    return pl.pallas_call(
        _tail_kernel,
        grid=(m // tm,),
        in_specs=[pl.BlockSpec((tm, d), lambda i: (i, 0)),
                  pl.BlockSpec((tm, pd), lambda i: (i, 0)),
                  pl.BlockSpec((pd, d), lambda i: (0, 0)),
                  pl.BlockSpec((d, d), lambda i: (0, 0)),
                  pl.BlockSpec((1, d), lambda i: (0, 0))],
        out_specs=pl.BlockSpec((tm, d), lambda i: (i, 0)),
        out_shape=jax.ShapeDtypeStruct((m, d), F32),
        compiler_params=pltpu.CompilerParams(dimension_semantics=("parallel",),
                                             vmem_limit_bytes=VMEM_LIMIT),
        name="tail",
    )(h, p2, ple_w, gate_w, final_g)


def _regroup_w_in(w):
    s0 = 3 * RW_WIDTH
    s1 = s0 + 2 * RW_LORA
    s2 = s1 + RW_WIDTH
    s3 = s2 + DS_Q_RANK
    s4 = s3 + DS_KV_RANK
    s5 = s4 + IDX_DIM
    s6 = s5 + IDX_HEADS
    pad = jnp.zeros((w.shape[0], Z_WIDTH - (COL_KX + IDX_DIM + IDX_HEADS)), w.dtype)
    return jnp.concatenate([w[:, 0:s0], w[:, s1:s2], w[:, s6:], w[:, s3:s4], w[:, s2:s3],
                            w[:, s0:s1], w[:, s4:s6], pad], axis=1)


def kernel(x, p, w_in, norm_g, rw_mu, rw_w0, rw_w_up, rw_a0, rw_a_up, rw_k_k, rw_k_a, rw_r_k, rw_ln_g, rw_ln_b, ds_q_norm_g, ds_kv_norm_g, idx_k_norm_g, ds_w_uq, ds_w_uk, ds_w_uv, idx_w_q, rel_bias, w_out, ple_w, ple_gate_w, final_g):
    B, T, D = x.shape
    depth = w_in.shape[0]
    assert depth == 1 and T % SK == 0 and T % RW_CHUNK == 0 and (B * T) % 512 == 0
    h = x.reshape(B * T, D)
    tab = _biastab(rel_bias)
    for i in range(depth):
        w = _regroup_w_in(w_in[i]).astype(BF16)
        z = _inproj(h, norm_g[i].reshape(1, D), w)
        ckv, kid = _kvprep(z, ds_kv_norm_g[i].reshape(1, -1), idx_k_norm_g[i].reshape(1, -1))
        o_rw = _rwkv(z, B, T, rw_mu[i], rw_w0[i], rw_a0[i], rw_k_k[i], rw_k_a[i],
                     rw_r_k[i].reshape(-1), rw_ln_g[i], rw_ln_b[i], rw_w_up[i], rw_a_up[i])
        o_ds = _dsa(z, ckv, kid, B, T, ds_q_norm_g[i], ds_w_uq[i], ds_w_uk[i], ds_w_uv[i],
                    idx_w_q[i], tab)
        h = _outproj(h, o_rw, o_ds, w_out[i].astype(BF16))
        h = _tail(h, p[i].reshape(B * T, -1), ple_w[i].astype(BF16), ple_gate_w[i].astype(BF16),
                  final_g.reshape(1, D))
    return h.reshape(B, T, D)
```

```python
import functools
import math

import jax
import jax.numpy as jnp
from jax import lax
from jax.experimental import pallas as pl
from jax.experimental.pallas import tpu as pltpu

F32 = jnp.float32
BF16 = jnp.bfloat16
I32 = jnp.int32

RW_WIDTH = 1024
RW_HEAD = 64
RW_HEADS = 16
RW_LORA = 64
DS_WIDTH = 1024
DS_HEAD = 64
DS_HEADS = 16
DS_Q_RANK = 384
DS_KV_RANK = 256
IDX_HEADS = 16
IDX_DIM = 64
TOPK_MAX = 256
CHUNK = 64
NUM_BUCKETS = 32
MAX_DISTANCE = 128
NORM_EPS = 1e-6
GN_EPS = 64e-5

COL_R, COL_K, COL_V, COL_GRW, COL_GDS = 0, 1024, 2048, 3072, 4096
COL_KV = 5120
COL_Q = 5376
COL_WA = 5760
COL_KX = 5888
Z_WIDTH = 6144

LANES = 128
QB = 128
SK = 512
HG = 8
RW_CHUNK = 64
INT_MIN = -2 ** 31
MASK_NEG = -1e30
VMEM_LIMIT = 52 * 1024 * 1024


def _sigmoid(x):
    return 1.0 / (1.0 + jnp.exp(-x))


def _dot(a, b):
    return jnp.dot(a, b, preferred_element_type=F32)


def _dot_nt(a, b):
    return lax.dot_general(a, b, (((1,), (1,)), ((), ())), preferred_element_type=F32)


def _dot_tn(a, b):
    return lax.dot_general(a, b, (((0,), (0,)), ((), ())), preferred_element_type=F32)


def _inproj_kernel(x_ref, g_ref, w_ref, o_ref, xn_ref):
    @pl.when(pl.program_id(1) == 0)
    def _():
        x = x_ref[...]
        ms = jnp.mean(x * x, axis=-1, keepdims=True)
        xn_ref[...] = (x * lax.rsqrt(ms + NORM_EPS) * g_ref[...]).astype(BF16)

    o_ref[...] = _dot(xn_ref[...], w_ref[...])


def _inproj(x2, g, w, tm=512, tn=768):
    m, d = x2.shape
    n = w.shape[1]
    return pl.pallas_call(
        _inproj_kernel,
        grid=(m // tm, n // tn),
        in_specs=[pl.BlockSpec((tm, d), lambda i, j: (i, 0)),
                  pl.BlockSpec((1, d), lambda i, j: (0, 0)),
                  pl.BlockSpec((d, tn), lambda i, j: (0, j))],
        out_specs=pl.BlockSpec((tm, tn), lambda i, j: (i, j)),
        out_shape=jax.ShapeDtypeStruct((m, n), F32),
        scratch_shapes=[pltpu.VMEM((tm, d), BF16)],
        compiler_params=pltpu.CompilerParams(dimension_semantics=("parallel", "arbitrary"),
                                             vmem_limit_bytes=VMEM_LIMIT),
        name="inproj",
    )(x2, g, w)


def _kvprep_kernel(kv_ref, kx_ref, gkv_ref, gik_ref, ckv_ref, kid_ref):
    kv = kv_ref[...]
    ms = jnp.mean(kv * kv, axis=-1, keepdims=True)
    ckv_ref[...] = (kv * lax.rsqrt(ms + NORM_EPS) * gkv_ref[...]).astype(BF16)
    ki = kx_ref[:, 0:IDX_DIM]
    ms2 = jnp.mean(ki * ki, axis=-1, keepdims=True)
    kid_ref[...] = (ki * lax.rsqrt(ms2 + NORM_EPS) * gik_ref[...]).astype(BF16)


def _kvprep(z, gkv, gik, tm=512):
    m = z.shape[0]
    return pl.pallas_call(
        _kvprep_kernel,
        grid=(m // tm,),
        in_specs=[pl.BlockSpec((tm, DS_KV_RANK), lambda i: (i, COL_KV // DS_KV_RANK)),
                  pl.BlockSpec((tm, LANES), lambda i: (i, COL_KX // LANES)),
                  pl.BlockSpec((1, DS_KV_RANK), lambda i: (0, 0)),
                  pl.BlockSpec((1, IDX_DIM), lambda i: (0, 0))],
        out_specs=[pl.BlockSpec((tm, DS_KV_RANK), lambda i: (i, 0)),
                   pl.BlockSpec((tm, IDX_DIM), lambda i: (i, 0))],
        out_shape=[jax.ShapeDtypeStruct((m, DS_KV_RANK), BF16),
                   jax.ShapeDtypeStruct((m, IDX_DIM), BF16)],
        compiler_params=pltpu.CompilerParams(dimension_semantics=("parallel",)),
        name="kvprep",
    )(z, z, gkv, gik)


def _rwkv_kernel(r_ref, k_ref, v_ref, g_ref, wa_ref,
                 mur_ref, muk_ref, muv_ref, muwa_ref,
                 w0_ref, a0_ref, kk_ref, ka_ref, rk_ref, lng_ref, lnb_ref,
                 wup_ref, aup_ref,
                 o_ref,
                 pr_ref, pk_ref, pv_ref, pwa_ref, st_ref):
    C = RW_CHUNK
    N = RW_HEAD

    @pl.when(pl.program_id(1) == 0)
    def _():
        pr_ref[...] = jnp.zeros_like(pr_ref)
        pk_ref[...] = jnp.zeros_like(pk_ref)
        pv_ref[...] = jnp.zeros_like(pv_ref)
        pwa_ref[...] = jnp.zeros_like(pwa_ref)
        st_ref[...] = jnp.zeros_like(st_ref)

    row = lax.broadcasted_iota(I32, (C, 1), 0)

    def shift(ref, prev_ref, mu_ref):
        z = ref[...]
        zp = pltpu.roll(z, 1, 0)
        zp = jnp.where(row == 0, prev_ref[...], zp)
        prev_ref[...] = z[C - 1:C, :]
        return z + mu_ref[...] * (zp - z)

    r = shift(r_ref, pr_ref, mur_ref)
    k = shift(k_ref, pk_ref, muk_ref)
    v = shift(v_ref, pv_ref, muv_ref)
    wa = shift(wa_ref, pwa_ref, muwa_ref)
    wd = wa[:, 0:RW_LORA]
    ad = wa[:, RW_LORA:2 * RW_LORA]

    wl = w0_ref[...] + _dot(jnp.tanh(wd).astype(BF16), wup_ref[...])
    nwl = -wl
    softplus = jnp.maximum(nwl, 0.0) + jnp.log1p(jnp.exp(-jnp.abs(nwl)))
    w_log = -softplus - 0.5
    lw = -jnp.exp(w_log)
    a = _sigmoid(a0_ref[...] + _dot(ad.astype(BF16), aup_ref[...]))
    kk = k * kk_ref[...]
    k2 = k * (1.0 + (a - 1.0) * ka_ref[...])

    ti = lax.broadcasted_iota(I32, (C, C), 0)
    tj = lax.broadcasted_iota(I32, (C, C), 1)
    incl = ti >= tj
    strict = ti > tj
    tri = jnp.where(incl, 1.0, 0.0).astype(F32)
    eye = jnp.where(ti == tj, 1.0, 0.0).astype(F32)
    cum = jnp.dot(tri, lw, preferred_element_type=F32, precision=lax.Precision.HIGHEST)
    p = jnp.exp(cum)
    pinv = jnp.exp(-cum)
    pprev = jnp.exp(cum - lw)

    g = g_ref[...]
    gate = g * _sigmoid(g)
    bonus_rk = r * k2 * rk_ref[...]
    lng = lng_ref[...]
    lnb = lnb_ref[...]

    for h in range(RW_HEADS):
        cs = slice(h * N, (h + 1) * N)
        kkh = kk[:, cs]
        nrm = jnp.sqrt(jnp.sum(kkh * kkh, axis=-1, keepdims=True))
        kkn = kkh / jnp.maximum(nrm, 1e-12)
        vh = v[:, cs]
        at = (-kkn) * pprev[:, cs]
        bt = (kkn * a[:, cs]) * pinv[:, cs]
        kt = k2[:, cs] * pinv[:, cs]
        rt = r[:, cs] * p[:, cs]
        ar = jnp.concatenate([at, rt], axis=0).astype(BF16)
        bk = jnp.concatenate([bt, kt], axis=0).astype(BF16)
        gram = _dot_nt(ar, bk)
        a_ab = jnp.where(strict, gram[0:C, 0:C], 0.0)
        a_ak = jnp.where(strict, gram[0:C, C:2 * C], 0.0)
        a_rb = jnp.where(incl, gram[C:2 * C, 0:C], 0.0)
        a_rk = jnp.where(incl, gram[C:2 * C, C:2 * C], 0.0)
        tm = eye + a_ab
        pw = a_ab
        n = 1
        while 2 * n < C:
            pwb = pw.astype(BF16)
            pw = _dot(pwb, pwb)
            tm = tm + _dot(pw.astype(BF16), tm.astype(BF16))
            n *= 2
        g0 = st_ref[h]
        g0b = g0.astype(BF16)
        vb = vh.astype(BF16)
        arg = _dot_nt(ar, g0b)
        x = arg[0:C] + _dot(a_ak.astype(BF16), vb)
        u = _dot(tm.astype(BF16), x.astype(BF16))
        ub = u.astype(BF16)
        y = arg[C:2 * C] + _dot(a_rb.astype(BF16), ub) + _dot(a_rk.astype(BF16), vb)
        plast = p[C - 1:C, cs]
        uv = jnp.concatenate([ub, vb], axis=0)
        st_ref[h] = (g0 + _dot_tn(uv, bk)) * plast
        mean = jnp.mean(y, axis=-1, keepdims=True)
        yc = y - mean
        var = jnp.mean(yc * yc, axis=-1, keepdims=True)
        yn = yc * lax.rsqrt(var + GN_EPS) * lng[:, cs] + lnb[:, cs]
        bonus = jnp.sum(bonus_rk[:, cs], axis=-1, keepdims=True) * vh
        o_ref[:, cs] = ((yn + bonus) * gate[:, cs]).astype(BF16)


def _rwkv(z, B, T, mu, w0, a0, k_k, k_a, r_k, ln_g, ln_b, w_up, a_up):
    C = RW_CHUNK
    nc = T // C
    W = RW_WIDTH
    row = lambda a: a.reshape(1, -1).astype(F32)
    mu_r, mu_k, mu_v = mu[0:W], mu[W:2 * W], mu[2 * W:3 * W]
    mu_wa = mu[3 * W:3 * W + 2 * RW_LORA]
    zspec = lambda col: pl.BlockSpec((C, W), lambda b, c: (b * nc + c, col // W))
    pspec = lambda width: pl.BlockSpec((1, width), lambda b, c: (0, 0))
    wspec = pl.BlockSpec((RW_LORA, W), lambda b, c: (0, 0))
    return pl.pallas_call(
        _rwkv_kernel,
        grid=(B, nc),
        in_specs=[zspec(COL_R), zspec(COL_K), zspec(COL_V), zspec(COL_GRW),
                  pl.BlockSpec((C, LANES), lambda b, c: (b * nc + c, COL_WA // LANES)),
                  pspec(W), pspec(W), pspec(W), pspec(LANES),
                  pspec(W), pspec(W), pspec(W), pspec(W), pspec(W), pspec(W), pspec(W),
                  wspec, wspec],
        out_specs=pl.BlockSpec((C, W), lambda b, c: (b * nc + c, 0)),
        out_shape=jax.ShapeDtypeStruct((B * T, W), BF16),
        scratch_shapes=[pltpu.VMEM((1, W), F32), pltpu.VMEM((1, W), F32), pltpu.VMEM((1, W), F32),
                        pltpu.VMEM((1, LANES), F32),
                        pltpu.VMEM((RW_HEADS, RW_HEAD, RW_HEAD), F32)],
        compiler_params=pltpu.CompilerParams(dimension_semantics=("parallel", "arbitrary"),
                                             vmem_limit_bytes=VMEM_LIMIT),
        name="rwkv",
    )(z, z, z, z, z,
      row(mu_r), row(mu_k), row(mu_v), row(mu_wa),
      row(w0), row(a0), row(k_k), row(k_a), row(r_k), row(ln_g), row(ln_b),
      w_up.astype(BF16), a_up.astype(BF16))


def _biastab_kernel(rb_ref, o_ref):
    nb = NUM_BUCKETS // 2
    max_exact = nb // 2
    r = lax.broadcasted_iota(I32, (QB, 2 * QB), 0)
    c = lax.broadcasted_iota(I32, (QB, 2 * QB), 1)
    rel = c - QB - r
    ret = jnp.where(rel > 0, nb, 0)
    n = jnp.abs(rel)
    nf = jnp.maximum(n, 1).astype(F32)
    large = max_exact + (jnp.log(nf / max_exact) / math.log(MAX_DISTANCE / max_exact)
                         * (nb - max_exact)).astype(I32)
    large = jnp.minimum(large, nb - 1)
    bucket = ret + jnp.where(n < max_exact, n, large)
    for h in range(DS_HEADS):
        far = rb_ref[nb - 1, h]
        acc = jnp.zeros((QB, 2 * QB), F32)
        for b in range(NUM_BUCKETS):
            acc = jnp.where(bucket == b, rb_ref[b, h] - far, acc)
        o_ref[h] = acc


def _biastab(rel_bias):
    return pl.pallas_call(
        _biastab_kernel,
        in_specs=[pl.BlockSpec(memory_space=pltpu.SMEM)],
        out_specs=pl.BlockSpec(memory_space=pltpu.VMEM),
        out_shape=jax.ShapeDtypeStruct((DS_HEADS, QB, 2 * QB), F32),
        name="biastab",
    )(rel_bias.astype(F32))


def _fold_lanes(x):
    acc = x[:, 0:LANES]
    for j in range(1, x.shape[1] // LANES):
        acc = acc + x[:, j * LANES:(j + 1) * LANES]
    return acc


def _dsa_kernel(ql_ref, kx_ref, gds_ref, kid_ref, ckv_ref, qg_ref, wq_ref, wuk_ref, wuv_ref, tab_ref,
                o_ref,
                key_ref, msk_ref, lg_ref, acc_ref, ls_ref, qa_ref, qi_ref, oh_ref, *, topk):
    i = pl.program_id(1)
    q0 = i * QB
    ntile = jnp.right_shift(q0 + (QB + SK - 1), SK.bit_length() - 1)

    ql = ql_ref[...]
    ms = jnp.mean(ql * ql, axis=-1, keepdims=True)
    qn = (ql * lax.rsqrt(ms + NORM_EPS) * qg_ref[...]).astype(BF16)
    q = _dot(qn, wq_ref[...])
    for h in range(DS_HEADS):
        qh = q[:, h * DS_HEAD:(h + 1) * DS_HEAD].astype(BF16)
        qa_ref[h] = (_dot(qh, wuk_ref[h]) * (DS_HEAD ** -0.5)).astype(BF16)
        qi_ref[h] = q[:, DS_WIDTH + h * IDX_DIM:DS_WIDTH + (h + 1) * IDX_DIM].astype(BF16)
    wi = kx_ref[:, IDX_DIM:IDX_DIM + IDX_HEADS] * (IDX_HEADS ** -0.5 * IDX_DIM ** -0.5)

    rowi = lax.broadcasted_iota(I32, (QB, 1), 0)
    csh = CHUNK.bit_length() - 1
    limit = jnp.left_shift(jnp.right_shift(q0 + rowi, csh) + 1, csh)
    coli = lax.broadcasted_iota(I32, (1, SK), 1)

    def score_tile(kt, carry):
        off = pl.multiple_of(kt * SK, SK)
        kid = kid_ref[pl.ds(off, SK), :]
        s = jnp.zeros((QB, SK), F32)
        for h in range(IDX_HEADS):
            lg = _dot_nt(qi_ref[h], kid)
            s = s + wi[:, h:h + 1] * jnp.maximum(lg, 0.0)
        bits = pltpu.bitcast(s, I32)
        key = jnp.where(bits < 0, bits ^ 0x7FFFFFFF, bits)
        adm = (off + coli) < limit
        key_ref[:, pl.ds(off, SK)] = jnp.where(adm, key, INT_MIN)
        return carry

    lax.fori_loop(0, ntile, score_tile, 0)

    def count(pred):
        def body(kt, acc):
            off = pl.multiple_of(kt * SK, SK)
            keys = key_ref[:, pl.ds(off, SK)]
            return acc + _fold_lanes(jnp.where(pred(keys, off), 1, 0).astype(I32))
        acc = lax.fori_loop(0, ntile, body, jnp.zeros((QB, LANES), I32))
        return jnp.sum(acc, axis=1, keepdims=True)

    def bit_step(it, lo):
        inc = jnp.left_shift(jnp.int32(1), 31 - it)
        cand = lo + inc
        cnt = count(lambda keys, off: keys >= cand)
        return jnp.where(cnt >= topk, cand, lo)

    thr = lax.fori_loop(0, 32, bit_step, jnp.full((QB, 1), INT_MIN, I32))
    cnt_gt = count(lambda keys, off: keys > thr)
    nbits = max(1, (key_ref.shape[1] - 1).bit_length())

    def idx_step(it, m):
        cand = m + jnp.left_shift(jnp.int32(1), nbits - 1 - it)
        cnt = cnt_gt + count(lambda keys, off: (keys == thr) & ((off + coli) < cand))
        return jnp.where(cnt < topk, cand, m)

    cut = lax.fori_loop(0, nbits, idx_step, jnp.zeros((QB, 1), I32))

    def mask_tile(kt, carry):
        off = pl.multiple_of(kt * SK, SK)
        keys = key_ref[:, pl.ds(off, SK)]
        sel = (keys > thr) | ((keys == thr) & ((off + coli) <= cut))
        sel = sel & (keys != INT_MIN)
        msk_ref[:, pl.ds(off, SK)] = jnp.where(sel, 0.0, MASK_NEG).astype(F32)
        return carry

    lax.fori_loop(0, ntile, mask_tile, 0)

    R = DS_KV_RANK
    for grp in range(DS_HEADS // HG):
        hs = grp * HG
        qa2 = qa_ref[hs:hs + HG].reshape(HG * QB, R)

        def logits_tile(kt, c):
            off = pl.multiple_of(kt * SK, SK)
            s = _dot_nt(qa2, ckv_ref[pl.ds(off, SK), :]).reshape(HG, QB, SK)
            lg_ref[:, :, pl.ds(off, SK)] = s + msk_ref[:, pl.ds(off, SK)][None]
            return c

        lax.fori_loop(0, ntile, logits_tile, 0)

        @pl.when(i > 0)
        def _():
            d1 = pl.multiple_of(q0 - QB, QB)
            s = _dot_nt(qa2, ckv_ref[pl.ds(d1, 2 * QB), :]).reshape(HG, QB, 2 * QB)
            lg_ref[:, :, pl.ds(d1, 2 * QB)] = (s + tab_ref[hs:hs + HG]
                                               + msk_ref[:, pl.ds(d1, 2 * QB)][None])

        @pl.when(i == 0)
        def _():
            s = _dot_nt(qa2, ckv_ref[0:QB, :]).reshape(HG, QB, QB)
            lg_ref[:, :, 0:QB] = s + tab_ref[hs:hs + HG, :, QB:2 * QB] + msk_ref[:, 0:QB][None]

        def max_tile(kt, m):
            off = pl.multiple_of(kt * SK, SK)
            t = lg_ref[:, :, pl.ds(off, SK)]
            acc = t[:, :, 0:LANES]
            for j in range(1, SK // LANES):
                acc = jnp.maximum(acc, t[:, :, j * LANES:(j + 1) * LANES])
            return jnp.maximum(m, acc)

        mx = lax.fori_loop(0, ntile, max_tile, jnp.full((HG, QB, LANES), MASK_NEG, F32))
        mx = jnp.max(mx, axis=2, keepdims=True)

        acc_ref[...] = jnp.zeros_like(acc_ref)
        ls_ref[...] = jnp.zeros_like(ls_ref)

        def pv_tile(kt, c):
            off = pl.multiple_of(kt * SK, SK)
            pr = jnp.exp(lg_ref[:, :, pl.ds(off, SK)] - mx)
            lsum = pr[:, :, 0:LANES]
            for j in range(1, SK // LANES):
                lsum = lsum + pr[:, :, j * LANES:(j + 1) * LANES]
            ls_ref[...] += lsum
            acc_ref[...] += _dot(pr.astype(BF16).reshape(HG * QB, SK), ckv_ref[pl.ds(off, SK), :])
            return c

        lax.fori_loop(0, ntile, pv_tile, 0)
        o_lat = acc_ref[...].reshape(HG, QB, R) / jnp.sum(ls_ref[...], axis=2, keepdims=True)
        oh_ref[hs:hs + HG] = lax.dot_general(o_lat.astype(BF16), wuv_ref[hs:hs + HG],
                                             (((2,), (1,)), ((0,), (0,))),
                                             preferred_element_type=F32)

    g = gds_ref[...]
    gate = g * _sigmoid(g)
    for h in range(DS_HEADS):
        cs = slice(h * DS_HEAD, (h + 1) * DS_HEAD)
        o_ref[:, cs] = (oh_ref[h] * gate[:, cs]).astype(BF16)


def _dsa(z, ckv, kid, B, T, q_norm_g, w_uq, w_uk, w_uv, iw_q, tab):
    nq = T // QB
    topk = min(TOPK_MAX, T // 4)
    wq = jnp.concatenate([w_uq, iw_q], axis=1).astype(BF16)
    wuk_t = jnp.transpose(w_uk, (1, 2, 0)).astype(BF16)
    wuv_h = jnp.transpose(w_uv, (1, 0, 2)).astype(BF16)
    const2 = lambda b, i: (0, 0)
    const3 = lambda b, i: (0, 0, 0)
    return pl.pallas_call(
        functools.partial(_dsa_kernel, topk=topk),
        grid=(B, nq),
        in_specs=[pl.BlockSpec((QB, DS_Q_RANK), lambda b, i: (b * nq + i, COL_Q // DS_Q_RANK)),
                  pl.BlockSpec((QB, LANES), lambda b, i: (b * nq + i, COL_KX // LANES)),
                  pl.BlockSpec((QB, DS_WIDTH), lambda b, i: (b * nq + i, COL_GDS // DS_WIDTH)),
                  pl.BlockSpec((T, IDX_DIM), lambda b, i: (b, 0)),
                  pl.BlockSpec((T, DS_KV_RANK), lambda b, i: (b, 0)),
                  pl.BlockSpec((1, DS_Q_RANK), const2),
                  pl.BlockSpec((DS_Q_RANK, 2 * DS_WIDTH), const2),
                  pl.BlockSpec((DS_HEADS, DS_HEAD, DS_KV_RANK), const3),
                  pl.BlockSpec((DS_HEADS, DS_KV_RANK, DS_HEAD), const3),
                  pl.BlockSpec((DS_HEADS, QB, 2 * QB), const3)],
        out_specs=pl.BlockSpec((QB, DS_WIDTH), lambda b, i: (b * nq + i, 0)),
        out_shape=jax.ShapeDtypeStruct((B * T, DS_WIDTH), BF16),
        scratch_shapes=[pltpu.VMEM((QB, T), I32),
                        pltpu.VMEM((QB, T), F32),
                        pltpu.VMEM((HG, QB, T), F32),
                        pltpu.VMEM((HG * QB, DS_KV_RANK), F32),
                        pltpu.VMEM((HG, QB, LANES), F32),
                        pltpu.VMEM((DS_HEADS, QB, DS_KV_RANK), BF16),
                        pltpu.VMEM((IDX_HEADS, QB, IDX_DIM), BF16),
                        pltpu.VMEM((DS_HEADS, QB, DS_HEAD), F32)],
        compiler_params=pltpu.CompilerParams(dimension_semantics=("parallel", "arbitrary"),
                                             vmem_limit_bytes=VMEM_LIMIT),
        name="dsa",
    )(z, z, z, kid, ckv, q_norm_g.reshape(1, -1).astype(F32), wq, wuk_t, wuv_h, tab)


def _outproj_kernel(x_ref, a1_ref, a2_ref, w1_ref, w2_ref, o_ref):
    o_ref[...] = x_ref[...] + _dot(a1_ref[...], w1_ref[...]) + _dot(a2_ref[...], w2_ref[...])


def _outproj(x2, o_rw, o_ds, w_out, tm=512, tn=1024):
    m, d = x2.shape
    kh = o_rw.shape[1]
    return pl.pallas_call(
        _outproj_kernel,
        grid=(m // tm, d // tn),
        in_specs=[pl.BlockSpec((tm, tn), lambda i, j: (i, j)),
                  pl.BlockSpec((tm, kh), lambda i, j: (i, 0)),
                  pl.BlockSpec((tm, kh), lambda i, j: (i, 0)),
                  pl.BlockSpec((kh, tn), lambda i, j: (0, j)),
                  pl.BlockSpec((kh, tn), lambda i, j: (1, j))],
        out_specs=pl.BlockSpec((tm, tn), lambda i, j: (i, j)),
        out_shape=jax.ShapeDtypeStruct((m, d), F32),
        compiler_params=pltpu.CompilerParams(dimension_semantics=("parallel", "arbitrary"),
                                             vmem_limit_bytes=VMEM_LIMIT),
        name="outproj",
    )(x2, o_rw, o_ds, w_out, w_out)


def _tail_kernel(h_ref, p_ref, pw_ref, gw_ref, fg_ref, o_ref):
    h = h_ref[...]
    e = _dot(p_ref[...].astype(BF16), pw_ref[...])
    gate = _sigmoid(_dot(h.astype(BF16), gw_ref[...]))
    h2 = h + e * gate
    ms = jnp.mean(h2 * h2, axis=-1, keepdims=True)
    o_ref[...] = h2 * lax.rsqrt(ms + NORM_EPS) * fg_ref[...]


def _tail(h, p2, ple_w, gate_w, final_g, tm=256):
    m, d = h.shape
    pd = p2.shape[1]
    return pl.pallas_call(
        _tail_kernel,
        grid=(m // tm,),
        in_specs=[pl.BlockSpec((tm, d), lambda i: (i, 0)),
                  pl.BlockSpec((tm, pd), lambda i: (i, 0)),
                  pl.BlockSpec((pd, d), lambda i: (0, 0)),
                  pl.BlockSpec((d, d), lambda i: (0, 0)),
                  pl.BlockSpec((1, d), lambda i: (0, 0))],
        out_specs=pl.BlockSpec((tm, d), lambda i: (i, 0)),
        out_shape=jax.ShapeDtypeStruct((m, d), F32),
        compiler_params=pltpu.CompilerParams(dimension_semantics=("parallel",),
                                             vmem_limit_bytes=VMEM_LIMIT),
        name="tail",
    )(h, p2, ple_w, gate_w, final_g)


def _regroup_w_in(w):
    s0 = 3 * RW_WIDTH
    s1 = s0 + 2 * RW_LORA
    s2 = s1 + RW_WIDTH
    s3 = s2 + DS_Q_RANK
    s4 = s3 + DS_KV_RANK
    s5 = s4 + IDX_DIM
    s6 = s5 + IDX_HEADS
    pad = jnp.zeros((w.shape[0], Z_WIDTH - (COL_KX + IDX_DIM + IDX_HEADS)), w.dtype)
    return jnp.concatenate([w[:, 0:s0], w[:, s1:s2], w[:, s6:], w[:, s3:s4], w[:, s2:s3],
                            w[:, s0:s1], w[:, s4:s6], pad], axis=1)


def kernel(x, p, w_in, norm_g, rw_mu, rw_w0, rw_w_up, rw_a0, rw_a_up, rw_k_k, rw_k_a, rw_r_k, rw_ln_g, rw_ln_b, ds_q_norm_g, ds_kv_norm_g, idx_k_norm_g, ds_w_uq, ds_w_uk, ds_w_uv, idx_w_q, rel_bias, w_out, ple_w, ple_gate_w, final_g):
    B, T, D = x.shape
    depth = w_in.shape[0]
    assert depth == 1 and T % SK == 0 and T % RW_CHUNK == 0 and (B * T) % 512 == 0
    h = x.reshape(B * T, D)
    tab = _biastab(rel_bias)
    for i in range(depth):
        w = _regroup_w_in(w_in[i]).astype(BF16)
        z = _inproj(h, norm_g[i].reshape(1, D), w)
        ckv, kid = _kvprep(z, ds_kv_norm_g[i].reshape(1, -1), idx_k_norm_g[i].reshape(1, -1))
        o_rw = _rwkv(z, B, T, rw_mu[i], rw_w0[i], rw_a0[i], rw_k_k[i], rw_k_a[i],
                     rw_r_k[i].reshape(-1), rw_ln_g[i], rw_ln_b[i], rw_w_up[i], rw_a_up[i])
        o_ds = _dsa(z, ckv, kid, B, T, ds_q_norm_g[i], ds_w_uq[i], ds_w_uk[i], ds_w_uv[i],
                    idx_w_q[i], tab)
        h = _outproj(h, o_rw, o_ds, w_out[i].astype(BF16))
        h = _tail(h, p[i].reshape(B * T, -1), ple_w[i].astype(BF16), ple_gate_w[i].astype(BF16),
                  final_g.reshape(1, D))
    return h.reshape(B, T, D)
```

```python
import functools
import math

import jax
import jax.numpy as jnp
from jax import lax
from jax.experimental import pallas as pl
from jax.experimental.pallas import tpu as pltpu

F32 = jnp.float32
BF16 = jnp.bfloat16
I32 = jnp.int32

RW_WIDTH = 1024
RW_HEAD = 64
RW_HEADS = 16
RW_LORA = 64
DS_WIDTH = 1024
DS_HEAD = 64
DS_HEADS = 16
DS_Q_RANK = 384
DS_KV_RANK = 256
IDX_HEADS = 16
IDX_DIM = 64
TOPK_MAX = 256
CHUNK = 64
NUM_BUCKETS = 32
MAX_DISTANCE = 128
NORM_EPS = 1e-6
GN_EPS = 64e-5

COL_R, COL_K, COL_V, COL_GRW, COL_GDS = 0, 1024, 2048, 3072, 4096
COL_KV = 5120
COL_Q = 5376
COL_WA = 5760
COL_KX = 5888
Z_WIDTH = 6144

LANES = 128
QB = 128
SK = 512
HG = 8
RW_CHUNK = 64
INT_MIN = -2 ** 31
MASK_NEG = -1e30
VMEM_LIMIT = 52 * 1024 * 1024


def _sigmoid(x):
    return 1.0 / (1.0 + jnp.exp(-x))


def _dot(a, b):
    return jnp.dot(a, b, preferred_element_type=F32)


def _dot_nt(a, b):
    return lax.dot_general(a, b, (((1,), (1,)), ((), ())), preferred_element_type=F32)


def _dot_tn(a, b):
    return lax.dot_general(a, b, (((0,), (0,)), ((), ())), preferred_element_type=F32)


def _inproj_kernel(x_ref, g_ref, w_ref, o_ref, xn_ref):
    @pl.when(pl.program_id(1) == 0)
    def _():
        x = x_ref[...]
        ms = jnp.mean(x * x, axis=-1, keepdims=True)
        xn_ref[...] = (x * lax.rsqrt(ms + NORM_EPS) * g_ref[...]).astype(BF16)

    o_ref[...] = _dot(xn_ref[...], w_ref[...])


def _inproj(x2, g, w, tm=512, tn=768):
    m, d = x2.shape
    n = w.shape[1]
    return pl.pallas_call(
        _inproj_kernel,
        grid=(m // tm, n // tn),
        in_specs=[pl.BlockSpec((tm, d), lambda i, j: (i, 0)),
                  pl.BlockSpec((1, d), lambda i, j: (0, 0)),
                  pl.BlockSpec((d, tn), lambda i, j: (0, j))],
        out_specs=pl.BlockSpec((tm, tn), lambda i, j: (i, j)),
        out_shape=jax.ShapeDtypeStruct((m, n), F32),
        scratch_shapes=[pltpu.VMEM((tm, d), BF16)],
        compiler_params=pltpu.CompilerParams(dimension_semantics=("parallel", "arbitrary"),
                                             vmem_limit_bytes=VMEM_LIMIT),
        name="inproj",
    )(x2, g, w)


def _kvprep_kernel(kv_ref, kx_ref, gkv_ref, gik_ref, ckv_ref, kid_ref):
    kv = kv_ref[...]
    ms = jnp.mean(kv * kv, axis=-1, keepdims=True)
    ckv_ref[...] = (kv * lax.rsqrt(ms + NORM_EPS) * gkv_ref[...]).astype(BF16)
    ki = kx_ref[:, 0:IDX_DIM]
    ms2 = jnp.mean(ki * ki, axis=-1, keepdims=True)
    kid_ref[...] = (ki * lax.rsqrt(ms2 + NORM_EPS) * gik_ref[...]).astype(BF16)


def _kvprep(z, gkv, gik, tm=512):
    m = z.shape[0]
    return pl.pallas_call(
        _kvprep_kernel,
        grid=(m // tm,),
        in_specs=[pl.BlockSpec((tm, DS_KV_RANK), lambda i: (i, COL_KV // DS_KV_RANK)),
                  pl.BlockSpec((tm, LANES), lambda i: (i, COL_KX // LANES)),
                  pl.BlockSpec((1, DS_KV_RANK), lambda i: (0, 0)),
                  pl.BlockSpec((1, IDX_DIM), lambda i: (0, 0))],
        out_specs=[pl.BlockSpec((tm, DS_KV_RANK), lambda i: (i, 0)),
                   pl.BlockSpec((tm, IDX_DIM), lambda i: (i, 0))],
        out_shape=[jax.ShapeDtypeStruct((m, DS_KV_RANK), BF16),
                   jax.ShapeDtypeStruct((m, IDX_DIM), BF16)],
        compiler_params=pltpu.CompilerParams(dimension_semantics=("parallel",)),
        name="kvprep",
    )(z, z, gkv, gik)


def _rwkv_kernel(r_ref, k_ref, v_ref, g_ref, wa_ref,
                 mur_ref, muk_ref, muv_ref, muwa_ref,
                 w0_ref, a0_ref, kk_ref, ka_ref, rk_ref, lng_ref, lnb_ref,
                 wup_ref, aup_ref,
                 o_ref,
                 pr_ref, pk_ref, pv_ref, pwa_ref, st_ref):
    C = RW_CHUNK
    N = RW_HEAD

    @pl.when(pl.program_id(1) == 0)
    def _():
        pr_ref[...] = jnp.zeros_like(pr_ref)
        pk_ref[...] = jnp.zeros_like(pk_ref)
        pv_ref[...] = jnp.zeros_like(pv_ref)
        pwa_ref[...] = jnp.zeros_like(pwa_ref)
        st_ref[...] = jnp.zeros_like(st_ref)

    row = lax.broadcasted_iota(I32, (C, 1), 0)

    def shift(ref, prev_ref, mu_ref):
        z = ref[...]
        zp = pltpu.roll(z, 1, 0)
        zp = jnp.where(row == 0, prev_ref[...], zp)
        prev_ref[...] = z[C - 1:C, :]
        return z + mu_ref[...] * (zp - z)

    r = shift(r_ref, pr_ref, mur_ref)
    k = shift(k_ref, pk_ref, muk_ref)
    v = shift(v_ref, pv_ref, muv_ref)
    wa = shift(wa_ref, pwa_ref, muwa_ref)
    wd = wa[:, 0:RW_LORA]
    ad = wa[:, RW_LORA:2 * RW_LORA]

    wl = w0_ref[...] + _dot(jnp.tanh(wd).astype(BF16), wup_ref[...])
    nwl = -wl
    softplus = jnp.maximum(nwl, 0.0) + jnp.log1p(jnp.exp(-jnp.abs(nwl)))
    w_log = -softplus - 0.5
    lw = -jnp.exp(w_log)
    a = _sigmoid(a0_ref[...] + _dot(ad.astype(BF16), aup_ref[...]))
    kk = k * kk_ref[...]
    k2 = k * (1.0 + (a - 1.0) * ka_ref[...])

    ti = lax.broadcasted_iota(I32, (C, C), 0)
    tj = lax.broadcasted_iota(I32, (C, C), 1)
    incl = ti >= tj
    strict = ti > tj
    tri = jnp.where(incl, 1.0, 0.0).astype(F32)
    cum = jnp.dot(tri, lw, preferred_element_type=F32, precision=lax.Precision.HIGHEST)
    p = jnp.exp(cum)
    pinv = jnp.exp(-cum)
    pprev = jnp.exp(cum - lw)
    tot = jnp.dot(jnp.ones((C, C), F32), lw, preferred_element_type=F32,
                  precision=lax.Precision.HIGHEST)
    pend = jnp.exp(tot)

    g = g_ref[...]
    gate = g * _sigmoid(g)

    NP = RW_HEADS // 2

    def pairs(x):
        return jnp.stack([x[:, j * LANES:(j + 1) * LANES] for j in range(NP)], axis=0)

    lane = lax.broadcasted_iota(I32, (1, 1, LANES), 2)
    m_lo = jnp.where(lane < N, 1.0, 0.0).astype(F32)
    m_hi = 1.0 - m_lo
    bi = lax.broadcasted_iota(I32, (LANES, LANES), 0)
    bj = lax.broadcasted_iota(I32, (LANES, LANES), 1)
    same_head = (bi < N) == (bj < N)
    ones_bd = jnp.where(same_head, 1.0, 0.0).astype(BF16)

    def head_sum(x):
        return _dot(x.reshape(NP * C, LANES).astype(BF16), ones_bd).reshape(NP, C, LANES)

    def halves(x):
        return jnp.concatenate([x * m_lo, x * m_hi], axis=1)

    def bmm(x, y):
        return lax.dot_general(x, y, (((2,), (1,)), ((0,), (0,))), preferred_element_type=F32)

    def bmm_nt(x, y):
        return lax.dot_general(x, y, (((2,), (2,)), ((0,), (0,))), preferred_element_type=F32)

    r_p, k2_p, v_p, a_p = pairs(r), pairs(k2), pairs(v), pairs(a)
    p_p, pinv_p, pprev_p = pairs(p), pairs(pinv), pairs(pprev)
    kk_p = pairs(kk)
    kkn = kk_p / jnp.maximum(jnp.sqrt(head_sum(kk_p * kk_p)), 1e-12)
    at = (-kkn) * pprev_p
    bt = (kkn * a_p) * pinv_p
    kt = k2_p * pinv_p
    rt = r_p * p_p
    pend_p = pairs(pend)
    pend2 = jnp.concatenate([pend_p, pend_p], axis=1)

    nmat = bmm_nt(halves(at).astype(BF16), bt.astype(BF16)).reshape(2 * NP, C, C)
    nmat = jnp.where(strict, nmat, 0.0)
    lhs2 = jnp.concatenate([at, rt], axis=1).astype(BF16)
    rhs4 = jnp.concatenate([halves(kt), halves(bt)], axis=1).astype(BF16)
    gc = bmm_nt(lhs2, rhs4)
    def block_mask(nblk, cmp):
        wi = lax.broadcasted_iota(I32, (C, nblk * C), 0)
        wj = lax.broadcasted_iota(I32, (C, nblk * C), 1) & (C - 1)
        return cmp(wi, wj)

    strict2 = block_mask(2, lambda i_, j_: i_ > j_)
    incl4 = block_mask(4, lambda i_, j_: i_ >= j_)
    a_ak = jnp.where(strict2, gc[:, 0:C, 0:2 * C], 0.0)
    a_rkb = jnp.where(incl4, gc[:, C:2 * C, :], 0.0)

    g0 = st_ref[...]
    sg = bmm_nt(lhs2, g0.astype(BF16))
    vm2 = halves(v_p).astype(BF16)
    x0 = sg[:, 0:C] + bmm(a_ak.astype(BF16), vm2)
    xm = halves(x0).reshape(2 * NP, C, LANES)
    pw = nmat.astype(BF16)
    xm = xm + bmm(pw, xm.astype(BF16))
    n = 1
    while 2 * n < C:
        pw = bmm(pw, pw).astype(BF16)
        xm = xm + bmm(pw, xm.astype(BF16))
        n *= 2
    um2 = xm.reshape(NP, 2 * C, LANES)
    um2b = um2.astype(BF16)
    y = sg[:, C:2 * C] + bmm(a_rkb.astype(BF16), jnp.concatenate([vm2, um2b], axis=1))
    u = um2[:, 0:C] + um2[:, C:2 * C]
    uv = jnp.concatenate([u, v_p], axis=1).astype(BF16)
    bkh = (jnp.concatenate([bt, kt], axis=1) * pend2).astype(BF16)
    upd = lax.dot_general(uv, bkh, (((1,), (1,)), ((0,), (0,))), preferred_element_type=F32)
    st_ref[...] = g0 * pend2 + jnp.where(same_head, upd, 0.0)

    inv_n = 1.0 / N
    yc = y - head_sum(y) * inv_n
    var = head_sum(yc * yc) * inv_n
    yn = yc * lax.rsqrt(var + GN_EPS) * lng_ref[...] + lnb_ref[...]
    bonus = head_sum(r_p * k2_p * rk_ref[...]) * v_p
    out = (yn + bonus) * pairs(gate)
    for j in range(NP):
        o_ref[:, j * LANES:(j + 1) * LANES] = out[j].astype(BF16)


def _rwkv(z, B, T, mu, w0, a0, k_k, k_a, r_k, ln_g, ln_b, w_up, a_up):
    C = RW_CHUNK
    nc = T // C
    W = RW_WIDTH
    row = lambda a: a.reshape(1, -1).astype(F32)
    mu_r, mu_k, mu_v = mu[0:W], mu[W:2 * W], mu[2 * W:3 * W]
    mu_wa = mu[3 * W:3 * W + 2 * RW_LORA]
    zspec = lambda col: pl.BlockSpec((C, W), lambda b, c: (b * nc + c, col // W))
    pspec = lambda width: pl.BlockSpec((1, width), lambda b, c: (0, 0))
    npairs = RW_HEADS // 2
    prow = lambda a: a.reshape(npairs, 1, LANES).astype(F32)
    ppspec = pl.BlockSpec((npairs, 1, LANES), lambda b, c: (0, 0, 0))
    wspec = pl.BlockSpec((RW_LORA, W), lambda b, c: (0, 0))
    return pl.pallas_call(
        _rwkv_kernel,
        grid=(B, nc),
        in_specs=[zspec(COL_R), zspec(COL_K), zspec(COL_V), zspec(COL_GRW),
                  pl.BlockSpec((C, LANES), lambda b, c: (b * nc + c, COL_WA // LANES)),
                  pspec(W), pspec(W), pspec(W), pspec(LANES),
                  pspec(W), pspec(W), pspec(W), pspec(W), ppspec, ppspec, ppspec,
                  wspec, wspec],
        out_specs=pl.BlockSpec((C, W), lambda b, c: (b * nc + c, 0)),
        out_shape=jax.ShapeDtypeStruct((B * T, W), BF16),
        scratch_shapes=[pltpu.VMEM((1, W), F32), pltpu.VMEM((1, W), F32), pltpu.VMEM((1, W), F32),
                        pltpu.VMEM((1, LANES), F32),
                        pltpu.VMEM((RW_HEADS // 2, LANES, LANES), F32)],
        compiler_params=pltpu.CompilerParams(dimension_semantics=("parallel", "arbitrary"),
                                             vmem_limit_bytes=VMEM_LIMIT),
        name="rwkv",
    )(z, z, z, z, z,
      row(mu_r), row(mu_k), row(mu_v), row(mu_wa),
      row(w0), row(a0), row(k_k), row(k_a), prow(r_k), prow(ln_g), prow(ln_b),
      w_up.astype(BF16), a_up.astype(BF16))


def _biastab_kernel(rb_ref, o_ref):
    nb = NUM_BUCKETS // 2
    max_exact = nb // 2
    r = lax.broadcasted_iota(I32, (QB, 2 * QB), 0)
    c = lax.broadcasted_iota(I32, (QB, 2 * QB), 1)
    rel = c - QB - r
    ret = jnp.where(rel > 0, nb, 0)
    n = jnp.abs(rel)
    nf = jnp.maximum(n, 1).astype(F32)
    large = max_exact + (jnp.log(nf / max_exact) / math.log(MAX_DISTANCE / max_exact)
                         * (nb - max_exact)).astype(I32)
    large = jnp.minimum(large, nb - 1)
    bucket = ret + jnp.where(n < max_exact, n, large)
    for h in range(DS_HEADS):
        far = rb_ref[nb - 1, h]
        acc = jnp.zeros((QB, 2 * QB), F32)
        for b in range(NUM_BUCKETS):
            acc = jnp.where(bucket == b, rb_ref[b, h] - far, acc)
        o_ref[h] = acc


def _biastab(rel_bias):
    return pl.pallas_call(
        _biastab_kernel,
        in_specs=[pl.BlockSpec(memory_space=pltpu.SMEM)],
        out_specs=pl.BlockSpec(memory_space=pltpu.VMEM),
        out_shape=jax.ShapeDtypeStruct((DS_HEADS, QB, 2 * QB), F32),
        name="biastab",
    )(rel_bias.astype(F32))


def _fold_lanes(x):
    acc = x[:, 0:LANES]
    for j in range(1, x.shape[1] // LANES):
        acc = acc + x[:, j * LANES:(j + 1) * LANES]
    return acc


def _dsa_kernel(ql_ref, kx_ref, gds_ref, kid_ref, ckv_ref, qg_ref, wq_ref, wuk_ref, wuv_ref, tab_ref,
                o_ref,
                key_ref, msk_ref, lg_ref, acc_ref, ls_ref, qa_ref, qi_ref, oh_ref, *, topk):
    i = pl.program_id(1)
    q0 = i * QB
    ntile = jnp.right_shift(q0 + (QB + SK - 1), SK.bit_length() - 1)

    ql = ql_ref[...]
    ms = jnp.mean(ql * ql, axis=-1, keepdims=True)
    qn = (ql * lax.rsqrt(ms + NORM_EPS) * qg_ref[...]).astype(BF16)
    q = _dot(qn, wq_ref[...])
    for h in range(DS_HEADS):
        qh = q[:, h * DS_HEAD:(h + 1) * DS_HEAD].astype(BF16)
        qa_ref[h] = (_dot(qh, wuk_ref[h]) * (DS_HEAD ** -0.5)).astype(BF16)
        qi_ref[h] = q[:, DS_WIDTH + h * IDX_DIM:DS_WIDTH + (h + 1) * IDX_DIM].astype(BF16)
    wi = kx_ref[:, IDX_DIM:IDX_DIM + IDX_HEADS] * (IDX_HEADS ** -0.5 * IDX_DIM ** -0.5)

    rowi = lax.broadcasted_iota(I32, (QB, 1), 0)
    csh = CHUNK.bit_length() - 1
    limit = jnp.left_shift(jnp.right_shift(q0 + rowi, csh) + 1, csh)
    coli = lax.broadcasted_iota(I32, (1, SK), 1)

    def score_tile(kt, carry):
        off = pl.multiple_of(kt * SK, SK)
        kid = kid_ref[pl.ds(off, SK), :]
        s = jnp.zeros((QB, SK), F32)
        for h in range(IDX_HEADS):
            lg = _dot_nt(qi_ref[h], kid)
            s = s + wi[:, h:h + 1] * jnp.maximum(lg, 0.0)
        bits = pltpu.bitcast(s, I32)
        key = jnp.where(bits < 0, bits ^ 0x7FFFFFFF, bits)
        adm = (off + coli) < limit
        key_ref[:, pl.ds(off, SK)] = jnp.where(adm, key, INT_MIN)
        return carry

    lax.fori_loop(0, ntile, score_tile, 0)

    def count(pred):
        def body(kt, acc):
            off = pl.multiple_of(kt * SK, SK)
            keys = key_ref[:, pl.ds(off, SK)]
            return acc + _fold_lanes(jnp.where(pred(keys, off), 1, 0).astype(I32))
        acc = lax.fori_loop(0, ntile, body, jnp.zeros((QB, LANES), I32))
        return jnp.sum(acc, axis=1, keepdims=True)

    def bit_step(it, lo):
        inc = jnp.left_shift(jnp.int32(1), 31 - it)
        cand = lo + inc
        cnt = count(lambda keys, off: keys >= cand)
        return jnp.where(cnt >= topk, cand, lo)

    thr = lax.fori_loop(0, 32, bit_step, jnp.full((QB, 1), INT_MIN, I32))
    cnt_gt = count(lambda keys, off: keys > thr)
    nbits = max(1, (key_ref.shape[1] - 1).bit_length())

    def idx_step(it, m):
        cand = m + jnp.left_shift(jnp.int32(1), nbits - 1 - it)
        cnt = cnt_gt + count(lambda keys, off: (keys == thr) & ((off + coli) < cand))
        return jnp.where(cnt < topk, cand, m)

    cut = lax.fori_loop(0, nbits, idx_step, jnp.zeros((QB, 1), I32))

    def mask_tile(kt, carry):
        off = pl.multiple_of(kt * SK, SK)
        keys = key_ref[:, pl.ds(off, SK)]
        sel = (keys > thr) | ((keys == thr) & ((off + coli) <= cut))
        sel = sel & (keys != INT_MIN)
        msk_ref[:, pl.ds(off, SK)] = jnp.where(sel, 0.0, MASK_NEG).astype(F32)
        return carry

    lax.fori_loop(0, ntile, mask_tile, 0)

    R = DS_KV_RANK
    for grp in range(DS_HEADS // HG):
        hs = grp * HG
        qa2 = qa_ref[hs:hs + HG].reshape(HG * QB, R)

        def logits_tile(kt, c):
            off = pl.multiple_of(kt * SK, SK)
            s = _dot_nt(qa2, ckv_ref[pl.ds(off, SK), :]).reshape(HG, QB, SK)
            lg_ref[:, :, pl.ds(off, SK)] = s + msk_ref[:, pl.ds(off, SK)][None]
            return c

        lax.fori_loop(0, ntile, logits_tile, 0)

        @pl.when(i > 0)
        def _():
            d1 = pl.multiple_of(q0 - QB, QB)
            s = _dot_nt(qa2, ckv_ref[pl.ds(d1, 2 * QB), :]).reshape(HG, QB, 2 * QB)
            lg_ref[:, :, pl.ds(d1, 2 * QB)] = (s + tab_ref[hs:hs + HG]
                                               + msk_ref[:, pl.ds(d1, 2 * QB)][None])

        @pl.when(i == 0)
        def _():
            s = _dot_nt(qa2, ckv_ref[0:QB, :]).reshape(HG, QB, QB)
            lg_ref[:, :, 0:QB] = s + tab_ref[hs:hs + HG, :, QB:2 * QB] + msk_ref[:, 0:QB][None]

        def max_tile(kt, m):
            off = pl.multiple_of(kt * SK, SK)
            t = lg_ref[:, :, pl.ds(off, SK)]
            acc = t[:, :, 0:LANES]
            for j in range(1, SK // LANES):
                acc = jnp.maximum(acc, t[:, :, j * LANES:(j + 1) * LANES])
            return jnp.maximum(m, acc)

        mx = lax.fori_loop(0, ntile, max_tile, jnp.full((HG, QB, LANES), MASK_NEG, F32))
        mx = jnp.max(mx, axis=2, keepdims=True)

        acc_ref[...] = jnp.zeros_like(acc_ref)
        ls_ref[...] = jnp.zeros_like(ls_ref)

        def pv_tile(kt, c):
            off = pl.multiple_of(kt * SK, SK)
            pr = jnp.exp(lg_ref[:, :, pl.ds(off, SK)] - mx)
            lsum = pr[:, :, 0:LANES]
            for j in range(1, SK // LANES):
                lsum = lsum + pr[:, :, j * LANES:(j + 1) * LANES]
            ls_ref[...] += lsum
            acc_ref[...] += _dot(pr.astype(BF16).reshape(HG * QB, SK), ckv_ref[pl.ds(off, SK), :])
            return c

        lax.fori_loop(0, ntile, pv_tile, 0)
        o_lat = acc_ref[...].reshape(HG, QB, R) / jnp.sum(ls_ref[...], axis=2, keepdims=True)
        oh_ref[hs:hs + HG] = lax.dot_general(o_lat.astype(BF16), wuv_ref[hs:hs + HG],
                                             (((2,), (1,)), ((0,), (0,))),
                                             preferred_element_type=F32)

    g = gds_ref[...]
    gate = g * _sigmoid(g)
    for h in range(DS_HEADS):
        cs = slice(h * DS_HEAD, (h + 1) * DS_HEAD)
        o_ref[:, cs] = (oh_ref[h] * gate[:, cs]).astype(BF16)


def _dsa(z, ckv, kid, B, T, q_norm_g, w_uq, w_uk, w_uv, iw_q, tab):
    nq = T // QB
    topk = min(TOPK_MAX, T // 4)
    wq = jnp.concatenate([w_uq, iw_q], axis=1).astype(BF16)
    wuk_t = jnp.transpose(w_uk, (1, 2, 0)).astype(BF16)
    wuv_h = jnp.transpose(w_uv, (1, 0, 2)).astype(BF16)
    const2 = lambda b, i: (0, 0)
    const3 = lambda b, i: (0, 0, 0)
    return pl.pallas_call(
        functools.partial(_dsa_kernel, topk=topk),
        grid=(B, nq),
        in_specs=[pl.BlockSpec((QB, DS_Q_RANK), lambda b, i: (b * nq + i, COL_Q // DS_Q_RANK)),
                  pl.BlockSpec((QB, LANES), lambda b, i: (b * nq + i, COL_KX // LANES)),
                  pl.BlockSpec((QB, DS_WIDTH), lambda b, i: (b * nq + i, COL_GDS // DS_WIDTH)),
                  pl.BlockSpec((T, IDX_DIM), lambda b, i: (b, 0)),
                  pl.BlockSpec((T, DS_KV_RANK), lambda b, i: (b, 0)),
                  pl.BlockSpec((1, DS_Q_RANK), const2),
                  pl.BlockSpec((DS_Q_RANK, 2 * DS_WIDTH), const2),
                  pl.BlockSpec((DS_HEADS, DS_HEAD, DS_KV_RANK), const3),
                  pl.BlockSpec((DS_HEADS, DS_KV_RANK, DS_HEAD), const3),
                  pl.BlockSpec((DS_HEADS, QB, 2 * QB), const3)],
        out_specs=pl.BlockSpec((QB, DS_WIDTH), lambda b, i: (b * nq + i, 0)),
        out_shape=jax.ShapeDtypeStruct((B * T, DS_WIDTH), BF16),
        scratch_shapes=[pltpu.VMEM((QB, T), I32),
                        pltpu.VMEM((QB, T), F32),
                        pltpu.VMEM((HG, QB, T), F32),
                        pltpu.VMEM((HG * QB, DS_KV_RANK), F32),
                        pltpu.VMEM((HG, QB, LANES), F32),
                        pltpu.VMEM((DS_HEADS, QB, DS_KV_RANK), BF16),
                        pltpu.VMEM((IDX_HEADS, QB, IDX_DIM), BF16),
                        pltpu.VMEM((DS_HEADS, QB, DS_HEAD), F32)],
        compiler_params=pltpu.CompilerParams(dimension_semantics=("parallel", "arbitrary"),
                                             vmem_limit_bytes=VMEM_LIMIT),
        name="dsa",
    )(z, z, z, kid, ckv, q_norm_g.reshape(1, -1).astype(F32), wq, wuk_t, wuv_h, tab)


def _outproj_kernel(x_ref, a1_ref, a2_ref, w1_ref, w2_ref, o_ref):
    o_ref[...] = x_ref[...] + _dot(a1_ref[...], w1_ref[...]) + _dot(a2_ref[...], w2_ref[...])


def _outproj(x2, o_rw, o_ds, w_out, tm=512, tn=1024):
    m, d = x2.shape
    kh = o_rw.shape[1]
    return pl.pallas_call(
        _outproj_kernel,
        grid=(m // tm, d // tn),
        in_specs=[pl.BlockSpec((tm, tn), lambda i, j: (i, j)),
                  pl.BlockSpec((tm, kh), lambda i, j: (i, 0)),
                  pl.BlockSpec((tm, kh), lambda i, j: (i, 0)),
                  pl.BlockSpec((kh, tn), lambda i, j: (0, j)),
                  pl.BlockSpec((kh, tn), lambda i, j: (1, j))],
        out_specs=pl.BlockSpec((tm, tn), lambda i, j: (i, j)),
        out_shape=jax.ShapeDtypeStruct((m, d), F32),
        compiler_params=pltpu.CompilerParams(dimension_semantics=("parallel", "arbitrary"),
                                             vmem_limit_bytes=VMEM_LIMIT),
        name="outproj",
    )(x2, o_rw, o_ds, w_out, w_out)


def _tail_kernel(h_ref, p_ref, pw_ref, gw_ref, fg_ref, o_ref):
    h = h_ref[...]
    e = _dot(p_ref[...].astype(BF16), pw_ref[...])
    gate = _sigmoid(_dot(h.astype(BF16), gw_ref[...]))
    h2 = h + e * gate
    ms = jnp.mean(h2 * h2, axis=-1, keepdims=True)
    o_ref[...] = h2 * lax.rsqrt(ms + NORM_EPS) * fg_ref[...]


def _tail(h, p2, ple_w, gate_w, final_g, tm=256):
    m, d = h.shape
    pd = p2.shape[1]
    return pl.pallas_call(
        _tail_kernel,
        grid=(m // tm,),
        in_specs=[pl.BlockSpec((tm, d), lambda i: (i, 0)),
                  pl.BlockSpec((tm, pd), lambda i: (i, 0)),
                  pl.BlockSpec((pd, d), lambda i: (0, 0)),
                  pl.BlockSpec((d, d), lambda i: (0, 0)),
                  pl.BlockSpec((1, d), lambda i: (0, 0))],
        out_specs=pl.BlockSpec((tm, d), lambda i: (i, 0)),
        out_shape=jax.ShapeDtypeStruct((m, d), F32),
        compiler_params=pltpu.CompilerParams(dimension_semantics=("parallel",),
                                             vmem_limit_bytes=VMEM_LIMIT),
        name="tail",
    )(h, p2, ple_w, gate_w, final_g)


def _regroup_w_in(w):
    s0 = 3 * RW_WIDTH
    s1 = s0 + 2 * RW_LORA
    s2 = s1 + RW_WIDTH
    s3 = s2 + DS_Q_RANK
    s4 = s3 + DS_KV_RANK
    s5 = s4 + IDX_DIM
    s6 = s5 + IDX_HEADS
    pad = jnp.zeros((w.shape[0], Z_WIDTH - (COL_KX + IDX_DIM + IDX_HEADS)), w.dtype)
    return jnp.concatenate([w[:, 0:s0], w[:, s1:s2], w[:, s6:], w[:, s3:s4], w[:, s2:s3],
                            w[:, s0:s1], w[:, s4:s6], pad], axis=1)


def kernel(x, p, w_in, norm_g, rw_mu, rw_w0, rw_w_up, rw_a0, rw_a_up, rw_k_k, rw_k_a, rw_r_k, rw_ln_g, rw_ln_b, ds_q_norm_g, ds_kv_norm_g, idx_k_norm_g, ds_w_uq, ds_w_uk, ds_w_uv, idx_w_q, rel_bias, w_out, ple_w, ple_gate_w, final_g):
    B, T, D = x.shape
    depth = w_in.shape[0]
    assert depth == 1 and T % SK == 0 and T % RW_CHUNK == 0 and (B * T) % 512 == 0
    h = x.reshape(B * T, D)
    tab = _biastab(rel_bias)
    for i in range(depth):
        w = _regroup_w_in(w_in[i]).astype(BF16)
        z = _inproj(h, norm_g[i].reshape(1, D), w)
        ckv, kid = _kvprep(z, ds_kv_norm_g[i].reshape(1, -1), idx_k_norm_g[i].reshape(1, -1))
        o_rw = _rwkv(z, B, T, rw_mu[i], rw_w0[i], rw_a0[i], rw_k_k[i], rw_k_a[i],
                     rw_r_k[i].reshape(-1), rw_ln_g[i], rw_ln_b[i], rw_w_up[i], rw_a_up[i])
        o_ds = _dsa(z, ckv, kid, B, T, ds_q_norm_g[i], ds_w_uq[i], ds_w_uk[i], ds_w_uv[i],
                    idx_w_q[i], tab)
        h = _outproj(h, o_rw, o_ds, w_out[i].astype(BF16))
        h = _tail(h, p[i].reshape(B * T, -1), ple_w[i].astype(BF16), ple_gate_w[i].astype(BF16),
                  final_g.reshape(1, D))
    return h.reshape(B, T, D)
```

```python
import functools
import math

import jax
import jax.numpy as jnp
from jax import lax
from jax.experimental import pallas as pl
from jax.experimental.pallas import tpu as pltpu

F32 = jnp.float32
BF16 = jnp.bfloat16
I32 = jnp.int32

RW_WIDTH = 1024
RW_HEAD = 64
RW_HEADS = 16
RW_LORA = 64
DS_WIDTH = 1024
DS_HEAD = 64
DS_HEADS = 16
DS_Q_RANK = 384
DS_KV_RANK = 256
IDX_HEADS = 16
IDX_DIM = 64
TOPK_MAX = 256
CHUNK = 64
NUM_BUCKETS = 32
MAX_DISTANCE = 128
NORM_EPS = 1e-6
GN_EPS = 64e-5

COL_R, COL_K, COL_V, COL_GRW, COL_GDS = 0, 1024, 2048, 3072, 4096
COL_KV = 5120
COL_Q = 5376
COL_WA = 5760
COL_KX = 5888
Z_WIDTH = 6144

LANES = 128
SUBLANES = 8
QB = 128
SK = 512
HG = 8
RW_CHUNK = 64
INT_MIN = -2 ** 31
MASK_NEG = -1e30
LOG2E = 1.4426950408889634
ONES_ROWS = 16
VMEM_LIMIT = 52 * 1024 * 1024


def _sigmoid(x):
    return 1.0 / (1.0 + jnp.exp(-x))


def _dot(a, b):
    return jnp.dot(a, b, preferred_element_type=F32)


def _dot_nt(a, b):
    return lax.dot_general(a, b, (((1,), (1,)), ((), ())), preferred_element_type=F32)


def _dot_tn(a, b):
    return lax.dot_general(a, b, (((0,), (0,)), ((), ())), preferred_element_type=F32)


def _inproj_kernel(x_ref, g_ref, w_ref, o_ref, xn_ref):
    @pl.when(pl.program_id(1) == 0)
    def _():
        x = x_ref[...]
        ms = jnp.mean(x * x, axis=-1, keepdims=True)
        xn_ref[...] = (x * lax.rsqrt(ms + NORM_EPS) * g_ref[...]).astype(BF16)

    o_ref[...] = _dot(xn_ref[...], w_ref[...])


def _inproj(x2, g, w, tm=512, tn=768):
    m, d = x2.shape
    n = w.shape[1]
    return pl.pallas_call(
        _inproj_kernel,
        grid=(m // tm, n // tn),
        in_specs=[pl.BlockSpec((tm, d), lambda i, j: (i, 0)),
                  pl.BlockSpec((1, d), lambda i, j: (0, 0)),
                  pl.BlockSpec((d, tn), lambda i, j: (0, j))],
        out_specs=pl.BlockSpec((tm, tn), lambda i, j: (i, j)),
        out_shape=jax.ShapeDtypeStruct((m, n), F32),
        scratch_shapes=[pltpu.VMEM((tm, d), BF16)],
        compiler_params=pltpu.CompilerParams(dimension_semantics=("parallel", "arbitrary"),
                                             vmem_limit_bytes=VMEM_LIMIT),
        name="inproj",
    )(x2, g, w)


def _kvprep_kernel(kv_ref, kx_ref, gkv_ref, gik_ref, ckv_ref, kid_ref):
    kv = kv_ref[...]
    ms = jnp.mean(kv * kv, axis=-1, keepdims=True)
    ckv_ref[...] = (kv * lax.rsqrt(ms + NORM_EPS) * gkv_ref[...]).astype(BF16)
    ki = kx_ref[:, 0:IDX_DIM]
    ms2 = jnp.mean(ki * ki, axis=-1, keepdims=True)
    kid_ref[...] = (ki * lax.rsqrt(ms2 + NORM_EPS) * gik_ref[...]).astype(BF16)


def _kvprep(z, gkv, gik, tm=512):
    m = z.shape[0]
    return pl.pallas_call(
        _kvprep_kernel,
        grid=(m // tm,),
        in_specs=[pl.BlockSpec((tm, DS_KV_RANK), lambda i: (i, COL_KV // DS_KV_RANK)),
                  pl.BlockSpec((tm, LANES), lambda i: (i, COL_KX // LANES)),
                  pl.BlockSpec((1, DS_KV_RANK), lambda i: (0, 0)),
                  pl.BlockSpec((1, IDX_DIM), lambda i: (0, 0))],
        out_specs=[pl.BlockSpec((tm, DS_KV_RANK), lambda i: (i, 0)),
                   pl.BlockSpec((tm, IDX_DIM), lambda i: (i, 0))],
        out_shape=[jax.ShapeDtypeStruct((m, DS_KV_RANK), BF16),
                   jax.ShapeDtypeStruct((m, IDX_DIM), BF16)],
        compiler_params=pltpu.CompilerParams(dimension_semantics=("parallel",)),
        name="kvprep",
    )(z, z, gkv, gik)


def _rwkv_kernel(r_ref, k_ref, v_ref, g_ref, wa_ref,
                 mur_ref, muk_ref, muv_ref, muwa_ref,
                 w0_ref, a0_ref, kk_ref, ka_ref, rk_ref, lng_ref, lnb_ref,
                 wup_ref, aup_ref,
                 o_ref,
                 pr_ref, pk_ref, pv_ref, pwa_ref, st_ref):
    C = RW_CHUNK
    N = RW_HEAD

    @pl.when(pl.program_id(1) == 0)
    def _():
        pr_ref[...] = jnp.zeros_like(pr_ref)
        pk_ref[...] = jnp.zeros_like(pk_ref)
        pv_ref[...] = jnp.zeros_like(pv_ref)
        pwa_ref[...] = jnp.zeros_like(pwa_ref)
        st_ref[...] = jnp.zeros_like(st_ref)

    row = lax.broadcasted_iota(I32, (C, 1), 0)

    def shift(ref, prev_ref, mu_ref):
        z = ref[...]
        zp = pltpu.roll(z, 1, 0)
        zp = jnp.where(row == 0, prev_ref[...], zp)
        prev_ref[...] = z[C - 1:C, :]
        return z + mu_ref[...] * (zp - z)

    r = shift(r_ref, pr_ref, mur_ref)
    k = shift(k_ref, pk_ref, muk_ref)
    v = shift(v_ref, pv_ref, muv_ref)
    wa = shift(wa_ref, pwa_ref, muwa_ref)
    wd = wa[:, 0:RW_LORA]
    ad = wa[:, RW_LORA:2 * RW_LORA]

    wl = w0_ref[...] + _dot(jnp.tanh(wd).astype(BF16), wup_ref[...])
    nwl = -wl
    softplus = jnp.maximum(nwl, 0.0) + jnp.log1p(jnp.exp(-jnp.abs(nwl)))
    w_log = -softplus - 0.5
    lw = -jnp.exp(w_log)
    a = _sigmoid(a0_ref[...] + _dot(ad.astype(BF16), aup_ref[...]))
    kk = k * kk_ref[...]
    k2 = k * (1.0 + (a - 1.0) * ka_ref[...])

    ti = lax.broadcasted_iota(I32, (C, C), 0)
    tj = lax.broadcasted_iota(I32, (C, C), 1)
    incl = ti >= tj
    strict = ti > tj
    tri = jnp.where(incl, 1.0, 0.0).astype(F32)
    cum = jnp.dot(tri, lw, preferred_element_type=F32, precision=lax.Precision.HIGHEST)
    p = jnp.exp(cum)
    pinv = jnp.exp(-cum)
    pprev = jnp.exp(cum - lw)
    tot = jnp.dot(jnp.ones((C, C), F32), lw, preferred_element_type=F32,
                  precision=lax.Precision.HIGHEST)
    pend = jnp.exp(tot)

    g = g_ref[...]
    gate = g * _sigmoid(g)

    NP = RW_HEADS // 2

    def pairs(x):
        return jnp.stack([x[:, j * LANES:(j + 1) * LANES] for j in range(NP)], axis=0)

    lane = lax.broadcasted_iota(I32, (1, 1, LANES), 2)
    m_lo = jnp.where(lane < N, 1.0, 0.0).astype(F32)
    m_hi = 1.0 - m_lo
    bi = lax.broadcasted_iota(I32, (LANES, LANES), 0)
    bj = lax.broadcasted_iota(I32, (LANES, LANES), 1)
    same_head = (bi < N) == (bj < N)
    ones_bd = jnp.where(same_head, 1.0, 0.0).astype(BF16)

    def head_sum(x):
        return _dot(x.reshape(NP * C, LANES).astype(BF16), ones_bd).reshape(NP, C, LANES)

    def halves(x):
        return jnp.concatenate([x * m_lo, x * m_hi], axis=1)

    def bmm(x, y):
        return lax.dot_general(x, y, (((2,), (1,)), ((0,), (0,))), preferred_element_type=F32)

    def bmm_nt(x, y):
        return lax.dot_general(x, y, (((2,), (2,)), ((0,), (0,))), preferred_element_type=F32)

    def block_mask(nblk, cmp):
        wi = lax.broadcasted_iota(I32, (C, nblk * C), 0)
        wj = lax.broadcasted_iota(I32, (C, nblk * C), 1) & (C - 1)
        return cmp(wi, wj)

    r_p, k2_p, v_p, a_p = pairs(r), pairs(k2), pairs(v), pairs(a)
    p_p, pinv_p, pprev_p = pairs(p), pairs(pinv), pairs(pprev)
    kk_p = pairs(kk)
    kkn = kk_p / jnp.maximum(jnp.sqrt(head_sum(kk_p * kk_p)), 1e-12)
    at = (-kkn) * pprev_p
    bt = (kkn * a_p) * pinv_p
    kt = k2_p * pinv_p
    rt = r_p * p_p
    pend_p = pairs(pend)
    pend2 = jnp.concatenate([pend_p, pend_p], axis=1)

    nmat = bmm_nt(halves(at).astype(BF16), bt.astype(BF16)).reshape(2 * NP, C, C)
    nmat = jnp.where(strict, nmat, 0.0)
    lhs2 = jnp.concatenate([at, rt], axis=1).astype(BF16)
    rhs4 = jnp.concatenate([halves(kt), halves(bt)], axis=1).astype(BF16)
    gc = bmm_nt(lhs2, rhs4)
    strict2 = block_mask(2, lambda i_, j_: i_ > j_)
    incl4 = block_mask(4, lambda i_, j_: i_ >= j_)
    a_ak = jnp.where(strict2, gc[:, 0:C, 0:2 * C], 0.0)
    a_rkb = jnp.where(incl4, gc[:, C:2 * C, :], 0.0)

    g0 = st_ref[...]
    sg = bmm_nt(lhs2, g0.astype(BF16))
    vm2 = halves(v_p).astype(BF16)
    x0 = sg[:, 0:C] + bmm(a_ak.astype(BF16), vm2)
    xm = halves(x0).reshape(2 * NP, C, LANES)
    pw = nmat.astype(BF16)
    xm = xm + bmm(pw, xm.astype(BF16))
    n = 1
    while 2 * n < C:
        pw = bmm(pw, pw).astype(BF16)
        xm = xm + bmm(pw, xm.astype(BF16))
        n *= 2
    um2 = xm.reshape(NP, 2 * C, LANES)
    um2b = um2.astype(BF16)
    y = sg[:, C:2 * C] + bmm(a_rkb.astype(BF16), jnp.concatenate([vm2, um2b], axis=1))
    u = um2[:, 0:C] + um2[:, C:2 * C]
    uv = jnp.concatenate([u, v_p], axis=1).astype(BF16)
    bkh = (jnp.concatenate([bt, kt], axis=1) * pend2).astype(BF16)
    upd = lax.dot_general(uv, bkh, (((1,), (1,)), ((0,), (0,))), preferred_element_type=F32)
    st_ref[...] = g0 * pend2 + jnp.where(same_head, upd, 0.0)

    inv_n = 1.0 / N
    yc = y - head_sum(y) * inv_n
    var = head_sum(yc * yc) * inv_n
    yn = yc * lax.rsqrt(var + GN_EPS) * lng_ref[...] + lnb_ref[...]
    bonus = head_sum(r_p * k2_p * rk_ref[...]) * v_p
    out = (yn + bonus) * pairs(gate)
    for j in range(NP):
        o_ref[:, j * LANES:(j + 1) * LANES] = out[j].astype(BF16)


def _rwkv(z, B, T, mu, w0, a0, k_k, k_a, r_k, ln_g, ln_b, w_up, a_up):
    C = RW_CHUNK
    nc = T // C
    W = RW_WIDTH
    row = lambda a: a.reshape(1, -1).astype(F32)
    mu_r, mu_k, mu_v = mu[0:W], mu[W:2 * W], mu[2 * W:3 * W]
    mu_wa = mu[3 * W:3 * W + 2 * RW_LORA]
    zspec = lambda col: pl.BlockSpec((C, W), lambda b, c: (b * nc + c, col // W))
    pspec = lambda width: pl.BlockSpec((1, width), lambda b, c: (0, 0))
    npairs = RW_HEADS // 2
    prow = lambda a: a.reshape(npairs, 1, LANES).astype(F32)
    ppspec = pl.BlockSpec((npairs, 1, LANES), lambda b, c: (0, 0, 0))
    wspec = pl.BlockSpec((RW_LORA, W), lambda b, c: (0, 0))
    return pl.pallas_call(
        _rwkv_kernel,
        grid=(B, nc),
        in_specs=[zspec(COL_R), zspec(COL_K), zspec(COL_V), zspec(COL_GRW),
                  pl.BlockSpec((C, LANES), lambda b, c: (b * nc + c, COL_WA // LANES)),
                  pspec(W), pspec(W), pspec(W), pspec(LANES),
                  pspec(W), pspec(W), pspec(W), pspec(W), ppspec, ppspec, ppspec,
                  wspec, wspec],
        out_specs=pl.BlockSpec((C, W), lambda b, c: (b * nc + c, 0)),
        out_shape=jax.ShapeDtypeStruct((B * T, W), BF16),
        scratch_shapes=[pltpu.VMEM((1, W), F32), pltpu.VMEM((1, W), F32), pltpu.VMEM((1, W), F32),
                        pltpu.VMEM((1, LANES), F32),
                        pltpu.VMEM((RW_HEADS // 2, LANES, LANES), F32)],
        compiler_params=pltpu.CompilerParams(dimension_semantics=("parallel", "arbitrary"),
                                             vmem_limit_bytes=VMEM_LIMIT),
        name="rwkv",
    )(z, z, z, z, z,
      row(mu_r), row(mu_k), row(mu_v), row(mu_wa),
      row(w0), row(a0), row(k_k), row(k_a), prow(r_k), prow(ln_g), prow(ln_b),
      w_up.astype(BF16), a_up.astype(BF16))


def _biastab_kernel(rb_ref, o_ref):
    nb = NUM_BUCKETS // 2
    max_exact = nb // 2
    c = lax.broadcasted_iota(I32, (2 * QB, QB), 0)
    r = lax.broadcasted_iota(I32, (2 * QB, QB), 1)
    rel = c - QB - r
    ret = jnp.where(rel > 0, nb, 0)
    n = jnp.abs(rel)
    nf = jnp.maximum(n, 1).astype(F32)
    large = max_exact + (jnp.log(nf / max_exact) / math.log(MAX_DISTANCE / max_exact)
                         * (nb - max_exact)).astype(I32)
    large = jnp.minimum(large, nb - 1)
    bucket = ret + jnp.where(n < max_exact, n, large)
    for h in range(DS_HEADS):
        far = rb_ref[nb - 1, h]
        acc = jnp.zeros((2 * QB, QB), F32)
        for b in range(NUM_BUCKETS):
            acc = jnp.where(bucket == b, rb_ref[b, h] - far, acc)
        o_ref[:, h * QB:(h + 1) * QB] = acc * LOG2E


def _biastab(rel_bias):
    return pl.pallas_call(
        _biastab_kernel,
        in_specs=[pl.BlockSpec(memory_space=pltpu.SMEM)],
        out_specs=pl.BlockSpec(memory_space=pltpu.VMEM),
        out_shape=jax.ShapeDtypeStruct((2 * QB, DS_HEADS * QB), F32),
        name="biastab",
    )(rel_bias.astype(F32))


def _fold_rows(x, op):
    n = x.shape[0] // SUBLANES
    accs = [x[j * SUBLANES:(j + 1) * SUBLANES] for j in range(min(4, n))]
    for j in range(4, n):
        accs[j % 4] = op(accs[j % 4], x[j * SUBLANES:(j + 1) * SUBLANES])
    while len(accs) > 1:
        accs = [op(accs[j], accs[j + 1]) for j in range(0, len(accs) - 1, 2)] + (
            [accs[-1]] if len(accs) % 2 else [])
    return accs[0]


def _dsa_kernel(ql_ref, kx_ref, gds_ref, kid_ref, ckv_ref, ckvt_ref, qg_ref, wq_ref, wuk_ref, wuvt_ref,
                tab_ref, o_ref,
                key_ref, msk_ref, lg_ref, acc_ref, qat_ref, qit_ref, w_ref, out_ref, *, topk):
    i = pl.program_id(1)
    q0 = i * QB
    ntile = jnp.right_shift(q0 + (QB + SK - 1), SK.bit_length() - 1)
    R = DS_KV_RANK
    GW = HG * QB

    ql = ql_ref[...]
    ms = jnp.mean(ql * ql, axis=-1, keepdims=True)
    qn = (ql * lax.rsqrt(ms + NORM_EPS) * qg_ref[...]).astype(BF16)
    qt = _dot(qn, wq_ref[...]).T
    for h in range(DS_HEADS):
        qh = qt[h * DS_HEAD:(h + 1) * DS_HEAD, :].astype(BF16)
        qat_ref[:, h * QB:(h + 1) * QB] = (_dot(wuk_ref[h], qh) * (DS_HEAD ** -0.5 * LOG2E)).astype(BF16)
    for pr in range(IDX_HEADS // 2):
        base = DS_WIDTH + 2 * pr * IDX_DIM
        qit_ref[pr] = jnp.concatenate([qt[base:base + IDX_DIM, :],
                                       qt[base + IDX_DIM:base + 2 * IDX_DIM, :]], axis=1).astype(BF16)
    w_ref[...] = kx_ref[...].T[IDX_DIM:IDX_DIM + IDX_HEADS, :] * (IDX_HEADS ** -0.5 * IDX_DIM ** -0.5)

    lanei = lax.broadcasted_iota(I32, (1, QB), 1)
    csh = CHUNK.bit_length() - 1
    limit = jnp.left_shift(jnp.right_shift(q0 + lanei, csh) + 1, csh)
    rowi = lax.broadcasted_iota(I32, (SK, QB), 0)

    def score_tile(kt, carry):
        off = pl.multiple_of(kt * SK, SK)
        kid = kid_ref[pl.ds(off, SK), :]
        s = jnp.zeros((SK, QB), F32)
        for pr in range(IDX_HEADS // 2):
            lg = _dot(kid, qit_ref[pr])
            s = s + w_ref[2 * pr:2 * pr + 1, :] * jnp.maximum(lg[:, 0:QB], 0.0)
            s = s + w_ref[2 * pr + 1:2 * pr + 2, :] * jnp.maximum(lg[:, QB:2 * QB], 0.0)
        bits = pltpu.bitcast(s, I32)
        key = jnp.where(bits < 0, bits ^ 0x7FFFFFFF, bits)
        adm = (off + rowi) < limit
        key_ref[pl.ds(off, SK), :] = jnp.where(adm, key, INT_MIN)
        return carry

    lax.fori_loop(0, ntile, score_tile, 0)

    def count(pred):
        def body(kt, acc):
            off = pl.multiple_of(kt * SK, SK)
            keys = key_ref[pl.ds(off, SK), :]
            return acc + _fold_rows(jnp.where(pred(keys, off), 1, 0).astype(I32), jnp.add)
        acc = lax.fori_loop(0, ntile, body, jnp.zeros((SUBLANES, QB), I32))
        return jnp.sum(acc, axis=0, keepdims=True)

    def bit_step(it, lo):
        cand = lo + jnp.left_shift(jnp.int32(1), 31 - it)
        cnt = count(lambda keys, off: keys >= cand)
        return jnp.where(cnt >= topk, cand, lo)

    thr = lax.fori_loop(0, 32, bit_step, jnp.full((1, QB), INT_MIN, I32))
    cnt_gt = count(lambda keys, off: keys > thr)
    cnt_eq = count(lambda keys, off: keys == thr)
    nbits = max(1, (key_ref.shape[0] - 1).bit_length())
    tied = (cnt_gt + cnt_eq > topk) & (thr != INT_MIN)

    def search_cut():
        def idx_step(it, m):
            cand = m + jnp.left_shift(jnp.int32(1), nbits - 1 - it)
            cnt = cnt_gt + count(lambda keys, off: (keys == thr) & ((off + rowi) < cand))
            return jnp.where(cnt < topk, cand, m)
        return lax.fori_loop(0, nbits, idx_step, jnp.zeros((1, QB), I32))

    cut = lax.cond(jnp.max(jnp.where(tied, 1, 0)) > 0, search_cut,
                   lambda: jnp.full((1, QB), key_ref.shape[0], I32))

    def mask_tile(kt, carry):
        off = pl.multiple_of(kt * SK, SK)
        keys = key_ref[pl.ds(off, SK), :]
        sel = (keys > thr) | ((keys == thr) & ((off + rowi) <= cut))
        sel = sel & (keys != INT_MIN)
        msk_ref[pl.ds(off, SK), :] = jnp.where(sel, 0.0, MASK_NEG).astype(F32)
        return carry

    lax.fori_loop(0, ntile, mask_tile, 0)

    def store_logits(rows, s, bias_rows, gc):
        mk = msk_ref[rows, :]
        tmax = []
        for h in range(HG):
            cs = slice(h * QB, (h + 1) * QB)
            t = s[:, cs] + mk
            if bias_rows is not None:
                t = t + tab_ref[bias_rows, gc + h * QB:gc + (h + 1) * QB]
            lg_ref[rows, cs] = t
            tmax.append(_fold_rows(t, jnp.maximum))
        return jnp.concatenate(tmax, axis=1)

    for grp in range(DS_HEADS // HG):
        gc = grp * GW
        qat = qat_ref[:, gc:gc + GW]

        def logits_tile(kt, m):
            off = pl.multiple_of(kt * SK, SK)
            rows = pl.ds(off, SK)
            s = _dot(ckv_ref[rows, :], qat)
            tmax = store_logits(rows, s, None, gc)
            return jnp.where(kt < ntile - 2, jnp.maximum(m, tmax), m)

        mx = lax.fori_loop(0, ntile, logits_tile, jnp.full((SUBLANES, GW), MASK_NEG, F32))

        @pl.when(i > 0)
        def _():
            rows = pl.ds(pl.multiple_of(q0 - QB, QB), 2 * QB)
            store_logits(rows, _dot(ckv_ref[rows, :], qat), slice(0, 2 * QB), gc)

        @pl.when(i == 0)
        def _():
            rows = slice(0, QB)
            store_logits(rows, _dot(ckv_ref[rows, :], qat), slice(QB, 2 * QB), gc)

        def max_tile(kt, m):
            off = pl.multiple_of(kt * SK, SK)
            return jnp.maximum(m, _fold_rows(lg_ref[pl.ds(off, SK), :], jnp.maximum))

        mx = lax.fori_loop(jnp.maximum(ntile - 2, 0), ntile, max_tile, mx)
        mx = jnp.max(mx, axis=0, keepdims=True)

        acc_ref[...] = jnp.zeros_like(acc_ref)

        def pv_tile(kt, c):
            off = pl.multiple_of(kt * SK, SK)
            pr = jnp.exp2(lg_ref[pl.ds(off, SK), :] - mx)
            acc_ref[...] += _dot(ckvt_ref[:, pl.ds(off, SK)], pr.astype(BF16))
            return c

        lax.fori_loop(0, ntile, pv_tile, 0)
        o_lat = acc_ref[0:R, :] * (1.0 / acc_ref[R:R + 1, :])
        for h in range(HG):
            hh = grp * HG + h
            out_ref[hh * DS_HEAD:(hh + 1) * DS_HEAD, :] = _dot(
                wuvt_ref[hh], o_lat[:, h * QB:(h + 1) * QB].astype(BF16))

    g = gds_ref[...]
    o_ref[...] = (out_ref[...].T * (g * _sigmoid(g))).astype(BF16)


def _dsa(z, ckv, kid, B, T, q_norm_g, w_uq, w_uk, w_uv, iw_q, tab):
    nq = T // QB
    topk = min(TOPK_MAX, T // 4)
    R = DS_KV_RANK
    wq = jnp.concatenate([w_uq, iw_q], axis=1).astype(BF16)
    wuk_h = jnp.transpose(w_uk, (1, 0, 2)).astype(BF16)
    wuv_t = jnp.transpose(w_uv, (1, 2, 0)).astype(BF16)
    ckv_t = jnp.concatenate([jnp.swapaxes(ckv.reshape(B, T, R), 1, 2),
                             jnp.ones((B, ONES_ROWS, T), BF16)], axis=1).reshape(B * (R + ONES_ROWS), T)
    const2 = lambda b, i: (0, 0)
    const3 = lambda b, i: (0, 0, 0)
    return pl.pallas_call(
        functools.partial(_dsa_kernel, topk=topk),
        grid=(B, nq),
        in_specs=[pl.BlockSpec((QB, DS_Q_RANK), lambda b, i: (b * nq + i, COL_Q // DS_Q_RANK)),
                  pl.BlockSpec((QB, LANES), lambda b, i: (b * nq + i, COL_KX // LANES)),
                  pl.BlockSpec((QB, DS_WIDTH), lambda b, i: (b * nq + i, COL_GDS // DS_WIDTH)),
                  pl.BlockSpec((T, IDX_DIM), lambda b, i: (b, 0)),
                  pl.BlockSpec((T, R), lambda b, i: (b, 0)),
                  pl.BlockSpec((R + ONES_ROWS, T), lambda b, i: (b, 0)),
                  pl.BlockSpec((1, DS_Q_RANK), const2),
                  pl.BlockSpec((DS_Q_RANK, 2 * DS_WIDTH), const2),
                  pl.BlockSpec((DS_HEADS, R, DS_HEAD), const3),
                  pl.BlockSpec((DS_HEADS, DS_HEAD, R), const3),
                  pl.BlockSpec((2 * QB, DS_HEADS * QB), const2)],
        out_specs=pl.BlockSpec((QB, DS_WIDTH), lambda b, i: (b * nq + i, 0)),
        out_shape=jax.ShapeDtypeStruct((B * T, DS_WIDTH), BF16),
        scratch_shapes=[pltpu.VMEM((T, QB), I32),
                        pltpu.VMEM((T, QB), F32),
                        pltpu.VMEM((T, HG * QB), F32),
                        pltpu.VMEM((R + ONES_ROWS, HG * QB), F32),
                        pltpu.VMEM((R, DS_HEADS * QB), BF16),
                        pltpu.VMEM((IDX_HEADS // 2, IDX_DIM, 2 * QB), BF16),
                        pltpu.VMEM((IDX_HEADS, QB), F32),
                        pltpu.VMEM((DS_WIDTH, QB), F32)],
        compiler_params=pltpu.CompilerParams(dimension_semantics=("parallel", "arbitrary"),
                                             vmem_limit_bytes=VMEM_LIMIT),
        name="dsa",
    )(z, z, z, kid, ckv, ckv_t, q_norm_g.reshape(1, -1).astype(F32), wq, wuk_h, wuv_t, tab)


def _outproj_kernel(x_ref, a1_ref, a2_ref, w1_ref, w2_ref, o_ref):
    o_ref[...] = x_ref[...] + _dot(a1_ref[...], w1_ref[...]) + _dot(a2_ref[...], w2_ref[...])


def _outproj(x2, o_rw, o_ds, w_out, tm=512, tn=1024):
    m, d = x2.shape
    kh = o_rw.shape[1]
    return pl.pallas_call(
        _outproj_kernel,
        grid=(m // tm, d // tn),
        in_specs=[pl.BlockSpec((tm, tn), lambda i, j: (i, j)),
                  pl.BlockSpec((tm, kh), lambda i, j: (i, 0)),
                  pl.BlockSpec((tm, kh), lambda i, j: (i, 0)),
                  pl.BlockSpec((kh, tn), lambda i, j: (0, j)),
                  pl.BlockSpec((kh, tn), lambda i, j: (1, j))],
        out_specs=pl.BlockSpec((tm, tn), lambda i, j: (i, j)),
        out_shape=jax.ShapeDtypeStruct((m, d), F32),
        compiler_params=pltpu.CompilerParams(dimension_semantics=("parallel", "arbitrary"),
                                             vmem_limit_bytes=VMEM_LIMIT),
        name="outproj",
    )(x2, o_rw, o_ds, w_out, w_out)


def _tail_kernel(h_ref, p_ref, pw_ref, gw_ref, fg_ref, o_ref):
    h = h_ref[...]
    e = _dot(p_ref[...].astype(BF16), pw_ref[...])
    gate = _sigmoid(_dot(h.astype(BF16), gw_ref[...]))
    h2 = h + e * gate
    ms = jnp.mean(h2 * h2, axis=-1, keepdims=True)
    o_ref[...] = h2 * lax.rsqrt(ms + NORM_EPS) * fg_ref[...]


def _tail(h, p2, ple_w, gate_w, final_g, tm=256):
    m, d = h.shape
    pd = p2.shape[1]
    return pl.pallas_call(
        _tail_kernel,
        grid=(m // tm,),
        in_specs=[pl.BlockSpec((tm, d), lambda i: (i, 0)),
                  pl.BlockSpec((tm, pd), lambda i: (i, 0)),
                  pl.BlockSpec((pd, d), lambda i: (0, 0)),
                  pl.BlockSpec((d, d), lambda i: (0, 0)),
                  pl.BlockSpec((1, d), lambda i: (0, 0))],
        out_specs=pl.BlockSpec((tm, d), lambda i: (i, 0)),
        out_shape=jax.ShapeDtypeStruct((m, d), F32),
        compiler_params=pltpu.CompilerParams(dimension_semantics=("parallel",),
                                             vmem_limit_bytes=VMEM_LIMIT),
        name="tail",
    )(h, p2, ple_w, gate_w, final_g)


def _regroup_w_in(w):
    s0 = 3 * RW_WIDTH
    s1 = s0 + 2 * RW_LORA
    s2 = s1 + RW_WIDTH
    s3 = s2 + DS_Q_RANK
    s4 = s3 + DS_KV_RANK
    s5 = s4 + IDX_DIM
    s6 = s5 + IDX_HEADS
    pad = jnp.zeros((w.shape[0], Z_WIDTH - (COL_KX + IDX_DIM + IDX_HEADS)), w.dtype)
    return jnp.concatenate([w[:, 0:s0], w[:, s1:s2], w[:, s6:], w[:, s3:s4], w[:, s2:s3],
                            w[:, s0:s1], w[:, s4:s6], pad], axis=1)


def kernel(x, p, w_in, norm_g, rw_mu, rw_w0, rw_w_up, rw_a0, rw_a_up, rw_k_k, rw_k_a, rw_r_k, rw_ln_g, rw_ln_b, ds_q_norm_g, ds_kv_norm_g, idx_k_norm_g, ds_w_uq, ds_w_uk, ds_w_uv, idx_w_q, rel_bias, w_out, ple_w, ple_gate_w, final_g):
    B, T, D = x.shape
    depth = w_in.shape[0]
    assert depth == 1 and T % SK == 0 and T % RW_CHUNK == 0 and (B * T) % 512 == 0
    h = x.reshape(B * T, D)
    tab = _biastab(rel_bias)
    for i in range(depth):
        w = _regroup_w_in(w_in[i]).astype(BF16)
        z = _inproj(h, norm_g[i].reshape(1, D), w)
        ckv, kid = _kvprep(z, ds_kv_norm_g[i].reshape(1, -1), idx_k_norm_g[i].reshape(1, -1))
        o_rw = _rwkv(z, B, T, rw_mu[i], rw_w0[i], rw_a0[i], rw_k_k[i], rw_k_a[i],
                     rw_r_k[i].reshape(-1), rw_ln_g[i], rw_ln_b[i], rw_w_up[i], rw_a_up[i])
        o_ds = _dsa(z, ckv, kid, B, T, ds_q_norm_g[i], ds_w_uq[i], ds_w_uk[i], ds_w_uv[i],
                    idx_w_q[i], tab)
        h = _outproj(h, o_rw, o_ds, w_out[i].astype(BF16))
        h = _tail(h, p[i].reshape(B * T, -1), ple_w[i].astype(BF16), ple_gate_w[i].astype(BF16),
                  final_g.reshape(1, D))
    return h.reshape(B, T, D)
```

```python
import functools
import math

import jax
import jax.numpy as jnp
from jax import lax
from jax.experimental import pallas as pl
from jax.experimental.pallas import tpu as pltpu

F32 = jnp.float32
BF16 = jnp.bfloat16
I32 = jnp.int32

RW_WIDTH = 1024
RW_HEAD = 64
RW_HEADS = 16
RW_LORA = 64
DS_WIDTH = 1024
DS_HEAD = 64
DS_HEADS = 16
DS_Q_RANK = 384
DS_KV_RANK = 256
IDX_HEADS = 16
IDX_DIM = 64
TOPK_MAX = 256
CHUNK = 64
NUM_BUCKETS = 32
MAX_DISTANCE = 128
NORM_EPS = 1e-6
GN_EPS = 64e-5

COL_R, COL_K, COL_V, COL_GRW, COL_GDS = 0, 1024, 2048, 3072, 4096
COL_KV = 5120
COL_Q = 5376
COL_WA = 5760
COL_KX = 5888
Z_WIDTH = 6144

LANES = 128
SUBLANES = 8
QB = 128
SK = 512
HG = 8
RW_CHUNK = 64
INT_MIN = -2 ** 31
KEY_NEG_INF = -2139095041
MASK_NEG = -1e30
LOG2E = 1.4426950408889634
ONES_ROWS = 16
VMEM_LIMIT = 52 * 1024 * 1024


def _sigmoid(x):
    return 1.0 / (1.0 + jnp.exp(-x))


def _dot(a, b):
    return jnp.dot(a, b, preferred_element_type=F32)


def _dot_nt(a, b):
    return lax.dot_general(a, b, (((1,), (1,)), ((), ())), preferred_element_type=F32)


def _dot_tn(a, b):
    return lax.dot_general(a, b, (((0,), (0,)), ((), ())), preferred_element_type=F32)


def _inproj_kernel(x_ref, g_ref, w_ref, o_ref, *, tn):
    x = x_ref[...]
    ms = jnp.mean(x * x, axis=-1, keepdims=True)
    xn = (x * lax.rsqrt(ms + NORM_EPS) * g_ref[...]).astype(BF16)
    for j in range(o_ref.shape[1] // tn):
        o_ref[:, j * tn:(j + 1) * tn] = _dot(xn, w_ref[:, j * tn:(j + 1) * tn])


def _inproj(x2, g, w, tm=256, tn=768):
    m, d = x2.shape
    n = w.shape[1]
    return pl.pallas_call(
        functools.partial(_inproj_kernel, tn=tn),
        grid=(m // tm,),
        in_specs=[pl.BlockSpec((tm, d), lambda i: (i, 0)),
                  pl.BlockSpec((1, d), lambda i: (0, 0)),
                  pl.BlockSpec((d, n), lambda i: (0, 0), pipeline_mode=pl.Buffered(1))],
        out_specs=pl.BlockSpec((tm, n), lambda i: (i, 0)),
        out_shape=jax.ShapeDtypeStruct((m, n), F32),
        compiler_params=pltpu.CompilerParams(dimension_semantics=("parallel",),
                                             vmem_limit_bytes=VMEM_LIMIT),
        name="inproj",
    )(x2, g, w)


def _kvprep_kernel(kv_ref, kx_ref, gkv_ref, gik_ref, ckv_ref, kid_ref):
    kv = kv_ref[...]
    ms = jnp.mean(kv * kv, axis=-1, keepdims=True)
    ckv_ref[...] = (kv * lax.rsqrt(ms + NORM_EPS) * gkv_ref[...]).astype(BF16)
    ki = kx_ref[:, 0:IDX_DIM]
    ms2 = jnp.mean(ki * ki, axis=-1, keepdims=True)
    kid_ref[...] = (ki * lax.rsqrt(ms2 + NORM_EPS) * gik_ref[...]).astype(BF16)


def _kvprep(z, gkv, gik, tm=512):
    m = z.shape[0]
    return pl.pallas_call(
        _kvprep_kernel,
        grid=(m // tm,),
        in_specs=[pl.BlockSpec((tm, DS_KV_RANK), lambda i: (i, COL_KV // DS_KV_RANK)),
                  pl.BlockSpec((tm, LANES), lambda i: (i, COL_KX // LANES)),
                  pl.BlockSpec((1, DS_KV_RANK), lambda i: (0, 0)),
                  pl.BlockSpec((1, IDX_DIM), lambda i: (0, 0))],
        out_specs=[pl.BlockSpec((tm, DS_KV_RANK), lambda i: (i, 0)),
                   pl.BlockSpec((tm, IDX_DIM), lambda i: (i, 0))],
        out_shape=[jax.ShapeDtypeStruct((m, DS_KV_RANK), BF16),
                   jax.ShapeDtypeStruct((m, IDX_DIM), BF16)],
        compiler_params=pltpu.CompilerParams(dimension_semantics=("parallel",)),
        name="kvprep",
    )(z, z, gkv, gik)


def _rwkv_kernel(r_ref, k_ref, v_ref, g_ref, wa_ref,
                 mur_ref, muk_ref, muv_ref, muwa_ref,
                 w0_ref, a0_ref, kk_ref, ka_ref, rk_ref, lng_ref, lnb_ref,
                 wup_ref, aup_ref,
                 o_ref,
                 pr_ref, pk_ref, pv_ref, pwa_ref, st_ref):
    C = RW_CHUNK
    N = RW_HEAD

    @pl.when(pl.program_id(1) == 0)
    def _():
        pr_ref[...] = jnp.zeros_like(pr_ref)
        pk_ref[...] = jnp.zeros_like(pk_ref)
        pv_ref[...] = jnp.zeros_like(pv_ref)
        pwa_ref[...] = jnp.zeros_like(pwa_ref)
        st_ref[...] = jnp.zeros_like(st_ref)

    row = lax.broadcasted_iota(I32, (C, 1), 0)

    def shift(ref, prev_ref, mu_ref):
        z = ref[...]
        zp = pltpu.roll(z, 1, 0)
        zp = jnp.where(row == 0, prev_ref[...], zp)
        prev_ref[...] = z[C - 1:C, :]
        return z + mu_ref[...] * (zp - z)

    r = shift(r_ref, pr_ref, mur_ref)
    k = shift(k_ref, pk_ref, muk_ref)
    v = shift(v_ref, pv_ref, muv_ref)
    wa = shift(wa_ref, pwa_ref, muwa_ref)
    wd = wa[:, 0:RW_LORA]
    ad = wa[:, RW_LORA:2 * RW_LORA]

    wl = w0_ref[...] + _dot(jnp.tanh(wd).astype(BF16), wup_ref[...])
    nwl = -wl
    softplus = jnp.maximum(nwl, 0.0) + jnp.log1p(jnp.exp(-jnp.abs(nwl)))
    w_log = -softplus - 0.5
    lw = -jnp.exp(w_log)
    a = _sigmoid(a0_ref[...] + _dot(ad.astype(BF16), aup_ref[...]))
    kk = k * kk_ref[...]
    k2 = k * (1.0 + (a - 1.0) * ka_ref[...])

    ti = lax.broadcasted_iota(I32, (C, C), 0)
    tj = lax.broadcasted_iota(I32, (C, C), 1)
    incl = ti >= tj
    strict = ti > tj
    tri = jnp.where(incl, 1.0, 0.0).astype(F32)
    cum = jnp.dot(tri, lw, preferred_element_type=F32, precision=lax.Precision.HIGHEST)
    p = jnp.exp(cum)
    pinv = jnp.exp(-cum)
    pprev = jnp.exp(cum - lw)
    tot = jnp.dot(jnp.ones((C, C), F32), lw, preferred_element_type=F32,
                  precision=lax.Precision.HIGHEST)
    pend = jnp.exp(tot)

    g = g_ref[...]
    gate = g * _sigmoid(g)

    NP = RW_HEADS // 2

    def pairs(x):
        return jnp.stack([x[:, j * LANES:(j + 1) * LANES] for j in range(NP)], axis=0)

    lane = lax.broadcasted_iota(I32, (1, 1, LANES), 2)
    m_lo = jnp.where(lane < N, 1.0, 0.0).astype(F32)
    m_hi = 1.0 - m_lo
    bi = lax.broadcasted_iota(I32, (LANES, LANES), 0)
    bj = lax.broadcasted_iota(I32, (LANES, LANES), 1)
    same_head = (bi < N) == (bj < N)
    ones_bd = jnp.where(same_head, 1.0, 0.0).astype(BF16)

    def head_sum(x):
        return _dot(x.reshape(NP * C, LANES).astype(BF16), ones_bd).reshape(NP, C, LANES)

    def halves(x):
        return jnp.concatenate([x * m_lo, x * m_hi], axis=1)

    def bmm(x, y):
        return lax.dot_general(x, y, (((2,), (1,)), ((0,), (0,))), preferred_element_type=F32)

    def bmm_nt(x, y):
        return lax.dot_general(x, y, (((2,), (2,)), ((0,), (0,))), preferred_element_type=F32)

    def block_mask(nblk, cmp):
        wi = lax.broadcasted_iota(I32, (C, nblk * C), 0)
        wj = lax.broadcasted_iota(I32, (C, nblk * C), 1) & (C - 1)
        return cmp(wi, wj)

    r_p, k2_p, v_p, a_p = pairs(r), pairs(k2), pairs(v), pairs(a)
    p_p, pinv_p, pprev_p = pairs(p), pairs(pinv), pairs(pprev)
    kk_p = pairs(kk)
    kkn = kk_p / jnp.maximum(jnp.sqrt(head_sum(kk_p * kk_p)), 1e-12)
    at = (-kkn) * pprev_p
    bt = (kkn * a_p) * pinv_p
    kt = k2_p * pinv_p
    rt = r_p * p_p
    pend_p = pairs(pend)
    pend2 = jnp.concatenate([pend_p, pend_p], axis=1)

    nmat = bmm_nt(halves(at).astype(BF16), bt.astype(BF16)).reshape(2 * NP, C, C)
    nmat = jnp.where(strict, nmat, 0.0)
    lhs2 = jnp.concatenate([at, rt], axis=1).astype(BF16)
    rhs4 = jnp.concatenate([halves(kt), halves(bt)], axis=1).astype(BF16)
    gc = bmm_nt(lhs2, rhs4)
    strict2 = block_mask(2, lambda i_, j_: i_ > j_)
    incl4 = block_mask(4, lambda i_, j_: i_ >= j_)
    a_ak = jnp.where(strict2, gc[:, 0:C, 0:2 * C], 0.0)
    a_rkb = jnp.where(incl4, gc[:, C:2 * C, :], 0.0)

    g0 = st_ref[...]
    sg = bmm_nt(lhs2, g0.astype(BF16))
    vm2 = halves(v_p).astype(BF16)
    x0 = sg[:, 0:C] + bmm(a_ak.astype(BF16), vm2)
    xm = halves(x0).reshape(2 * NP, C, LANES)
    pw = nmat.astype(BF16)
    xm = xm + bmm(pw, xm.astype(BF16))
    n = 1
    while 2 * n < C:
        pw = bmm(pw, pw).astype(BF16)
        xm = xm + bmm(pw, xm.astype(BF16))
        n *= 2
    um2 = xm.reshape(NP, 2 * C, LANES)
    um2b = um2.astype(BF16)
    y = sg[:, C:2 * C] + bmm(a_rkb.astype(BF16), jnp.concatenate([vm2, um2b], axis=1))
    u = um2[:, 0:C] + um2[:, C:2 * C]
    uv = jnp.concatenate([u, v_p], axis=1).astype(BF16)
    bkh = (jnp.concatenate([bt, kt], axis=1) * pend2).astype(BF16)
    upd = lax.dot_general(uv, bkh, (((1,), (1,)), ((0,), (0,))), preferred_element_type=F32)
    st_ref[...] = g0 * pend2 + jnp.where(same_head, upd, 0.0)

    inv_n = 1.0 / N
    yc = y - head_sum(y) * inv_n
    var = head_sum(yc * yc) * inv_n
    yn = yc * lax.rsqrt(var + GN_EPS) * lng_ref[...] + lnb_ref[...]
    bonus = head_sum(r_p * k2_p * rk_ref[...]) * v_p
    out = (yn + bonus) * pairs(gate)
    for j in range(NP):
        o_ref[:, j * LANES:(j + 1) * LANES] = out[j].astype(BF16)


def _rwkv(z, B, T, mu, w0, a0, k_k, k_a, r_k, ln_g, ln_b, w_up, a_up):
    C = RW_CHUNK
    nc = T // C
    W = RW_WIDTH
    row = lambda a: a.reshape(1, -1).astype(F32)
    mu_r, mu_k, mu_v = mu[0:W], mu[W:2 * W], mu[2 * W:3 * W]
    mu_wa = mu[3 * W:3 * W + 2 * RW_LORA]
    zspec = lambda col: pl.BlockSpec((C, W), lambda b, c: (b * nc + c, col // W))
    pspec = lambda width: pl.BlockSpec((1, width), lambda b, c: (0, 0))
    npairs = RW_HEADS // 2
    prow = lambda a: a.reshape(npairs, 1, LANES).astype(F32)
    ppspec = pl.BlockSpec((npairs, 1, LANES), lambda b, c: (0, 0, 0))
    wspec = pl.BlockSpec((RW_LORA, W), lambda b, c: (0, 0))
    return pl.pallas_call(
        _rwkv_kernel,
        grid=(B, nc),
        in_specs=[zspec(COL_R), zspec(COL_K), zspec(COL_V), zspec(COL_GRW),
                  pl.BlockSpec((C, LANES), lambda b, c: (b * nc + c, COL_WA // LANES)),
                  pspec(W), pspec(W), pspec(W), pspec(LANES),
                  pspec(W), pspec(W), pspec(W), pspec(W), ppspec, ppspec, ppspec,
                  wspec, wspec],
        out_specs=pl.BlockSpec((C, W), lambda b, c: (b * nc + c, 0)),
        out_shape=jax.ShapeDtypeStruct((B * T, W), BF16),
        scratch_shapes=[pltpu.VMEM((1, W), F32), pltpu.VMEM((1, W), F32), pltpu.VMEM((1, W), F32),
                        pltpu.VMEM((1, LANES), F32),
                        pltpu.VMEM((RW_HEADS // 2, LANES, LANES), F32)],
        compiler_params=pltpu.CompilerParams(dimension_semantics=("parallel", "arbitrary"),
                                             vmem_limit_bytes=VMEM_LIMIT),
        name="rwkv",
    )(z, z, z, z, z,
      row(mu_r), row(mu_k), row(mu_v), row(mu_wa),
      row(w0), row(a0), row(k_k), row(k_a), prow(r_k), prow(ln_g), prow(ln_b),
      w_up.astype(BF16), a_up.astype(BF16))


def _biastab_kernel(rb_ref, o_ref):
    nb = NUM_BUCKETS // 2
    max_exact = nb // 2
    c = lax.broadcasted_iota(I32, (2 * QB, QB), 0)
    r = lax.broadcasted_iota(I32, (2 * QB, QB), 1)
    rel = c - QB - r
    ret = jnp.where(rel > 0, nb, 0)
    n = jnp.abs(rel)
    nf = jnp.maximum(n, 1).astype(F32)
    large = max_exact + (jnp.log(nf / max_exact) / math.log(MAX_DISTANCE / max_exact)
                         * (nb - max_exact)).astype(I32)
    large = jnp.minimum(large, nb - 1) & (NUM_BUCKETS - 1)
    bucket = ret + jnp.where(n < max_exact, n, large)
    for h in range(DS_HEADS):
        far = rb_ref[nb - 1, h]
        acc = jnp.zeros((2 * QB, QB), F32)
        for b in range(NUM_BUCKETS):
            acc = jnp.where(bucket == b, rb_ref[b, h] - far, acc)
        o_ref[:, h * QB:(h + 1) * QB] = acc * LOG2E


def _biastab(rel_bias):
    return pl.pallas_call(
        _biastab_kernel,
        in_specs=[pl.BlockSpec(memory_space=pltpu.SMEM)],
        out_specs=pl.BlockSpec(memory_space=pltpu.VMEM),
        out_shape=jax.ShapeDtypeStruct((2 * QB, DS_HEADS * QB), F32),
        name="biastab",
    )(rel_bias.astype(F32))


def _fold_rows(x, op):
    n = x.shape[0] // SUBLANES
    accs = [x[j * SUBLANES:(j + 1) * SUBLANES] for j in range(min(4, n))]
    for j in range(4, n):
        accs[j % 4] = op(accs[j % 4], x[j * SUBLANES:(j + 1) * SUBLANES])
    while len(accs) > 1:
        accs = [op(accs[j], accs[j + 1]) for j in range(0, len(accs) - 1, 2)] + (
            [accs[-1]] if len(accs) % 2 else [])
    return accs[0]


def _dsa_kernel(ql_ref, kx_ref, gds_ref, kid_ref, ckv_ref, ckvt_ref, qg_ref, wq_ref, wuk_ref, wuvt_ref,
                tab_ref, o_ref,
                sc_ref, msk_ref, lg_ref, acc_ref, qat_ref, qit_ref, w_ref, out_ref, *, topk):
    i = pl.program_id(1)
    q0 = i * QB
    ntile = jnp.right_shift(q0 + (QB + SK - 1), SK.bit_length() - 1)
    R = DS_KV_RANK
    GW = HG * QB

    ql = ql_ref[...]
    ms = jnp.mean(ql * ql, axis=-1, keepdims=True)
    qn = (ql * lax.rsqrt(ms + NORM_EPS) * qg_ref[...]).astype(BF16)
    qt = _dot(qn, wq_ref[...]).T
    for h in range(DS_HEADS):
        qh = qt[h * DS_HEAD:(h + 1) * DS_HEAD, :].astype(BF16)
        qat_ref[:, h * QB:(h + 1) * QB] = (_dot(wuk_ref[h], qh) * (DS_HEAD ** -0.5 * LOG2E)).astype(BF16)
    for pr in range(IDX_HEADS // 2):
        base = DS_WIDTH + 2 * pr * IDX_DIM
        qit_ref[pr] = jnp.concatenate([qt[base:base + IDX_DIM, :],
                                       qt[base + IDX_DIM:base + 2 * IDX_DIM, :]], axis=1).astype(BF16)
    w_ref[...] = kx_ref[...].T[IDX_DIM:IDX_DIM + IDX_HEADS, :] * (IDX_HEADS ** -0.5 * IDX_DIM ** -0.5)

    lanei = lax.broadcasted_iota(I32, (1, QB), 1)
    csh = CHUNK.bit_length() - 1
    limit = jnp.left_shift(jnp.right_shift(q0 + lanei, csh) + 1, csh)
    rowi = lax.broadcasted_iota(I32, (SK, QB), 0)

    def score_tile(kt, carry):
        off = pl.multiple_of(kt * SK, SK)
        kid = kid_ref[pl.ds(off, SK), :]
        s = jnp.zeros((SK, QB), F32)
        for pr in range(IDX_HEADS // 2):
            lg = _dot(kid, qit_ref[pr])
            s = s + w_ref[2 * pr:2 * pr + 1, :] * jnp.maximum(lg[:, 0:QB], 0.0)
            s = s + w_ref[2 * pr + 1:2 * pr + 2, :] * jnp.maximum(lg[:, QB:2 * QB], 0.0)
        adm = (off + rowi) < limit
        sc_ref[pl.ds(off, SK), :] = jnp.where(adm, s, -jnp.inf)
        return carry

    lax.fori_loop(0, ntile, score_tile, 0)

    def key_to_f32(key):
        return pltpu.bitcast(jnp.where(key < 0, key ^ 0x7FFFFFFF, key), F32)

    def count(pred):
        def body(kt, acc):
            off = pl.multiple_of(kt * SK, SK)
            sc = sc_ref[pl.ds(off, SK), :]
            return acc + _fold_rows(jnp.where(pred(sc, off), 1, 0).astype(I32), jnp.add)
        acc = lax.fori_loop(0, ntile, body, jnp.zeros((SUBLANES, QB), I32))
        return jnp.sum(acc, axis=0, keepdims=True)

    def bit_step(it, lo):
        cand = lo + jnp.left_shift(jnp.int32(1), 31 - it)
        cf = key_to_f32(cand)
        cnt = count(lambda sc, off: sc >= cf)
        return jnp.where(cnt >= topk, cand, lo)

    lo = lax.fori_loop(0, 32, bit_step, jnp.full((1, QB), INT_MIN, I32))
    has_thr = lo > KEY_NEG_INF
    thr = key_to_f32(jnp.maximum(lo, KEY_NEG_INF))
    cnt_gt = count(lambda sc, off: sc > thr)
    cnt_eq = count(lambda sc, off: sc == thr)
    nbits = max(1, (sc_ref.shape[0] - 1).bit_length())
    tied = (cnt_gt + cnt_eq > topk) & has_thr

    def search_cut():
        def idx_step(it, m):
            cand = m + jnp.left_shift(jnp.int32(1), nbits - 1 - it)
            cnt = cnt_gt + count(lambda sc, off: (sc == thr) & ((off + rowi) < cand))
            return jnp.where(cnt < topk, cand, m)
        return lax.fori_loop(0, nbits, idx_step, jnp.zeros((1, QB), I32))

    cut = lax.cond(jnp.max(jnp.where(tied, 1, 0)) > 0, search_cut,
                   lambda: jnp.full((1, QB), sc_ref.shape[0], I32))

    def mask_tile(kt, carry):
        off = pl.multiple_of(kt * SK, SK)
        sc = sc_ref[pl.ds(off, SK), :]
        sel = (sc > thr) | ((sc == thr) & ((off + rowi) <= cut) & has_thr)
        msk_ref[pl.ds(off, SK), :] = jnp.where(sel, 0.0, MASK_NEG).astype(F32)
        return carry

    lax.fori_loop(0, ntile, mask_tile, 0)

    def store_logits(rows, s, bias_rows, gc):
        mk = msk_ref[rows, :]
        tmax = []
        for h in range(HG):
            cs = slice(h * QB, (h + 1) * QB)
            t = s[:, cs] + mk
            if bias_rows is not None:
                t = t + tab_ref[bias_rows, gc + h * QB:gc + (h + 1) * QB]
            lg_ref[rows, cs] = t
            tmax.append(_fold_rows(t, jnp.maximum))
        return jnp.concatenate(tmax, axis=1)

    for grp in range(DS_HEADS // HG):
        gc = grp * GW
        qat = qat_ref[:, gc:gc + GW]

        def logits_tile(kt, m):
            off = pl.multiple_of(kt * SK, SK)
            rows = pl.ds(off, SK)
            s = _dot(ckv_ref[rows, :], qat)
            tmax = store_logits(rows, s, None, gc)
            return jnp.where(kt < ntile - 2, jnp.maximum(m, tmax), m)

        mx = lax.fori_loop(0, ntile, logits_tile, jnp.full((SUBLANES, GW), MASK_NEG, F32))

        @pl.when(i > 0)
        def _():
            rows = pl.ds(pl.multiple_of(q0 - QB, QB), 2 * QB)
            store_logits(rows, _dot(ckv_ref[rows, :], qat), slice(0, 2 * QB), gc)

        @pl.when(i == 0)
        def _():
            rows = slice(0, QB)
            store_logits(rows, _dot(ckv_ref[rows, :], qat), slice(QB, 2 * QB), gc)

        def max_tile(kt, m):
            off = pl.multiple_of(kt * SK, SK)
            return jnp.maximum(m, _fold_rows(lg_ref[pl.ds(off, SK), :], jnp.maximum))

        mx = lax.fori_loop(jnp.maximum(ntile - 2, 0), ntile, max_tile, mx)
        mx = jnp.max(mx, axis=0, keepdims=True)

        acc_ref[...] = jnp.zeros_like(acc_ref)

        def pv_tile(kt, c):
            off = pl.multiple_of(kt * SK, SK)
            pr = jnp.exp2(lg_ref[pl.ds(off, SK), :] - mx)
            acc_ref[...] += _dot(ckvt_ref[:, pl.ds(off, SK)], pr.astype(BF16))
            return c

        lax.fori_loop(0, ntile, pv_tile, 0)
        o_lat = acc_ref[0:R, :] * (1.0 / acc_ref[R:R + 1, :])
        for h in range(HG):
            hh = grp * HG + h
            out_ref[hh * DS_HEAD:(hh + 1) * DS_HEAD, :] = _dot(
                wuvt_ref[hh], o_lat[:, h * QB:(h + 1) * QB].astype(BF16))

    g = gds_ref[...]
    o_ref[...] = (out_ref[...].T * (g * _sigmoid(g))).astype(BF16)


def _dsa(z, ckv, kid, B, T, q_norm_g, w_uq, w_uk, w_uv, iw_q, tab):
    nq = T // QB
    topk = min(TOPK_MAX, T // 4)
    R = DS_KV_RANK
    wq = jnp.concatenate([w_uq, iw_q], axis=1).astype(BF16)
    wuk_h = jnp.transpose(w_uk, (1, 0, 2)).astype(BF16)
    wuv_t = jnp.transpose(w_uv, (1, 2, 0)).astype(BF16)
    ckv_t = jnp.concatenate([jnp.swapaxes(ckv.reshape(B, T, R), 1, 2),
                             jnp.ones((B, ONES_ROWS, T), BF16)], axis=1).reshape(B * (R + ONES_ROWS), T)
    const2 = lambda b, i: (0, 0)
    const3 = lambda b, i: (0, 0, 0)
    resident = dict(pipeline_mode=pl.Buffered(1))
    return pl.pallas_call(
        functools.partial(_dsa_kernel, topk=topk),
        grid=(B, nq),
        in_specs=[pl.BlockSpec((QB, DS_Q_RANK), lambda b, i: (b * nq + i, COL_Q // DS_Q_RANK)),
                  pl.BlockSpec((QB, LANES), lambda b, i: (b * nq + i, COL_KX // LANES)),
                  pl.BlockSpec((QB, DS_WIDTH), lambda b, i: (b * nq + i, COL_GDS // DS_WIDTH)),
                  pl.BlockSpec((T, IDX_DIM), lambda b, i: (b, 0), **resident),
                  pl.BlockSpec((T, R), lambda b, i: (b, 0), **resident),
                  pl.BlockSpec((R + ONES_ROWS, T), lambda b, i: (b, 0), **resident),
                  pl.BlockSpec((1, DS_Q_RANK), const2),
                  pl.BlockSpec((DS_Q_RANK, 2 * DS_WIDTH), const2, **resident),
                  pl.BlockSpec((DS_HEADS, R, DS_HEAD), const3, **resident),
                  pl.BlockSpec((DS_HEADS, DS_HEAD, R), const3, **resident),
                  pl.BlockSpec((2 * QB, DS_HEADS * QB), const2, **resident)],
        out_specs=pl.BlockSpec((QB, DS_WIDTH), lambda b, i: (b * nq + i, 0)),
        out_shape=jax.ShapeDtypeStruct((B * T, DS_WIDTH), BF16),
        scratch_shapes=[pltpu.VMEM((T, QB), F32),
                        pltpu.VMEM((T, QB), F32),
                        pltpu.VMEM((T, HG * QB), F32),
                        pltpu.VMEM((R + ONES_ROWS, HG * QB), F32),
                        pltpu.VMEM((R, DS_HEADS * QB), BF16),
                        pltpu.VMEM((IDX_HEADS // 2, IDX_DIM, 2 * QB), BF16),
                        pltpu.VMEM((IDX_HEADS, QB), F32),
                        pltpu.VMEM((DS_WIDTH, QB), F32)],
        compiler_params=pltpu.CompilerParams(dimension_semantics=("parallel", "arbitrary"),
                                             vmem_limit_bytes=VMEM_LIMIT),
        name="dsa",
    )(z, z, z, kid, ckv, ckv_t, q_norm_g.reshape(1, -1).astype(F32), wq, wuk_h, wuv_t, tab)


def _post_kernel(x_ref, a1_ref, a2_ref, p_ref, w1_ref, w2_ref, pw_ref, gw_ref, fg_ref, o_ref):
    h = x_ref[...] + _dot(a1_ref[...], w1_ref[...]) + _dot(a2_ref[...], w2_ref[...])
    e = _dot(p_ref[...].astype(BF16), pw_ref[...])
    gate = _sigmoid(_dot(h.astype(BF16), gw_ref[...]))
    h2 = h + e * gate
    ms = jnp.mean(h2 * h2, axis=-1, keepdims=True)
    o_ref[...] = h2 * lax.rsqrt(ms + NORM_EPS) * fg_ref[...]


def _post(x2, o_rw, o_ds, p2, w_out, ple_w, gate_w, final_g, tm=256):
    m, d = x2.shape
    kh = o_rw.shape[1]
    pd = p2.shape[1]
    resident = dict(pipeline_mode=pl.Buffered(1))
    return pl.pallas_call(
        _post_kernel,
        grid=(m // tm,),
        in_specs=[pl.BlockSpec((tm, d), lambda i: (i, 0)),
                  pl.BlockSpec((tm, kh), lambda i: (i, 0)),
                  pl.BlockSpec((tm, kh), lambda i: (i, 0)),
                  pl.BlockSpec((tm, pd), lambda i: (i, 0)),
                  pl.BlockSpec((kh, d), lambda i: (0, 0), **resident),
                  pl.BlockSpec((kh, d), lambda i: (1, 0), **resident),
                  pl.BlockSpec((pd, d), lambda i: (0, 0), **resident),
                  pl.BlockSpec((d, d), lambda i: (0, 0), **resident),
                  pl.BlockSpec((1, d), lambda i: (0, 0))],
        out_specs=pl.BlockSpec((tm, d), lambda i: (i, 0)),
        out_shape=jax.ShapeDtypeStruct((m, d), F32),
        compiler_params=pltpu.CompilerParams(dimension_semantics=("parallel",),
                                             vmem_limit_bytes=VMEM_LIMIT),
        name="post",
    )(x2, o_rw, o_ds, p2, w_out, w_out, ple_w, gate_w, final_g)


def _regroup_w_in(w):
    s0 = 3 * RW_WIDTH
    s1 = s0 + 2 * RW_LORA
    s2 = s1 + RW_WIDTH
    s3 = s2 + DS_Q_RANK
    s4 = s3 + DS_KV_RANK
    s5 = s4 + IDX_DIM
    s6 = s5 + IDX_HEADS
    w = w.astype(BF16)
    pad = jnp.zeros((w.shape[0], Z_WIDTH - (COL_KX + IDX_DIM + IDX_HEADS)), w.dtype)
    return jnp.concatenate([w[:, 0:s0], w[:, s1:s2], w[:, s6:], w[:, s3:s4], w[:, s2:s3],
                            w[:, s0:s1], w[:, s4:s6], pad], axis=1)


def kernel(x, p, w_in, norm_g, rw_mu, rw_w0, rw_w_up, rw_a0, rw_a_up, rw_k_k, rw_k_a, rw_r_k, rw_ln_g, rw_ln_b, ds_q_norm_g, ds_kv_norm_g, idx_k_norm_g, ds_w_uq, ds_w_uk, ds_w_uv, idx_w_q, rel_bias, w_out, ple_w, ple_gate_w, final_g):
    B, T, D = x.shape
    depth = w_in.shape[0]
    assert depth == 1 and T % SK == 0 and T % RW_CHUNK == 0 and (B * T) % 512 == 0
    h = x.reshape(B * T, D)
    tab = _biastab(rel_bias)
    for i in range(depth):
        w = _regroup_w_in(w_in[i])
        z = _inproj(h, norm_g[i].reshape(1, D), w)
        ckv, kid = _kvprep(z, ds_kv_norm_g[i].reshape(1, -1), idx_k_norm_g[i].reshape(1, -1))
        o_rw = _rwkv(z, B, T, rw_mu[i], rw_w0[i], rw_a0[i], rw_k_k[i], rw_k_a[i],
                     rw_r_k[i].reshape(-1), rw_ln_g[i], rw_ln_b[i], rw_w_up[i], rw_a_up[i])
        o_ds = _dsa(z, ckv, kid, B, T, ds_q_norm_g[i], ds_w_uq[i], ds_w_uk[i], ds_w_uv[i],
                    idx_w_q[i], tab)
        h = _post(h, o_rw, o_ds, p[i].reshape(B * T, -1), w_out[i].astype(BF16),
                  ple_w[i].astype(BF16), ple_gate_w[i].astype(BF16), final_g.reshape(1, D))
    return h.reshape(B, T, D)
```

```python
import functools
import math

import jax
import jax.numpy as jnp
from jax import lax
from jax.experimental import pallas as pl
from jax.experimental.pallas import tpu as pltpu

F32 = jnp.float32
BF16 = jnp.bfloat16
I32 = jnp.int32

RW_WIDTH = 1024
RW_HEAD = 64
RW_HEADS = 16
RW_LORA = 64
DS_WIDTH = 1024
DS_HEAD = 64
DS_HEADS = 16
DS_Q_RANK = 384
DS_KV_RANK = 256
IDX_HEADS = 16
IDX_DIM = 64
TOPK_MAX = 256
CHUNK = 64
NUM_BUCKETS = 32
MAX_DISTANCE = 128
NORM_EPS = 1e-6
GN_EPS = 64e-5

COL_R, COL_K, COL_V, COL_GRW, COL_GDS = 0, 1024, 2048, 3072, 4096
COL_KV = 5120
COL_Q = 5376
COL_WA = 5760
COL_KX = 5888
Z_WIDTH = 6144

LANES = 128
SUBLANES = 8
QB = 128
SK = 512
HG = 8
RW_CHUNK = 64
INT_MIN = -2 ** 31
KEY_NEG_INF = -2139095041
MASK_NEG = -1e30
LOG2E = 1.4426950408889634
ONES_ROWS = 16
VMEM_LIMIT = 52 * 1024 * 1024


def _sigmoid(x):
    return 1.0 / (1.0 + jnp.exp(-x))


def _dot(a, b):
    return jnp.dot(a, b, preferred_element_type=F32)


def _dot_nt(a, b):
    return lax.dot_general(a, b, (((1,), (1,)), ((), ())), preferred_element_type=F32)


def _dot_tn(a, b):
    return lax.dot_general(a, b, (((0,), (0,)), ((), ())), preferred_element_type=F32)


def _inproj_kernel(x_ref, g_ref, w_ref, o_ref, *, tn):
    x = x_ref[...]
    ms = jnp.mean(x * x, axis=-1, keepdims=True)
    xn = (x * lax.rsqrt(ms + NORM_EPS) * g_ref[...]).astype(BF16)
    for j in range(o_ref.shape[1] // tn):
        o_ref[:, j * tn:(j + 1) * tn] = _dot(xn, w_ref[:, j * tn:(j + 1) * tn])


def _inproj(x2, g, w, tm=256, tn=768):
    m, d = x2.shape
    n = w.shape[1]
    return pl.pallas_call(
        functools.partial(_inproj_kernel, tn=tn),
        grid=(m // tm,),
        in_specs=[pl.BlockSpec((tm, d), lambda i: (i, 0)),
                  pl.BlockSpec((1, d), lambda i: (0, 0)),
                  pl.BlockSpec((d, n), lambda i: (0, 0), pipeline_mode=pl.Buffered(1))],
        out_specs=pl.BlockSpec((tm, n), lambda i: (i, 0)),
        out_shape=jax.ShapeDtypeStruct((m, n), F32),
        compiler_params=pltpu.CompilerParams(dimension_semantics=("parallel",),
                                             vmem_limit_bytes=VMEM_LIMIT),
        name="inproj",
    )(x2, g, w)


def _kvprep_kernel(kv_ref, kx_ref, gkv_ref, gik_ref, ckv_ref, kid_ref):
    kv = kv_ref[...]
    ms = jnp.mean(kv * kv, axis=-1, keepdims=True)
    ckv_ref[...] = (kv * lax.rsqrt(ms + NORM_EPS) * gkv_ref[...]).astype(BF16)
    ki = kx_ref[:, 0:IDX_DIM]
    ms2 = jnp.mean(ki * ki, axis=-1, keepdims=True)
    kid_ref[...] = (ki * lax.rsqrt(ms2 + NORM_EPS) * gik_ref[...]).astype(BF16)


def _kvprep(z, gkv, gik, tm=512):
    m = z.shape[0]
    return pl.pallas_call(
        _kvprep_kernel,
        grid=(m // tm,),
        in_specs=[pl.BlockSpec((tm, DS_KV_RANK), lambda i: (i, COL_KV // DS_KV_RANK)),
                  pl.BlockSpec((tm, LANES), lambda i: (i, COL_KX // LANES)),
                  pl.BlockSpec((1, DS_KV_RANK), lambda i: (0, 0)),
                  pl.BlockSpec((1, IDX_DIM), lambda i: (0, 0))],
        out_specs=[pl.BlockSpec((tm, DS_KV_RANK), lambda i: (i, 0)),
                   pl.BlockSpec((tm, IDX_DIM), lambda i: (i, 0))],
        out_shape=[jax.ShapeDtypeStruct((m, DS_KV_RANK), BF16),
                   jax.ShapeDtypeStruct((m, IDX_DIM), BF16)],
        compiler_params=pltpu.CompilerParams(dimension_semantics=("parallel",)),
        name="kvprep",
    )(z, z, gkv, gik)


def _rwkv_kernel(r_ref, k_ref, v_ref, g_ref, wa_ref,
                 mur_ref, muk_ref, muv_ref, muwa_ref,
                 w0_ref, a0_ref, kk_ref, ka_ref, rk_ref, lng_ref, lnb_ref,
                 wup_ref, aup_ref,
                 o_ref,
                 pr_ref, pk_ref, pv_ref, pwa_ref, st_ref, *, nb):
    C = RW_CHUNK
    N = RW_HEAD

    @pl.when(pl.program_id(0) == 0)
    def _():
        pr_ref[...] = jnp.zeros_like(pr_ref)
        pk_ref[...] = jnp.zeros_like(pk_ref)
        pv_ref[...] = jnp.zeros_like(pv_ref)
        pwa_ref[...] = jnp.zeros_like(pwa_ref)
        st_ref[...] = jnp.zeros_like(st_ref)

    row = lax.broadcasted_iota(I32, (nb * C, 1), 0)

    def shift(ref, prev_ref, mu_ref):
        z = ref[...].reshape(nb * C, ref.shape[2])
        zp = pltpu.roll(z, 1, 0)
        for b in range(nb):
            zp = jnp.where(row == b * C, prev_ref[b], zp)
            prev_ref[b] = z[(b + 1) * C - 1:(b + 1) * C, :]
        return z + mu_ref[...] * (zp - z)

    r = shift(r_ref, pr_ref, mur_ref)
    k = shift(k_ref, pk_ref, muk_ref)
    v = shift(v_ref, pv_ref, muv_ref)
    wa = shift(wa_ref, pwa_ref, muwa_ref)
    wd = wa[:, 0:RW_LORA]
    ad = wa[:, RW_LORA:2 * RW_LORA]

    wl = w0_ref[...] + _dot(jnp.tanh(wd).astype(BF16), wup_ref[...])
    nwl = -wl
    softplus = jnp.maximum(nwl, 0.0) + jnp.log1p(jnp.exp(-jnp.abs(nwl)))
    w_log = -softplus - 0.5
    lw = -jnp.exp(w_log)
    a = _sigmoid(a0_ref[...] + _dot(ad.astype(BF16), aup_ref[...]))
    kk = k * kk_ref[...]
    k2 = k * (1.0 + (a - 1.0) * ka_ref[...])

    ti = lax.broadcasted_iota(I32, (C, C), 0)
    tj = lax.broadcasted_iota(I32, (C, C), 1)
    incl = ti >= tj
    strict = ti > tj
    tri = jnp.where(incl, 1.0, 0.0).astype(F32)

    def per_row(mat, x):
        return jnp.concatenate(
            [jnp.dot(mat, x[b * C:(b + 1) * C], preferred_element_type=F32,
                     precision=lax.Precision.HIGHEST) for b in range(nb)], axis=0)

    cum = per_row(tri, lw)
    p = jnp.exp(cum)
    pinv = jnp.exp(-cum)
    pprev = jnp.exp(cum - lw)
    tot = per_row(jnp.ones((C, C), F32), lw)
    pend = jnp.exp(tot)

    g = g_ref[...].reshape(nb * C, RW_WIDTH)
    gate = g * _sigmoid(g)

    HP = RW_HEADS // 2
    NP = nb * HP

    def pairs(x):
        return jnp.stack([x[b * C:(b + 1) * C, j * LANES:(j + 1) * LANES]
                          for b in range(nb) for j in range(HP)], axis=0)

    def per_pair(ref):
        return jnp.concatenate([ref[...]] * nb, axis=0)

    lane = lax.broadcasted_iota(I32, (1, 1, LANES), 2)
    m_lo = jnp.where(lane < N, 1.0, 0.0).astype(F32)
    m_hi = 1.0 - m_lo
    bi = lax.broadcasted_iota(I32, (LANES, LANES), 0)
    bj = lax.broadcasted_iota(I32, (LANES, LANES), 1)
    same_head = (bi < N) == (bj < N)
    ones_bd = jnp.where(same_head, 1.0, 0.0).astype(BF16)

    def head_sum(x):
        return _dot(x.reshape(NP * C, LANES).astype(BF16), ones_bd).reshape(NP, C, LANES)

    def halves(x):
        return jnp.concatenate([x * m_lo, x * m_hi], axis=1)

    def bmm(x, y):
        return lax.dot_general(x, y, (((2,), (1,)), ((0,), (0,))), preferred_element_type=F32)

    def bmm_nt(x, y):
        return lax.dot_general(x, y, (((2,), (2,)), ((0,), (0,))), preferred_element_type=F32)

    def block_mask(nblk, cmp):
        wi = lax.broadcasted_iota(I32, (C, nblk * C), 0)
        wj = lax.broadcasted_iota(I32, (C, nblk * C), 1) & (C - 1)
        return cmp(wi, wj)

    r_p, k2_p, v_p, a_p = pairs(r), pairs(k2), pairs(v), pairs(a)
    p_p, pinv_p, pprev_p = pairs(p), pairs(pinv), pairs(pprev)
    kk_p = pairs(kk)
    kkn = kk_p / jnp.maximum(jnp.sqrt(head_sum(kk_p * kk_p)), 1e-12)
    at = (-kkn) * pprev_p
    bt = (kkn * a_p) * pinv_p
    kt = k2_p * pinv_p
    rt = r_p * p_p
    pend_p = pairs(pend)
    pend2 = jnp.concatenate([pend_p, pend_p], axis=1)

    lhs2 = jnp.concatenate([at, rt], axis=1).astype(BF16)
    rhs4 = jnp.concatenate([halves(kt), halves(bt)], axis=1).astype(BF16)
    gc = bmm_nt(lhs2, rhs4)
    strict2 = block_mask(2, lambda i_, j_: i_ > j_)
    incl4 = block_mask(4, lambda i_, j_: i_ >= j_)
    a_ak = jnp.where(strict2, gc[:, 0:C, 0:2 * C], 0.0)
    nmat = jnp.where(strict2, gc[:, 0:C, 2 * C:4 * C], 0.0)
    a_rkb = jnp.where(incl4, gc[:, C:2 * C, :], 0.0)

    g0 = st_ref[...]
    sg = bmm_nt(lhs2, g0.astype(BF16))
    vm2 = halves(v_p).astype(BF16)
    u = sg[:, 0:C] + bmm(a_ak.astype(BF16), vm2)
    pw = nmat
    u = u + bmm(pw.astype(BF16), halves(u).astype(BF16))
    n = 1
    while 2 * n < C:
        pw = bmm(pw.astype(BF16), halves(pw).astype(BF16))
        u = u + bmm(pw.astype(BF16), halves(u).astype(BF16))
        n *= 2
    um2b = halves(u).astype(BF16)
    y = sg[:, C:2 * C] + bmm(a_rkb.astype(BF16), jnp.concatenate([vm2, um2b], axis=1))
    uv = jnp.concatenate([u, v_p], axis=1).astype(BF16)
    bkh = (jnp.concatenate([bt, kt], axis=1) * pend2).astype(BF16)
    upd = lax.dot_general(uv, bkh, (((1,), (1,)), ((0,), (0,))), preferred_element_type=F32)
    st_ref[...] = g0 * pend2 + jnp.where(same_head, upd, 0.0)

    inv_n = 1.0 / N
    yc = y - head_sum(y) * inv_n
    var = head_sum(yc * yc) * inv_n
    yn = yc * lax.rsqrt(var + GN_EPS) * per_pair(lng_ref) + per_pair(lnb_ref)
    bonus = head_sum(r_p * k2_p * per_pair(rk_ref)) * v_p
    out = (yn + bonus) * pairs(gate)
    for b in range(nb):
        for j in range(HP):
            o_ref[b, :, j * LANES:(j + 1) * LANES] = out[b * HP + j].astype(BF16)


def _rwkv(z, B, T, mu, w0, a0, k_k, k_a, r_k, ln_g, ln_b, w_up, a_up):
    C = RW_CHUNK
    W = RW_WIDTH
    z3 = z.reshape(B, T, z.shape[1])
    row = lambda a: a.reshape(1, -1).astype(F32)
    mu_r, mu_k, mu_v = mu[0:W], mu[W:2 * W], mu[2 * W:3 * W]
    mu_wa = mu[3 * W:3 * W + 2 * RW_LORA]
    zspec = lambda col: pl.BlockSpec((B, C, W), lambda c: (0, c, col // W))
    pspec = lambda width: pl.BlockSpec((1, width), lambda c: (0, 0))
    npairs = RW_HEADS // 2
    prow = lambda a: a.reshape(npairs, 1, LANES).astype(F32)
    ppspec = pl.BlockSpec((npairs, 1, LANES), lambda c: (0, 0, 0))
    wspec = pl.BlockSpec((RW_LORA, W), lambda c: (0, 0))
    out = pl.pallas_call(
        functools.partial(_rwkv_kernel, nb=B),
        grid=(T // C,),
        in_specs=[zspec(COL_R), zspec(COL_K), zspec(COL_V), zspec(COL_GRW),
                  pl.BlockSpec((B, C, LANES), lambda c: (0, c, COL_WA // LANES)),
                  pspec(W), pspec(W), pspec(W), pspec(LANES),
                  pspec(W), pspec(W), pspec(W), pspec(W), ppspec, ppspec, ppspec,
                  wspec, wspec],
        out_specs=pl.BlockSpec((B, C, W), lambda c: (0, c, 0)),
        out_shape=jax.ShapeDtypeStruct((B, T, W), BF16),
        scratch_shapes=[pltpu.VMEM((B, 1, W), F32), pltpu.VMEM((B, 1, W), F32), pltpu.VMEM((B, 1, W), F32),
                        pltpu.VMEM((B, 1, LANES), F32),
                        pltpu.VMEM((B * npairs, LANES, LANES), F32)],
        compiler_params=pltpu.CompilerParams(dimension_semantics=("arbitrary",),
                                             vmem_limit_bytes=VMEM_LIMIT),
        name="rwkv",
    )(z3, z3, z3, z3, z3,
      row(mu_r), row(mu_k), row(mu_v), row(mu_wa),
      row(w0), row(a0), row(k_k), row(k_a), prow(r_k), prow(ln_g), prow(ln_b),
      w_up.astype(BF16), a_up.astype(BF16))
    return out.reshape(B * T, W)


def _biastab_kernel(rb_ref, o_ref):
    nb = NUM_BUCKETS // 2
    max_exact = nb // 2
    c = lax.broadcasted_iota(I32, (2 * QB, QB), 0)
    r = lax.broadcasted_iota(I32, (2 * QB, QB), 1)
    rel = c - QB - r
    ret = jnp.where(rel > 0, nb, 0)
    n = jnp.abs(rel)
    nf = jnp.maximum(n, 1).astype(F32)
    large = max_exact + (jnp.log(nf / max_exact) / math.log(MAX_DISTANCE / max_exact)
                         * (nb - max_exact)).astype(I32)
    large = jnp.minimum(large, nb - 1) & (NUM_BUCKETS - 1)
    bucket = ret + jnp.where(n < max_exact, n, large)
    for h in range(DS_HEADS):
        far = rb_ref[nb - 1, h]
        acc = jnp.zeros((2 * QB, QB), F32)
        for b in range(NUM_BUCKETS):
            acc = jnp.where(bucket == b, rb_ref[b, h] - far, acc)
        o_ref[:, h * QB:(h + 1) * QB] = acc * LOG2E


def _biastab(rel_bias):
    return pl.pallas_call(
        _biastab_kernel,
        in_specs=[pl.BlockSpec(memory_space=pltpu.SMEM)],
        out_specs=pl.BlockSpec(memory_space=pltpu.VMEM),
        out_shape=jax.ShapeDtypeStruct((2 * QB, DS_HEADS * QB), F32),
        name="biastab",
    )(rel_bias.astype(F32))


def _fold_rows(x, op):
    n = x.shape[0] // SUBLANES
    accs = [x[j * SUBLANES:(j + 1) * SUBLANES] for j in range(min(4, n))]
    for j in range(4, n):
        accs[j % 4] = op(accs[j % 4], x[j * SUBLANES:(j + 1) * SUBLANES])
    while len(accs) > 1:
        accs = [op(accs[j], accs[j + 1]) for j in range(0, len(accs) - 1, 2)] + (
            [accs[-1]] if len(accs) % 2 else [])
    return accs[0]


def _dsa_kernel(ql_ref, kx_ref, gds_ref, kid_ref, ckv_ref, ckvt_ref, qg_ref, wq_ref, wuk_ref, wuvt_ref,
                tab_ref, o_ref,
                sc_ref, msk_ref, lg_ref, acc_ref, qat_ref, qit_ref, w_ref, out_ref, *, topk):
    i = pl.program_id(1)
    q0 = i * QB
    ntile = jnp.right_shift(q0 + (QB + SK - 1), SK.bit_length() - 1)
    R = DS_KV_RANK
    GW = HG * QB

    ql = ql_ref[...]
    ms = jnp.mean(ql * ql, axis=-1, keepdims=True)
    qn = (ql * lax.rsqrt(ms + NORM_EPS) * qg_ref[...]).astype(BF16)
    qt = _dot(qn, wq_ref[...]).T
    for h in range(DS_HEADS):
        qh = qt[h * DS_HEAD:(h + 1) * DS_HEAD, :].astype(BF16)
        qat_ref[:, h * QB:(h + 1) * QB] = (_dot(wuk_ref[h], qh) * (DS_HEAD ** -0.5 * LOG2E)).astype(BF16)
    for pr in range(IDX_HEADS // 2):
        base = DS_WIDTH + 2 * pr * IDX_DIM
        qit_ref[pr] = jnp.concatenate([qt[base:base + IDX_DIM, :],
                                       qt[base + IDX_DIM:base + 2 * IDX_DIM, :]], axis=1).astype(BF16)
    w_ref[...] = kx_ref[...].T[IDX_DIM:IDX_DIM + IDX_HEADS, :] * (IDX_HEADS ** -0.5 * IDX_DIM ** -0.5)

    lanei = lax.broadcasted_iota(I32, (1, QB), 1)
    csh = CHUNK.bit_length() - 1
    limit = jnp.left_shift(jnp.right_shift(q0 + lanei, csh) + 1, csh)
    rowi = lax.broadcasted_iota(I32, (SK, QB), 0)

    def score_tile(kt, carry):
        off = pl.multiple_of(kt * SK, SK)
        kid = kid_ref[pl.ds(off, SK), :]
        s = jnp.zeros((SK, QB), F32)
        for pr in range(IDX_HEADS // 2):
            lg = _dot(kid, qit_ref[pr])
            s = s + w_ref[2 * pr:2 * pr + 1, :] * jnp.maximum(lg[:, 0:QB], 0.0)
            s = s + w_ref[2 * pr + 1:2 * pr + 2, :] * jnp.maximum(lg[:, QB:2 * QB], 0.0)
        adm = (off + rowi) < limit
        sc_ref[pl.ds(off, SK), :] = jnp.where(adm, s, -jnp.inf)
        return carry

    lax.fori_loop(0, ntile, score_tile, 0)

    def key_to_f32(key):
        return pltpu.bitcast(jnp.where(key < 0, key ^ 0x7FFFFFFF, key), F32)

    def count(pred):
        def body(kt, acc):
            off = pl.multiple_of(kt * SK, SK)
            sc = sc_ref[pl.ds(off, SK), :]
            return acc + _fold_rows(jnp.where(pred(sc, off), 1, 0).astype(I32), jnp.add)
        acc = lax.fori_loop(0, ntile, body, jnp.zeros((SUBLANES, QB), I32))
        return jnp.sum(acc, axis=0, keepdims=True)

    def bit_step(it, lo):
        cand = lo + jnp.left_shift(jnp.int32(1), 31 - it)
        cf = key_to_f32(cand)
        cnt = count(lambda sc, off: sc >= cf)
        return jnp.where(cnt >= topk, cand, lo)

    lo = lax.fori_loop(0, 32, bit_step, jnp.full((1, QB), INT_MIN, I32))
    has_thr = lo > KEY_NEG_INF
    thr = key_to_f32(jnp.maximum(lo, KEY_NEG_INF))
    cnt_gt = count(lambda sc, off: sc > thr)
    cnt_eq = count(lambda sc, off: sc == thr)
    nbits = max(1, (sc_ref.shape[0] - 1).bit_length())
    tied = (cnt_gt + cnt_eq > topk) & has_thr

    def search_cut():
        def idx_step(it, m):
            cand = m + jnp.left_shift(jnp.int32(1), nbits - 1 - it)
            cnt = cnt_gt + count(lambda sc, off: (sc == thr) & ((off + rowi) < cand))
            return jnp.where(cnt < topk, cand, m)
        return lax.fori_loop(0, nbits, idx_step, jnp.zeros((1, QB), I32))

    cut = lax.cond(jnp.max(jnp.where(tied, 1, 0)) > 0, search_cut,
                   lambda: jnp.full((1, QB), sc_ref.shape[0], I32))

    def mask_tile(kt, carry):
        off = pl.multiple_of(kt * SK, SK)
        sc = sc_ref[pl.ds(off, SK), :]
        sel = (sc > thr) | ((sc == thr) & ((off + rowi) <= cut) & has_thr)
        msk_ref[pl.ds(off, SK), :] = jnp.where(sel, 0.0, MASK_NEG).astype(F32)
        return carry

    lax.fori_loop(0, ntile, mask_tile, 0)

    def store_logits(rows, s, bias_rows, gc):
        mk = msk_ref[rows, :]
        tmax = []
        for h in range(HG):
            cs = slice(h * QB, (h + 1) * QB)
            t = s[:, cs] + mk
            if bias_rows is not None:
                t = t + tab_ref[bias_rows, gc + h * QB:gc + (h + 1) * QB]
            lg_ref[rows, cs] = t
            tmax.append(_fold_rows(t, jnp.maximum))
        return jnp.concatenate(tmax, axis=1)

    for grp in range(DS_HEADS // HG):
        gc = grp * GW
        qat = qat_ref[:, gc:gc + GW]

        def logits_tile(kt, m):
            off = pl.multiple_of(kt * SK, SK)
            rows = pl.ds(off, SK)
            s = _dot(ckv_ref[rows, :], qat)
            tmax = store_logits(rows, s, None, gc)
            return jnp.where(kt < ntile - 2, jnp.maximum(m, tmax), m)

        mx = lax.fori_loop(0, ntile, logits_tile, jnp.full((SUBLANES, GW), MASK_NEG, F32))

        @pl.when(i > 0)
        def _():
            rows = pl.ds(pl.multiple_of(q0 - QB, QB), 2 * QB)
            store_logits(rows, _dot(ckv_ref[rows, :], qat), slice(0, 2 * QB), gc)

        @pl.when(i == 0)
        def _():
            rows = slice(0, QB)
            store_logits(rows, _dot(ckv_ref[rows, :], qat), slice(QB, 2 * QB), gc)

        def max_tile(kt, m):
            off = pl.multiple_of(kt * SK, SK)
            return jnp.maximum(m, _fold_rows(lg_ref[pl.ds(off, SK), :], jnp.maximum))

        mx = lax.fori_loop(jnp.maximum(ntile - 2, 0), ntile, max_tile, mx)
        mx = jnp.max(mx, axis=0, keepdims=True)

        acc_ref[...] = jnp.zeros_like(acc_ref)

        def pv_tile(kt, c):
            off = pl.multiple_of(kt * SK, SK)
            pr = jnp.exp2(lg_ref[pl.ds(off, SK), :] - mx)
            acc_ref[...] += _dot(ckvt_ref[:, pl.ds(off, SK)], pr.astype(BF16))
            return c

        lax.fori_loop(0, ntile, pv_tile, 0)
        o_lat = acc_ref[0:R, :] * (1.0 / acc_ref[R:R + 1, :])
        for h in range(HG):
            hh = grp * HG + h
            out_ref[hh * DS_HEAD:(hh + 1) * DS_HEAD, :] = _dot(
                wuvt_ref[hh], o_lat[:, h * QB:(h + 1) * QB].astype(BF16))

    g = gds_ref[...]
    o_ref[...] = (out_ref[...].T * (g * _sigmoid(g))).astype(BF16)


def _dsa(z, ckv, kid, B, T, q_norm_g, w_uq, w_uk, w_uv, iw_q, tab):
    nq = T // QB
    topk = min(TOPK_MAX, T // 4)
    R = DS_KV_RANK
    wq = jnp.concatenate([w_uq, iw_q], axis=1).astype(BF16)
    wuk_h = jnp.transpose(w_uk, (1, 0, 2)).astype(BF16)
    wuv_t = jnp.transpose(w_uv, (1, 2, 0)).astype(BF16)
    ckv_t = jnp.concatenate([jnp.swapaxes(ckv.reshape(B, T, R), 1, 2),
                             jnp.ones((B, ONES_ROWS, T), BF16)], axis=1).reshape(B * (R + ONES_ROWS), T)
    const2 = lambda b, i: (0, 0)
    const3 = lambda b, i: (0, 0, 0)
    resident = dict(pipeline_mode=pl.Buffered(1))
    return pl.pallas_call(
        functools.partial(_dsa_kernel, topk=topk),
        grid=(B, nq),
        in_specs=[pl.BlockSpec((QB, DS_Q_RANK), lambda b, i: (b * nq + i, COL_Q // DS_Q_RANK)),
                  pl.BlockSpec((QB, LANES), lambda b, i: (b * nq + i, COL_KX // LANES)),
                  pl.BlockSpec((QB, DS_WIDTH), lambda b, i: (b * nq + i, COL_GDS // DS_WIDTH)),
                  pl.BlockSpec((T, IDX_DIM), lambda b, i: (b, 0), **resident),
                  pl.BlockSpec((T, R), lambda b, i: (b, 0), **resident),
                  pl.BlockSpec((R + ONES_ROWS, T), lambda b, i: (b, 0), **resident),
                  pl.BlockSpec((1, DS_Q_RANK), const2),
                  pl.BlockSpec((DS_Q_RANK, 2 * DS_WIDTH), const2, **resident),
                  pl.BlockSpec((DS_HEADS, R, DS_HEAD), const3, **resident),
                  pl.BlockSpec((DS_HEADS, DS_HEAD, R), const3, **resident),
                  pl.BlockSpec((2 * QB, DS_HEADS * QB), const2, **resident)],
        out_specs=pl.BlockSpec((QB, DS_WIDTH), lambda b, i: (b * nq + i, 0)),
        out_shape=jax.ShapeDtypeStruct((B * T, DS_WIDTH), BF16),
        scratch_shapes=[pltpu.VMEM((T, QB), F32),
                        pltpu.VMEM((T, QB), F32),
                        pltpu.VMEM((T, HG * QB), F32),
                        pltpu.VMEM((R + ONES_ROWS, HG * QB), F32),
                        pltpu.VMEM((R, DS_HEADS * QB), BF16),
                        pltpu.VMEM((IDX_HEADS // 2, IDX_DIM, 2 * QB), BF16),
                        pltpu.VMEM((IDX_HEADS, QB), F32),
                        pltpu.VMEM((DS_WIDTH, QB), F32)],
        compiler_params=pltpu.CompilerParams(dimension_semantics=("parallel", "arbitrary"),
                                             vmem_limit_bytes=VMEM_LIMIT),
        name="dsa",
    )(z, z, z, kid, ckv, ckv_t, q_norm_g.reshape(1, -1).astype(F32), wq, wuk_h, wuv_t, tab)


def _post_kernel(x_ref, a1_ref, a2_ref, p_ref, w1_ref, w2_ref, pw_ref, gw_ref, fg_ref, o_ref):
    h = x_ref[...] + _dot(a1_ref[...], w1_ref[...]) + _dot(a2_ref[...], w2_ref[...])
    e = _dot(p_ref[...].astype(BF16), pw_ref[...])
    gate = _sigmoid(_dot(h.astype(BF16), gw_ref[...]))
    h2 = h + e * gate
    ms = jnp.mean(h2 * h2, axis=-1, keepdims=True)
    o_ref[...] = h2 * lax.rsqrt(ms + NORM_EPS) * fg_ref[...]


def _post(x2, o_rw, o_ds, p2, w_out, ple_w, gate_w, final_g, tm=256):
    m, d = x2.shape
    kh = o_rw.shape[1]
    pd = p2.shape[1]
    resident = dict(pipeline_mode=pl.Buffered(1))
    return pl.pallas_call(
        _post_kernel,
        grid=(m // tm,),
        in_specs=[pl.BlockSpec((tm, d), lambda i: (i, 0)),
                  pl.BlockSpec((tm, kh), lambda i: (i, 0)),
                  pl.BlockSpec((tm, kh), lambda i: (i, 0)),
                  pl.BlockSpec((tm, pd), lambda i: (i, 0)),
                  pl.BlockSpec((kh, d), lambda i: (0, 0), **resident),
                  pl.BlockSpec((kh, d), lambda i: (1, 0), **resident),
                  pl.BlockSpec((pd, d), lambda i: (0, 0), **resident),
                  pl.BlockSpec((d, d), lambda i: (0, 0), **resident),
                  pl.BlockSpec((1, d), lambda i: (0, 0))],
        out_specs=pl.BlockSpec((tm, d), lambda i: (i, 0)),
        out_shape=jax.ShapeDtypeStruct((m, d), F32),
        compiler_params=pltpu.CompilerParams(dimension_semantics=("parallel",),
                                             vmem_limit_bytes=VMEM_LIMIT),
        name="post",
    )(x2, o_rw, o_ds, p2, w_out, w_out, ple_w, gate_w, final_g)


def _regroup_w_in(w):
    s0 = 3 * RW_WIDTH
    s1 = s0 + 2 * RW_LORA
    s2 = s1 + RW_WIDTH
    s3 = s2 + DS_Q_RANK
    s4 = s3 + DS_KV_RANK
    s5 = s4 + IDX_DIM
    s6 = s5 + IDX_HEADS
    w = w.astype(BF16)
    pad = jnp.zeros((w.shape[0], Z_WIDTH - (COL_KX + IDX_DIM + IDX_HEADS)), w.dtype)
    return jnp.concatenate([w[:, 0:s0], w[:, s1:s2], w[:, s6:], w[:, s3:s4], w[:, s2:s3],
                            w[:, s0:s1], w[:, s4:s6], pad], axis=1)


def kernel(x, p, w_in, norm_g, rw_mu, rw_w0, rw_w_up, rw_a0, rw_a_up, rw_k_k, rw_k_a, rw_r_k, rw_ln_g, rw_ln_b, ds_q_norm_g, ds_kv_norm_g, idx_k_norm_g, ds_w_uq, ds_w_uk, ds_w_uv, idx_w_q, rel_bias, w_out, ple_w, ple_gate_w, final_g):
    B, T, D = x.shape
    depth = w_in.shape[0]
    assert depth == 1 and T % SK == 0 and T % RW_CHUNK == 0 and (B * T) % 512 == 0
    h = x.reshape(B * T, D)
    tab = _biastab(rel_bias)
    for i in range(depth):
        w = _regroup_w_in(w_in[i])
        z = _inproj(h, norm_g[i].reshape(1, D), w)
        ckv, kid = _kvprep(z, ds_kv_norm_g[i].reshape(1, -1), idx_k_norm_g[i].reshape(1, -1))
        o_rw = _rwkv(z, B, T, rw_mu[i], rw_w0[i], rw_a0[i], rw_k_k[i], rw_k_a[i],
                     rw_r_k[i].reshape(-1), rw_ln_g[i], rw_ln_b[i], rw_w_up[i], rw_a_up[i])
        o_ds = _dsa(z, ckv, kid, B, T, ds_q_norm_g[i], ds_w_uq[i], ds_w_uk[i], ds_w_uv[i],
                    idx_w_q[i], tab)
        h = _post(h, o_rw, o_ds, p[i].reshape(B * T, -1), w_out[i].astype(BF16),
                  ple_w[i].astype(BF16), ple_gate_w[i].astype(BF16), final_g.reshape(1, D))
    return h.reshape(B, T, D)
```

```python
import functools
import math

import jax
import jax.numpy as jnp
from jax import lax
from jax.experimental import pallas as pl
from jax.experimental.pallas import tpu as pltpu

F32 = jnp.float32
BF16 = jnp.bfloat16
I32 = jnp.int32

RW_WIDTH = 1024
RW_HEAD = 64
RW_HEADS = 16
RW_LORA = 64
DS_WIDTH = 1024
DS_HEAD = 64
DS_HEADS = 16
DS_Q_RANK = 384
DS_KV_RANK = 256
IDX_HEADS = 16
IDX_DIM = 64
TOPK_MAX = 256
CHUNK = 64
NUM_BUCKETS = 32
MAX_DISTANCE = 128
NORM_EPS = 1e-6
GN_EPS = 64e-5

COL_R, COL_K, COL_V, COL_GRW, COL_GDS = 0, 1024, 2048, 3072, 4096
COL_KV = 5120
COL_Q = 5376
COL_WA = 5760
COL_KX = 5888
Z_WIDTH = 6144

LANES = 128
SUBLANES = 8
QB = 128
SK = 512
RW_CHUNK = 64
INT_MIN = -2 ** 31
KEY_NEG_INF = -2139095041
MASK_NEG = -1e30
LOG2E = 1.4426950408889634
ONES_ROWS = 16
VMEM_LIMIT = 52 * 1024 * 1024


def _sigmoid(x):
    return 1.0 / (1.0 + jnp.exp(-x))


def _dot(a, b):
    return jnp.dot(a, b, preferred_element_type=F32)


def _dot_nt(a, b):
    return lax.dot_general(a, b, (((1,), (1,)), ((), ())), preferred_element_type=F32)


def _dot_tn(a, b):
    return lax.dot_general(a, b, (((0,), (0,)), ((), ())), preferred_element_type=F32)


def _inproj_kernel(x_ref, g_ref, w_ref, o_ref, *, tn):
    x = x_ref[...]
    ms = jnp.mean(x * x, axis=-1, keepdims=True)
    xn = (x * lax.rsqrt(ms + NORM_EPS) * g_ref[...]).astype(BF16)
    for j in range(o_ref.shape[1] // tn):
        o_ref[:, j * tn:(j + 1) * tn] = _dot(xn, w_ref[:, j * tn:(j + 1) * tn])


def _inproj(x2, g, w, tm=256, tn=768):
    m, d = x2.shape
    n = w.shape[1]
    return pl.pallas_call(
        functools.partial(_inproj_kernel, tn=tn),
        grid=(m // tm,),
        in_specs=[pl.BlockSpec((tm, d), lambda i: (i, 0)),
                  pl.BlockSpec((1, d), lambda i: (0, 0)),
                  pl.BlockSpec((d, n), lambda i: (0, 0), pipeline_mode=pl.Buffered(1))],
        out_specs=pl.BlockSpec((tm, n), lambda i: (i, 0)),
        out_shape=jax.ShapeDtypeStruct((m, n), F32),
        compiler_params=pltpu.CompilerParams(dimension_semantics=("parallel",),
                                             vmem_limit_bytes=VMEM_LIMIT),
        name="inproj",
    )(x2, g, w)


def _kvprep_kernel(kv_ref, kx_ref, gkv_ref, gik_ref, ckv_ref, kid_ref):
    kv = kv_ref[...]
    ms = jnp.mean(kv * kv, axis=-1, keepdims=True)
    ckv_ref[...] = (kv * lax.rsqrt(ms + NORM_EPS) * gkv_ref[...]).astype(BF16)
    ki = kx_ref[:, 0:IDX_DIM]
    ms2 = jnp.mean(ki * ki, axis=-1, keepdims=True)
    kid_ref[...] = (ki * lax.rsqrt(ms2 + NORM_EPS) * gik_ref[...]).astype(BF16)


def _kvprep(z, gkv, gik, tm=512):
    m = z.shape[0]
    return pl.pallas_call(
        _kvprep_kernel,
        grid=(m // tm,),
        in_specs=[pl.BlockSpec((tm, DS_KV_RANK), lambda i: (i, COL_KV // DS_KV_RANK)),
                  pl.BlockSpec((tm, LANES), lambda i: (i, COL_KX // LANES)),
                  pl.BlockSpec((1, DS_KV_RANK), lambda i: (0, 0)),
                  pl.BlockSpec((1, IDX_DIM), lambda i: (0, 0))],
        out_specs=[pl.BlockSpec((tm, DS_KV_RANK), lambda i: (i, 0)),
                   pl.BlockSpec((tm, IDX_DIM), lambda i: (i, 0))],
        out_shape=[jax.ShapeDtypeStruct((m, DS_KV_RANK), BF16),
                   jax.ShapeDtypeStruct((m, IDX_DIM), BF16)],
        compiler_params=pltpu.CompilerParams(dimension_semantics=("parallel",)),
        name="kvprep",
    )(z, z, gkv, gik)


def _rwkv_kernel(r_ref, k_ref, v_ref, g_ref, wa_ref,
                 mur_ref, muk_ref, muv_ref, muwa_ref,
                 w0_ref, a0_ref, kk_ref, ka_ref, rk_ref, lng_ref, lnb_ref,
                 wup_ref, aup_ref,
                 o_ref,
                 pr_ref, pk_ref, pv_ref, pwa_ref, st_ref, *, nb):
    C = RW_CHUNK
    N = RW_HEAD

    @pl.when(pl.program_id(0) == 0)
    def _():
        pr_ref[...] = jnp.zeros_like(pr_ref)
        pk_ref[...] = jnp.zeros_like(pk_ref)
        pv_ref[...] = jnp.zeros_like(pv_ref)
        pwa_ref[...] = jnp.zeros_like(pwa_ref)
        st_ref[...] = jnp.zeros_like(st_ref)

    row = lax.broadcasted_iota(I32, (nb * C, 1), 0)

    def shift(ref, prev_ref, mu_ref):
        z = ref[...].reshape(nb * C, ref.shape[2])
        zp = pltpu.roll(z, 1, 0)
        for b in range(nb):
            zp = jnp.where(row == b * C, prev_ref[b], zp)
            prev_ref[b] = z[(b + 1) * C - 1:(b + 1) * C, :]
        return z + mu_ref[...] * (zp - z)

    r = shift(r_ref, pr_ref, mur_ref)
    k = shift(k_ref, pk_ref, muk_ref)
    v = shift(v_ref, pv_ref, muv_ref)
    wa = shift(wa_ref, pwa_ref, muwa_ref)
    wd = wa[:, 0:RW_LORA]
    ad = wa[:, RW_LORA:2 * RW_LORA]

    wl = w0_ref[...] + _dot(jnp.tanh(wd).astype(BF16), wup_ref[...])
    nwl = -wl
    softplus = jnp.maximum(nwl, 0.0) + jnp.log1p(jnp.exp(-jnp.abs(nwl)))
    w_log = -softplus - 0.5
    lw = -jnp.exp(w_log)
    a = _sigmoid(a0_ref[...] + _dot(ad.astype(BF16), aup_ref[...]))
    kk = k * kk_ref[...]
    k2 = k * (1.0 + (a - 1.0) * ka_ref[...])

    ti = lax.broadcasted_iota(I32, (C, C), 0)
    tj = lax.broadcasted_iota(I32, (C, C), 1)
    incl = ti >= tj
    strict = ti > tj
    tri = jnp.where(incl, 1.0, 0.0).astype(F32)

    def per_row(mat, x):
        return jnp.concatenate(
            [jnp.dot(mat, x[b * C:(b + 1) * C], preferred_element_type=F32,
                     precision=lax.Precision.HIGHEST) for b in range(nb)], axis=0)

    cum = per_row(tri, lw)
    p = jnp.exp(cum)
    pinv = jnp.exp(-cum)
    pprev = jnp.exp(cum - lw)
    tot = per_row(jnp.ones((C, C), F32), lw)
    pend = jnp.exp(tot)

    g = g_ref[...].reshape(nb * C, RW_WIDTH)
    gate = g * _sigmoid(g)

    HP = RW_HEADS // 2
    NP = nb * HP

    def pairs(x):
        return jnp.stack([x[b * C:(b + 1) * C, j * LANES:(j + 1) * LANES]
                          for b in range(nb) for j in range(HP)], axis=0)

    def per_pair(ref):
        return jnp.concatenate([ref[...]] * nb, axis=0)

    lane = lax.broadcasted_iota(I32, (1, 1, LANES), 2)
    m_lo = jnp.where(lane < N, 1.0, 0.0).astype(F32)
    m_hi = 1.0 - m_lo
    bi = lax.broadcasted_iota(I32, (LANES, LANES), 0)
    bj = lax.broadcasted_iota(I32, (LANES, LANES), 1)
    same_head = (bi < N) == (bj < N)
    ones_bd = jnp.where(same_head, 1.0, 0.0).astype(BF16)

    def head_sum(x):
        return _dot(x.reshape(NP * C, LANES).astype(BF16), ones_bd).reshape(NP, C, LANES)

    def halves(x):
        return jnp.concatenate([x * m_lo, x * m_hi], axis=1)

    def bmm(x, y):
        return lax.dot_general(x, y, (((2,), (1,)), ((0,), (0,))), preferred_element_type=F32)

    def bmm_nt(x, y):
        return lax.dot_general(x, y, (((2,), (2,)), ((0,), (0,))), preferred_element_type=F32)

    def block_mask(nblk, cmp):
        wi = lax.broadcasted_iota(I32, (C, nblk * C), 0)
        wj = lax.broadcasted_iota(I32, (C, nblk * C), 1) & (C - 1)
        return cmp(wi, wj)

    r_p, k2_p, v_p, a_p = pairs(r), pairs(k2), pairs(v), pairs(a)
    p_p, pinv_p, pprev_p = pairs(p), pairs(pinv), pairs(pprev)
    kk_p = pairs(kk)
    kkn = kk_p / jnp.maximum(jnp.sqrt(head_sum(kk_p * kk_p)), 1e-12)
    at = (-kkn) * pprev_p
    bt = (kkn * a_p) * pinv_p
    kt = k2_p * pinv_p
    rt = r_p * p_p
    pend_p = pairs(pend)
    pend2 = jnp.concatenate([pend_p, pend_p], axis=1)

    lhs2 = jnp.concatenate([at, rt], axis=1).astype(BF16)
    rhs4 = jnp.concatenate([halves(kt), halves(bt)], axis=1).astype(BF16)
    gc = bmm_nt(lhs2, rhs4)
    strict2 = block_mask(2, lambda i_, j_: i_ > j_)
    incl4 = block_mask(4, lambda i_, j_: i_ >= j_)
    a_ak = jnp.where(strict2, gc[:, 0:C, 0:2 * C], 0.0)
    nmat = jnp.where(strict2, gc[:, 0:C, 2 * C:4 * C], 0.0)
    a_rkb = jnp.where(incl4, gc[:, C:2 * C, :], 0.0)

    g0 = st_ref[...]
    sg = bmm_nt(lhs2, g0.astype(BF16))
    vm2 = halves(v_p).astype(BF16)
    u = sg[:, 0:C] + bmm(a_ak.astype(BF16), vm2)
    pw = nmat
    u = u + bmm(pw.astype(BF16), halves(u).astype(BF16))
    n = 1
    while 2 * n < C:
        pw = bmm(pw.astype(BF16), halves(pw).astype(BF16))
        u = u + bmm(pw.astype(BF16), halves(u).astype(BF16))
        n *= 2
    um2b = halves(u).astype(BF16)
    y = sg[:, C:2 * C] + bmm(a_rkb.astype(BF16), jnp.concatenate([vm2, um2b], axis=1))
    uv = jnp.concatenate([u, v_p], axis=1).astype(BF16)
    bkh = (jnp.concatenate([bt, kt], axis=1) * pend2).astype(BF16)
    upd = lax.dot_general(uv, bkh, (((1,), (1,)), ((0,), (0,))), preferred_element_type=F32)
    st_ref[...] = g0 * pend2 + jnp.where(same_head, upd, 0.0)

    inv_n = 1.0 / N
    yc = y - head_sum(y) * inv_n
    var = head_sum(yc * yc) * inv_n
    yn = yc * lax.rsqrt(var + GN_EPS) * per_pair(lng_ref) + per_pair(lnb_ref)
    bonus = head_sum(r_p * k2_p * per_pair(rk_ref)) * v_p
    out = (yn + bonus) * pairs(gate)
    for b in range(nb):
        for j in range(HP):
            o_ref[b, :, j * LANES:(j + 1) * LANES] = out[b * HP + j].astype(BF16)


def _rwkv(z, B, T, mu, w0, a0, k_k, k_a, r_k, ln_g, ln_b, w_up, a_up):
    C = RW_CHUNK
    W = RW_WIDTH
    z3 = z.reshape(B, T, z.shape[1])
    row = lambda a: a.reshape(1, -1).astype(F32)
    mu_r, mu_k, mu_v = mu[0:W], mu[W:2 * W], mu[2 * W:3 * W]
    mu_wa = mu[3 * W:3 * W + 2 * RW_LORA]
    zspec = lambda col: pl.BlockSpec((B, C, W), lambda c: (0, c, col // W))
    pspec = lambda width: pl.BlockSpec((1, width), lambda c: (0, 0))
    npairs = RW_HEADS // 2
    prow = lambda a: a.reshape(npairs, 1, LANES).astype(F32)
    ppspec = pl.BlockSpec((npairs, 1, LANES), lambda c: (0, 0, 0))
    wspec = pl.BlockSpec((RW_LORA, W), lambda c: (0, 0))
    out = pl.pallas_call(
        functools.partial(_rwkv_kernel, nb=B),
        grid=(T // C,),
        in_specs=[zspec(COL_R), zspec(COL_K), zspec(COL_V), zspec(COL_GRW),
                  pl.BlockSpec((B, C, LANES), lambda c: (0, c, COL_WA // LANES)),
                  pspec(W), pspec(W), pspec(W), pspec(LANES),
                  pspec(W), pspec(W), pspec(W), pspec(W), ppspec, ppspec, ppspec,
                  wspec, wspec],
        out_specs=pl.BlockSpec((B, C, W), lambda c: (0, c, 0)),
        out_shape=jax.ShapeDtypeStruct((B, T, W), BF16),
        scratch_shapes=[pltpu.VMEM((B, 1, W), F32), pltpu.VMEM((B, 1, W), F32), pltpu.VMEM((B, 1, W), F32),
                        pltpu.VMEM((B, 1, LANES), F32),
                        pltpu.VMEM((B * npairs, LANES, LANES), F32)],
        compiler_params=pltpu.CompilerParams(dimension_semantics=("arbitrary",),
                                             vmem_limit_bytes=VMEM_LIMIT),
        name="rwkv",
    )(z3, z3, z3, z3, z3,
      row(mu_r), row(mu_k), row(mu_v), row(mu_wa),
      row(w0), row(a0), row(k_k), row(k_a), prow(r_k), prow(ln_g), prow(ln_b),
      w_up.astype(BF16), a_up.astype(BF16))
    return out.reshape(B * T, W)


def _biastab_kernel(rb_ref, o_ref):
    nb = NUM_BUCKETS // 2
    max_exact = nb // 2
    c = lax.broadcasted_iota(I32, (2 * QB, QB), 0)
    r = lax.broadcasted_iota(I32, (2 * QB, QB), 1)
    rel = c - QB - r
    ret = jnp.where(rel > 0, nb, 0)
    n = jnp.abs(rel)
    nf = jnp.maximum(n, 1).astype(F32)
    large = max_exact + (jnp.log(nf / max_exact) / math.log(MAX_DISTANCE / max_exact)
                         * (nb - max_exact)).astype(I32)
    large = jnp.minimum(large, nb - 1) & (NUM_BUCKETS - 1)
    bucket = ret + jnp.where(n < max_exact, n, large)
    for h in range(DS_HEADS):
        far = rb_ref[nb - 1, h]
        acc = jnp.zeros((2 * QB, QB), F32)
        for b in range(NUM_BUCKETS):
            acc = jnp.where(bucket == b, rb_ref[b, h] - far, acc)
        o_ref[:, h * QB:(h + 1) * QB] = acc * LOG2E


def _biastab(rel_bias):
    return pl.pallas_call(
        _biastab_kernel,
        in_specs=[pl.BlockSpec(memory_space=pltpu.SMEM)],
        out_specs=pl.BlockSpec(memory_space=pltpu.VMEM),
        out_shape=jax.ShapeDtypeStruct((2 * QB, DS_HEADS * QB), F32),
        name="biastab",
    )(rel_bias.astype(F32))


def _fold_rows(x, op):
    n = x.shape[0] // SUBLANES
    accs = [x[j * SUBLANES:(j + 1) * SUBLANES] for j in range(min(4, n))]
    for j in range(4, n):
        accs[j % 4] = op(accs[j % 4], x[j * SUBLANES:(j + 1) * SUBLANES])
    while len(accs) > 1:
        accs = [op(accs[j], accs[j + 1]) for j in range(0, len(accs) - 1, 2)] + (
            [accs[-1]] if len(accs) % 2 else [])
    return accs[0]


def _dsa_kernel(ql_ref, kx_ref, gds_ref, kid_ref, ckv_ref, ckvt_ref, qg_ref, wq_ref, wuk_ref, wuvt_ref,
                tab_ref, o_ref,
                sc_ref, msk_ref, lgt_ref, acc_ref, m_ref, qat_ref, qit_ref, w_ref, out_ref, *, topk):
    i = pl.program_id(1)
    q0 = i * QB
    ntile = jnp.right_shift(q0 + (QB + SK - 1), SK.bit_length() - 1)
    R = DS_KV_RANK
    GW = DS_HEADS * QB

    ql = ql_ref[...]
    ms = jnp.mean(ql * ql, axis=-1, keepdims=True)
    qn = (ql * lax.rsqrt(ms + NORM_EPS) * qg_ref[...]).astype(BF16)
    qt = _dot(qn, wq_ref[...]).T
    for h in range(DS_HEADS):
        qh = qt[h * DS_HEAD:(h + 1) * DS_HEAD, :].astype(BF16)
        qat_ref[:, h * QB:(h + 1) * QB] = (_dot(wuk_ref[h], qh) * (DS_HEAD ** -0.5 * LOG2E)).astype(BF16)
    for pr in range(IDX_HEADS // 2):
        base = DS_WIDTH + 2 * pr * IDX_DIM
        qit_ref[pr] = jnp.concatenate([qt[base:base + IDX_DIM, :],
                                       qt[base + IDX_DIM:base + 2 * IDX_DIM, :]], axis=1).astype(BF16)
    w_ref[...] = kx_ref[...].T[IDX_DIM:IDX_DIM + IDX_HEADS, :] * (IDX_HEADS ** -0.5 * IDX_DIM ** -0.5)

    lanei = lax.broadcasted_iota(I32, (1, QB), 1)
    csh = CHUNK.bit_length() - 1
    limit = jnp.left_shift(jnp.right_shift(q0 + lanei, csh) + 1, csh)
    rowi = lax.broadcasted_iota(I32, (SK, QB), 0)

    def score_tile(kt, carry):
        off = pl.multiple_of(kt * SK, SK)
        kid = kid_ref[pl.ds(off, SK), :]
        s = jnp.zeros((SK, QB), F32)
        for pr in range(IDX_HEADS // 2):
            lg = _dot(kid, qit_ref[pr])
            s = s + w_ref[2 * pr:2 * pr + 1, :] * jnp.maximum(lg[:, 0:QB], 0.0)
            s = s + w_ref[2 * pr + 1:2 * pr + 2, :] * jnp.maximum(lg[:, QB:2 * QB], 0.0)
        adm = (off + rowi) < limit
        sc_ref[pl.ds(off, SK), :] = jnp.where(adm, s, -jnp.inf)
        return carry

    lax.fori_loop(0, ntile, score_tile, 0)

    def key_to_f32(key):
        return pltpu.bitcast(jnp.where(key < 0, key ^ 0x7FFFFFFF, key), F32)

    def count(pred):
        def body(kt, acc):
            off = pl.multiple_of(kt * SK, SK)
            sc = sc_ref[pl.ds(off, SK), :]
            return acc + _fold_rows(jnp.where(pred(sc, off), 1, 0).astype(I32), jnp.add)
        acc = lax.fori_loop(0, ntile, body, jnp.zeros((SUBLANES, QB), I32))
        return jnp.sum(acc, axis=0, keepdims=True)

    def bit_step(it, lo):
        cand = lo + jnp.left_shift(jnp.int32(1), 31 - it)
        cf = key_to_f32(cand)
        cnt = count(lambda sc, off: sc >= cf)
        return jnp.where(cnt >= topk, cand, lo)

    lo = lax.fori_loop(0, 32, bit_step, jnp.full((1, QB), INT_MIN, I32))
    has_thr = lo > KEY_NEG_INF
    thr = key_to_f32(jnp.maximum(lo, KEY_NEG_INF))
    cnt_gt = count(lambda sc, off: sc > thr)
    cnt_eq = count(lambda sc, off: sc == thr)
    nbits = max(1, (sc_ref.shape[0] - 1).bit_length())
    tied = (cnt_gt + cnt_eq > topk) & has_thr

    def search_cut():
        def idx_step(it, m):
            cand = m + jnp.left_shift(jnp.int32(1), nbits - 1 - it)
            cnt = cnt_gt + count(lambda sc, off: (sc == thr) & ((off + rowi) < cand))
            return jnp.where(cnt < topk, cand, m)
        return lax.fori_loop(0, nbits, idx_step, jnp.zeros((1, QB), I32))

    cut = lax.cond(jnp.max(jnp.where(tied, 1, 0)) > 0, search_cut,
                   lambda: jnp.full((1, QB), sc_ref.shape[0], I32))

    def mask_tile(kt, carry):
        off = pl.multiple_of(kt * SK, SK)
        sc = sc_ref[pl.ds(off, SK), :]
        sel = (sc > thr) | ((sc == thr) & ((off + rowi) <= cut) & has_thr)
        msk_ref[pl.ds(off, SK), :] = jnp.where(sel, 0.0, MASK_NEG).astype(F32)
        return carry

    lax.fori_loop(0, ntile, mask_tile, 0)

    m_ref[...] = jnp.full_like(m_ref, MASK_NEG)
    acc_ref[...] = jnp.zeros_like(acc_ref)
    near_lo = q0 - QB

    def attend(rows, nrows, off, bias_rows, skip_near):
        s = _dot(ckv_ref[rows, :], qat_ref[...])
        mk = msk_ref[rows, :]
        if skip_near:
            rr = off + rowi
            mk = jnp.where((rr >= near_lo) & (rr < near_lo + 2 * QB), MASK_NEG, mk)
        m_old = m_ref[...]
        tmax = []
        for h in range(DS_HEADS):
            cs = slice(h * QB, (h + 1) * QB)
            t = s[:, cs] + mk
            if bias_rows is not None:
                t = t + tab_ref[bias_rows, cs]
            lgt_ref[0:nrows, cs] = t
            tmax.append(jnp.max(_fold_rows(t, jnp.maximum), axis=0, keepdims=True))
        m_new = jnp.maximum(m_old, jnp.concatenate(tmax, axis=1))
        m_ref[...] = m_new
        pr = jnp.exp2(lgt_ref[0:nrows, :] - m_new).astype(BF16)
        acc_ref[...] = acc_ref[...] * jnp.exp2(m_old - m_new) + _dot(ckvt_ref[:, rows], pr)

    def far_tile(kt, c):
        off = pl.multiple_of(kt * SK, SK)
        attend(pl.ds(off, SK), SK, off, None, False)
        return c

    def edge_tile(kt, c):
        off = pl.multiple_of(kt * SK, SK)
        attend(pl.ds(off, SK), SK, off, None, True)
        return c

    nfar = jnp.maximum(ntile - 2, 0)
    lax.fori_loop(0, nfar, far_tile, 0)
    lax.fori_loop(nfar, ntile, edge_tile, 0)

    @pl.when(i > 0)
    def _():
        attend(pl.ds(pl.multiple_of(near_lo, QB), 2 * QB), 2 * QB, None, slice(0, 2 * QB), False)

    @pl.when(i == 0)
    def _():
        attend(slice(0, QB), QB, None, slice(QB, 2 * QB), False)

    o_lat = acc_ref[0:R, :] * (1.0 / acc_ref[R:R + 1, :])
    for h in range(DS_HEADS):
        out_ref[h * DS_HEAD:(h + 1) * DS_HEAD, :] = _dot(
            wuvt_ref[h], o_lat[:, h * QB:(h + 1) * QB].astype(BF16))

    g = gds_ref[...]
    o_ref[...] = (out_ref[...].T * (g * _sigmoid(g))).astype(BF16)


def _dsa(z, ckv, kid, B, T, q_norm_g, w_uq, w_uk, w_uv, iw_q, tab):
    nq = T // QB
    topk = min(TOPK_MAX, T // 4)
    R = DS_KV_RANK
    wq = jnp.concatenate([w_uq, iw_q], axis=1).astype(BF16)
    wuk_h = jnp.transpose(w_uk, (1, 0, 2)).astype(BF16)
    wuv_t = jnp.transpose(w_uv, (1, 2, 0)).astype(BF16)
    ckv_t = jnp.concatenate([jnp.swapaxes(ckv.reshape(B, T, R), 1, 2),
                             jnp.ones((B, ONES_ROWS, T), BF16)], axis=1).reshape(B * (R + ONES_ROWS), T)
    const2 = lambda b, i: (0, 0)
    const3 = lambda b, i: (0, 0, 0)
    resident = dict(pipeline_mode=pl.Buffered(1))
    return pl.pallas_call(
        functools.partial(_dsa_kernel, topk=topk),
        grid=(B, nq),
        in_specs=[pl.BlockSpec((QB, DS_Q_RANK), lambda b, i: (b * nq + i, COL_Q // DS_Q_RANK)),
                  pl.BlockSpec((QB, LANES), lambda b, i: (b * nq + i, COL_KX // LANES)),
                  pl.BlockSpec((QB, DS_WIDTH), lambda b, i: (b * nq + i, COL_GDS // DS_WIDTH)),
                  pl.BlockSpec((T, IDX_DIM), lambda b, i: (b, 0), **resident),
                  pl.BlockSpec((T, R), lambda b, i: (b, 0), **resident),
                  pl.BlockSpec((R + ONES_ROWS, T), lambda b, i: (b, 0), **resident),
                  pl.BlockSpec((1, DS_Q_RANK), const2),
                  pl.BlockSpec((DS_Q_RANK, 2 * DS_WIDTH), const2, **resident),
                  pl.BlockSpec((DS_HEADS, R, DS_HEAD), const3, **resident),
                  pl.BlockSpec((DS_HEADS, DS_HEAD, R), const3, **resident),
                  pl.BlockSpec((2 * QB, DS_HEADS * QB), const2, **resident)],
        out_specs=pl.BlockSpec((QB, DS_WIDTH), lambda b, i: (b * nq + i, 0)),
        out_shape=jax.ShapeDtypeStruct((B * T, DS_WIDTH), BF16),
        scratch_shapes=[pltpu.VMEM((T, QB), F32),
                        pltpu.VMEM((T, QB), F32),
                        pltpu.VMEM((SK, DS_HEADS * QB), F32),
                        pltpu.VMEM((R + ONES_ROWS, DS_HEADS * QB), F32),
                        pltpu.VMEM((1, DS_HEADS * QB), F32),
                        pltpu.VMEM((R, DS_HEADS * QB), BF16),
                        pltpu.VMEM((IDX_HEADS // 2, IDX_DIM, 2 * QB), BF16),
                        pltpu.VMEM((IDX_HEADS, QB), F32),
                        pltpu.VMEM((DS_WIDTH, QB), F32)],
        compiler_params=pltpu.CompilerParams(dimension_semantics=("parallel", "arbitrary"),
                                             vmem_limit_bytes=VMEM_LIMIT),
        name="dsa",
    )(z, z, z, kid, ckv, ckv_t, q_norm_g.reshape(1, -1).astype(F32), wq, wuk_h, wuv_t, tab)


def _post_kernel(x_ref, a1_ref, a2_ref, p_ref, w1_ref, w2_ref, pw_ref, gw_ref, fg_ref, o_ref):
    h = x_ref[...] + _dot(a1_ref[...], w1_ref[...]) + _dot(a2_ref[...], w2_ref[...])
    e = _dot(p_ref[...].astype(BF16), pw_ref[...])
    gate = _sigmoid(_dot(h.astype(BF16), gw_ref[...]))
    h2 = h + e * gate
    ms = jnp.mean(h2 * h2, axis=-1, keepdims=True)
    o_ref[...] = h2 * lax.rsqrt(ms + NORM_EPS) * fg_ref[...]


def _post(x2, o_rw, o_ds, p2, w_out, ple_w, gate_w, final_g, tm=256):
    m, d = x2.shape
    kh = o_rw.shape[1]
    pd = p2.shape[1]
    resident = dict(pipeline_mode=pl.Buffered(1))
    return pl.pallas_call(
        _post_kernel,
        grid=(m // tm,),
        in_specs=[pl.BlockSpec((tm, d), lambda i: (i, 0)),
                  pl.BlockSpec((tm, kh), lambda i: (i, 0)),
                  pl.BlockSpec((tm, kh), lambda i: (i, 0)),
                  pl.BlockSpec((tm, pd), lambda i: (i, 0)),
                  pl.BlockSpec((kh, d), lambda i: (0, 0), **resident),
                  pl.BlockSpec((kh, d), lambda i: (1, 0), **resident),
                  pl.BlockSpec((pd, d), lambda i: (0, 0), **resident),
                  pl.BlockSpec((d, d), lambda i: (0, 0), **resident),
                  pl.BlockSpec((1, d), lambda i: (0, 0))],
        out_specs=pl.BlockSpec((tm, d), lambda i: (i, 0)),
        out_shape=jax.ShapeDtypeStruct((m, d), F32),
        compiler_params=pltpu.CompilerParams(dimension_semantics=("parallel",),
                                             vmem_limit_bytes=VMEM_LIMIT),
        name="post",
    )(x2, o_rw, o_ds, p2, w_out, w_out, ple_w, gate_w, final_g)


def _regroup_w_in(w):
    s0 = 3 * RW_WIDTH
    s1 = s0 + 2 * RW_LORA
    s2 = s1 + RW_WIDTH
    s3 = s2 + DS_Q_RANK
    s4 = s3 + DS_KV_RANK
    s5 = s4 + IDX_DIM
    s6 = s5 + IDX_HEADS
    w = w.astype(BF16)
    pad = jnp.zeros((w.shape[0], Z_WIDTH - (COL_KX + IDX_DIM + IDX_HEADS)), w.dtype)
    return jnp.concatenate([w[:, 0:s0], w[:, s1:s2], w[:, s6:], w[:, s3:s4], w[:, s2:s3],
                            w[:, s0:s1], w[:, s4:s6], pad], axis=1)


def kernel(x, p, w_in, norm_g, rw_mu, rw_w0, rw_w_up, rw_a0, rw_a_up, rw_k_k, rw_k_a, rw_r_k, rw_ln_g, rw_ln_b, ds_q_norm_g, ds_kv_norm_g, idx_k_norm_g, ds_w_uq, ds_w_uk, ds_w_uv, idx_w_q, rel_bias, w_out, ple_w, ple_gate_w, final_g):
    B, T, D = x.shape
    depth = w_in.shape[0]
    assert depth == 1 and T % SK == 0 and T % RW_CHUNK == 0 and (B * T) % 512 == 0
    h = x.reshape(B * T, D)
    tab = _biastab(rel_bias)
    for i in range(depth):
        w = _regroup_w_in(w_in[i])
        z = _inproj(h, norm_g[i].reshape(1, D), w)
        ckv, kid = _kvprep(z, ds_kv_norm_g[i].reshape(1, -1), idx_k_norm_g[i].reshape(1, -1))
        o_rw = _rwkv(z, B, T, rw_mu[i], rw_w0[i], rw_a0[i], rw_k_k[i], rw_k_a[i],
                     rw_r_k[i].reshape(-1), rw_ln_g[i], rw_ln_b[i], rw_w_up[i], rw_a_up[i])
        o_ds = _dsa(z, ckv, kid, B, T, ds_q_norm_g[i], ds_w_uq[i], ds_w_uk[i], ds_w_uv[i],
                    idx_w_q[i], tab)
        h = _post(h, o_rw, o_ds, p[i].reshape(B * T, -1), w_out[i].astype(BF16),
                  ple_w[i].astype(BF16), ple_gate_w[i].astype(BF16), final_g.reshape(1, D))
    return h.reshape(B, T, D)
```

```python
import functools
import math

import jax
import jax.numpy as jnp
from jax import lax
from jax.experimental import pallas as pl
from jax.experimental.pallas import tpu as pltpu

F32 = jnp.float32
BF16 = jnp.bfloat16
I32 = jnp.int32

RW_WIDTH = 1024
RW_HEAD = 64
RW_HEADS = 16
RW_LORA = 64
DS_WIDTH = 1024
DS_HEAD = 64
DS_HEADS = 16
DS_Q_RANK = 384
DS_KV_RANK = 256
IDX_HEADS = 16
IDX_DIM = 64
TOPK_MAX = 256
CHUNK = 64
NUM_BUCKETS = 32
MAX_DISTANCE = 128
NORM_EPS = 1e-6
GN_EPS = 64e-5

COL_R, COL_K, COL_V, COL_GRW, COL_GDS = 0, 1024, 2048, 3072, 4096
COL_KV = 5120
COL_Q = 5376
COL_WA = 5760
COL_KX = 5888
Z_WIDTH = 6144

LANES = 128
SUBLANES = 8
QB = 128
SK = 512
BIAS_ROWS = 2 * SK + 2 * QB
RW_CHUNK = 64
INT_MIN = -2 ** 31
KEY_NEG_INF = -2139095041
MASK_NEG = -1e30
LOG2E = 1.4426950408889634
ONES_ROWS = 16
VMEM_LIMIT = 52 * 1024 * 1024


def _sigmoid(x):
    return 1.0 / (1.0 + jnp.exp(-x))


def _dot(a, b):
    return jnp.dot(a, b, preferred_element_type=F32)


def _dot_nt(a, b):
    return lax.dot_general(a, b, (((1,), (1,)), ((), ())), preferred_element_type=F32)


def _dot_tn(a, b):
    return lax.dot_general(a, b, (((0,), (0,)), ((), ())), preferred_element_type=F32)


def _inproj_kernel(x_ref, g_ref, w_ref, o_ref, *, tn):
    x = x_ref[...]
    ms = jnp.mean(x * x, axis=-1, keepdims=True)
    xn = (x * lax.rsqrt(ms + NORM_EPS) * g_ref[...]).astype(BF16)
    for j in range(o_ref.shape[1] // tn):
        o_ref[:, j * tn:(j + 1) * tn] = _dot(xn, w_ref[:, j * tn:(j + 1) * tn])


def _inproj(x2, g, w, tm=256, tn=768):
    m, d = x2.shape
    n = w.shape[1]
    return pl.pallas_call(
        functools.partial(_inproj_kernel, tn=tn),
        grid=(m // tm,),
        in_specs=[pl.BlockSpec((tm, d), lambda i: (i, 0)),
                  pl.BlockSpec((1, d), lambda i: (0, 0)),
                  pl.BlockSpec((d, n), lambda i: (0, 0), pipeline_mode=pl.Buffered(1))],
        out_specs=pl.BlockSpec((tm, n), lambda i: (i, 0)),
        out_shape=jax.ShapeDtypeStruct((m, n), F32),
        compiler_params=pltpu.CompilerParams(dimension_semantics=("parallel",),
                                             vmem_limit_bytes=VMEM_LIMIT),
        name="inproj",
    )(x2, g, w)


def _kvprep_kernel(kv_ref, kx_ref, gkv_ref, gik_ref, ckv_ref, kid_ref):
    kv = kv_ref[...]
    ms = jnp.mean(kv * kv, axis=-1, keepdims=True)
    ckv_ref[...] = (kv * lax.rsqrt(ms + NORM_EPS) * gkv_ref[...]).astype(BF16)
    ki = kx_ref[:, 0:IDX_DIM]
    ms2 = jnp.mean(ki * ki, axis=-1, keepdims=True)
    kid_ref[...] = (ki * lax.rsqrt(ms2 + NORM_EPS) * gik_ref[...]).astype(BF16)


def _kvprep(z, gkv, gik, tm=512):
    m = z.shape[0]
    return pl.pallas_call(
        _kvprep_kernel,
        grid=(m // tm,),
        in_specs=[pl.BlockSpec((tm, DS_KV_RANK), lambda i: (i, COL_KV // DS_KV_RANK)),
                  pl.BlockSpec((tm, LANES), lambda i: (i, COL_KX // LANES)),
                  pl.BlockSpec((1, DS_KV_RANK), lambda i: (0, 0)),
                  pl.BlockSpec((1, IDX_DIM), lambda i: (0, 0))],
        out_specs=[pl.BlockSpec((tm, DS_KV_RANK), lambda i: (i, 0)),
                   pl.BlockSpec((tm, IDX_DIM), lambda i: (i, 0))],
        out_shape=[jax.ShapeDtypeStruct((m, DS_KV_RANK), BF16),
                   jax.ShapeDtypeStruct((m, IDX_DIM), BF16)],
        compiler_params=pltpu.CompilerParams(dimension_semantics=("parallel",)),
        name="kvprep",
    )(z, z, gkv, gik)


def _rwkv_kernel(r_ref, k_ref, v_ref, g_ref, wa_ref,
                 mur_ref, muk_ref, muv_ref, muwa_ref,
                 w0_ref, a0_ref, kk_ref, ka_ref, rk_ref, lng_ref, lnb_ref,
                 wup_ref, aup_ref,
                 o_ref,
                 pr_ref, pk_ref, pv_ref, pwa_ref, st_ref, *, nb):
    C = RW_CHUNK
    N = RW_HEAD

    @pl.when(pl.program_id(0) == 0)
    def _():
        pr_ref[...] = jnp.zeros_like(pr_ref)
        pk_ref[...] = jnp.zeros_like(pk_ref)
        pv_ref[...] = jnp.zeros_like(pv_ref)
        pwa_ref[...] = jnp.zeros_like(pwa_ref)
        st_ref[...] = jnp.zeros_like(st_ref)

    row = lax.broadcasted_iota(I32, (nb * C, 1), 0)

    def shift(ref, prev_ref, mu_ref):
        z = ref[...].reshape(nb * C, ref.shape[2])
        zp = pltpu.roll(z, 1, 0)
        for b in range(nb):
            zp = jnp.where(row == b * C, prev_ref[b], zp)
            prev_ref[b] = z[(b + 1) * C - 1:(b + 1) * C, :]
        return z + mu_ref[...] * (zp - z)

    r = shift(r_ref, pr_ref, mur_ref)
    k = shift(k_ref, pk_ref, muk_ref)
    v = shift(v_ref, pv_ref, muv_ref)
    wa = shift(wa_ref, pwa_ref, muwa_ref)
    wd = wa[:, 0:RW_LORA]
    ad = wa[:, RW_LORA:2 * RW_LORA]

    wl = w0_ref[...] + _dot(jnp.tanh(wd).astype(BF16), wup_ref[...])
    nwl = -wl
    softplus = jnp.maximum(nwl, 0.0) + jnp.log1p(jnp.exp(-jnp.abs(nwl)))
    w_log = -softplus - 0.5
    lw = -jnp.exp(w_log)
    a = _sigmoid(a0_ref[...] + _dot(ad.astype(BF16), aup_ref[...]))
    kk = k * kk_ref[...]
    k2 = k * (1.0 + (a - 1.0) * ka_ref[...])

    ti = lax.broadcasted_iota(I32, (C, C), 0)
    tj = lax.broadcasted_iota(I32, (C, C), 1)
    incl = ti >= tj
    strict = ti > tj
    tri = jnp.where(incl, 1.0, 0.0).astype(F32)

    def per_row(mat, x):
        return jnp.concatenate(
            [jnp.dot(mat, x[b * C:(b + 1) * C], preferred_element_type=F32,
                     precision=lax.Precision.HIGHEST) for b in range(nb)], axis=0)

    cum = per_row(tri, lw)
    p = jnp.exp(cum)
    pinv = jnp.exp(-cum)
    pprev = jnp.exp(cum - lw)
    tot = per_row(jnp.ones((C, C), F32), lw)
    pend = jnp.exp(tot)

    g = g_ref[...].reshape(nb * C, RW_WIDTH)
    gate = g * _sigmoid(g)

    HP = RW_HEADS // 2
    NP = nb * HP

    def pairs(x):
        return jnp.stack([x[b * C:(b + 1) * C, j * LANES:(j + 1) * LANES]
                          for b in range(nb) for j in range(HP)], axis=0)

    def per_pair(ref):
        return jnp.concatenate([ref[...]] * nb, axis=0)

    lane = lax.broadcasted_iota(I32, (1, 1, LANES), 2)
    m_lo = jnp.where(lane < N, 1.0, 0.0).astype(F32)
    m_hi = 1.0 - m_lo
    bi = lax.broadcasted_iota(I32, (LANES, LANES), 0)
    bj = lax.broadcasted_iota(I32, (LANES, LANES), 1)
    same_head = (bi < N) == (bj < N)
    ones_bd = jnp.where(same_head, 1.0, 0.0).astype(BF16)

    def head_sum(x):
        return _dot(x.reshape(NP * C, LANES).astype(BF16), ones_bd).reshape(NP, C, LANES)

    def halves(x):
        return jnp.concatenate([x * m_lo, x * m_hi], axis=1)

    def bmm(x, y):
        return lax.dot_general(x, y, (((2,), (1,)), ((0,), (0,))), preferred_element_type=F32)

    def bmm_nt(x, y):
        return lax.dot_general(x, y, (((2,), (2,)), ((0,), (0,))), preferred_element_type=F32)

    def block_mask(nblk, cmp):
        wi = lax.broadcasted_iota(I32, (C, nblk * C), 0)
        wj = lax.broadcasted_iota(I32, (C, nblk * C), 1) & (C - 1)
        return cmp(wi, wj)

    r_p, k2_p, v_p, a_p = pairs(r), pairs(k2), pairs(v), pairs(a)
    p_p, pinv_p, pprev_p = pairs(p), pairs(pinv), pairs(pprev)
    kk_p = pairs(kk)
    kkn = kk_p / jnp.maximum(jnp.sqrt(head_sum(kk_p * kk_p)), 1e-12)
    at = (-kkn) * pprev_p
    bt = (kkn * a_p) * pinv_p
    kt = k2_p * pinv_p
    rt = r_p * p_p
    pend_p = pairs(pend)
    pend2 = jnp.concatenate([pend_p, pend_p], axis=1)

    lhs2 = jnp.concatenate([at, rt], axis=1).astype(BF16)
    rhs4 = jnp.concatenate([halves(kt), halves(bt)], axis=1).astype(BF16)
    gc = bmm_nt(lhs2, rhs4)
    strict2 = block_mask(2, lambda i_, j_: i_ > j_)
    incl4 = block_mask(4, lambda i_, j_: i_ >= j_)
    a_ak = jnp.where(strict2, gc[:, 0:C, 0:2 * C], 0.0)
    nmat = jnp.where(strict2, gc[:, 0:C, 2 * C:4 * C], 0.0)
    a_rkb = jnp.where(incl4, gc[:, C:2 * C, :], 0.0)

    g0 = st_ref[...]
    sg = bmm_nt(lhs2, g0.astype(BF16))
    vm2 = halves(v_p).astype(BF16)
    u = sg[:, 0:C] + bmm(a_ak.astype(BF16), vm2)
    pw = nmat
    u = u + bmm(pw.astype(BF16), halves(u).astype(BF16))
    n = 1
    while 2 * n < C:
        pw = bmm(pw.astype(BF16), halves(pw).astype(BF16))
        u = u + bmm(pw.astype(BF16), halves(u).astype(BF16))
        n *= 2
    um2b = halves(u).astype(BF16)
    y = sg[:, C:2 * C] + bmm(a_rkb.astype(BF16), jnp.concatenate([vm2, um2b], axis=1))
    uv = jnp.concatenate([u, v_p], axis=1).astype(BF16)
    bkh = (jnp.concatenate([bt, kt], axis=1) * pend2).astype(BF16)
    upd = lax.dot_general(uv, bkh, (((1,), (1,)), ((0,), (0,))), preferred_element_type=F32)
    st_ref[...] = g0 * pend2 + jnp.where(same_head, upd, 0.0)

    inv_n = 1.0 / N
    yc = y - head_sum(y) * inv_n
    var = head_sum(yc * yc) * inv_n
    yn = yc * lax.rsqrt(var + GN_EPS) * per_pair(lng_ref) + per_pair(lnb_ref)
    bonus = head_sum(r_p * k2_p * per_pair(rk_ref)) * v_p
    out = (yn + bonus) * pairs(gate)
    for b in range(nb):
        for j in range(HP):
            o_ref[b, :, j * LANES:(j + 1) * LANES] = out[b * HP + j].astype(BF16)


def _rwkv(z, B, T, mu, w0, a0, k_k, k_a, r_k, ln_g, ln_b, w_up, a_up):
    C = RW_CHUNK
    W = RW_WIDTH
    z3 = z.reshape(B, T, z.shape[1])
    row = lambda a: a.reshape(1, -1).astype(F32)
    mu_r, mu_k, mu_v = mu[0:W], mu[W:2 * W], mu[2 * W:3 * W]
    mu_wa = mu[3 * W:3 * W + 2 * RW_LORA]
    zspec = lambda col: pl.BlockSpec((B, C, W), lambda c: (0, c, col // W))
    pspec = lambda width: pl.BlockSpec((1, width), lambda c: (0, 0))
    npairs = RW_HEADS // 2
    prow = lambda a: a.reshape(npairs, 1, LANES).astype(F32)
    ppspec = pl.BlockSpec((npairs, 1, LANES), lambda c: (0, 0, 0))
    wspec = pl.BlockSpec((RW_LORA, W), lambda c: (0, 0))
    out = pl.pallas_call(
        functools.partial(_rwkv_kernel, nb=B),
        grid=(T // C,),
        in_specs=[zspec(COL_R), zspec(COL_K), zspec(COL_V), zspec(COL_GRW),
                  pl.BlockSpec((B, C, LANES), lambda c: (0, c, COL_WA // LANES)),
                  pspec(W), pspec(W), pspec(W), pspec(LANES),
                  pspec(W), pspec(W), pspec(W), pspec(W), ppspec, ppspec, ppspec,
                  wspec, wspec],
        out_specs=pl.BlockSpec((B, C, W), lambda c: (0, c, 0)),
        out_shape=jax.ShapeDtypeStruct((B, T, W), BF16),
        scratch_shapes=[pltpu.VMEM((B, 1, W), F32), pltpu.VMEM((B, 1, W), F32), pltpu.VMEM((B, 1, W), F32),
                        pltpu.VMEM((B, 1, LANES), F32),
                        pltpu.VMEM((B * npairs, LANES, LANES), F32)],
        compiler_params=pltpu.CompilerParams(dimension_semantics=("arbitrary",),
                                             vmem_limit_bytes=VMEM_LIMIT),
        name="rwkv",
    )(z3, z3, z3, z3, z3,
      row(mu_r), row(mu_k), row(mu_v), row(mu_wa),
      row(w0), row(a0), row(k_k), row(k_a), prow(r_k), prow(ln_g), prow(ln_b),
      w_up.astype(BF16), a_up.astype(BF16))
    return out.reshape(B * T, W)


def _biastab_kernel(rb_ref, o_ref):
    o_ref[...] = jnp.zeros_like(o_ref)
    nb = NUM_BUCKETS // 2
    max_exact = nb // 2
    c = lax.broadcasted_iota(I32, (2 * QB, QB), 0)
    r = lax.broadcasted_iota(I32, (2 * QB, QB), 1)
    rel = c - QB - r
    ret = jnp.where(rel > 0, nb, 0)
    n = jnp.abs(rel)
    nf = jnp.maximum(n, 1).astype(F32)
    large = max_exact + (jnp.log(nf / max_exact) / math.log(MAX_DISTANCE / max_exact)
                         * (nb - max_exact)).astype(I32)
    large = jnp.minimum(large, nb - 1) & (NUM_BUCKETS - 1)
    bucket = ret + jnp.where(n < max_exact, n, large)
    for h in range(DS_HEADS):
        far = rb_ref[nb - 1, h]
        acc = jnp.zeros((2 * QB, QB), F32)
        for b in range(NUM_BUCKETS):
            acc = jnp.where(bucket == b, rb_ref[b, h] - far, acc)
        o_ref[SK:SK + 2 * QB, h * QB:(h + 1) * QB] = acc * LOG2E


def _biastab(rel_bias):
    return pl.pallas_call(
        _biastab_kernel,
        in_specs=[pl.BlockSpec(memory_space=pltpu.SMEM)],
        out_specs=pl.BlockSpec(memory_space=pltpu.VMEM),
        out_shape=jax.ShapeDtypeStruct((BIAS_ROWS, DS_HEADS * QB), F32),
        compiler_params=pltpu.CompilerParams(vmem_limit_bytes=VMEM_LIMIT),
        name="biastab",
    )(rel_bias.astype(F32))


def _fold_rows(x, op):
    n = x.shape[0] // SUBLANES
    accs = [x[j * SUBLANES:(j + 1) * SUBLANES] for j in range(min(4, n))]
    for j in range(4, n):
        accs[j % 4] = op(accs[j % 4], x[j * SUBLANES:(j + 1) * SUBLANES])
    while len(accs) > 1:
        accs = [op(accs[j], accs[j + 1]) for j in range(0, len(accs) - 1, 2)] + (
            [accs[-1]] if len(accs) % 2 else [])
    return accs[0]


def _dsa_kernel(ql_ref, kx_ref, gds_ref, kid_ref, ckv_ref, ckvt_ref, qg_ref, wq_ref, wuk_ref, wuvt_ref,
                tab_ref, o_ref,
                sc_ref, msk_ref, lgt_ref, acc_ref, m_ref, qat_ref, qit_ref, w_ref, out_ref, *, topk):
    i = pl.program_id(1)
    q0 = i * QB
    ntile = jnp.right_shift(q0 + (QB + SK - 1), SK.bit_length() - 1)
    R = DS_KV_RANK
    GW = DS_HEADS * QB

    ql = ql_ref[...]
    ms = jnp.mean(ql * ql, axis=-1, keepdims=True)
    qn = (ql * lax.rsqrt(ms + NORM_EPS) * qg_ref[...]).astype(BF16)
    qt = _dot(qn, wq_ref[...]).T
    for h in range(DS_HEADS):
        qh = qt[h * DS_HEAD:(h + 1) * DS_HEAD, :].astype(BF16)
        qat_ref[:, h * QB:(h + 1) * QB] = (_dot(wuk_ref[h], qh) * (DS_HEAD ** -0.5 * LOG2E)).astype(BF16)
    for pr in range(IDX_HEADS // 2):
        base = DS_WIDTH + 2 * pr * IDX_DIM
        qit_ref[pr] = jnp.concatenate([qt[base:base + IDX_DIM, :],
                                       qt[base + IDX_DIM:base + 2 * IDX_DIM, :]], axis=1).astype(BF16)
    w_ref[...] = kx_ref[...].T[IDX_DIM:IDX_DIM + IDX_HEADS, :] * (IDX_HEADS ** -0.5 * IDX_DIM ** -0.5)

    lanei = lax.broadcasted_iota(I32, (1, QB), 1)
    csh = CHUNK.bit_length() - 1
    limit = jnp.left_shift(jnp.right_shift(q0 + lanei, csh) + 1, csh)
    rowi = lax.broadcasted_iota(I32, (SK, QB), 0)

    def score_tile(kt, carry):
        off = pl.multiple_of(kt * SK, SK)
        kid = kid_ref[pl.ds(off, SK), :]
        s = jnp.zeros((SK, QB), F32)
        for pr in range(IDX_HEADS // 2):
            lg = _dot(kid, qit_ref[pr])
            s = s + w_ref[2 * pr:2 * pr + 1, :] * jnp.maximum(lg[:, 0:QB], 0.0)
            s = s + w_ref[2 * pr + 1:2 * pr + 2, :] * jnp.maximum(lg[:, QB:2 * QB], 0.0)
        adm = (off + rowi) < limit
        sc_ref[pl.ds(off, SK), :] = jnp.where(adm, s, -jnp.inf)
        return carry

    lax.fori_loop(0, ntile, score_tile, 0)

    def key_to_f32(key):
        return pltpu.bitcast(jnp.where(key < 0, key ^ 0x7FFFFFFF, key), F32)

    def count(pred):
        def body(kt, acc):
            off = pl.multiple_of(kt * SK, SK)
            sc = sc_ref[pl.ds(off, SK), :]
            return acc + _fold_rows(jnp.where(pred(sc, off), 1, 0).astype(I32), jnp.add)
        acc = lax.fori_loop(0, ntile, body, jnp.zeros((SUBLANES, QB), I32))
        return jnp.sum(acc, axis=0, keepdims=True)

    def bit_step(it, lo):
        cand = lo + jnp.left_shift(jnp.int32(1), 31 - it)
        cf = key_to_f32(cand)
        cnt = count(lambda sc, off: sc >= cf)
        return jnp.where(cnt >= topk, cand, lo)

    lo = lax.fori_loop(0, 32, bit_step, jnp.full((1, QB), INT_MIN, I32))
    has_thr = lo > KEY_NEG_INF
    thr = key_to_f32(jnp.maximum(lo, KEY_NEG_INF))
    cnt_gt = count(lambda sc, off: sc > thr)
    cnt_eq = count(lambda sc, off: sc == thr)
    nbits = max(1, (sc_ref.shape[0] - 1).bit_length())
    tied = (cnt_gt + cnt_eq > topk) & has_thr

    def search_cut():
        def idx_step(it, m):
            cand = m + jnp.left_shift(jnp.int32(1), nbits - 1 - it)
            cnt = cnt_gt + count(lambda sc, off: (sc == thr) & ((off + rowi) < cand))
            return jnp.where(cnt < topk, cand, m)
        return lax.fori_loop(0, nbits, idx_step, jnp.zeros((1, QB), I32))

    cut = lax.cond(jnp.max(jnp.where(tied, 1, 0)) > 0, search_cut,
                   lambda: jnp.full((1, QB), sc_ref.shape[0], I32))

    def mask_tile(kt, carry):
        off = pl.multiple_of(kt * SK, SK)
        sc = sc_ref[pl.ds(off, SK), :]
        sel = (sc > thr) | ((sc == thr) & ((off + rowi) <= cut) & has_thr)
        msk_ref[pl.ds(off, SK), :] = jnp.where(sel, 0.0, MASK_NEG).astype(F32)
        return carry

    lax.fori_loop(0, ntile, mask_tile, 0)

    m_ref[...] = jnp.full_like(m_ref, MASK_NEG)
    acc_ref[...] = jnp.zeros_like(acc_ref)
    near_lo = q0 - QB

    def attend(off, bias_off):
        rows = pl.ds(off, SK)
        s = _dot(ckv_ref[rows, :], qat_ref[...])
        mk = msk_ref[rows, :]
        m_old = m_ref[...]
        tmax = []
        for h in range(DS_HEADS):
            cs = slice(h * QB, (h + 1) * QB)
            t = s[:, cs] + mk
            if bias_off is not None:
                t = t + tab_ref[pl.ds(bias_off, SK), cs]
            lgt_ref[:, cs] = t
            tmax.append(jnp.max(_fold_rows(t, jnp.maximum), axis=0, keepdims=True))
        m_new = jnp.maximum(m_old, jnp.concatenate(tmax, axis=1))
        m_ref[...] = m_new
        pr = jnp.exp2(lgt_ref[...] - m_new).astype(BF16)
        acc_ref[...] = acc_ref[...] * jnp.exp2(m_old - m_new) + _dot(ckvt_ref[:, rows], pr)

    def far_tile(kt, c):
        attend(pl.multiple_of(kt * SK, SK), None)
        return c

    def edge_tile(kt, c):
        off = pl.multiple_of(kt * SK, SK)
        attend(off, pl.multiple_of(jnp.maximum(SK + off - near_lo, 0), QB))
        return c

    nfar = jnp.maximum(ntile - 2, 0)
    lax.fori_loop(0, nfar, far_tile, 0)
    lax.fori_loop(nfar, ntile, edge_tile, 0)

    o_lat = acc_ref[0:R, :] * (1.0 / acc_ref[R:R + 1, :])
    for h in range(DS_HEADS):
        out_ref[h * DS_HEAD:(h + 1) * DS_HEAD, :] = _dot(
            wuvt_ref[h], o_lat[:, h * QB:(h + 1) * QB].astype(BF16))

    g = gds_ref[...]
    o_ref[...] = (out_ref[...].T * (g * _sigmoid(g))).astype(BF16)


def _dsa(z, ckv, kid, B, T, q_norm_g, w_uq, w_uk, w_uv, iw_q, tab):
    nq = T // QB
    topk = min(TOPK_MAX, T // 4)
    R = DS_KV_RANK
    wq = jnp.concatenate([w_uq, iw_q], axis=1).astype(BF16)
    wuk_h = jnp.transpose(w_uk, (1, 0, 2)).astype(BF16)
    wuv_t = jnp.transpose(w_uv, (1, 2, 0)).astype(BF16)
    ckv_t = jnp.concatenate([jnp.swapaxes(ckv.reshape(B, T, R), 1, 2),
                             jnp.ones((B, ONES_ROWS, T), BF16)], axis=1).reshape(B * (R + ONES_ROWS), T)
    const2 = lambda b, i: (0, 0)
    const3 = lambda b, i: (0, 0, 0)
    resident = dict(pipeline_mode=pl.Buffered(1))
    return pl.pallas_call(
        functools.partial(_dsa_kernel, topk=topk),
        grid=(B, nq),
        in_specs=[pl.BlockSpec((QB, DS_Q_RANK), lambda b, i: (b * nq + i, COL_Q // DS_Q_RANK)),
                  pl.BlockSpec((QB, LANES), lambda b, i: (b * nq + i, COL_KX // LANES)),
                  pl.BlockSpec((QB, DS_WIDTH), lambda b, i: (b * nq + i, COL_GDS // DS_WIDTH)),
                  pl.BlockSpec((T, IDX_DIM), lambda b, i: (b, 0), **resident),
                  pl.BlockSpec((T, R), lambda b, i: (b, 0), **resident),
                  pl.BlockSpec((R + ONES_ROWS, T), lambda b, i: (b, 0), **resident),
                  pl.BlockSpec((1, DS_Q_RANK), const2),
                  pl.BlockSpec((DS_Q_RANK, 2 * DS_WIDTH), const2, **resident),
                  pl.BlockSpec((DS_HEADS, R, DS_HEAD), const3, **resident),
                  pl.BlockSpec((DS_HEADS, DS_HEAD, R), const3, **resident),
                  pl.BlockSpec((BIAS_ROWS, DS_HEADS * QB), const2, **resident)],
        out_specs=pl.BlockSpec((QB, DS_WIDTH), lambda b, i: (b * nq + i, 0)),
        out_shape=jax.ShapeDtypeStruct((B * T, DS_WIDTH), BF16),
        scratch_shapes=[pltpu.VMEM((T, QB), F32),
                        pltpu.VMEM((T, QB), F32),
                        pltpu.VMEM((SK, DS_HEADS * QB), F32),
                        pltpu.VMEM((R + ONES_ROWS, DS_HEADS * QB), F32),
                        pltpu.VMEM((1, DS_HEADS * QB), F32),
                        pltpu.VMEM((R, DS_HEADS * QB), BF16),
                        pltpu.VMEM((IDX_HEADS // 2, IDX_DIM, 2 * QB), BF16),
                        pltpu.VMEM((IDX_HEADS, QB), F32),
                        pltpu.VMEM((DS_WIDTH, QB), F32)],
        compiler_params=pltpu.CompilerParams(dimension_semantics=("parallel", "arbitrary"),
                                             vmem_limit_bytes=VMEM_LIMIT),
        name="dsa",
    )(z, z, z, kid, ckv, ckv_t, q_norm_g.reshape(1, -1).astype(F32), wq, wuk_h, wuv_t, tab)


def _post_kernel(x_ref, a1_ref, a2_ref, p_ref, w1_ref, w2_ref, pw_ref, gw_ref, fg_ref, o_ref):
    h = x_ref[...] + _dot(a1_ref[...], w1_ref[...]) + _dot(a2_ref[...], w2_ref[...])
    e = _dot(p_ref[...].astype(BF16), pw_ref[...])
    gate = _sigmoid(_dot(h.astype(BF16), gw_ref[...]))
    h2 = h + e * gate
    ms = jnp.mean(h2 * h2, axis=-1, keepdims=True)
    o_ref[...] = h2 * lax.rsqrt(ms + NORM_EPS) * fg_ref[...]


def _post(x2, o_rw, o_ds, p2, w_out, ple_w, gate_w, final_g, tm=256):
    m, d = x2.shape
    kh = o_rw.shape[1]
    pd = p2.shape[1]
    resident = dict(pipeline_mode=pl.Buffered(1))
    return pl.pallas_call(
        _post_kernel,
        grid=(m // tm,),
        in_specs=[pl.BlockSpec((tm, d), lambda i: (i, 0)),
                  pl.BlockSpec((tm, kh), lambda i: (i, 0)),
                  pl.BlockSpec((tm, kh), lambda i: (i, 0)),
                  pl.BlockSpec((tm, pd), lambda i: (i, 0)),
                  pl.BlockSpec((kh, d), lambda i: (0, 0), **resident),
                  pl.BlockSpec((kh, d), lambda i: (1, 0), **resident),
                  pl.BlockSpec((pd, d), lambda i: (0, 0), **resident),
                  pl.BlockSpec((d, d), lambda i: (0, 0), **resident),
                  pl.BlockSpec((1, d), lambda i: (0, 0))],
        out_specs=pl.BlockSpec((tm, d), lambda i: (i, 0)),
        out_shape=jax.ShapeDtypeStruct((m, d), F32),
        compiler_params=pltpu.CompilerParams(dimension_semantics=("parallel",),
                                             vmem_limit_bytes=VMEM_LIMIT),
        name="post",
    )(x2, o_rw, o_ds, p2, w_out, w_out, ple_w, gate_w, final_g)


def _regroup_w_in(w):
    s0 = 3 * RW_WIDTH
    s1 = s0 + 2 * RW_LORA
    s2 = s1 + RW_WIDTH
    s3 = s2 + DS_Q_RANK
    s4 = s3 + DS_KV_RANK
    s5 = s4 + IDX_DIM
    s6 = s5 + IDX_HEADS
    w = w.astype(BF16)
    pad = jnp.zeros((w.shape[0], Z_WIDTH - (COL_KX + IDX_DIM + IDX_HEADS)), w.dtype)
    return jnp.concatenate([w[:, 0:s0], w[:, s1:s2], w[:, s6:], w[:, s3:s4], w[:, s2:s3],
                            w[:, s0:s1], w[:, s4:s6], pad], axis=1)


def kernel(x, p, w_in, norm_g, rw_mu, rw_w0, rw_w_up, rw_a0, rw_a_up, rw_k_k, rw_k_a, rw_r_k, rw_ln_g, rw_ln_b, ds_q_norm_g, ds_kv_norm_g, idx_k_norm_g, ds_w_uq, ds_w_uk, ds_w_uv, idx_w_q, rel_bias, w_out, ple_w, ple_gate_w, final_g):
    B, T, D = x.shape
    depth = w_in.shape[0]
    assert depth == 1 and T % SK == 0 and T % RW_CHUNK == 0 and (B * T) % 512 == 0
    h = x.reshape(B * T, D)
    tab = _biastab(rel_bias)
    for i in range(depth):
        w = _regroup_w_in(w_in[i])
        z = _inproj(h, norm_g[i].reshape(1, D), w)
        ckv, kid = _kvprep(z, ds_kv_norm_g[i].reshape(1, -1), idx_k_norm_g[i].reshape(1, -1))
        o_rw = _rwkv(z, B, T, rw_mu[i], rw_w0[i], rw_a0[i], rw_k_k[i], rw_k_a[i],
                     rw_r_k[i].reshape(-1), rw_ln_g[i], rw_ln_b[i], rw_w_up[i], rw_a_up[i])
        o_ds = _dsa(z, ckv, kid, B, T, ds_q_norm_g[i], ds_w_uq[i], ds_w_uk[i], ds_w_uv[i],
                    idx_w_q[i], tab)
        h = _post(h, o_rw, o_ds, p[i].reshape(B * T, -1), w_out[i].astype(BF16),
                  ple_w[i].astype(BF16), ple_gate_w[i].astype(BF16), final_g.reshape(1, D))
    return h.reshape(B, T, D)
```

```python
import functools
import math

import jax
import jax.numpy as jnp
from jax import lax
from jax.experimental import pallas as pl
from jax.experimental.pallas import tpu as pltpu

F32 = jnp.float32
BF16 = jnp.bfloat16
I32 = jnp.int32

RW_WIDTH = 1024
RW_HEAD = 64
RW_HEADS = 16
RW_LORA = 64
DS_WIDTH = 1024
DS_HEAD = 64
DS_HEADS = 16
DS_Q_RANK = 384
DS_KV_RANK = 256
IDX_HEADS = 16
IDX_DIM = 64
TOPK_MAX = 256
CHUNK = 64
NUM_BUCKETS = 32
MAX_DISTANCE = 128
NORM_EPS = 1e-6
GN_EPS = 64e-5

COL_R, COL_K, COL_V, COL_GRW, COL_GDS = 0, 1024, 2048, 3072, 4096
COL_KV = 5120
COL_Q = 5376
COL_WA = 5760
COL_KX = 5888
Z_WIDTH = 6016

LANES = 128
SUBLANES = 8
QB = 128
SK = 512
BIAS_ROWS = 2 * SK + 2 * QB
RW_CHUNK = 64
INT_MIN = -2 ** 31
KEY_NEG_INF = -2139095041
SEARCH_MIN_BITS = 20
MASK_NEG = -1e30
LOG2E = 1.4426950408889634
ONES_ROWS = 16
VMEM_LIMIT = 52 * 1024 * 1024


def _sigmoid(x):
    return 1.0 / (1.0 + jnp.exp(-x))


def _dot(a, b):
    return jnp.dot(a, b, preferred_element_type=F32)


def _dot_nt(a, b):
    return lax.dot_general(a, b, (((1,), (1,)), ((), ())), preferred_element_type=F32)


def _dot_tn(a, b):
    return lax.dot_general(a, b, (((0,), (0,)), ((), ())), preferred_element_type=F32)


def _inproj_plan():
    s_wa = 3 * RW_WIDTH
    s_grw = s_wa + 2 * RW_LORA
    s_q = s_grw + RW_WIDTH
    s_kv = s_q + DS_Q_RANK
    s_kx = s_kv + DS_KV_RANK
    return [(COL_R, 0, 3 * RW_WIDTH), (COL_GRW, s_grw, RW_WIDTH), (COL_GDS, None, DS_WIDTH),
            (COL_KV, s_kv, DS_KV_RANK), (COL_Q, s_q, DS_Q_RANK), (COL_WA, s_wa, LANES),
            (COL_KX, s_kx, LANES)]


def _inproj_kernel(x_ref, g_ref, w_ref, wg_ref, o_ref, *, tn):
    x = x_ref[...]
    ms = jnp.mean(x * x, axis=-1, keepdims=True)
    xn = (x * lax.rsqrt(ms + NORM_EPS) * g_ref[...]).astype(BF16)
    for dst, src, width in _inproj_plan():
        for j in range(0, width, tn):
            w = min(tn, width - j)
            wt = wg_ref[:, j:j + w] if src is None else w_ref[:, src + j:src + j + w]
            o_ref[:, dst + j:dst + j + w] = _dot(xn, wt)


def _inproj(x2, g, w, w_gds, tm=256, tn=1024):
    m, d = x2.shape
    resident = dict(pipeline_mode=pl.Buffered(1))
    return pl.pallas_call(
        functools.partial(_inproj_kernel, tn=tn),
        grid=(m // tm,),
        in_specs=[pl.BlockSpec((tm, d), lambda i: (i, 0)),
                  pl.BlockSpec((1, d), lambda i: (0, 0)),
                  pl.BlockSpec(w.shape, lambda i: (0, 0), **resident),
                  pl.BlockSpec(w_gds.shape, lambda i: (0, 0), **resident)],
        out_specs=pl.BlockSpec((tm, Z_WIDTH), lambda i: (i, 0)),
        out_shape=jax.ShapeDtypeStruct((m, Z_WIDTH), F32),
        compiler_params=pltpu.CompilerParams(dimension_semantics=("parallel",),
                                             vmem_limit_bytes=VMEM_LIMIT),
        name="inproj",
    )(x2, g, w, w_gds)


def _kvprep_kernel(kv_ref, kx_ref, gkv_ref, gik_ref, ckv_ref, kid_ref):
    kv = kv_ref[...]
    ms = jnp.mean(kv * kv, axis=-1, keepdims=True)
    ckv_ref[...] = (kv * lax.rsqrt(ms + NORM_EPS) * gkv_ref[...]).astype(BF16)
    ki = kx_ref[:, 0:IDX_DIM]
    ms2 = jnp.mean(ki * ki, axis=-1, keepdims=True)
    kid_ref[...] = (ki * lax.rsqrt(ms2 + NORM_EPS) * gik_ref[...]).astype(BF16)


def _kvprep(z, gkv, gik, tm=512):
    m = z.shape[0]
    return pl.pallas_call(
        _kvprep_kernel,
        grid=(m // tm,),
        in_specs=[pl.BlockSpec((tm, DS_KV_RANK), lambda i: (i, COL_KV // DS_KV_RANK)),
                  pl.BlockSpec((tm, LANES), lambda i: (i, COL_KX // LANES)),
                  pl.BlockSpec((1, DS_KV_RANK), lambda i: (0, 0)),
                  pl.BlockSpec((1, IDX_DIM), lambda i: (0, 0))],
        out_specs=[pl.BlockSpec((tm, DS_KV_RANK), lambda i: (i, 0)),
                   pl.BlockSpec((tm, IDX_DIM), lambda i: (i, 0))],
        out_shape=[jax.ShapeDtypeStruct((m, DS_KV_RANK), BF16),
                   jax.ShapeDtypeStruct((m, IDX_DIM), BF16)],
        compiler_params=pltpu.CompilerParams(dimension_semantics=("parallel",)),
        name="kvprep",
    )(z, z, gkv, gik)


def _rwkv_kernel(r_ref, k_ref, v_ref, g_ref, wa_ref,
                 mur_ref, muk_ref, muv_ref, muwa_ref,
                 w0_ref, a0_ref, kk_ref, ka_ref, rk_ref, lng_ref, lnb_ref,
                 wup_ref, aup_ref,
                 o_ref,
                 pr_ref, pk_ref, pv_ref, pwa_ref, st_ref, *, nb):
    C = RW_CHUNK
    N = RW_HEAD

    @pl.when(pl.program_id(0) == 0)
    def _():
        pr_ref[...] = jnp.zeros_like(pr_ref)
        pk_ref[...] = jnp.zeros_like(pk_ref)
        pv_ref[...] = jnp.zeros_like(pv_ref)
        pwa_ref[...] = jnp.zeros_like(pwa_ref)
        st_ref[...] = jnp.zeros_like(st_ref)

    row = lax.broadcasted_iota(I32, (nb * C, 1), 0)

    def shift(ref, prev_ref, mu_ref):
        z = ref[...].reshape(nb * C, ref.shape[2])
        zp = pltpu.roll(z, 1, 0)
        for b in range(nb):
            zp = jnp.where(row == b * C, prev_ref[b], zp)
            prev_ref[b] = z[(b + 1) * C - 1:(b + 1) * C, :]
        return z + mu_ref[...] * (zp - z)

    r = shift(r_ref, pr_ref, mur_ref)
    k = shift(k_ref, pk_ref, muk_ref)
    v = shift(v_ref, pv_ref, muv_ref)
    wa = shift(wa_ref, pwa_ref, muwa_ref)
    wd = wa[:, 0:RW_LORA]
    ad = wa[:, RW_LORA:2 * RW_LORA]

    wl = w0_ref[...] + _dot(jnp.tanh(wd).astype(BF16), wup_ref[...])
    nwl = -wl
    softplus = jnp.maximum(nwl, 0.0) + jnp.log1p(jnp.exp(-jnp.abs(nwl)))
    w_log = -softplus - 0.5
    lw = -jnp.exp(w_log)
    a = _sigmoid(a0_ref[...] + _dot(ad.astype(BF16), aup_ref[...]))
    kk = k * kk_ref[...]
    k2 = k * (1.0 + (a - 1.0) * ka_ref[...])

    ti = lax.broadcasted_iota(I32, (C, C), 0)
    tj = lax.broadcasted_iota(I32, (C, C), 1)
    incl = ti >= tj
    strict = ti > tj
    tri = jnp.where(incl, 1.0, 0.0).astype(F32)

    def per_row(mat, x):
        return jnp.concatenate(
            [jnp.dot(mat, x[b * C:(b + 1) * C], preferred_element_type=F32,
                     precision=lax.Precision.HIGHEST) for b in range(nb)], axis=0)

    cum = per_row(tri, lw)
    p = jnp.exp(cum)
    pinv = jnp.exp(-cum)
    pprev = jnp.exp(cum - lw)
    tot = per_row(jnp.ones((C, C), F32), lw)
    pend = jnp.exp(tot)

    g = g_ref[...].reshape(nb * C, RW_WIDTH)
    gate = g * _sigmoid(g)

    HP = RW_HEADS // 2
    NP = nb * HP

    def pairs(x):
        return jnp.stack([x[b * C:(b + 1) * C, j * LANES:(j + 1) * LANES]
                          for b in range(nb) for j in range(HP)], axis=0)

    def per_pair(ref):
        return jnp.concatenate([ref[...]] * nb, axis=0)

    lane = lax.broadcasted_iota(I32, (1, 1, LANES), 2)
    m_lo = jnp.where(lane < N, 1.0, 0.0).astype(F32)
    m_hi = 1.0 - m_lo
    bi = lax.broadcasted_iota(I32, (LANES, LANES), 0)
    bj = lax.broadcasted_iota(I32, (LANES, LANES), 1)
    same_head = (bi < N) == (bj < N)
    ones_bd = jnp.where(same_head, 1.0, 0.0).astype(BF16)

    def head_sum(x):
        return _dot(x.reshape(NP * C, LANES).astype(BF16), ones_bd).reshape(NP, C, LANES)

    def halves(x):
        return jnp.concatenate([x * m_lo, x * m_hi], axis=1)

    def bmm(x, y):
        return lax.dot_general(x, y, (((2,), (1,)), ((0,), (0,))), preferred_element_type=F32)

    def bmm_nt(x, y):
        return lax.dot_general(x, y, (((2,), (2,)), ((0,), (0,))), preferred_element_type=F32)

    def block_mask(nblk, cmp):
        wi = lax.broadcasted_iota(I32, (C, nblk * C), 0)
        wj = lax.broadcasted_iota(I32, (C, nblk * C), 1) & (C - 1)
        return cmp(wi, wj)

    r_p, k2_p, v_p, a_p = pairs(r), pairs(k2), pairs(v), pairs(a)
    p_p, pinv_p, pprev_p = pairs(p), pairs(pinv), pairs(pprev)
    kk_p = pairs(kk)
    kkn = kk_p / jnp.maximum(jnp.sqrt(head_sum(kk_p * kk_p)), 1e-12)
    at = (-kkn) * pprev_p
    bt = (kkn * a_p) * pinv_p
    kt = k2_p * pinv_p
    rt = r_p * p_p
    pend_p = pairs(pend)
    pend2 = jnp.concatenate([pend_p, pend_p], axis=1)

    lhs2 = jnp.concatenate([at, rt], axis=1).astype(BF16)
    rhs4 = jnp.concatenate([halves(kt), halves(bt)], axis=1).astype(BF16)
    gc = bmm_nt(lhs2, rhs4)
    strict2 = block_mask(2, lambda i_, j_: i_ > j_)
    incl4 = block_mask(4, lambda i_, j_: i_ >= j_)
    a_ak = jnp.where(strict2, gc[:, 0:C, 0:2 * C], 0.0)
    nmat = jnp.where(strict2, gc[:, 0:C, 2 * C:4 * C], 0.0)
    a_rkb = jnp.where(incl4, gc[:, C:2 * C, :], 0.0)

    g0 = st_ref[...]
    sg = bmm_nt(lhs2, g0.astype(BF16))
    vm2 = halves(v_p).astype(BF16)
    u = sg[:, 0:C] + bmm(a_ak.astype(BF16), vm2)
    pw = nmat
    u = u + bmm(pw.astype(BF16), halves(u).astype(BF16))
    n = 1
    while 2 * n < C:
        pw = bmm(pw.astype(BF16), halves(pw).astype(BF16))
        u = u + bmm(pw.astype(BF16), halves(u).astype(BF16))
        n *= 2
    um2b = halves(u).astype(BF16)
    y = sg[:, C:2 * C] + bmm(a_rkb.astype(BF16), jnp.concatenate([vm2, um2b], axis=1))
    uv = jnp.concatenate([u, v_p], axis=1).astype(BF16)
    bkh = (jnp.concatenate([bt, kt], axis=1) * pend2).astype(BF16)
    upd = lax.dot_general(uv, bkh, (((1,), (1,)), ((0,), (0,))), preferred_element_type=F32)
    st_ref[...] = g0 * pend2 + jnp.where(same_head, upd, 0.0)

    inv_n = 1.0 / N
    yc = y - head_sum(y) * inv_n
    var = head_sum(yc * yc) * inv_n
    yn = yc * lax.rsqrt(var + GN_EPS) * per_pair(lng_ref) + per_pair(lnb_ref)
    bonus = head_sum(r_p * k2_p * per_pair(rk_ref)) * v_p
    out = (yn + bonus) * pairs(gate)
    for b in range(nb):
        for j in range(HP):
            o_ref[b, :, j * LANES:(j + 1) * LANES] = out[b * HP + j].astype(BF16)


def _rwkv(z, B, T, mu, w0, a0, k_k, k_a, r_k, ln_g, ln_b, w_up, a_up):
    C = RW_CHUNK
    W = RW_WIDTH
    z3 = z.reshape(B, T, z.shape[1])
    row = lambda a: a.reshape(1, -1).astype(F32)
    mu_r, mu_k, mu_v = mu[0:W], mu[W:2 * W], mu[2 * W:3 * W]
    mu_wa = mu[3 * W:3 * W + 2 * RW_LORA]
    zspec = lambda col: pl.BlockSpec((B, C, W), lambda c: (0, c, col // W))
    pspec = lambda width: pl.BlockSpec((1, width), lambda c: (0, 0))
    npairs = RW_HEADS // 2
    prow = lambda a: a.reshape(npairs, 1, LANES).astype(F32)
    ppspec = pl.BlockSpec((npairs, 1, LANES), lambda c: (0, 0, 0))
    wspec = pl.BlockSpec((RW_LORA, W), lambda c: (0, 0))
    out = pl.pallas_call(
        functools.partial(_rwkv_kernel, nb=B),
        grid=(T // C,),
        in_specs=[zspec(COL_R), zspec(COL_K), zspec(COL_V), zspec(COL_GRW),
                  pl.BlockSpec((B, C, LANES), lambda c: (0, c, COL_WA // LANES)),
                  pspec(W), pspec(W), pspec(W), pspec(LANES),
                  pspec(W), pspec(W), pspec(W), pspec(W), ppspec, ppspec, ppspec,
                  wspec, wspec],
        out_specs=pl.BlockSpec((B, C, W), lambda c: (0, c, 0)),
        out_shape=jax.ShapeDtypeStruct((B, T, W), BF16),
        scratch_shapes=[pltpu.VMEM((B, 1, W), F32), pltpu.VMEM((B, 1, W), F32), pltpu.VMEM((B, 1, W), F32),
                        pltpu.VMEM((B, 1, LANES), F32),
                        pltpu.VMEM((B * npairs, LANES, LANES), F32)],
        compiler_params=pltpu.CompilerParams(dimension_semantics=("arbitrary",),
                                             vmem_limit_bytes=VMEM_LIMIT),
        name="rwkv",
    )(z3, z3, z3, z3, z3,
      row(mu_r), row(mu_k), row(mu_v), row(mu_wa),
      row(w0), row(a0), row(k_k), row(k_a), prow(r_k), prow(ln_g), prow(ln_b),
      w_up.astype(BF16), a_up.astype(BF16))
    return out.reshape(B * T, W)


def _biastab_kernel(rb_ref, o_ref):
    o_ref[...] = jnp.zeros_like(o_ref)
    nb = NUM_BUCKETS // 2
    max_exact = nb // 2
    c = lax.broadcasted_iota(I32, (2 * QB, QB), 0)
    r = lax.broadcasted_iota(I32, (2 * QB, QB), 1)
    rel = c - QB - r
    ret = jnp.where(rel > 0, nb, 0)
    n = jnp.abs(rel)
    nf = jnp.maximum(n, 1).astype(F32)
    large = max_exact + (jnp.log(nf / max_exact) / math.log(MAX_DISTANCE / max_exact)
                         * (nb - max_exact)).astype(I32)
    large = jnp.minimum(large, nb - 1) & (NUM_BUCKETS - 1)
    bucket = ret + jnp.where(n < max_exact, n, large)
    for h in range(DS_HEADS):
        far = rb_ref[nb - 1, h]
        acc = jnp.zeros((2 * QB, QB), F32)
        for b in range(NUM_BUCKETS):
            acc = jnp.where(bucket == b, rb_ref[b, h] - far, acc)
        o_ref[SK:SK + 2 * QB, h * QB:(h + 1) * QB] = acc * LOG2E


def _biastab(rel_bias):
    return pl.pallas_call(
        _biastab_kernel,
        in_specs=[pl.BlockSpec(memory_space=pltpu.SMEM)],
        out_specs=pl.BlockSpec(memory_space=pltpu.VMEM),
        out_shape=jax.ShapeDtypeStruct((BIAS_ROWS, DS_HEADS * QB), F32),
        compiler_params=pltpu.CompilerParams(vmem_limit_bytes=VMEM_LIMIT),
        name="biastab",
    )(rel_bias.astype(F32))


def _fold_rows(x, op):
    n = x.shape[0] // SUBLANES
    accs = [x[j * SUBLANES:(j + 1) * SUBLANES] for j in range(min(4, n))]
    for j in range(4, n):
        accs[j % 4] = op(accs[j % 4], x[j * SUBLANES:(j + 1) * SUBLANES])
    while len(accs) > 1:
        accs = [op(accs[j], accs[j + 1]) for j in range(0, len(accs) - 1, 2)] + (
            [accs[-1]] if len(accs) % 2 else [])
    return accs[0]


def _dsa_kernel(ql_ref, kx_ref, gds_ref, kid_ref, ckv_ref, ckvt_ref, qg_ref, wq_ref, wuk_ref, wuvt_ref,
                tab_ref, o_ref,
                sc_ref, msk_ref, lgt_ref, acc_ref, m_ref, qat_ref, qit_ref, w_ref, out_ref, *, topk):
    i = pl.program_id(1)
    q0 = i * QB
    ntile = jnp.right_shift(q0 + (QB + SK - 1), SK.bit_length() - 1)
    R = DS_KV_RANK
    GW = DS_HEADS * QB

    ql = ql_ref[...]
    ms = jnp.mean(ql * ql, axis=-1, keepdims=True)
    qn = (ql * lax.rsqrt(ms + NORM_EPS) * qg_ref[...]).astype(BF16)
    qt = _dot(qn, wq_ref[...]).T
    for h in range(DS_HEADS):
        qh = qt[h * DS_HEAD:(h + 1) * DS_HEAD, :].astype(BF16)
        qat_ref[:, h * QB:(h + 1) * QB] = (_dot(wuk_ref[h], qh) * (DS_HEAD ** -0.5 * LOG2E)).astype(BF16)
    for pr in range(IDX_HEADS // 2):
        base = DS_WIDTH + 2 * pr * IDX_DIM
        qit_ref[pr] = jnp.concatenate([qt[base:base + IDX_DIM, :],
                                       qt[base + IDX_DIM:base + 2 * IDX_DIM, :]], axis=1).astype(BF16)
    w_ref[...] = kx_ref[...].T[IDX_DIM:IDX_DIM + IDX_HEADS, :] * (IDX_HEADS ** -0.5 * IDX_DIM ** -0.5)

    lanei = lax.broadcasted_iota(I32, (1, QB), 1)
    csh = CHUNK.bit_length() - 1
    limit = jnp.left_shift(jnp.right_shift(q0 + lanei, csh) + 1, csh)
    rowi = lax.broadcasted_iota(I32, (SK, QB), 0)

    def score_tile(kt, carry):
        off = pl.multiple_of(kt * SK, SK)
        kid = kid_ref[pl.ds(off, SK), :]
        s = jnp.zeros((SK, QB), F32)
        for pr in range(IDX_HEADS // 2):
            lg = _dot(kid, qit_ref[pr])
            s = s + w_ref[2 * pr:2 * pr + 1, :] * jnp.maximum(lg[:, 0:QB], 0.0)
            s = s + w_ref[2 * pr + 1:2 * pr + 2, :] * jnp.maximum(lg[:, QB:2 * QB], 0.0)
        adm = (off + rowi) < limit
        sc_ref[pl.ds(off, SK), :] = jnp.where(adm, s, -jnp.inf)
        return carry

    lax.fori_loop(0, ntile, score_tile, 0)

    def key_to_f32(key):
        return pltpu.bitcast(jnp.where(key < 0, key ^ 0x7FFFFFFF, key), F32)

    def count(pred):
        def body(kt, acc):
            off = pl.multiple_of(kt * SK, SK)
            sc = sc_ref[pl.ds(off, SK), :]
            return acc + _fold_rows(jnp.where(pred(sc, off), 1, 0).astype(I32), jnp.add)
        acc = lax.fori_loop(0, ntile, body, jnp.zeros((SUBLANES, QB), I32))
        return jnp.sum(acc, axis=0, keepdims=True)

    def bit_step(it, carry):
        lo, cnt_lo = carry
        cand = lo + jnp.left_shift(jnp.int32(1), 31 - it)
        cf = key_to_f32(cand)
        cnt = count(lambda sc, off: sc >= cf)
        take = cnt >= topk
        return jnp.where(take, cand, lo), jnp.where(take, cnt, cnt_lo)

    def all_settled(cnt_lo):
        return jnp.min(jnp.where((cnt_lo == topk) | (limit < topk), 1, 0))

    lo, cnt_lo = lax.fori_loop(0, SEARCH_MIN_BITS, bit_step,
                               (jnp.full((1, QB), INT_MIN, I32), jnp.full((1, QB), 2 ** 30, I32)))

    def more_bits(c):
        it, lo, cnt_lo, _ = c
        lo, cnt_lo = bit_step(it, (lo, cnt_lo))
        return it + 1, lo, cnt_lo, all_settled(cnt_lo)

    _, lo, _, _ = lax.while_loop(lambda c: (c[0] < 32) & (c[3] == 0), more_bits,
                                 (jnp.int32(SEARCH_MIN_BITS), lo, cnt_lo, all_settled(cnt_lo)))
    has_thr = lo > KEY_NEG_INF
    thr = key_to_f32(jnp.maximum(lo, KEY_NEG_INF))
    cnt_gt = count(lambda sc, off: sc > thr)
    cnt_eq = count(lambda sc, off: sc == thr)
    nbits = max(1, (sc_ref.shape[0] - 1).bit_length())
    tied = (cnt_gt + cnt_eq > topk) & has_thr

    def search_cut():
        def idx_step(it, m):
            cand = m + jnp.left_shift(jnp.int32(1), nbits - 1 - it)
            cnt = cnt_gt + count(lambda sc, off: (sc == thr) & ((off + rowi) < cand))
            return jnp.where(cnt < topk, cand, m)
        return lax.fori_loop(0, nbits, idx_step, jnp.zeros((1, QB), I32))

    cut = lax.cond(jnp.max(jnp.where(tied, 1, 0)) > 0, search_cut,
                   lambda: jnp.full((1, QB), sc_ref.shape[0], I32))

    def mask_tile(kt, carry):
        off = pl.multiple_of(kt * SK, SK)
        sc = sc_ref[pl.ds(off, SK), :]
        sel = (sc > thr) | ((sc == thr) & ((off + rowi) <= cut) & has_thr)
        msk_ref[pl.ds(off, SK), :] = jnp.where(sel, 0.0, MASK_NEG).astype(F32)
        return carry

    lax.fori_loop(0, ntile, mask_tile, 0)

    m_ref[...] = jnp.full_like(m_ref, MASK_NEG)
    acc_ref[...] = jnp.zeros_like(acc_ref)
    near_lo = q0 - QB

    def attend(off, bias_off):
        rows = pl.ds(off, SK)
        s = _dot(ckv_ref[rows, :], qat_ref[...])
        mk = msk_ref[rows, :]
        m_old = m_ref[...]
        tmax = []
        for h in range(DS_HEADS):
            cs = slice(h * QB, (h + 1) * QB)
            t = s[:, cs] + mk
            if bias_off is not None:
                t = t + tab_ref[pl.ds(bias_off, SK), cs]
            lgt_ref[:, cs] = t
            tmax.append(jnp.max(_fold_rows(t, jnp.maximum), axis=0, keepdims=True))
        m_new = jnp.maximum(m_old, jnp.concatenate(tmax, axis=1))
        m_ref[...] = m_new
        pr = jnp.exp2(lgt_ref[...] - m_new).astype(BF16)
        acc_ref[...] = acc_ref[...] * jnp.exp2(m_old - m_new) + _dot(ckvt_ref[:, rows], pr)

    def far_tile(kt, c):
        attend(pl.multiple_of(kt * SK, SK), None)
        return c

    def edge_tile(kt, c):
        off = pl.multiple_of(kt * SK, SK)
        attend(off, pl.multiple_of(jnp.maximum(SK + off - near_lo, 0), QB))
        return c

    nfar = jnp.maximum(ntile - 2, 0)
    lax.fori_loop(0, nfar, far_tile, 0)
    lax.fori_loop(nfar, ntile, edge_tile, 0)

    o_lat = acc_ref[0:R, :] * (1.0 / acc_ref[R:R + 1, :])
    for h in range(DS_HEADS):
        out_ref[h * DS_HEAD:(h + 1) * DS_HEAD, :] = _dot(
            wuvt_ref[h], o_lat[:, h * QB:(h + 1) * QB].astype(BF16))

    g = gds_ref[...]
    o_ref[...] = (out_ref[...].T * (g * _sigmoid(g))).astype(BF16)


def _dsa(z, ckv, kid, B, T, q_norm_g, w_uq, w_uk, w_uv, iw_q, tab):
    nq = T // QB
    topk = min(TOPK_MAX, T // 4)
    R = DS_KV_RANK
    wq = jnp.concatenate([w_uq, iw_q], axis=1).astype(BF16)
    wuk_h = jnp.transpose(w_uk, (1, 0, 2)).astype(BF16)
    wuv_t = jnp.transpose(w_uv, (1, 2, 0)).astype(BF16)
    ckv_t = jnp.concatenate([jnp.swapaxes(ckv.reshape(B, T, R), 1, 2),
                             jnp.ones((B, ONES_ROWS, T), BF16)], axis=1).reshape(B * (R + ONES_ROWS), T)
    const2 = lambda b, i: (0, 0)
    const3 = lambda b, i: (0, 0, 0)
    resident = dict(pipeline_mode=pl.Buffered(1))
    return pl.pallas_call(
        functools.partial(_dsa_kernel, topk=topk),
        grid=(B, nq),
        in_specs=[pl.BlockSpec((QB, DS_Q_RANK), lambda b, i: (b * nq + i, COL_Q // DS_Q_RANK)),
                  pl.BlockSpec((QB, LANES), lambda b, i: (b * nq + i, COL_KX // LANES)),
                  pl.BlockSpec((QB, DS_WIDTH), lambda b, i: (b * nq + i, COL_GDS // DS_WIDTH)),
                  pl.BlockSpec((T, IDX_DIM), lambda b, i: (b, 0), **resident),
                  pl.BlockSpec((T, R), lambda b, i: (b, 0), **resident),
                  pl.BlockSpec((R + ONES_ROWS, T), lambda b, i: (b, 0), **resident),
                  pl.BlockSpec((1, DS_Q_RANK), const2),
                  pl.BlockSpec((DS_Q_RANK, 2 * DS_WIDTH), const2, **resident),
                  pl.BlockSpec((DS_HEADS, R, DS_HEAD), const3, **resident),
                  pl.BlockSpec((DS_HEADS, DS_HEAD, R), const3, **resident),
                  pl.BlockSpec((BIAS_ROWS, DS_HEADS * QB), const2, **resident)],
        out_specs=pl.BlockSpec((QB, DS_WIDTH), lambda b, i: (b * nq + i, 0)),
        out_shape=jax.ShapeDtypeStruct((B * T, DS_WIDTH), BF16),
        scratch_shapes=[pltpu.VMEM((T, QB), F32),
                        pltpu.VMEM((T, QB), F32),
                        pltpu.VMEM((SK, DS_HEADS * QB), F32),
                        pltpu.VMEM((R + ONES_ROWS, DS_HEADS * QB), F32),
                        pltpu.VMEM((1, DS_HEADS * QB), F32),
                        pltpu.VMEM((R, DS_HEADS * QB), BF16),
                        pltpu.VMEM((IDX_HEADS // 2, IDX_DIM, 2 * QB), BF16),
                        pltpu.VMEM((IDX_HEADS, QB), F32),
                        pltpu.VMEM((DS_WIDTH, QB), F32)],
        compiler_params=pltpu.CompilerParams(dimension_semantics=("parallel", "arbitrary"),
                                             vmem_limit_bytes=VMEM_LIMIT),
        name="dsa",
    )(z, z, z, kid, ckv, ckv_t, q_norm_g.reshape(1, -1).astype(F32), wq, wuk_h, wuv_t, tab)


def _post_kernel(x_ref, a1_ref, a2_ref, p_ref, w1_ref, w2_ref, pw_ref, gw_ref, fg_ref, o_ref):
    h = x_ref[...] + _dot(a1_ref[...], w1_ref[...]) + _dot(a2_ref[...], w2_ref[...])
    e = _dot(p_ref[...].astype(BF16), pw_ref[...])
    gate = _sigmoid(_dot(h.astype(BF16), gw_ref[...]))
    h2 = h + e * gate
    ms = jnp.mean(h2 * h2, axis=-1, keepdims=True)
    o_ref[...] = h2 * lax.rsqrt(ms + NORM_EPS) * fg_ref[...]


def _post(x2, o_rw, o_ds, p2, w_out, ple_w, gate_w, final_g, tm=256):
    m, d = x2.shape
    kh = o_rw.shape[1]
    pd = p2.shape[1]
    resident = dict(pipeline_mode=pl.Buffered(1))
    return pl.pallas_call(
        _post_kernel,
        grid=(m // tm,),
        in_specs=[pl.BlockSpec((tm, d), lambda i: (i, 0)),
                  pl.BlockSpec((tm, kh), lambda i: (i, 0)),
                  pl.BlockSpec((tm, kh), lambda i: (i, 0)),
                  pl.BlockSpec((tm, pd), lambda i: (i, 0)),
                  pl.BlockSpec((kh, d), lambda i: (0, 0), **resident),
                  pl.BlockSpec((kh, d), lambda i: (1, 0), **resident),
                  pl.BlockSpec((pd, d), lambda i: (0, 0), **resident),
                  pl.BlockSpec((d, d), lambda i: (0, 0), **resident),
                  pl.BlockSpec((1, d), lambda i: (0, 0))],
        out_specs=pl.BlockSpec((tm, d), lambda i: (i, 0)),
        out_shape=jax.ShapeDtypeStruct((m, d), F32),
        compiler_params=pltpu.CompilerParams(dimension_semantics=("parallel",),
                                             vmem_limit_bytes=VMEM_LIMIT),
        name="post",
    )(x2, o_rw, o_ds, p2, w_out, w_out, ple_w, gate_w, final_g)


def _split_w_in(w):
    w = w.astype(BF16)
    return w, w[:, w.shape[1] - DS_WIDTH:]


def kernel(x, p, w_in, norm_g, rw_mu, rw_w0, rw_w_up, rw_a0, rw_a_up, rw_k_k, rw_k_a, rw_r_k, rw_ln_g, rw_ln_b, ds_q_norm_g, ds_kv_norm_g, idx_k_norm_g, ds_w_uq, ds_w_uk, ds_w_uv, idx_w_q, rel_bias, w_out, ple_w, ple_gate_w, final_g):
    B, T, D = x.shape
    depth = w_in.shape[0]
    assert depth == 1 and T % SK == 0 and T % RW_CHUNK == 0 and (B * T) % 512 == 0
    h = x.reshape(B * T, D)
    tab = _biastab(rel_bias)
    for i in range(depth):
        z = _inproj(h, norm_g[i].reshape(1, D), *_split_w_in(w_in[i]))
        ckv, kid = _kvprep(z, ds_kv_norm_g[i].reshape(1, -1), idx_k_norm_g[i].reshape(1, -1))
        o_rw = _rwkv(z, B, T, rw_mu[i], rw_w0[i], rw_a0[i], rw_k_k[i], rw_k_a[i],
                     rw_r_k[i].reshape(-1), rw_ln_g[i], rw_ln_b[i], rw_w_up[i], rw_a_up[i])
        o_ds = _dsa(z, ckv, kid, B, T, ds_q_norm_g[i], ds_w_uq[i], ds_w_uk[i], ds_w_uv[i],
                    idx_w_q[i], tab)
        h = _post(h, o_rw, o_ds, p[i].reshape(B * T, -1), w_out[i].astype(BF16),
                  ple_w[i].astype(BF16), ple_gate_w[i].astype(BF16), final_g.reshape(1, D))
    return h.reshape(B, T, D)
```

```python
import functools
import math

import jax
import jax.numpy as jnp
from jax import lax
from jax.experimental import pallas as pl
from jax.experimental.pallas import tpu as pltpu

F32 = jnp.float32
BF16 = jnp.bfloat16
I32 = jnp.int32

RW_WIDTH = 1024
RW_HEAD = 64
RW_HEADS = 16
RW_LORA = 64
DS_WIDTH = 1024
DS_HEAD = 64
DS_HEADS = 16
DS_Q_RANK = 384
DS_KV_RANK = 256
IDX_HEADS = 16
IDX_DIM = 64
TOPK_MAX = 256
CHUNK = 64
NUM_BUCKETS = 32
MAX_DISTANCE = 128
NORM_EPS = 1e-6
GN_EPS = 64e-5

COL_R, COL_K, COL_V, COL_GRW, COL_GDS = 0, 1024, 2048, 3072, 4096
COL_KV = 5120
COL_Q = 5376
COL_WA = 5760
COL_KX = 5888
Z_WIDTH = 6016

LANES = 128
SUBLANES = 8
QB = 128
SK = 512
BIAS_ROWS = 2 * SK + 2 * QB
RW_CHUNK = 64
INT_MIN = -2 ** 31
KEY_NEG_INF = -2139095041
SEARCH_MIN_BITS = 23
SEARCH_STEP_BITS = 3
assert (32 - SEARCH_MIN_BITS) % SEARCH_STEP_BITS == 0
MASK_NEG = -1e30
LOG2E = 1.4426950408889634
ONES_ROWS = 16
VMEM_LIMIT = 52 * 1024 * 1024


def _sigmoid(x):
    return 1.0 / (1.0 + jnp.exp(-x))


def _dot(a, b):
    return jnp.dot(a, b, preferred_element_type=F32)


def _dot_nt(a, b):
    return lax.dot_general(a, b, (((1,), (1,)), ((), ())), preferred_element_type=F32)


def _dot_tn(a, b):
    return lax.dot_general(a, b, (((0,), (0,)), ((), ())), preferred_element_type=F32)


def _inproj_plan():
    s_wa = 3 * RW_WIDTH
    s_grw = s_wa + 2 * RW_LORA
    s_q = s_grw + RW_WIDTH
    s_kv = s_q + DS_Q_RANK
    s_kx = s_kv + DS_KV_RANK
    return [(COL_R, 0, 3 * RW_WIDTH), (COL_GRW, s_grw, RW_WIDTH), (COL_GDS, None, DS_WIDTH),
            (COL_KV, s_kv, DS_KV_RANK), (COL_Q, s_q, DS_Q_RANK), (COL_WA, s_wa, LANES),
            (COL_KX, s_kx, LANES)]


def _inproj_kernel(x_ref, g_ref, w_ref, wg_ref, o_ref, *, tn):
    x = x_ref[...]
    ms = jnp.mean(x * x, axis=-1, keepdims=True)
    xn = (x * lax.rsqrt(ms + NORM_EPS) * g_ref[...]).astype(BF16)
    for dst, src, width in _inproj_plan():
        for j in range(0, width, tn):
            w = min(tn, width - j)
            wt = wg_ref[:, j:j + w] if src is None else w_ref[:, src + j:src + j + w]
            o_ref[:, dst + j:dst + j + w] = _dot(xn, wt)


def _inproj(x2, g, w, w_gds, tm=256, tn=1024):
    m, d = x2.shape
    resident = dict(pipeline_mode=pl.Buffered(1))
    return pl.pallas_call(
        functools.partial(_inproj_kernel, tn=tn),
        grid=(m // tm,),
        in_specs=[pl.BlockSpec((tm, d), lambda i: (i, 0)),
                  pl.BlockSpec((1, d), lambda i: (0, 0)),
                  pl.BlockSpec(w.shape, lambda i: (0, 0), **resident),
                  pl.BlockSpec(w_gds.shape, lambda i: (0, 0), **resident)],
        out_specs=pl.BlockSpec((tm, Z_WIDTH), lambda i: (i, 0)),
        out_shape=jax.ShapeDtypeStruct((m, Z_WIDTH), F32),
        compiler_params=pltpu.CompilerParams(dimension_semantics=("parallel",),
                                             vmem_limit_bytes=VMEM_LIMIT),
        name="inproj",
    )(x2, g, w, w_gds)


def _kvprep_kernel(kv_ref, kx_ref, gkv_ref, gik_ref, ckv_ref, kid_ref):
    kv = kv_ref[...]
    ms = jnp.mean(kv * kv, axis=-1, keepdims=True)
    ckv_ref[...] = (kv * lax.rsqrt(ms + NORM_EPS) * gkv_ref[...]).astype(BF16)
    ki = kx_ref[:, 0:IDX_DIM]
    ms2 = jnp.mean(ki * ki, axis=-1, keepdims=True)
    kid_ref[...] = (ki * lax.rsqrt(ms2 + NORM_EPS) * gik_ref[...]).astype(BF16)


def _kvprep(z, gkv, gik, tm=512):
    m = z.shape[0]
    return pl.pallas_call(
        _kvprep_kernel,
        grid=(m // tm,),
        in_specs=[pl.BlockSpec((tm, DS_KV_RANK), lambda i: (i, COL_KV // DS_KV_RANK)),
                  pl.BlockSpec((tm, LANES), lambda i: (i, COL_KX // LANES)),
                  pl.BlockSpec((1, DS_KV_RANK), lambda i: (0, 0)),
                  pl.BlockSpec((1, IDX_DIM), lambda i: (0, 0))],
        out_specs=[pl.BlockSpec((tm, DS_KV_RANK), lambda i: (i, 0)),
                   pl.BlockSpec((tm, IDX_DIM), lambda i: (i, 0))],
        out_shape=[jax.ShapeDtypeStruct((m, DS_KV_RANK), BF16),
                   jax.ShapeDtypeStruct((m, IDX_DIM), BF16)],
        compiler_params=pltpu.CompilerParams(dimension_semantics=("parallel",)),
        name="kvprep",
    )(z, z, gkv, gik)


def _rwkv_kernel(r_ref, k_ref, v_ref, g_ref, wa_ref,
                 mur_ref, muk_ref, muv_ref, muwa_ref,
                 w0_ref, a0_ref, kk_ref, ka_ref, rk_ref, lng_ref, lnb_ref,
                 wup_ref, aup_ref,
                 o_ref,
                 pr_ref, pk_ref, pv_ref, pwa_ref, st_ref, *, nb):
    C = RW_CHUNK
    N = RW_HEAD

    @pl.when(pl.program_id(0) == 0)
    def _():
        pr_ref[...] = jnp.zeros_like(pr_ref)
        pk_ref[...] = jnp.zeros_like(pk_ref)
        pv_ref[...] = jnp.zeros_like(pv_ref)
        pwa_ref[...] = jnp.zeros_like(pwa_ref)
        st_ref[...] = jnp.zeros_like(st_ref)

    row = lax.broadcasted_iota(I32, (nb * C, 1), 0)

    def shift(ref, prev_ref, mu_ref):
        z = ref[...].reshape(nb * C, ref.shape[2])
        zp = pltpu.roll(z, 1, 0)
        for b in range(nb):
            zp = jnp.where(row == b * C, prev_ref[b], zp)
            prev_ref[b] = z[(b + 1) * C - 1:(b + 1) * C, :]
        return z + mu_ref[...] * (zp - z)

    r = shift(r_ref, pr_ref, mur_ref)
    k = shift(k_ref, pk_ref, muk_ref)
    v = shift(v_ref, pv_ref, muv_ref)
    wa = shift(wa_ref, pwa_ref, muwa_ref)
    wd = wa[:, 0:RW_LORA]
    ad = wa[:, RW_LORA:2 * RW_LORA]

    wl = w0_ref[...] + _dot(jnp.tanh(wd).astype(BF16), wup_ref[...])
    nwl = -wl
    softplus = jnp.maximum(nwl, 0.0) + jnp.log(1.0 + jnp.exp(-jnp.abs(nwl)))
    w_log = -softplus - 0.5
    lw = -jnp.exp(w_log)
    a = _sigmoid(a0_ref[...] + _dot(ad.astype(BF16), aup_ref[...]))
    kk = k * kk_ref[...]
    k2 = k * (1.0 + (a - 1.0) * ka_ref[...])

    ti = lax.broadcasted_iota(I32, (C, C), 0)
    tj = lax.broadcasted_iota(I32, (C, C), 1)
    incl = ti >= tj
    strict = ti > tj
    tri = jnp.where(incl, 1.0, 0.0).astype(F32)

    def per_row(mat, x):
        return jnp.concatenate(
            [jnp.dot(mat, x[b * C:(b + 1) * C], preferred_element_type=F32,
                     precision=lax.Precision.HIGHEST) for b in range(nb)], axis=0)

    cum = per_row(tri, lw)
    p = jnp.exp(cum)
    pinv = jnp.exp(-cum)
    pprev = jnp.exp(cum - lw)
    tot = per_row(jnp.ones((C, C), F32), lw)
    pend = jnp.exp(tot)

    g = g_ref[...].reshape(nb * C, RW_WIDTH)
    gate = g * _sigmoid(g)

    HP = RW_HEADS // 2
    NP = nb * HP

    def pairs(x):
        return jnp.stack([x[b * C:(b + 1) * C, j * LANES:(j + 1) * LANES]
                          for b in range(nb) for j in range(HP)], axis=0)

    def per_pair(ref):
        return jnp.concatenate([ref[...]] * nb, axis=0)

    lane = lax.broadcasted_iota(I32, (1, 1, LANES), 2)
    m_lo = jnp.where(lane < N, 1.0, 0.0).astype(BF16)
    m_hi = jnp.where(lane < N, 0.0, 1.0).astype(BF16)
    bi = lax.broadcasted_iota(I32, (LANES, LANES), 0)
    bj = lax.broadcasted_iota(I32, (LANES, LANES), 1)
    same_head = (bi < N) == (bj < N)
    ones_bd = jnp.where(same_head, 1.0, 0.0).astype(BF16)

    def head_sum(x):
        return _dot(x.reshape(NP * C, LANES).astype(BF16), ones_bd).reshape(NP, C, LANES)

    def halves(x):
        xb = x.astype(BF16)
        return jnp.concatenate([xb * m_lo, xb * m_hi], axis=1)

    def bmm(x, y):
        return lax.dot_general(x, y, (((2,), (1,)), ((0,), (0,))), preferred_element_type=F32)

    def bmm_nt(x, y):
        return lax.dot_general(x, y, (((2,), (2,)), ((0,), (0,))), preferred_element_type=F32)

    def block_mask(nblk, cmp):
        wi = lax.broadcasted_iota(I32, (C, nblk * C), 0)
        wj = lax.broadcasted_iota(I32, (C, nblk * C), 1) & (C - 1)
        return cmp(wi, wj)

    r_p, k2_p, v_p, a_p = pairs(r), pairs(k2), pairs(v), pairs(a)
    p_p, pinv_p, pprev_p = pairs(p), pairs(pinv), pairs(pprev)
    kk_p = pairs(kk)
    kkn = kk_p * lax.rsqrt(jnp.maximum(head_sum(kk_p * kk_p), 1e-24))
    at = (-kkn) * pprev_p
    bt = (kkn * a_p) * pinv_p
    kt = k2_p * pinv_p
    rt = r_p * p_p
    pend_p = pairs(pend)
    pend2 = jnp.concatenate([pend_p, pend_p], axis=1)

    lhs2 = jnp.concatenate([at, rt], axis=1).astype(BF16)
    rhs4 = jnp.concatenate([halves(kt), halves(bt)], axis=1)
    gc = bmm_nt(lhs2, rhs4)
    strict2 = block_mask(2, lambda i_, j_: i_ > j_)
    incl4 = block_mask(4, lambda i_, j_: i_ >= j_)
    a_ak = jnp.where(strict2, gc[:, 0:C, 0:2 * C], 0.0)
    nmat = jnp.where(strict2, gc[:, 0:C, 2 * C:4 * C], 0.0)
    a_rkb = jnp.where(incl4, gc[:, C:2 * C, :], 0.0)

    g0 = st_ref[...]
    sg = bmm_nt(lhs2, g0.astype(BF16))
    vm2 = halves(v_p)
    u = sg[:, 0:C] + bmm(a_ak.astype(BF16), vm2)
    pw = nmat.astype(BF16)
    u = u + bmm(pw, halves(u))
    n = 1
    while 2 * n < C:
        pw = bmm(pw, halves(pw)).astype(BF16)
        u = u + bmm(pw, halves(u))
        n *= 2
    um2b = halves(u)
    y = sg[:, C:2 * C] + bmm(a_rkb.astype(BF16), jnp.concatenate([vm2, um2b], axis=1))
    uv = jnp.concatenate([u, v_p], axis=1).astype(BF16)
    bkh = (jnp.concatenate([bt, kt], axis=1) * pend2).astype(BF16)
    upd = lax.dot_general(uv, bkh, (((1,), (1,)), ((0,), (0,))), preferred_element_type=F32)
    st_ref[...] = g0 * pend2 + jnp.where(same_head, upd, 0.0)

    inv_n = 1.0 / N
    yc = y - head_sum(y) * inv_n
    var = head_sum(yc * yc) * inv_n
    yn = yc * lax.rsqrt(var + GN_EPS) * per_pair(lng_ref) + per_pair(lnb_ref)
    bonus = head_sum(r_p * k2_p * per_pair(rk_ref)) * v_p
    out = (yn + bonus) * pairs(gate)
    for b in range(nb):
        for j in range(HP):
            o_ref[b, :, j * LANES:(j + 1) * LANES] = out[b * HP + j].astype(BF16)


def _rwkv(z, B, T, mu, w0, a0, k_k, k_a, r_k, ln_g, ln_b, w_up, a_up):
    C = RW_CHUNK
    W = RW_WIDTH
    z3 = z.reshape(B, T, z.shape[1])
    row = lambda a: a.reshape(1, -1).astype(F32)
    mu_r, mu_k, mu_v = mu[0:W], mu[W:2 * W], mu[2 * W:3 * W]
    mu_wa = mu[3 * W:3 * W + 2 * RW_LORA]
    zspec = lambda col: pl.BlockSpec((B, C, W), lambda c: (0, c, col // W))
    pspec = lambda width: pl.BlockSpec((1, width), lambda c: (0, 0))
    npairs = RW_HEADS // 2
    prow = lambda a: a.reshape(npairs, 1, LANES).astype(F32)
    ppspec = pl.BlockSpec((npairs, 1, LANES), lambda c: (0, 0, 0))
    wspec = pl.BlockSpec((RW_LORA, W), lambda c: (0, 0))
    out = pl.pallas_call(
        functools.partial(_rwkv_kernel, nb=B),
        grid=(T // C,),
        in_specs=[zspec(COL_R), zspec(COL_K), zspec(COL_V), zspec(COL_GRW),
                  pl.BlockSpec((B, C, LANES), lambda c: (0, c, COL_WA // LANES)),
                  pspec(W), pspec(W), pspec(W), pspec(LANES),
                  pspec(W), pspec(W), pspec(W), pspec(W), ppspec, ppspec, ppspec,
                  wspec, wspec],
        out_specs=pl.BlockSpec((B, C, W), lambda c: (0, c, 0)),
        out_shape=jax.ShapeDtypeStruct((B, T, W), BF16),
        scratch_shapes=[pltpu.VMEM((B, 1, W), F32), pltpu.VMEM((B, 1, W), F32), pltpu.VMEM((B, 1, W), F32),
                        pltpu.VMEM((B, 1, LANES), F32),
                        pltpu.VMEM((B * npairs, LANES, LANES), F32)],
        compiler_params=pltpu.CompilerParams(dimension_semantics=("arbitrary",),
                                             vmem_limit_bytes=VMEM_LIMIT),
        name="rwkv",
    )(z3, z3, z3, z3, z3,
      row(mu_r), row(mu_k), row(mu_v), row(mu_wa),
      row(w0), row(a0), row(k_k), row(k_a), prow(r_k), prow(ln_g), prow(ln_b),
      w_up.astype(BF16), a_up.astype(BF16))
    return out.reshape(B * T, W)


def _biastab_kernel(rb_ref, o_ref):
    o_ref[...] = jnp.zeros_like(o_ref)
    nb = NUM_BUCKETS // 2
    max_exact = nb // 2
    c = lax.broadcasted_iota(I32, (2 * QB, QB), 0)
    r = lax.broadcasted_iota(I32, (2 * QB, QB), 1)
    rel = c - QB - r
    ret = jnp.where(rel > 0, nb, 0)
    n = jnp.abs(rel)
    nf = jnp.maximum(n, 1).astype(F32)
    large = max_exact + (jnp.log(nf / max_exact) / math.log(MAX_DISTANCE / max_exact)
                         * (nb - max_exact)).astype(I32)
    large = jnp.minimum(large, nb - 1) & (NUM_BUCKETS - 1)
    bucket = ret + jnp.where(n < max_exact, n, large)
    for h in range(DS_HEADS):
        far = rb_ref[nb - 1, h]
        acc = jnp.zeros((2 * QB, QB), F32)
        for b in range(NUM_BUCKETS):
            acc = jnp.where(bucket == b, rb_ref[b, h] - far, acc)
        o_ref[SK:SK + 2 * QB, h * QB:(h + 1) * QB] = acc * LOG2E


def _biastab(rel_bias):
    return pl.pallas_call(
        _biastab_kernel,
        in_specs=[pl.BlockSpec(memory_space=pltpu.SMEM)],
        out_specs=pl.BlockSpec(memory_space=pltpu.VMEM),
        out_shape=jax.ShapeDtypeStruct((BIAS_ROWS, DS_HEADS * QB), F32),
        compiler_params=pltpu.CompilerParams(vmem_limit_bytes=VMEM_LIMIT),
        name="biastab",
    )(rel_bias.astype(F32))


def _fold_rows(x, op):
    n = x.shape[0] // SUBLANES
    accs = [x[j * SUBLANES:(j + 1) * SUBLANES] for j in range(min(4, n))]
    for j in range(4, n):
        accs[j % 4] = op(accs[j % 4], x[j * SUBLANES:(j + 1) * SUBLANES])
    while len(accs) > 1:
        accs = [op(accs[j], accs[j + 1]) for j in range(0, len(accs) - 1, 2)] + (
            [accs[-1]] if len(accs) % 2 else [])
    return accs[0]


def _dsa_kernel(ql_ref, kx_ref, gds_ref, kid_ref, ckv_ref, ckvt_ref, qg_ref, wq_ref, wuk_ref, wuvt_ref,
                tab_ref, o_ref,
                sc_ref, msk_ref, lgt_ref, acc_ref, m_ref, qat_ref, qit_ref, w_ref, out_ref, *, topk):
    i = pl.program_id(1)
    q0 = i * QB
    ntile = jnp.right_shift(q0 + (QB + SK - 1), SK.bit_length() - 1)
    R = DS_KV_RANK
    GW = DS_HEADS * QB

    ql = ql_ref[...]
    ms = jnp.mean(ql * ql, axis=-1, keepdims=True)
    qn = (ql * lax.rsqrt(ms + NORM_EPS) * qg_ref[...]).astype(BF16)
    qt = _dot(qn, wq_ref[...]).T
    for h in range(DS_HEADS):
        qh = qt[h * DS_HEAD:(h + 1) * DS_HEAD, :].astype(BF16)
        qat_ref[:, h * QB:(h + 1) * QB] = (_dot(wuk_ref[h], qh) * (DS_HEAD ** -0.5 * LOG2E)).astype(BF16)
    for pr in range(IDX_HEADS // 2):
        base = DS_WIDTH + 2 * pr * IDX_DIM
        qit_ref[pr] = jnp.concatenate([qt[base:base + IDX_DIM, :],
                                       qt[base + IDX_DIM:base + 2 * IDX_DIM, :]], axis=1).astype(BF16)
    w_ref[...] = kx_ref[...].T[IDX_DIM:IDX_DIM + IDX_HEADS, :] * (IDX_HEADS ** -0.5 * IDX_DIM ** -0.5)

    lanei = lax.broadcasted_iota(I32, (1, QB), 1)
    csh = CHUNK.bit_length() - 1
    limit = jnp.left_shift(jnp.right_shift(q0 + lanei, csh) + 1, csh)
    rowi = lax.broadcasted_iota(I32, (SK, QB), 0)

    def score_tile(kt, carry):
        off = pl.multiple_of(kt * SK, SK)
        kid = kid_ref[pl.ds(off, SK), :]
        s = jnp.zeros((SK, QB), F32)
        for pr in range(IDX_HEADS // 2):
            lg = _dot(kid, qit_ref[pr])
            s = s + w_ref[2 * pr:2 * pr + 1, :] * jnp.maximum(lg[:, 0:QB], 0.0)
            s = s + w_ref[2 * pr + 1:2 * pr + 2, :] * jnp.maximum(lg[:, QB:2 * QB], 0.0)
        adm = (off + rowi) < limit
        sc_ref[pl.ds(off, SK), :] = jnp.where(adm, s, -jnp.inf)
        return carry

    lax.fori_loop(0, ntile, score_tile, 0)

    def key_to_f32(key):
        return pltpu.bitcast(jnp.where(key < 0, key ^ 0x7FFFFFFF, key), F32)

    def count(pred):
        def body(kt, acc):
            off = pl.multiple_of(kt * SK, SK)
            sc = sc_ref[pl.ds(off, SK), :]
            return acc + _fold_rows(jnp.where(pred(sc, off), 1, 0).astype(I32), jnp.add)
        acc = lax.fori_loop(0, ntile, body, jnp.zeros((SUBLANES, QB), I32))
        return jnp.sum(acc, axis=0, keepdims=True)

    def bit_step(it, carry):
        lo, cnt_lo = carry
        cand = lo + jnp.left_shift(jnp.int32(1), 31 - it)
        cf = key_to_f32(cand)
        cnt = count(lambda sc, off: sc >= cf)
        take = cnt >= topk
        return jnp.where(take, cand, lo), jnp.where(take, cnt, cnt_lo)

    def all_settled(cnt_lo):
        return jnp.min(jnp.where((cnt_lo == topk) | (limit < topk), 1, 0))

    lo, cnt_lo = lax.fori_loop(0, SEARCH_MIN_BITS, bit_step,
                               (jnp.full((1, QB), INT_MIN, I32), jnp.full((1, QB), 2 ** 30, I32)))

    def more_bits(c):
        it, lo, cnt_lo, _ = c
        lo, cnt_lo = lax.fori_loop(it, it + SEARCH_STEP_BITS, bit_step, (lo, cnt_lo))
        return it + SEARCH_STEP_BITS, lo, cnt_lo, all_settled(cnt_lo)

    _, lo, _, _ = lax.while_loop(lambda c: (c[0] < 32) & (c[3] == 0), more_bits,
                                 (jnp.int32(SEARCH_MIN_BITS), lo, cnt_lo, all_settled(cnt_lo)))
    has_thr = lo > KEY_NEG_INF
    thr = key_to_f32(jnp.maximum(lo, KEY_NEG_INF))
    cnt_gt = count(lambda sc, off: sc > thr)
    cnt_eq = count(lambda sc, off: sc == thr)
    nbits = max(1, (sc_ref.shape[0] - 1).bit_length())
    tied = (cnt_gt + cnt_eq > topk) & has_thr

    def search_cut():
        def idx_step(it, m):
            cand = m + jnp.left_shift(jnp.int32(1), nbits - 1 - it)
            cnt = cnt_gt + count(lambda sc, off: (sc == thr) & ((off + rowi) < cand))
            return jnp.where(cnt < topk, cand, m)
        return lax.fori_loop(0, nbits, idx_step, jnp.zeros((1, QB), I32))

    cut = lax.cond(jnp.max(jnp.where(tied, 1, 0)) > 0, search_cut,
                   lambda: jnp.full((1, QB), sc_ref.shape[0], I32))

    def mask_tile(kt, carry):
        off = pl.multiple_of(kt * SK, SK)
        sc = sc_ref[pl.ds(off, SK), :]
        sel = (sc > thr) | ((sc == thr) & ((off + rowi) <= cut) & has_thr)
        msk_ref[pl.ds(off, SK), :] = jnp.where(sel, 0.0, MASK_NEG).astype(F32)
        return carry

    lax.fori_loop(0, ntile, mask_tile, 0)

    m_ref[...] = jnp.full_like(m_ref, MASK_NEG)
    acc_ref[...] = jnp.zeros_like(acc_ref)
    near_lo = q0 - QB

    def attend(off, bias_off):
        rows = pl.ds(off, SK)
        s = _dot(ckv_ref[rows, :], qat_ref[...])
        mk = msk_ref[rows, :]
        m_old = m_ref[...]
        tmax = []
        for h in range(DS_HEADS):
            cs = slice(h * QB, (h + 1) * QB)
            t = s[:, cs] + mk
            if bias_off is not None:
                t = t + tab_ref[pl.ds(bias_off, SK), cs]
            lgt_ref[:, cs] = t
            tmax.append(jnp.max(_fold_rows(t, jnp.maximum), axis=0, keepdims=True))
        m_new = jnp.maximum(m_old, jnp.concatenate(tmax, axis=1))
        m_ref[...] = m_new
        pr = jnp.exp2(lgt_ref[...] - m_new).astype(BF16)
        acc_ref[...] = acc_ref[...] * jnp.exp2(m_old - m_new) + _dot(ckvt_ref[:, rows], pr)

    def far_tile(kt, c):
        attend(pl.multiple_of(kt * SK, SK), None)
        return c

    def edge_tile(kt, c):
        off = pl.multiple_of(kt * SK, SK)
        attend(off, pl.multiple_of(jnp.maximum(SK + off - near_lo, 0), QB))
        return c

    nfar = jnp.maximum(ntile - 2, 0)
    lax.fori_loop(0, nfar, far_tile, 0)
    lax.fori_loop(nfar, ntile, edge_tile, 0)

    o_lat = acc_ref[0:R, :] * (1.0 / acc_ref[R:R + 1, :])
    for h in range(DS_HEADS):
        out_ref[h * DS_HEAD:(h + 1) * DS_HEAD, :] = _dot(
            wuvt_ref[h], o_lat[:, h * QB:(h + 1) * QB].astype(BF16))

    g = gds_ref[...]
    o_ref[...] = (out_ref[...].T * (g * _sigmoid(g))).astype(BF16)


def _dsa(z, ckv, kid, B, T, q_norm_g, w_uq, w_uk, w_uv, iw_q, tab):
    nq = T // QB
    topk = min(TOPK_MAX, T // 4)
    R = DS_KV_RANK
    wq = jnp.concatenate([w_uq, iw_q], axis=1).astype(BF16)
    wuk_h = jnp.transpose(w_uk, (1, 0, 2)).astype(BF16)
    wuv_t = jnp.transpose(w_uv, (1, 2, 0)).astype(BF16)
    ckv_t = jnp.concatenate([jnp.swapaxes(ckv.reshape(B, T, R), 1, 2),
                             jnp.ones((B, ONES_ROWS, T), BF16)], axis=1).reshape(B * (R + ONES_ROWS), T)
    const2 = lambda b, i: (0, 0)
    const3 = lambda b, i: (0, 0, 0)
    resident = dict(pipeline_mode=pl.Buffered(1))
    return pl.pallas_call(
        functools.partial(_dsa_kernel, topk=topk),
        grid=(B, nq),
        in_specs=[pl.BlockSpec((QB, DS_Q_RANK), lambda b, i: (b * nq + i, COL_Q // DS_Q_RANK)),
                  pl.BlockSpec((QB, LANES), lambda b, i: (b * nq + i, COL_KX // LANES)),
                  pl.BlockSpec((QB, DS_WIDTH), lambda b, i: (b * nq + i, COL_GDS // DS_WIDTH)),
                  pl.BlockSpec((T, IDX_DIM), lambda b, i: (b, 0), **resident),
                  pl.BlockSpec((T, R), lambda b, i: (b, 0), **resident),
                  pl.BlockSpec((R + ONES_ROWS, T), lambda b, i: (b, 0), **resident),
                  pl.BlockSpec((1, DS_Q_RANK), const2),
                  pl.BlockSpec((DS_Q_RANK, 2 * DS_WIDTH), const2, **resident),
                  pl.BlockSpec((DS_HEADS, R, DS_HEAD), const3, **resident),
                  pl.BlockSpec((DS_HEADS, DS_HEAD, R), const3, **resident),
                  pl.BlockSpec((BIAS_ROWS, DS_HEADS * QB), const2, **resident)],
        out_specs=pl.BlockSpec((QB, DS_WIDTH), lambda b, i: (b * nq + i, 0)),
        out_shape=jax.ShapeDtypeStruct((B * T, DS_WIDTH), BF16),
        scratch_shapes=[pltpu.VMEM((T, QB), F32),
                        pltpu.VMEM((T, QB), F32),
                        pltpu.VMEM((SK, DS_HEADS * QB), F32),
                        pltpu.VMEM((R + ONES_ROWS, DS_HEADS * QB), F32),
                        pltpu.VMEM((1, DS_HEADS * QB), F32),
                        pltpu.VMEM((R, DS_HEADS * QB), BF16),
                        pltpu.VMEM((IDX_HEADS // 2, IDX_DIM, 2 * QB), BF16),
                        pltpu.VMEM((IDX_HEADS, QB), F32),
                        pltpu.VMEM((DS_WIDTH, QB), F32)],
        compiler_params=pltpu.CompilerParams(dimension_semantics=("parallel", "arbitrary"),
                                             vmem_limit_bytes=VMEM_LIMIT),
        name="dsa",
    )(z, z, z, kid, ckv, ckv_t, q_norm_g.reshape(1, -1).astype(F32), wq, wuk_h, wuv_t, tab)


def _post_kernel(x_ref, a1_ref, a2_ref, p_ref, w1_ref, w2_ref, pw_ref, gw_ref, fg_ref, o_ref):
    h = x_ref[...] + _dot(a1_ref[...], w1_ref[...]) + _dot(a2_ref[...], w2_ref[...])
    e = _dot(p_ref[...].astype(BF16), pw_ref[...])
    gate = _sigmoid(_dot(h.astype(BF16), gw_ref[...]))
    h2 = h + e * gate
    ms = jnp.mean(h2 * h2, axis=-1, keepdims=True)
    o_ref[...] = h2 * lax.rsqrt(ms + NORM_EPS) * fg_ref[...]


def _post(x2, o_rw, o_ds, p2, w_out, ple_w, gate_w, final_g, tm=256):
    m, d = x2.shape
    kh = o_rw.shape[1]
    pd = p2.shape[1]
    resident = dict(pipeline_mode=pl.Buffered(1))
    return pl.pallas_call(
        _post_kernel,
        grid=(m // tm,),
        in_specs=[pl.BlockSpec((tm, d), lambda i: (i, 0)),
                  pl.BlockSpec((tm, kh), lambda i: (i, 0)),
                  pl.BlockSpec((tm, kh), lambda i: (i, 0)),
                  pl.BlockSpec((tm, pd), lambda i: (i, 0)),
                  pl.BlockSpec((kh, d), lambda i: (0, 0), **resident),
                  pl.BlockSpec((kh, d), lambda i: (1, 0), **resident),
                  pl.BlockSpec((pd, d), lambda i: (0, 0), **resident),
                  pl.BlockSpec((d, d), lambda i: (0, 0), **resident),
                  pl.BlockSpec((1, d), lambda i: (0, 0))],
        out_specs=pl.BlockSpec((tm, d), lambda i: (i, 0)),
        out_shape=jax.ShapeDtypeStruct((m, d), F32),
        compiler_params=pltpu.CompilerParams(dimension_semantics=("parallel",),
                                             vmem_limit_bytes=VMEM_LIMIT),
        name="post",
    )(x2, o_rw, o_ds, p2, w_out, w_out, ple_w, gate_w, final_g)


def _split_w_in(w):
    w = w.astype(BF16)
    return w, w[:, w.shape[1] - DS_WIDTH:]


def kernel(x, p, w_in, norm_g, rw_mu, rw_w0, rw_w_up, rw_a0, rw_a_up, rw_k_k, rw_k_a, rw_r_k, rw_ln_g, rw_ln_b, ds_q_norm_g, ds_kv_norm_g, idx_k_norm_g, ds_w_uq, ds_w_uk, ds_w_uv, idx_w_q, rel_bias, w_out, ple_w, ple_gate_w, final_g):
    B, T, D = x.shape
    depth = w_in.shape[0]
    assert depth == 1 and T % SK == 0 and T % RW_CHUNK == 0 and (B * T) % 512 == 0
    h = x.reshape(B * T, D)
    tab = _biastab(rel_bias)
    for i in range(depth):
        z = _inproj(h, norm_g[i].reshape(1, D), *_split_w_in(w_in[i]))
        ckv, kid = _kvprep(z, ds_kv_norm_g[i].reshape(1, -1), idx_k_norm_g[i].reshape(1, -1))
        o_rw = _rwkv(z, B, T, rw_mu[i], rw_w0[i], rw_a0[i], rw_k_k[i], rw_k_a[i],
                     rw_r_k[i].reshape(-1), rw_ln_g[i], rw_ln_b[i], rw_w_up[i], rw_a_up[i])
        o_ds = _dsa(z, ckv, kid, B, T, ds_q_norm_g[i], ds_w_uq[i], ds_w_uk[i], ds_w_uv[i],
                    idx_w_q[i], tab)
        h = _post(h, o_rw, o_ds, p[i].reshape(B * T, -1), w_out[i].astype(BF16),
                  ple_w[i].astype(BF16), ple_gate_w[i].astype(BF16), final_g.reshape(1, D))
    return h.reshape(B, T, D)
```

```python
import functools
import math

import jax
import jax.numpy as jnp
from jax import lax
from jax.experimental import pallas as pl
from jax.experimental.pallas import tpu as pltpu

F32 = jnp.float32
BF16 = jnp.bfloat16
I32 = jnp.int32

RW_WIDTH = 1024
RW_HEAD = 64
RW_HEADS = 16
RW_LORA = 64
DS_WIDTH = 1024
DS_HEAD = 64
DS_HEADS = 16
DS_Q_RANK = 384
DS_KV_RANK = 256
IDX_HEADS = 16
IDX_DIM = 64
TOPK_MAX = 256
CHUNK = 64
NUM_BUCKETS = 32
MAX_DISTANCE = 128
NORM_EPS = 1e-6
GN_EPS = 64e-5

COL_R, COL_K, COL_V, COL_GRW, COL_GDS = 0, 1024, 2048, 3072, 4096
COL_KV = 5120
COL_Q = 5376
COL_WA = 5760
COL_KX = 5888
Z_WIDTH = 6016

LANES = 128
SUBLANES = 8
QB = 128
SK = 512
BIAS_ROWS = 2 * SK + 2 * QB
RW_CHUNK = 64
INT_MIN = -2 ** 31
KEY_NEG_INF = -2139095041
SEARCH_MIN_BITS = 23
SEARCH_STEP_BITS = 3
assert (32 - SEARCH_MIN_BITS) % SEARCH_STEP_BITS == 0
MASK_NEG = -1e30
LOG2E = 1.4426950408889634
ONES_ROWS = 16
VMEM_LIMIT = 52 * 1024 * 1024


def _sigmoid(x):
    return 1.0 / (1.0 + jnp.exp(-x))


def _dot(a, b):
    return jnp.dot(a, b, preferred_element_type=F32)


def _dot_nt(a, b):
    return lax.dot_general(a, b, (((1,), (1,)), ((), ())), preferred_element_type=F32)


def _dot_tn(a, b):
    return lax.dot_general(a, b, (((0,), (0,)), ((), ())), preferred_element_type=F32)


def _inproj_plan():
    s_wa = 3 * RW_WIDTH
    s_grw = s_wa + 2 * RW_LORA
    s_q = s_grw + RW_WIDTH
    s_kv = s_q + DS_Q_RANK
    s_kx = s_kv + DS_KV_RANK
    return [(COL_R, 0, 3 * RW_WIDTH), (COL_GRW, s_grw, RW_WIDTH), (COL_GDS, None, DS_WIDTH),
            (COL_KV, s_kv, DS_KV_RANK), (COL_Q, s_q, DS_Q_RANK), (COL_WA, s_wa, LANES),
            (COL_KX, s_kx, LANES)]


def _inproj_kernel(x_ref, g_ref, w_ref, wg_ref, gkv_ref, gik_ref, o_ref, ckv_ref, kid_ref, *, tn):
    x = x_ref[...]
    ms = jnp.mean(x * x, axis=-1, keepdims=True)
    xn = (x * lax.rsqrt(ms + NORM_EPS) * g_ref[...]).astype(BF16)
    for dst, src, width in _inproj_plan():
        for j in range(0, width, tn):
            w = min(tn, width - j)
            wt = wg_ref[:, j:j + w] if src is None else w_ref[:, src + j:src + j + w]
            o_ref[:, dst + j:dst + j + w] = _dot(xn, wt)
    kv = o_ref[:, COL_KV:COL_KV + DS_KV_RANK]
    ckv_ref[...] = (kv * lax.rsqrt(jnp.mean(kv * kv, axis=-1, keepdims=True) + NORM_EPS)
                    * gkv_ref[...]).astype(BF16)
    ki = o_ref[:, COL_KX:COL_KX + IDX_DIM]
    kid_ref[...] = (ki * lax.rsqrt(jnp.mean(ki * ki, axis=-1, keepdims=True) + NORM_EPS)
                    * gik_ref[...]).astype(BF16)


def _inproj(x2, g, w, w_gds, gkv, gik, tm=256, tn=768):
    m, d = x2.shape
    resident = dict(pipeline_mode=pl.Buffered(1))
    return pl.pallas_call(
        functools.partial(_inproj_kernel, tn=tn),
        grid=(m // tm,),
        in_specs=[pl.BlockSpec((tm, d), lambda i: (i, 0)),
                  pl.BlockSpec((1, d), lambda i: (0, 0)),
                  pl.BlockSpec(w.shape, lambda i: (0, 0), **resident),
                  pl.BlockSpec(w_gds.shape, lambda i: (0, 0), **resident),
                  pl.BlockSpec((1, DS_KV_RANK), lambda i: (0, 0)),
                  pl.BlockSpec((1, IDX_DIM), lambda i: (0, 0))],
        out_specs=[pl.BlockSpec((tm, Z_WIDTH), lambda i: (i, 0)),
                   pl.BlockSpec((tm, DS_KV_RANK), lambda i: (i, 0)),
                   pl.BlockSpec((tm, IDX_DIM), lambda i: (i, 0))],
        out_shape=[jax.ShapeDtypeStruct((m, Z_WIDTH), F32),
                   jax.ShapeDtypeStruct((m, DS_KV_RANK), BF16),
                   jax.ShapeDtypeStruct((m, IDX_DIM), BF16)],
        compiler_params=pltpu.CompilerParams(dimension_semantics=("parallel",),
                                             vmem_limit_bytes=VMEM_LIMIT),
        name="inproj",
    )(x2, g, w, w_gds, gkv, gik)


def _rwkv_kernel(r_ref, k_ref, v_ref, g_ref, wa_ref,
                 mur_ref, muk_ref, muv_ref, muwa_ref,
                 w0_ref, a0_ref, kk_ref, ka_ref, rk_ref, lng_ref, lnb_ref,
                 wup_ref, aup_ref,
                 o_ref,
                 pr_ref, pk_ref, pv_ref, pwa_ref, st_ref, *, nb):
    C = RW_CHUNK
    N = RW_HEAD

    @pl.when(pl.program_id(0) == 0)
    def _():
        pr_ref[...] = jnp.zeros_like(pr_ref)
        pk_ref[...] = jnp.zeros_like(pk_ref)
        pv_ref[...] = jnp.zeros_like(pv_ref)
        pwa_ref[...] = jnp.zeros_like(pwa_ref)
        st_ref[...] = jnp.zeros_like(st_ref)

    row = lax.broadcasted_iota(I32, (nb * C, 1), 0)

    def shift(ref, prev_ref, mu_ref):
        z = ref[...].reshape(nb * C, ref.shape[2])
        zp = pltpu.roll(z, 1, 0)
        for b in range(nb):
            zp = jnp.where(row == b * C, prev_ref[b], zp)
            prev_ref[b] = z[(b + 1) * C - 1:(b + 1) * C, :]
        return z + mu_ref[...] * (zp - z)

    r = shift(r_ref, pr_ref, mur_ref)
    k = shift(k_ref, pk_ref, muk_ref)
    v = shift(v_ref, pv_ref, muv_ref)
    wa = shift(wa_ref, pwa_ref, muwa_ref)
    wd = wa[:, 0:RW_LORA]
    ad = wa[:, RW_LORA:2 * RW_LORA]

    wl = w0_ref[...] + _dot(jnp.tanh(wd).astype(BF16), wup_ref[...])
    nwl = -wl
    softplus = jnp.maximum(nwl, 0.0) + jnp.log(1.0 + jnp.exp(-jnp.abs(nwl)))
    w_log = -softplus - 0.5
    lw = -jnp.exp(w_log)
    a = _sigmoid(a0_ref[...] + _dot(ad.astype(BF16), aup_ref[...]))
    kk = k * kk_ref[...]
    k2 = k * (1.0 + (a - 1.0) * ka_ref[...])

    ti = lax.broadcasted_iota(I32, (C, C), 0)
    tj = lax.broadcasted_iota(I32, (C, C), 1)
    incl = ti >= tj
    strict = ti > tj
    tri = jnp.where(incl, 1.0, 0.0).astype(F32)

    def per_row(mat, x):
        return jnp.concatenate(
            [jnp.dot(mat, x[b * C:(b + 1) * C], preferred_element_type=F32,
                     precision=lax.Precision.HIGHEST) for b in range(nb)], axis=0)

    cum = per_row(tri, lw)
    p = jnp.exp(cum)
    pinv = jnp.exp(-cum)
    pprev = jnp.exp(cum - lw)
    tot = per_row(jnp.ones((C, C), F32), lw)
    pend = jnp.exp(tot)

    g = g_ref[...].reshape(nb * C, RW_WIDTH)
    gate = g * _sigmoid(g)

    HP = RW_HEADS // 2
    NP = nb * HP

    def pairs(x):
        return jnp.stack([x[b * C:(b + 1) * C, j * LANES:(j + 1) * LANES]
                          for b in range(nb) for j in range(HP)], axis=0)

    def per_pair(ref):
        return jnp.concatenate([ref[...]] * nb, axis=0)

    lane = lax.broadcasted_iota(I32, (1, 1, LANES), 2)
    m_lo = jnp.where(lane < N, 1.0, 0.0).astype(BF16)
    m_hi = jnp.where(lane < N, 0.0, 1.0).astype(BF16)
    bi = lax.broadcasted_iota(I32, (LANES, LANES), 0)
    bj = lax.broadcasted_iota(I32, (LANES, LANES), 1)
    same_head = (bi < N) == (bj < N)
    ones_bd = jnp.where(same_head, 1.0, 0.0).astype(BF16)

    def head_sum(x):
        return _dot(x.reshape(NP * C, LANES).astype(BF16), ones_bd).reshape(NP, C, LANES)

    def halves(x):
        xb = x.astype(BF16)
        return jnp.concatenate([xb * m_lo, xb * m_hi], axis=1)

    def bmm(x, y):
        return lax.dot_general(x, y, (((2,), (1,)), ((0,), (0,))), preferred_element_type=F32)

    def bmm_nt(x, y):
        return lax.dot_general(x, y, (((2,), (2,)), ((0,), (0,))), preferred_element_type=F32)

    def block_mask(nblk, cmp):
        wi = lax.broadcasted_iota(I32, (C, nblk * C), 0)
        wj = lax.broadcasted_iota(I32, (C, nblk * C), 1) & (C - 1)
        return cmp(wi, wj)

    r_p, k2_p, v_p, a_p = pairs(r), pairs(k2), pairs(v), pairs(a)
    p_p, pinv_p, pprev_p = pairs(p), pairs(pinv), pairs(pprev)
    kk_p = pairs(kk)
    kkn = kk_p * lax.rsqrt(jnp.maximum(head_sum(kk_p * kk_p), 1e-24))
    at = (-kkn) * pprev_p
    bt = (kkn * a_p) * pinv_p
    kt = k2_p * pinv_p
    rt = r_p * p_p
    pend_p = pairs(pend)
    pend2 = jnp.concatenate([pend_p, pend_p], axis=1)

    lhs2 = jnp.concatenate([at, rt], axis=1).astype(BF16)
    rhs4 = jnp.concatenate([halves(kt), halves(bt)], axis=1)
    gc = bmm_nt(lhs2, rhs4)
    strict2 = block_mask(2, lambda i_, j_: i_ > j_)
    incl4 = block_mask(4, lambda i_, j_: i_ >= j_)
    a_ak = jnp.where(strict2, gc[:, 0:C, 0:2 * C], 0.0)
    nmat = jnp.where(strict2, gc[:, 0:C, 2 * C:4 * C], 0.0)
    a_rkb = jnp.where(incl4, gc[:, C:2 * C, :], 0.0)

    g0 = st_ref[...]
    sg = bmm_nt(lhs2, g0.astype(BF16))
    vm2 = halves(v_p)
    u = sg[:, 0:C] + bmm(a_ak.astype(BF16), vm2)
    pw = nmat.astype(BF16)
    u = u + bmm(pw, halves(u))
    n = 1
    while 2 * n < C:
        pw = bmm(pw, halves(pw)).astype(BF16)
        u = u + bmm(pw, halves(u))
        n *= 2
    um2b = halves(u)
    y = sg[:, C:2 * C] + bmm(a_rkb.astype(BF16), jnp.concatenate([vm2, um2b], axis=1))
    uv = jnp.concatenate([u, v_p], axis=1).astype(BF16)
    bkh = (jnp.concatenate([bt, kt], axis=1) * pend2).astype(BF16)
    upd = lax.dot_general(uv, bkh, (((1,), (1,)), ((0,), (0,))), preferred_element_type=F32)
    st_ref[...] = g0 * pend2 + jnp.where(same_head, upd, 0.0)

    inv_n = 1.0 / N
    yc = y - head_sum(y) * inv_n
    var = head_sum(yc * yc) * inv_n
    yn = yc * lax.rsqrt(var + GN_EPS) * per_pair(lng_ref) + per_pair(lnb_ref)
    bonus = head_sum(r_p * k2_p * per_pair(rk_ref)) * v_p
    out = (yn + bonus) * pairs(gate)
    for b in range(nb):
        for j in range(HP):
            o_ref[b, :, j * LANES:(j + 1) * LANES] = out[b * HP + j].astype(BF16)


def _rwkv(z, B, T, mu, w0, a0, k_k, k_a, r_k, ln_g, ln_b, w_up, a_up):
    C = RW_CHUNK
    W = RW_WIDTH
    z3 = z.reshape(B, T, z.shape[1])
    row = lambda a: a.reshape(1, -1).astype(F32)
    mu_r, mu_k, mu_v = mu[0:W], mu[W:2 * W], mu[2 * W:3 * W]
    mu_wa = mu[3 * W:3 * W + 2 * RW_LORA]
    zspec = lambda col: pl.BlockSpec((B, C, W), lambda c: (0, c, col // W))
    pspec = lambda width: pl.BlockSpec((1, width), lambda c: (0, 0))
    npairs = RW_HEADS // 2
    prow = lambda a: a.reshape(npairs, 1, LANES).astype(F32)
    ppspec = pl.BlockSpec((npairs, 1, LANES), lambda c: (0, 0, 0))
    wspec = pl.BlockSpec((RW_LORA, W), lambda c: (0, 0))
    out = pl.pallas_call(
        functools.partial(_rwkv_kernel, nb=B),
        grid=(T // C,),
        in_specs=[zspec(COL_R), zspec(COL_K), zspec(COL_V), zspec(COL_GRW),
                  pl.BlockSpec((B, C, LANES), lambda c: (0, c, COL_WA // LANES)),
                  pspec(W), pspec(W), pspec(W), pspec(LANES),
                  pspec(W), pspec(W), pspec(W), pspec(W), ppspec, ppspec, ppspec,
                  wspec, wspec],
        out_specs=pl.BlockSpec((B, C, W), lambda c: (0, c, 0)),
        out_shape=jax.ShapeDtypeStruct((B, T, W), BF16),
        scratch_shapes=[pltpu.VMEM((B, 1, W), F32), pltpu.VMEM((B, 1, W), F32), pltpu.VMEM((B, 1, W), F32),
                        pltpu.VMEM((B, 1, LANES), F32),
                        pltpu.VMEM((B * npairs, LANES, LANES), F32)],
        compiler_params=pltpu.CompilerParams(dimension_semantics=("arbitrary",),
                                             vmem_limit_bytes=VMEM_LIMIT),
        name="rwkv",
    )(z3, z3, z3, z3, z3,
      row(mu_r), row(mu_k), row(mu_v), row(mu_wa),
      row(w0), row(a0), row(k_k), row(k_a), prow(r_k), prow(ln_g), prow(ln_b),
      w_up.astype(BF16), a_up.astype(BF16))
    return out.reshape(B * T, W)


def _biastab_kernel(rb_ref, o_ref):
    o_ref[...] = jnp.zeros_like(o_ref)
    nb = NUM_BUCKETS // 2
    max_exact = nb // 2
    c = lax.broadcasted_iota(I32, (2 * QB, QB), 0)
    r = lax.broadcasted_iota(I32, (2 * QB, QB), 1)
    rel = c - QB - r
    ret = jnp.where(rel > 0, nb, 0)
    n = jnp.abs(rel)
    nf = jnp.maximum(n, 1).astype(F32)
    large = max_exact + (jnp.log(nf / max_exact) / math.log(MAX_DISTANCE / max_exact)
                         * (nb - max_exact)).astype(I32)
    large = jnp.minimum(large, nb - 1) & (NUM_BUCKETS - 1)
    bucket = ret + jnp.where(n < max_exact, n, large)
    for h in range(DS_HEADS):
        far = rb_ref[nb - 1, h]
        acc = jnp.zeros((2 * QB, QB), F32)
        for b in range(NUM_BUCKETS):
            acc = jnp.where(bucket == b, rb_ref[b, h] - far, acc)
        o_ref[SK:SK + 2 * QB, h * QB:(h + 1) * QB] = acc * LOG2E


def _biastab(rel_bias):
    return pl.pallas_call(
        _biastab_kernel,
        in_specs=[pl.BlockSpec(memory_space=pltpu.SMEM)],
        out_specs=pl.BlockSpec(memory_space=pltpu.VMEM),
        out_shape=jax.ShapeDtypeStruct((BIAS_ROWS, DS_HEADS * QB), F32),
        compiler_params=pltpu.CompilerParams(vmem_limit_bytes=VMEM_LIMIT),
        name="biastab",
    )(rel_bias.astype(F32))


def _fold_rows(x, op):
    n = x.shape[0] // SUBLANES
    accs = [x[j * SUBLANES:(j + 1) * SUBLANES] for j in range(min(4, n))]
    for j in range(4, n):
        accs[j % 4] = op(accs[j % 4], x[j * SUBLANES:(j + 1) * SUBLANES])
    while len(accs) > 1:
        accs = [op(accs[j], accs[j + 1]) for j in range(0, len(accs) - 1, 2)] + (
            [accs[-1]] if len(accs) % 2 else [])
    return accs[0]


def _dsa_kernel(ql_ref, kx_ref, gds_ref, kid_ref, ckv_ref, ckvt_ref, qg_ref, wq_ref, wuk_ref, wuvt_ref,
                tab_ref, o_ref,
                sc_ref, lgt_ref, acc_ref, m_ref, qat_ref, qit_ref, w_ref, out_ref, *, topk):
    i = pl.program_id(1)
    q0 = i * QB
    ntile = jnp.right_shift(q0 + (QB + SK - 1), SK.bit_length() - 1)
    R = DS_KV_RANK
    GW = DS_HEADS * QB

    ql = ql_ref[...]
    ms = jnp.mean(ql * ql, axis=-1, keepdims=True)
    qn = (ql * lax.rsqrt(ms + NORM_EPS) * qg_ref[...]).astype(BF16)
    qt = _dot(qn, wq_ref[...]).T
    for h in range(DS_HEADS):
        qh = qt[h * DS_HEAD:(h + 1) * DS_HEAD, :].astype(BF16)
        qat_ref[:, h * QB:(h + 1) * QB] = (_dot(wuk_ref[h], qh) * (DS_HEAD ** -0.5 * LOG2E)).astype(BF16)
    for pr in range(IDX_HEADS // 2):
        base = DS_WIDTH + 2 * pr * IDX_DIM
        qit_ref[pr] = jnp.concatenate([qt[base:base + IDX_DIM, :],
                                       qt[base + IDX_DIM:base + 2 * IDX_DIM, :]], axis=1).astype(BF16)
    w_ref[...] = kx_ref[...].T[IDX_DIM:IDX_DIM + IDX_HEADS, :] * (IDX_HEADS ** -0.5 * IDX_DIM ** -0.5)

    lanei = lax.broadcasted_iota(I32, (1, QB), 1)
    csh = CHUNK.bit_length() - 1
    limit = jnp.left_shift(jnp.right_shift(q0 + lanei, csh) + 1, csh)
    rowi = lax.broadcasted_iota(I32, (SK, QB), 0)

    def score_tile(kt, carry):
        off = pl.multiple_of(kt * SK, SK)
        kid = kid_ref[pl.ds(off, SK), :]
        s = jnp.zeros((SK, QB), F32)
        for pr in range(IDX_HEADS // 2):
            lg = _dot(kid, qit_ref[pr])
            s = s + w_ref[2 * pr:2 * pr + 1, :] * jnp.maximum(lg[:, 0:QB], 0.0)
            s = s + w_ref[2 * pr + 1:2 * pr + 2, :] * jnp.maximum(lg[:, QB:2 * QB], 0.0)
        adm = (off + rowi) < limit
        sc_ref[pl.ds(off, SK), :] = jnp.where(adm, s, -jnp.inf)
        return carry

    lax.fori_loop(0, ntile, score_tile, 0)

    def key_to_f32(key):
        return pltpu.bitcast(jnp.where(key < 0, key ^ 0x7FFFFFFF, key), F32)

    def count(pred):
        def body(kt, acc):
            off = pl.multiple_of(kt * SK, SK)
            sc = sc_ref[pl.ds(off, SK), :]
            return acc + _fold_rows(jnp.where(pred(sc, off), 1, 0).astype(I32), jnp.add)
        acc = lax.fori_loop(0, ntile, body, jnp.zeros((SUBLANES, QB), I32))
        return jnp.sum(acc, axis=0, keepdims=True)

    def bit_step(it, carry):
        lo, cnt_lo = carry
        cand = lo + jnp.left_shift(jnp.int32(1), 31 - it)
        cf = key_to_f32(cand)
        cnt = count(lambda sc, off: sc >= cf)
        take = cnt >= topk
        return jnp.where(take, cand, lo), jnp.where(take, cnt, cnt_lo)

    def all_settled(cnt_lo):
        return jnp.min(jnp.where((cnt_lo == topk) | (limit < topk), 1, 0))

    lo, cnt_lo = lax.fori_loop(0, SEARCH_MIN_BITS, bit_step,
                               (jnp.full((1, QB), INT_MIN, I32), jnp.full((1, QB), 2 ** 30, I32)))

    def more_bits(c):
        it, lo, cnt_lo, _ = c
        lo, cnt_lo = lax.fori_loop(it, it + SEARCH_STEP_BITS, bit_step, (lo, cnt_lo))
        return it + SEARCH_STEP_BITS, lo, cnt_lo, all_settled(cnt_lo)

    _, lo, _, settled = lax.while_loop(lambda c: (c[0] < 32) & (c[3] == 0), more_bits,
                                       (jnp.int32(SEARCH_MIN_BITS), lo, cnt_lo, all_settled(cnt_lo)))
    has_thr = lo > KEY_NEG_INF
    thr = key_to_f32(jnp.maximum(lo, KEY_NEG_INF))
    nbits = max(1, (sc_ref.shape[0] - 1).bit_length())
    take_all_ties = jnp.full((1, QB), sc_ref.shape[0], I32)

    def tie_cut():
        cnt_gt = count(lambda sc, off: sc > thr)
        cnt_eq = count(lambda sc, off: sc == thr)
        tied = (cnt_gt + cnt_eq > topk) & has_thr

        def search_cut():
            def idx_step(it, m):
                cand = m + jnp.left_shift(jnp.int32(1), nbits - 1 - it)
                cnt = cnt_gt + count(lambda sc, off: (sc == thr) & ((off + rowi) < cand))
                return jnp.where(cnt < topk, cand, m)
            return lax.fori_loop(0, nbits, idx_step, jnp.zeros((1, QB), I32))

        return lax.cond(jnp.max(jnp.where(tied, 1, 0)) > 0, search_cut, lambda: take_all_ties)

    cut = lax.cond(settled == 1, lambda: take_all_ties, tie_cut)
    cut = jnp.where(has_thr, cut, -1)

    def selection_mask(off):
        sc = sc_ref[pl.ds(off, SK), :]
        sel = (sc > thr) | ((sc == thr) & ((off + rowi) <= cut))
        return jnp.where(sel, 0.0, MASK_NEG).astype(F32)

    m_ref[...] = jnp.full_like(m_ref, MASK_NEG)
    acc_ref[...] = jnp.zeros_like(acc_ref)
    near_lo = q0 - QB

    def attend(off, bias_off):
        rows = pl.ds(off, SK)
        s = _dot(ckv_ref[rows, :], qat_ref[...])
        mk = selection_mask(off)
        m_old = m_ref[...]
        tmax = []
        for h in range(DS_HEADS):
            cs = slice(h * QB, (h + 1) * QB)
            t = s[:, cs] + mk
            if bias_off is not None:
                t = t + tab_ref[pl.ds(bias_off, SK), cs]
            lgt_ref[:, cs] = t
            tmax.append(jnp.max(_fold_rows(t, jnp.maximum), axis=0, keepdims=True))
        m_new = jnp.maximum(m_old, jnp.concatenate(tmax, axis=1))
        m_ref[...] = m_new
        pr = jnp.exp2(lgt_ref[...] - m_new).astype(BF16)
        acc_ref[...] = acc_ref[...] * jnp.exp2(m_old - m_new) + _dot(ckvt_ref[:, rows], pr)

    def far_tile(kt, c):
        attend(pl.multiple_of(kt * SK, SK), None)
        return c

    def edge_tile(kt, c):
        off = pl.multiple_of(kt * SK, SK)
        attend(off, pl.multiple_of(jnp.maximum(SK + off - near_lo, 0), QB))
        return c

    nfar = jnp.maximum(ntile - 2, 0)
    lax.fori_loop(0, nfar, far_tile, 0)
    lax.fori_loop(nfar, ntile, edge_tile, 0)

    o_lat = acc_ref[0:R, :] * (1.0 / acc_ref[R:R + 1, :])
    for h in range(DS_HEADS):
        out_ref[h * DS_HEAD:(h + 1) * DS_HEAD, :] = _dot(
            wuvt_ref[h], o_lat[:, h * QB:(h + 1) * QB].astype(BF16))

    g = gds_ref[...]
    o_ref[...] = (out_ref[...].T * (g * _sigmoid(g))).astype(BF16)


def _dsa(z, ckv, kid, B, T, q_norm_g, w_uq, w_uk, w_uv, iw_q, tab):
    nq = T // QB
    topk = min(TOPK_MAX, T // 4)
    R = DS_KV_RANK
    wq = jnp.concatenate([w_uq, iw_q], axis=1).astype(BF16)
    wuk_h = jnp.transpose(w_uk, (1, 0, 2)).astype(BF16)
    wuv_t = jnp.transpose(w_uv, (1, 2, 0)).astype(BF16)
    ckv_t = jnp.concatenate([jnp.swapaxes(ckv.reshape(B, T, R), 1, 2),
                             jnp.ones((B, ONES_ROWS, T), BF16)], axis=1).reshape(B * (R + ONES_ROWS), T)
    const2 = lambda b, i: (0, 0)
    const3 = lambda b, i: (0, 0, 0)
    resident = dict(pipeline_mode=pl.Buffered(1))
    return pl.pallas_call(
        functools.partial(_dsa_kernel, topk=topk),
        grid=(B, nq),
        in_specs=[pl.BlockSpec((QB, DS_Q_RANK), lambda b, i: (b * nq + i, COL_Q // DS_Q_RANK)),
                  pl.BlockSpec((QB, LANES), lambda b, i: (b * nq + i, COL_KX // LANES)),
                  pl.BlockSpec((QB, DS_WIDTH), lambda b, i: (b * nq + i, COL_GDS // DS_WIDTH)),
                  pl.BlockSpec((T, IDX_DIM), lambda b, i: (b, 0), **resident),
                  pl.BlockSpec((T, R), lambda b, i: (b, 0), **resident),
                  pl.BlockSpec((R + ONES_ROWS, T), lambda b, i: (b, 0), **resident),
                  pl.BlockSpec((1, DS_Q_RANK), const2),
                  pl.BlockSpec((DS_Q_RANK, 2 * DS_WIDTH), const2, **resident),
                  pl.BlockSpec((DS_HEADS, R, DS_HEAD), const3, **resident),
                  pl.BlockSpec((DS_HEADS, DS_HEAD, R), const3, **resident),
                  pl.BlockSpec((BIAS_ROWS, DS_HEADS * QB), const2, **resident)],
        out_specs=pl.BlockSpec((QB, DS_WIDTH), lambda b, i: (b * nq + i, 0)),
        out_shape=jax.ShapeDtypeStruct((B * T, DS_WIDTH), BF16),
        scratch_shapes=[pltpu.VMEM((T, QB), F32),
                        pltpu.VMEM((SK, DS_HEADS * QB), F32),
                        pltpu.VMEM((R + ONES_ROWS, DS_HEADS * QB), F32),
                        pltpu.VMEM((1, DS_HEADS * QB), F32),
                        pltpu.VMEM((R, DS_HEADS * QB), BF16),
                        pltpu.VMEM((IDX_HEADS // 2, IDX_DIM, 2 * QB), BF16),
                        pltpu.VMEM((IDX_HEADS, QB), F32),
                        pltpu.VMEM((DS_WIDTH, QB), F32)],
        compiler_params=pltpu.CompilerParams(dimension_semantics=("parallel", "arbitrary"),
                                             vmem_limit_bytes=VMEM_LIMIT),
        name="dsa",
    )(z, z, z, kid, ckv, ckv_t, q_norm_g.reshape(1, -1).astype(F32), wq, wuk_h, wuv_t, tab)


def _post_kernel(x_ref, a1_ref, a2_ref, p_ref, w1_ref, w2_ref, pw_ref, gw_ref, fg_ref, o_ref):
    h = x_ref[...] + _dot(a1_ref[...], w1_ref[...]) + _dot(a2_ref[...], w2_ref[...])
    e = _dot(p_ref[...].astype(BF16), pw_ref[...])
    gate = _sigmoid(_dot(h.astype(BF16), gw_ref[...]))
    h2 = h + e * gate
    ms = jnp.mean(h2 * h2, axis=-1, keepdims=True)
    o_ref[...] = h2 * lax.rsqrt(ms + NORM_EPS) * fg_ref[...]


def _post(x2, o_rw, o_ds, p2, w_out, ple_w, gate_w, final_g, tm=256):
    m, d = x2.shape
    kh = o_rw.shape[1]
    pd = p2.shape[1]
    resident = dict(pipeline_mode=pl.Buffered(1))
    return pl.pallas_call(
        _post_kernel,
        grid=(m // tm,),
        in_specs=[pl.BlockSpec((tm, d), lambda i: (i, 0)),
                  pl.BlockSpec((tm, kh), lambda i: (i, 0)),
                  pl.BlockSpec((tm, kh), lambda i: (i, 0)),
                  pl.BlockSpec((tm, pd), lambda i: (i, 0)),
                  pl.BlockSpec((kh, d), lambda i: (0, 0), **resident),
                  pl.BlockSpec((kh, d), lambda i: (1, 0), **resident),
                  pl.BlockSpec((pd, d), lambda i: (0, 0), **resident),
                  pl.BlockSpec((d, d), lambda i: (0, 0), **resident),
                  pl.BlockSpec((1, d), lambda i: (0, 0))],
        out_specs=pl.BlockSpec((tm, d), lambda i: (i, 0)),
        out_shape=jax.ShapeDtypeStruct((m, d), F32),
        compiler_params=pltpu.CompilerParams(dimension_semantics=("parallel",),
                                             vmem_limit_bytes=VMEM_LIMIT),
        name="post",
    )(x2, o_rw, o_ds, p2, w_out, w_out, ple_w, gate_w, final_g)


def _split_w_in(w):
    w = w.astype(BF16)
    return w, w[:, w.shape[1] - DS_WIDTH:]


def kernel(x, p, w_in, norm_g, rw_mu, rw_w0, rw_w_up, rw_a0, rw_a_up, rw_k_k, rw_k_a, rw_r_k, rw_ln_g, rw_ln_b, ds_q_norm_g, ds_kv_norm_g, idx_k_norm_g, ds_w_uq, ds_w_uk, ds_w_uv, idx_w_q, rel_bias, w_out, ple_w, ple_gate_w, final_g):
    B, T, D = x.shape
    depth = w_in.shape[0]
    assert depth == 1 and T % SK == 0 and T % RW_CHUNK == 0 and (B * T) % 512 == 0
    h = x.reshape(B * T, D)
    tab = _biastab(rel_bias)
    for i in range(depth):
        z, ckv, kid = _inproj(h, norm_g[i].reshape(1, D), *_split_w_in(w_in[i]),
                              ds_kv_norm_g[i].reshape(1, -1), idx_k_norm_g[i].reshape(1, -1))
        o_rw = _rwkv(z, B, T, rw_mu[i], rw_w0[i], rw_a0[i], rw_k_k[i], rw_k_a[i],
                     rw_r_k[i].reshape(-1), rw_ln_g[i], rw_ln_b[i], rw_w_up[i], rw_a_up[i])
        o_ds = _dsa(z, ckv, kid, B, T, ds_q_norm_g[i], ds_w_uq[i], ds_w_uk[i], ds_w_uv[i],
                    idx_w_q[i], tab)
        h = _post(h, o_rw, o_ds, p[i].reshape(B * T, -1), w_out[i].astype(BF16),
                  ple_w[i].astype(BF16), ple_gate_w[i].astype(BF16), final_g.reshape(1, D))
    return h.reshape(B, T, D)
```

```python
import functools
import math

import jax
import jax.numpy as jnp
from jax import lax
from jax.experimental import pallas as pl
from jax.experimental.pallas import tpu as pltpu

F32 = jnp.float32
BF16 = jnp.bfloat16
I32 = jnp.int32

RW_WIDTH = 1024
RW_HEAD = 64
RW_HEADS = 16
RW_LORA = 64
DS_WIDTH = 1024
DS_HEAD = 64
DS_HEADS = 16
DS_Q_RANK = 384
DS_KV_RANK = 256
IDX_HEADS = 16
IDX_DIM = 64
TOPK_MAX = 256
CHUNK = 64
NUM_BUCKETS = 32
MAX_DISTANCE = 128
NORM_EPS = 1e-6
GN_EPS = 64e-5

COL_R, COL_K, COL_V, COL_GRW, COL_GDS = 0, 1024, 2048, 3072, 4096
COL_KV = 5120
COL_Q = 5376
COL_WA = 5760
COL_KX = 5888
Z_WIDTH = 6016

LANES = 128
SUBLANES = 8
QB = 256
BQ = 128
SK = 512
BIAS_ROWS = 2 * SK + 2 * BQ
RW_CHUNK = 64
INT_MIN = -2 ** 31
KEY_NEG_INF = -2139095041
SEARCH_MIN_BITS = 23
SEARCH_STEP_BITS = 3
assert (32 - SEARCH_MIN_BITS) % SEARCH_STEP_BITS == 0
MASK_NEG = -1e30
LOG2E = 1.4426950408889634
ONES_ROWS = 16
VMEM_LIMIT = 52 * 1024 * 1024


def _sigmoid(x):
    return 1.0 / (1.0 + jnp.exp(-x))


def _dot(a, b):
    return jnp.dot(a, b, preferred_element_type=F32)


def _dot_nt(a, b):
    return lax.dot_general(a, b, (((1,), (1,)), ((), ())), preferred_element_type=F32)


def _dot_tn(a, b):
    return lax.dot_general(a, b, (((0,), (0,)), ((), ())), preferred_element_type=F32)


def _inproj_plan():
    s_wa = 3 * RW_WIDTH
    s_grw = s_wa + 2 * RW_LORA
    s_q = s_grw + RW_WIDTH
    s_kv = s_q + DS_Q_RANK
    s_kx = s_kv + DS_KV_RANK
    return [(COL_R, 0, 3 * RW_WIDTH), (COL_GRW, s_grw, RW_WIDTH), (COL_GDS, None, DS_WIDTH),
            (COL_KV, s_kv, DS_KV_RANK), (COL_Q, s_q, DS_Q_RANK), (COL_WA, s_wa, LANES),
            (COL_KX, s_kx, LANES)]


def _inproj_kernel(x_ref, g_ref, w_ref, wg_ref, gkv_ref, gik_ref, o_ref, ckv_ref, kid_ref, *, tn):
    x = x_ref[...]
    ms = jnp.mean(x * x, axis=-1, keepdims=True)
    xn = (x * lax.rsqrt(ms + NORM_EPS) * g_ref[...]).astype(BF16)
    for dst, src, width in _inproj_plan():
        for j in range(0, width, tn):
            w = min(tn, width - j)
            wt = wg_ref[:, j:j + w] if src is None else w_ref[:, src + j:src + j + w]
            o_ref[:, dst + j:dst + j + w] = _dot(xn, wt)
    kv = o_ref[:, COL_KV:COL_KV + DS_KV_RANK]
    ckv_ref[...] = (kv * lax.rsqrt(jnp.mean(kv * kv, axis=-1, keepdims=True) + NORM_EPS)
                    * gkv_ref[...]).astype(BF16)
    ki = o_ref[:, COL_KX:COL_KX + IDX_DIM]
    kid_ref[...] = (ki * lax.rsqrt(jnp.mean(ki * ki, axis=-1, keepdims=True) + NORM_EPS)
                    * gik_ref[...]).astype(BF16)


def _inproj(x2, g, w, w_gds, gkv, gik, tm=256, tn=768):
    m, d = x2.shape
    resident = dict(pipeline_mode=pl.Buffered(1))
    return pl.pallas_call(
        functools.partial(_inproj_kernel, tn=tn),
        grid=(m // tm,),
        in_specs=[pl.BlockSpec((tm, d), lambda i: (i, 0)),
                  pl.BlockSpec((1, d), lambda i: (0, 0)),
                  pl.BlockSpec(w.shape, lambda i: (0, 0), **resident),
                  pl.BlockSpec(w_gds.shape, lambda i: (0, 0), **resident),
                  pl.BlockSpec((1, DS_KV_RANK), lambda i: (0, 0)),
                  pl.BlockSpec((1, IDX_DIM), lambda i: (0, 0))],
        out_specs=[pl.BlockSpec((tm, Z_WIDTH), lambda i: (i, 0)),
                   pl.BlockSpec((tm, DS_KV_RANK), lambda i: (i, 0)),
                   pl.BlockSpec((tm, IDX_DIM), lambda i: (i, 0))],
        out_shape=[jax.ShapeDtypeStruct((m, Z_WIDTH), F32),
                   jax.ShapeDtypeStruct((m, DS_KV_RANK), BF16),
                   jax.ShapeDtypeStruct((m, IDX_DIM), BF16)],
        compiler_params=pltpu.CompilerParams(dimension_semantics=("parallel",),
                                             vmem_limit_bytes=VMEM_LIMIT),
        name="inproj",
    )(x2, g, w, w_gds, gkv, gik)


def _rwkv_kernel(r_ref, k_ref, v_ref, g_ref, wa_ref,
                 mur_ref, muk_ref, muv_ref, muwa_ref,
                 w0_ref, a0_ref, kk_ref, ka_ref, rk_ref, lng_ref, lnb_ref,
                 wup_ref, aup_ref,
                 o_ref,
                 pr_ref, pk_ref, pv_ref, pwa_ref, st_ref, *, nb):
    C = RW_CHUNK
    N = RW_HEAD

    @pl.when(pl.program_id(0) == 0)
    def _():
        pr_ref[...] = jnp.zeros_like(pr_ref)
        pk_ref[...] = jnp.zeros_like(pk_ref)
        pv_ref[...] = jnp.zeros_like(pv_ref)
        pwa_ref[...] = jnp.zeros_like(pwa_ref)
        st_ref[...] = jnp.zeros_like(st_ref)

    row = lax.broadcasted_iota(I32, (nb * C, 1), 0)

    def shift(ref, prev_ref, mu_ref):
        z = ref[...].reshape(nb * C, ref.shape[2])
        zp = pltpu.roll(z, 1, 0)
        for b in range(nb):
            zp = jnp.where(row == b * C, prev_ref[b], zp)
            prev_ref[b] = z[(b + 1) * C - 1:(b + 1) * C, :]
        return z + mu_ref[...] * (zp - z)

    r = shift(r_ref, pr_ref, mur_ref)
    k = shift(k_ref, pk_ref, muk_ref)
    v = shift(v_ref, pv_ref, muv_ref)
    wa = shift(wa_ref, pwa_ref, muwa_ref)
    wd = wa[:, 0:RW_LORA]
    ad = wa[:, RW_LORA:2 * RW_LORA]

    wl = w0_ref[...] + _dot(jnp.tanh(wd).astype(BF16), wup_ref[...])
    nwl = -wl
    softplus = jnp.maximum(nwl, 0.0) + jnp.log(1.0 + jnp.exp(-jnp.abs(nwl)))
    w_log = -softplus - 0.5
    lw = -jnp.exp(w_log)
    a = _sigmoid(a0_ref[...] + _dot(ad.astype(BF16), aup_ref[...]))
    kk = k * kk_ref[...]
    k2 = k * (1.0 + (a - 1.0) * ka_ref[...])

    ti = lax.broadcasted_iota(I32, (C, C), 0)
    tj = lax.broadcasted_iota(I32, (C, C), 1)
    incl = ti >= tj
    strict = ti > tj
    tri = jnp.where(incl, 1.0, 0.0).astype(F32)

    def per_row(mat, x):
        return jnp.concatenate(
            [jnp.dot(mat, x[b * C:(b + 1) * C], preferred_element_type=F32,
                     precision=lax.Precision.HIGHEST) for b in range(nb)], axis=0)

    cum = per_row(tri, lw)
    p = jnp.exp(cum)
    pinv = jnp.exp(-cum)
    pprev = jnp.exp(cum - lw)
    tot = per_row(jnp.ones((C, C), F32), lw)
    pend = jnp.exp(tot)

    g = g_ref[...].reshape(nb * C, RW_WIDTH)
    gate = g * _sigmoid(g)

    HP = RW_HEADS // 2
    NP = nb * HP

    def pairs(x):
        return jnp.stack([x[b * C:(b + 1) * C, j * LANES:(j + 1) * LANES]
                          for b in range(nb) for j in range(HP)], axis=0)

    def per_pair(ref):
        return jnp.concatenate([ref[...]] * nb, axis=0)

    lane = lax.broadcasted_iota(I32, (1, 1, LANES), 2)
    m_lo = jnp.where(lane < N, 1.0, 0.0).astype(BF16)
    m_hi = jnp.where(lane < N, 0.0, 1.0).astype(BF16)
    bi = lax.broadcasted_iota(I32, (LANES, LANES), 0)
    bj = lax.broadcasted_iota(I32, (LANES, LANES), 1)
    same_head = (bi < N) == (bj < N)
    ones_bd = jnp.where(same_head, 1.0, 0.0).astype(BF16)

    def head_sum(x):
        return _dot(x.reshape(NP * C, LANES).astype(BF16), ones_bd).reshape(NP, C, LANES)

    def halves(x):
        xb = x.astype(BF16)
        return jnp.concatenate([xb * m_lo, xb * m_hi], axis=1)

    def bmm(x, y):
        return lax.dot_general(x, y, (((2,), (1,)), ((0,), (0,))), preferred_element_type=F32)

    def bmm_nt(x, y):
        return lax.dot_general(x, y, (((2,), (2,)), ((0,), (0,))), preferred_element_type=F32)

    def block_mask(nblk, cmp):
        wi = lax.broadcasted_iota(I32, (C, nblk * C), 0)
        wj = lax.broadcasted_iota(I32, (C, nblk * C), 1) & (C - 1)
        return cmp(wi, wj)

    r_p, k2_p, v_p, a_p = pairs(r), pairs(k2), pairs(v), pairs(a)
    p_p, pinv_p, pprev_p = pairs(p), pairs(pinv), pairs(pprev)
    kk_p = pairs(kk)
    kkn = kk_p * lax.rsqrt(jnp.maximum(head_sum(kk_p * kk_p), 1e-24))
    at = (-kkn) * pprev_p
    bt = (kkn * a_p) * pinv_p
    kt = k2_p * pinv_p
    rt = r_p * p_p
    pend_p = pairs(pend)
    pend2 = jnp.concatenate([pend_p, pend_p], axis=1)

    lhs2 = jnp.concatenate([at, rt], axis=1).astype(BF16)
    rhs4 = jnp.concatenate([halves(kt), halves(bt)], axis=1)
    gc = bmm_nt(lhs2, rhs4)
    strict2 = block_mask(2, lambda i_, j_: i_ > j_)
    incl4 = block_mask(4, lambda i_, j_: i_ >= j_)
    a_ak = jnp.where(strict2, gc[:, 0:C, 0:2 * C], 0.0)
    nmat = jnp.where(strict2, gc[:, 0:C, 2 * C:4 * C], 0.0)
    a_rkb = jnp.where(incl4, gc[:, C:2 * C, :], 0.0)

    g0 = st_ref[...]
    sg = bmm_nt(lhs2, g0.astype(BF16))
    vm2 = halves(v_p)
    u = sg[:, 0:C] + bmm(a_ak.astype(BF16), vm2)
    pw = nmat.astype(BF16)
    u = u + bmm(pw, halves(u))
    n = 1
    while 2 * n < C:
        pw = bmm(pw, halves(pw)).astype(BF16)
        u = u + bmm(pw, halves(u))
        n *= 2
    um2b = halves(u)
    y = sg[:, C:2 * C] + bmm(a_rkb.astype(BF16), jnp.concatenate([vm2, um2b], axis=1))
    uv = jnp.concatenate([u, v_p], axis=1).astype(BF16)
    bkh = (jnp.concatenate([bt, kt], axis=1) * pend2).astype(BF16)
    upd = lax.dot_general(uv, bkh, (((1,), (1,)), ((0,), (0,))), preferred_element_type=F32)
    st_ref[...] = g0 * pend2 + jnp.where(same_head, upd, 0.0)

    inv_n = 1.0 / N
    yc = y - head_sum(y) * inv_n
    var = head_sum(yc * yc) * inv_n
    yn = yc * lax.rsqrt(var + GN_EPS) * per_pair(lng_ref) + per_pair(lnb_ref)
    bonus = head_sum(r_p * k2_p * per_pair(rk_ref)) * v_p
    out = (yn + bonus) * pairs(gate)
    for b in range(nb):
        for j in range(HP):
            o_ref[b, :, j * LANES:(j + 1) * LANES] = out[b * HP + j].astype(BF16)


def _rwkv(z, B, T, mu, w0, a0, k_k, k_a, r_k, ln_g, ln_b, w_up, a_up):
    C = RW_CHUNK
    W = RW_WIDTH
    z3 = z.reshape(B, T, z.shape[1])
    row = lambda a: a.reshape(1, -1).astype(F32)
    mu_r, mu_k, mu_v = mu[0:W], mu[W:2 * W], mu[2 * W:3 * W]
    mu_wa = mu[3 * W:3 * W + 2 * RW_LORA]
    zspec = lambda col: pl.BlockSpec((B, C, W), lambda c: (0, c, col // W))
    pspec = lambda width: pl.BlockSpec((1, width), lambda c: (0, 0))
    npairs = RW_HEADS // 2
    prow = lambda a: a.reshape(npairs, 1, LANES).astype(F32)
    ppspec = pl.BlockSpec((npairs, 1, LANES), lambda c: (0, 0, 0))
    wspec = pl.BlockSpec((RW_LORA, W), lambda c: (0, 0))
    out = pl.pallas_call(
        functools.partial(_rwkv_kernel, nb=B),
        grid=(T // C,),
        in_specs=[zspec(COL_R), zspec(COL_K), zspec(COL_V), zspec(COL_GRW),
                  pl.BlockSpec((B, C, LANES), lambda c: (0, c, COL_WA // LANES)),
                  pspec(W), pspec(W), pspec(W), pspec(LANES),
                  pspec(W), pspec(W), pspec(W), pspec(W), ppspec, ppspec, ppspec,
                  wspec, wspec],
        out_specs=pl.BlockSpec((B, C, W), lambda c: (0, c, 0)),
        out_shape=jax.ShapeDtypeStruct((B, T, W), BF16),
        scratch_shapes=[pltpu.VMEM((B, 1, W), F32), pltpu.VMEM((B, 1, W), F32), pltpu.VMEM((B, 1, W), F32),
                        pltpu.VMEM((B, 1, LANES), F32),
                        pltpu.VMEM((B * npairs, LANES, LANES), F32)],
        compiler_params=pltpu.CompilerParams(dimension_semantics=("arbitrary",),
                                             vmem_limit_bytes=VMEM_LIMIT),
        name="rwkv",
    )(z3, z3, z3, z3, z3,
      row(mu_r), row(mu_k), row(mu_v), row(mu_wa),
      row(w0), row(a0), row(k_k), row(k_a), prow(r_k), prow(ln_g), prow(ln_b),
      w_up.astype(BF16), a_up.astype(BF16))
    return out.reshape(B * T, W)


def _biastab_kernel(rb_ref, o_ref):
    o_ref[...] = jnp.zeros_like(o_ref)
    nb = NUM_BUCKETS // 2
    max_exact = nb // 2
    c = lax.broadcasted_iota(I32, (2 * BQ, BQ), 0)
    r = lax.broadcasted_iota(I32, (2 * BQ, BQ), 1)
    rel = c - BQ - r
    ret = jnp.where(rel > 0, nb, 0)
    n = jnp.abs(rel)
    nf = jnp.maximum(n, 1).astype(F32)
    large = max_exact + (jnp.log(nf / max_exact) / math.log(MAX_DISTANCE / max_exact)
                         * (nb - max_exact)).astype(I32)
    large = jnp.minimum(large, nb - 1) & (NUM_BUCKETS - 1)
    bucket = ret + jnp.where(n < max_exact, n, large)
    for h in range(DS_HEADS):
        far = rb_ref[nb - 1, h]
        acc = jnp.zeros((2 * BQ, BQ), F32)
        for b in range(NUM_BUCKETS):
            acc = jnp.where(bucket == b, rb_ref[b, h] - far, acc)
        o_ref[SK:SK + 2 * BQ, h * BQ:(h + 1) * BQ] = acc * LOG2E


def _biastab(rel_bias):
    return pl.pallas_call(
        _biastab_kernel,
        in_specs=[pl.BlockSpec(memory_space=pltpu.SMEM)],
        out_specs=pl.BlockSpec(memory_space=pltpu.VMEM),
        out_shape=jax.ShapeDtypeStruct((BIAS_ROWS, DS_HEADS * BQ), F32),
        compiler_params=pltpu.CompilerParams(vmem_limit_bytes=VMEM_LIMIT),
        name="biastab",
    )(rel_bias.astype(F32))


def _fold_rows(x, op):
    n = x.shape[0] // SUBLANES
    accs = [x[j * SUBLANES:(j + 1) * SUBLANES] for j in range(min(4, n))]
    for j in range(4, n):
        accs[j % 4] = op(accs[j % 4], x[j * SUBLANES:(j + 1) * SUBLANES])
    while len(accs) > 1:
        accs = [op(accs[j], accs[j + 1]) for j in range(0, len(accs) - 1, 2)] + (
            [accs[-1]] if len(accs) % 2 else [])
    return accs[0]


def _dsa_kernel(ql_ref, kx_ref, gds_ref, kid_ref, ckv_ref, ckvt_ref, qg_ref, wq_ref, wuk_ref, wuvt_ref,
                tab_ref, o_ref,
                sc_ref, lgt_ref, acc_ref, m_ref, qat_ref, qit_ref, w_ref, out_ref, *, topk):
    i = pl.program_id(1)
    q0 = i * QB
    ntile = jnp.right_shift(q0 + (QB + SK - 1), SK.bit_length() - 1)
    R = DS_KV_RANK
    GW = DS_HEADS * QB

    ql = ql_ref[...]
    ms = jnp.mean(ql * ql, axis=-1, keepdims=True)
    qn = (ql * lax.rsqrt(ms + NORM_EPS) * qg_ref[...]).astype(BF16)
    qt = _dot(qn, wq_ref[...]).T
    for h in range(DS_HEADS):
        qh = qt[h * DS_HEAD:(h + 1) * DS_HEAD, :].astype(BF16)
        qat_ref[:, h * QB:(h + 1) * QB] = (_dot(wuk_ref[h], qh) * (DS_HEAD ** -0.5 * LOG2E)).astype(BF16)
    for pr in range(IDX_HEADS // 2):
        base = DS_WIDTH + 2 * pr * IDX_DIM
        qit_ref[pr] = jnp.concatenate([qt[base:base + IDX_DIM, :],
                                       qt[base + IDX_DIM:base + 2 * IDX_DIM, :]], axis=1).astype(BF16)
    w_ref[...] = kx_ref[...].T[IDX_DIM:IDX_DIM + IDX_HEADS, :] * (IDX_HEADS ** -0.5 * IDX_DIM ** -0.5)

    lanei = lax.broadcasted_iota(I32, (1, QB), 1)
    csh = CHUNK.bit_length() - 1
    limit = jnp.left_shift(jnp.right_shift(q0 + lanei, csh) + 1, csh)
    rowi = lax.broadcasted_iota(I32, (SK, QB), 0)

    def score_tile(kt, carry):
        off = pl.multiple_of(kt * SK, SK)
        kid = kid_ref[pl.ds(off, SK), :]
        s = jnp.zeros((SK, QB), F32)
        for pr in range(IDX_HEADS // 2):
            lg = _dot(kid, qit_ref[pr])
            s = s + w_ref[2 * pr:2 * pr + 1, :] * jnp.maximum(lg[:, 0:QB], 0.0)
            s = s + w_ref[2 * pr + 1:2 * pr + 2, :] * jnp.maximum(lg[:, QB:2 * QB], 0.0)
        adm = (off + rowi) < limit
        sc_ref[pl.ds(off, SK), :] = jnp.where(adm, s, -jnp.inf)
        return carry

    lax.fori_loop(0, ntile, score_tile, 0)

    def key_to_f32(key):
        return pltpu.bitcast(jnp.where(key < 0, key ^ 0x7FFFFFFF, key), F32)

    def count(pred):
        def body(kt, acc):
            off = pl.multiple_of(kt * SK, SK)
            sc = sc_ref[pl.ds(off, SK), :]
            return acc + _fold_rows(jnp.where(pred(sc, off), 1, 0).astype(I32), jnp.add)
        acc = lax.fori_loop(0, ntile, body, jnp.zeros((SUBLANES, QB), I32))
        return jnp.sum(acc, axis=0, keepdims=True)

    def bit_step(it, carry):
        lo, cnt_lo = carry
        cand = lo + jnp.left_shift(jnp.int32(1), 31 - it)
        cf = key_to_f32(cand)
        cnt = count(lambda sc, off: sc >= cf)
        take = cnt >= topk
        return jnp.where(take, cand, lo), jnp.where(take, cnt, cnt_lo)

    def all_settled(cnt_lo):
        return jnp.min(jnp.where((cnt_lo == topk) | (limit < topk), 1, 0))

    lo, cnt_lo = lax.fori_loop(0, SEARCH_MIN_BITS, bit_step,
                               (jnp.full((1, QB), INT_MIN, I32), jnp.full((1, QB), 2 ** 30, I32)))

    def more_bits(c):
        it, lo, cnt_lo, _ = c
        lo, cnt_lo = lax.fori_loop(it, it + SEARCH_STEP_BITS, bit_step, (lo, cnt_lo))
        return it + SEARCH_STEP_BITS, lo, cnt_lo, all_settled(cnt_lo)

    _, lo, _, settled = lax.while_loop(lambda c: (c[0] < 32) & (c[3] == 0), more_bits,
                                       (jnp.int32(SEARCH_MIN_BITS), lo, cnt_lo, all_settled(cnt_lo)))
    has_thr = lo > KEY_NEG_INF
    thr = key_to_f32(jnp.maximum(lo, KEY_NEG_INF))
    nbits = max(1, (sc_ref.shape[0] - 1).bit_length())
    take_all_ties = jnp.full((1, QB), sc_ref.shape[0], I32)

    def tie_cut():
        cnt_gt = count(lambda sc, off: sc > thr)
        cnt_eq = count(lambda sc, off: sc == thr)
        tied = (cnt_gt + cnt_eq > topk) & has_thr

        def search_cut():
            def idx_step(it, m):
                cand = m + jnp.left_shift(jnp.int32(1), nbits - 1 - it)
                cnt = cnt_gt + count(lambda sc, off: (sc == thr) & ((off + rowi) < cand))
                return jnp.where(cnt < topk, cand, m)
            return lax.fori_loop(0, nbits, idx_step, jnp.zeros((1, QB), I32))

        return lax.cond(jnp.max(jnp.where(tied, 1, 0)) > 0, search_cut, lambda: take_all_ties)

    cut = lax.cond(settled == 1, lambda: take_all_ties, tie_cut)
    cut = jnp.where(has_thr, cut, -1)

    def selection_mask(off):
        sc = sc_ref[pl.ds(off, SK), :]
        sel = (sc > thr) | ((sc == thr) & ((off + rowi) <= cut))
        return jnp.where(sel, 0.0, MASK_NEG).astype(F32)

    m_ref[...] = jnp.full_like(m_ref, MASK_NEG)
    acc_ref[...] = jnp.zeros_like(acc_ref)
    near_lo = q0 - BQ

    def attend(off, bias_off):
        rows = pl.ds(off, SK)
        s = _dot(ckv_ref[rows, :], qat_ref[...])
        mk = selection_mask(off)
        m_old = m_ref[...]
        tmax = []
        for h in range(DS_HEADS):
            cs = slice(h * QB, (h + 1) * QB)
            t = s[:, cs] + mk
            if bias_off is not None:
                t = t + jnp.concatenate(
                    [tab_ref[pl.ds(pl.multiple_of(jnp.maximum(bias_off - j * BQ, 0), BQ), SK),
                             h * BQ:(h + 1) * BQ] for j in range(QB // BQ)], axis=1)
            lgt_ref[:, cs] = t
            tmax.append(jnp.max(_fold_rows(t, jnp.maximum), axis=0, keepdims=True))
        m_new = jnp.maximum(m_old, jnp.concatenate(tmax, axis=1))
        m_ref[...] = m_new
        pr = jnp.exp2(lgt_ref[...] - m_new).astype(BF16)
        acc_ref[...] = acc_ref[...] * jnp.exp2(m_old - m_new) + _dot(ckvt_ref[:, rows], pr)

    def far_tile(kt, c):
        attend(pl.multiple_of(kt * SK, SK), None)
        return c

    def edge_tile(kt, c):
        off = pl.multiple_of(kt * SK, SK)
        attend(off, SK + off - near_lo)
        return c

    nfar = jnp.maximum(ntile - 2, 0)
    lax.fori_loop(0, nfar, far_tile, 0)
    lax.fori_loop(nfar, ntile, edge_tile, 0)

    o_lat = acc_ref[0:R, :] * (1.0 / acc_ref[R:R + 1, :])
    for h in range(DS_HEADS):
        out_ref[h * DS_HEAD:(h + 1) * DS_HEAD, :] = _dot(
            wuvt_ref[h], o_lat[:, h * QB:(h + 1) * QB].astype(BF16))

    g = gds_ref[...]
    o_ref[...] = (out_ref[...].T * (g * _sigmoid(g))).astype(BF16)


def _dsa(z, ckv, kid, B, T, q_norm_g, w_uq, w_uk, w_uv, iw_q, tab):
    nq = T // QB
    topk = min(TOPK_MAX, T // 4)
    R = DS_KV_RANK
    wq = jnp.concatenate([w_uq, iw_q], axis=1).astype(BF16)
    wuk_h = jnp.transpose(w_uk, (1, 0, 2)).astype(BF16)
    wuv_t = jnp.transpose(w_uv, (1, 2, 0)).astype(BF16)
    ckv_t = jnp.concatenate([jnp.swapaxes(ckv.reshape(B, T, R), 1, 2),
                             jnp.ones((B, ONES_ROWS, T), BF16)], axis=1).reshape(B * (R + ONES_ROWS), T)
    const2 = lambda b, i: (0, 0)
    const3 = lambda b, i: (0, 0, 0)
    resident = dict(pipeline_mode=pl.Buffered(1))
    return pl.pallas_call(
        functools.partial(_dsa_kernel, topk=topk),
        grid=(B, nq),
        in_specs=[pl.BlockSpec((QB, DS_Q_RANK), lambda b, i: (b * nq + i, COL_Q // DS_Q_RANK)),
                  pl.BlockSpec((QB, LANES), lambda b, i: (b * nq + i, COL_KX // LANES)),
                  pl.BlockSpec((QB, DS_WIDTH), lambda b, i: (b * nq + i, COL_GDS // DS_WIDTH)),
                  pl.BlockSpec((T, IDX_DIM), lambda b, i: (b, 0), **resident),
                  pl.BlockSpec((T, R), lambda b, i: (b, 0), **resident),
                  pl.BlockSpec((R + ONES_ROWS, T), lambda b, i: (b, 0), **resident),
                  pl.BlockSpec((1, DS_Q_RANK), const2),
                  pl.BlockSpec((DS_Q_RANK, 2 * DS_WIDTH), const2, **resident),
                  pl.BlockSpec((DS_HEADS, R, DS_HEAD), const3, **resident),
                  pl.BlockSpec((DS_HEADS, DS_HEAD, R), const3, **resident),
                  pl.BlockSpec((BIAS_ROWS, DS_HEADS * BQ), const2, **resident)],
        out_specs=pl.BlockSpec((QB, DS_WIDTH), lambda b, i: (b * nq + i, 0)),
        out_shape=jax.ShapeDtypeStruct((B * T, DS_WIDTH), BF16),
        scratch_shapes=[pltpu.VMEM((T, QB), F32),
                        pltpu.VMEM((SK, DS_HEADS * QB), F32),
                        pltpu.VMEM((R + ONES_ROWS, DS_HEADS * QB), F32),
                        pltpu.VMEM((1, DS_HEADS * QB), F32),
                        pltpu.VMEM((R, DS_HEADS * QB), BF16),
                        pltpu.VMEM((IDX_HEADS // 2, IDX_DIM, 2 * QB), BF16),
                        pltpu.VMEM((IDX_HEADS, QB), F32),
                        pltpu.VMEM((DS_WIDTH, QB), F32)],
        compiler_params=pltpu.CompilerParams(dimension_semantics=("parallel", "arbitrary"),
                                             vmem_limit_bytes=VMEM_LIMIT),
        name="dsa",
    )(z, z, z, kid, ckv, ckv_t, q_norm_g.reshape(1, -1).astype(F32), wq, wuk_h, wuv_t, tab)


def _post_kernel(x_ref, a1_ref, a2_ref, p_ref, w1_ref, w2_ref, pw_ref, gw_ref, fg_ref, o_ref):
    h = x_ref[...] + _dot(a1_ref[...], w1_ref[...]) + _dot(a2_ref[...], w2_ref[...])
    e = _dot(p_ref[...].astype(BF16), pw_ref[...])
    gate = _sigmoid(_dot(h.astype(BF16), gw_ref[...]))
    h2 = h + e * gate
    ms = jnp.mean(h2 * h2, axis=-1, keepdims=True)
    o_ref[...] = h2 * lax.rsqrt(ms + NORM_EPS) * fg_ref[...]


def _post(x2, o_rw, o_ds, p2, w_out, ple_w, gate_w, final_g, tm=256):
    m, d = x2.shape
    kh = o_rw.shape[1]
    pd = p2.shape[1]
    resident = dict(pipeline_mode=pl.Buffered(1))
    return pl.pallas_call(
        _post_kernel,
        grid=(m // tm,),
        in_specs=[pl.BlockSpec((tm, d), lambda i: (i, 0)),
                  pl.BlockSpec((tm, kh), lambda i: (i, 0)),
                  pl.BlockSpec((tm, kh), lambda i: (i, 0)),
                  pl.BlockSpec((tm, pd), lambda i: (i, 0)),
                  pl.BlockSpec((kh, d), lambda i: (0, 0), **resident),
                  pl.BlockSpec((kh, d), lambda i: (1, 0), **resident),
                  pl.BlockSpec((pd, d), lambda i: (0, 0), **resident),
                  pl.BlockSpec((d, d), lambda i: (0, 0), **resident),
                  pl.BlockSpec((1, d), lambda i: (0, 0))],
        out_specs=pl.BlockSpec((tm, d), lambda i: (i, 0)),
        out_shape=jax.ShapeDtypeStruct((m, d), F32),
        compiler_params=pltpu.CompilerParams(dimension_semantics=("parallel",),
                                             vmem_limit_bytes=VMEM_LIMIT),
        name="post",
    )(x2, o_rw, o_ds, p2, w_out, w_out, ple_w, gate_w, final_g)


def _split_w_in(w):
    w = w.astype(BF16)
    return w, w[:, w.shape[1] - DS_WIDTH:]


def kernel(x, p, w_in, norm_g, rw_mu, rw_w0, rw_w_up, rw_a0, rw_a_up, rw_k_k, rw_k_a, rw_r_k, rw_ln_g, rw_ln_b, ds_q_norm_g, ds_kv_norm_g, idx_k_norm_g, ds_w_uq, ds_w_uk, ds_w_uv, idx_w_q, rel_bias, w_out, ple_w, ple_gate_w, final_g):
    B, T, D = x.shape
    depth = w_in.shape[0]
    assert depth == 1 and T % SK == 0 and T % RW_CHUNK == 0 and (B * T) % 512 == 0
    h = x.reshape(B * T, D)
    tab = _biastab(rel_bias)
    for i in range(depth):
        z, ckv, kid = _inproj(h, norm_g[i].reshape(1, D), *_split_w_in(w_in[i]),
                              ds_kv_norm_g[i].reshape(1, -1), idx_k_norm_g[i].reshape(1, -1))
        o_rw = _rwkv(z, B, T, rw_mu[i], rw_w0[i], rw_a0[i], rw_k_k[i], rw_k_a[i],
                     rw_r_k[i].reshape(-1), rw_ln_g[i], rw_ln_b[i], rw_w_up[i], rw_a_up[i])
        o_ds = _dsa(z, ckv, kid, B, T, ds_q_norm_g[i], ds_w_uq[i], ds_w_uk[i], ds_w_uv[i],
                    idx_w_q[i], tab)
        h = _post(h, o_rw, o_ds, p[i].reshape(B * T, -1), w_out[i].astype(BF16),
                  ple_w[i].astype(BF16), ple_gate_w[i].astype(BF16), final_g.reshape(1, D))
    return h.reshape(B, T, D)
```

```python
import functools
import math

import jax
import jax.numpy as jnp
from jax import lax
from jax.experimental import pallas as pl
from jax.experimental.pallas import tpu as pltpu

F32 = jnp.float32
BF16 = jnp.bfloat16
I32 = jnp.int32

RW_WIDTH = 1024
RW_HEAD = 64
RW_HEADS = 16
RW_LORA = 64
DS_WIDTH = 1024
DS_HEAD = 64
DS_HEADS = 16
DS_Q_RANK = 384
DS_KV_RANK = 256
IDX_HEADS = 16
IDX_DIM = 64
TOPK_MAX = 256
CHUNK = 64
NUM_BUCKETS = 32
MAX_DISTANCE = 128
NORM_EPS = 1e-6
GN_EPS = 64e-5

COL_R, COL_K, COL_V, COL_GRW, COL_GDS = 0, 1024, 2048, 3072, 4096
COL_KV = 5120
COL_Q = 5376
COL_WA = 5760
COL_KX = 5888
Z_WIDTH = 6016

LANES = 128
SUBLANES = 8
QB = 256
BQ = 128
SK = 512
BIAS_ROWS = 2 * SK + 2 * BQ
RW_CHUNK = 64
INT_MIN = -2 ** 31
KEY_NEG_INF = -2139095041
PLANE_ROWS = 32 * SUBLANES
SEARCH_MIN_BITS = 23
SEARCH_STEP_BITS = 3
assert (32 - SEARCH_MIN_BITS) % SEARCH_STEP_BITS == 0
MASK_NEG = -1e30
LOG2E = 1.4426950408889634
ONES_ROWS = 16
VMEM_LIMIT = 52 * 1024 * 1024


def _sigmoid(x):
    return 1.0 / (1.0 + jnp.exp(-x))


def _dot(a, b):
    return jnp.dot(a, b, preferred_element_type=F32)


def _dot_nt(a, b):
    return lax.dot_general(a, b, (((1,), (1,)), ((), ())), preferred_element_type=F32)


def _dot_tn(a, b):
    return lax.dot_general(a, b, (((0,), (0,)), ((), ())), preferred_element_type=F32)


def _inproj_plan():
    s_wa = 3 * RW_WIDTH
    s_grw = s_wa + 2 * RW_LORA
    s_q = s_grw + RW_WIDTH
    s_kv = s_q + DS_Q_RANK
    s_kx = s_kv + DS_KV_RANK
    return [(COL_R, 0, 3 * RW_WIDTH), (COL_GRW, s_grw, RW_WIDTH), (COL_GDS, None, DS_WIDTH),
            (COL_KV, s_kv, DS_KV_RANK), (COL_Q, s_q, DS_Q_RANK), (COL_WA, s_wa, LANES),
            (COL_KX, s_kx, LANES)]


def _inproj_kernel(x_ref, g_ref, w_ref, wg_ref, gkv_ref, gik_ref, o_ref, ckv_ref, kid_ref, *, tn):
    x = x_ref[...]
    ms = jnp.mean(x * x, axis=-1, keepdims=True)
    xn = (x * lax.rsqrt(ms + NORM_EPS) * g_ref[...]).astype(BF16)
    for dst, src, width in _inproj_plan():
        for j in range(0, width, tn):
            w = min(tn, width - j)
            wt = wg_ref[:, j:j + w] if src is None else w_ref[:, src + j:src + j + w]
            o_ref[:, dst + j:dst + j + w] = _dot(xn, wt)
    kv = o_ref[:, COL_KV:COL_KV + DS_KV_RANK]
    ckv_ref[...] = (kv * lax.rsqrt(jnp.mean(kv * kv, axis=-1, keepdims=True) + NORM_EPS)
                    * gkv_ref[...]).astype(BF16)
    ki = o_ref[:, COL_KX:COL_KX + IDX_DIM]
    kid_ref[...] = (ki * lax.rsqrt(jnp.mean(ki * ki, axis=-1, keepdims=True) + NORM_EPS)
                    * gik_ref[...]).astype(BF16)


def _inproj(x2, g, w, w_gds, gkv, gik, tm=256, tn=768):
    m, d = x2.shape
    resident = dict(pipeline_mode=pl.Buffered(1))
    return pl.pallas_call(
        functools.partial(_inproj_kernel, tn=tn),
        grid=(m // tm,),
        in_specs=[pl.BlockSpec((tm, d), lambda i: (i, 0)),
                  pl.BlockSpec((1, d), lambda i: (0, 0)),
                  pl.BlockSpec(w.shape, lambda i: (0, 0), **resident),
                  pl.BlockSpec(w_gds.shape, lambda i: (0, 0), **resident),
                  pl.BlockSpec((1, DS_KV_RANK), lambda i: (0, 0)),
                  pl.BlockSpec((1, IDX_DIM), lambda i: (0, 0))],
        out_specs=[pl.BlockSpec((tm, Z_WIDTH), lambda i: (i, 0)),
                   pl.BlockSpec((tm, DS_KV_RANK), lambda i: (i, 0)),
                   pl.BlockSpec((tm, IDX_DIM), lambda i: (i, 0))],
        out_shape=[jax.ShapeDtypeStruct((m, Z_WIDTH), F32),
                   jax.ShapeDtypeStruct((m, DS_KV_RANK), BF16),
                   jax.ShapeDtypeStruct((m, IDX_DIM), BF16)],
        compiler_params=pltpu.CompilerParams(dimension_semantics=("parallel",),
                                             vmem_limit_bytes=VMEM_LIMIT),
        name="inproj",
    )(x2, g, w, w_gds, gkv, gik)


def _rwkv_kernel(r_ref, k_ref, v_ref, g_ref, wa_ref,
                 mur_ref, muk_ref, muv_ref, muwa_ref,
                 w0_ref, a0_ref, kk_ref, ka_ref, rk_ref, lng_ref, lnb_ref,
                 wup_ref, aup_ref,
                 o_ref,
                 pr_ref, pk_ref, pv_ref, pwa_ref, st_ref, *, nb):
    C = RW_CHUNK
    N = RW_HEAD

    @pl.when(pl.program_id(0) == 0)
    def _():
        pr_ref[...] = jnp.zeros_like(pr_ref)
        pk_ref[...] = jnp.zeros_like(pk_ref)
        pv_ref[...] = jnp.zeros_like(pv_ref)
        pwa_ref[...] = jnp.zeros_like(pwa_ref)
        st_ref[...] = jnp.zeros_like(st_ref)

    row = lax.broadcasted_iota(I32, (nb * C, 1), 0)

    def shift(ref, prev_ref, mu_ref):
        z = ref[...].reshape(nb * C, ref.shape[2])
        zp = pltpu.roll(z, 1, 0)
        for b in range(nb):
            zp = jnp.where(row == b * C, prev_ref[b], zp)
            prev_ref[b] = z[(b + 1) * C - 1:(b + 1) * C, :]
        return z + mu_ref[...] * (zp - z)

    r = shift(r_ref, pr_ref, mur_ref)
    k = shift(k_ref, pk_ref, muk_ref)
    v = shift(v_ref, pv_ref, muv_ref)
    wa = shift(wa_ref, pwa_ref, muwa_ref)
    wd = wa[:, 0:RW_LORA]
    ad = wa[:, RW_LORA:2 * RW_LORA]

    wl = w0_ref[...] + _dot(jnp.tanh(wd).astype(BF16), wup_ref[...])
    nwl = -wl
    softplus = jnp.maximum(nwl, 0.0) + jnp.log(1.0 + jnp.exp(-jnp.abs(nwl)))
    w_log = -softplus - 0.5
    lw = -jnp.exp(w_log)
    a = _sigmoid(a0_ref[...] + _dot(ad.astype(BF16), aup_ref[...]))
    kk = k * kk_ref[...]
    k2 = k * (1.0 + (a - 1.0) * ka_ref[...])

    ti = lax.broadcasted_iota(I32, (C, C), 0)
    tj = lax.broadcasted_iota(I32, (C, C), 1)
    incl = ti >= tj
    strict = ti > tj
    tri = jnp.where(incl, 1.0, 0.0).astype(F32)

    def per_row(mat, x):
        return jnp.concatenate(
            [jnp.dot(mat, x[b * C:(b + 1) * C], preferred_element_type=F32,
                     precision=lax.Precision.HIGHEST) for b in range(nb)], axis=0)

    cum = per_row(tri, lw)
    p = jnp.exp(cum)
    pinv = jnp.exp(-cum)
    pprev = jnp.exp(cum - lw)
    tot = per_row(jnp.ones((C, C), F32), lw)
    pend = jnp.exp(tot)

    g = g_ref[...].reshape(nb * C, RW_WIDTH)
    gate = g * _sigmoid(g)

    HP = RW_HEADS // 2
    NP = nb * HP

    def pairs(x):
        return jnp.stack([x[b * C:(b + 1) * C, j * LANES:(j + 1) * LANES]
                          for b in range(nb) for j in range(HP)], axis=0)

    def per_pair(ref):
        return jnp.concatenate([ref[...]] * nb, axis=0)

    lane = lax.broadcasted_iota(I32, (1, 1, LANES), 2)
    m_lo = jnp.where(lane < N, 1.0, 0.0).astype(BF16)
    m_hi = jnp.where(lane < N, 0.0, 1.0).astype(BF16)
    bi = lax.broadcasted_iota(I32, (LANES, LANES), 0)
    bj = lax.broadcasted_iota(I32, (LANES, LANES), 1)
    same_head = (bi < N) == (bj < N)
    ones_bd = jnp.where(same_head, 1.0, 0.0).astype(BF16)

    def head_sum(x):
        return _dot(x.reshape(NP * C, LANES).astype(BF16), ones_bd).reshape(NP, C, LANES)

    def halves(x):
        xb = x.astype(BF16)
        return jnp.concatenate([xb * m_lo, xb * m_hi], axis=1)

    def bmm(x, y):
        return lax.dot_general(x, y, (((2,), (1,)), ((0,), (0,))), preferred_element_type=F32)

    def bmm_nt(x, y):
        return lax.dot_general(x, y, (((2,), (2,)), ((0,), (0,))), preferred_element_type=F32)

    def block_mask(nblk, cmp):
        wi = lax.broadcasted_iota(I32, (C, nblk * C), 0)
        wj = lax.broadcasted_iota(I32, (C, nblk * C), 1) & (C - 1)
        return cmp(wi, wj)

    r_p, k2_p, v_p, a_p = pairs(r), pairs(k2), pairs(v), pairs(a)
    p_p, pinv_p, pprev_p = pairs(p), pairs(pinv), pairs(pprev)
    kk_p = pairs(kk)
    kkn = kk_p * lax.rsqrt(jnp.maximum(head_sum(kk_p * kk_p), 1e-24))
    at = (-kkn) * pprev_p
    bt = (kkn * a_p) * pinv_p
    kt = k2_p * pinv_p
    rt = r_p * p_p
    pend_p = pairs(pend)
    pend2 = jnp.concatenate([pend_p, pend_p], axis=1)

    lhs2 = jnp.concatenate([at, rt], axis=1).astype(BF16)
    rhs4 = jnp.concatenate([halves(kt), halves(bt)], axis=1)
    gc = bmm_nt(lhs2, rhs4)
    strict2 = block_mask(2, lambda i_, j_: i_ > j_)
    incl4 = block_mask(4, lambda i_, j_: i_ >= j_)
    a_ak = jnp.where(strict2, gc[:, 0:C, 0:2 * C], 0.0)
    nmat = jnp.where(strict2, gc[:, 0:C, 2 * C:4 * C], 0.0)
    a_rkb = jnp.where(incl4, gc[:, C:2 * C, :], 0.0)

    g0 = st_ref[...]
    sg = bmm_nt(lhs2, g0.astype(BF16))
    vm2 = halves(v_p)
    u = sg[:, 0:C] + bmm(a_ak.astype(BF16), vm2)
    pw = nmat.astype(BF16)
    u = u + bmm(pw, halves(u))
    n = 1
    while 2 * n < C:
        pw = bmm(pw, halves(pw)).astype(BF16)
        u = u + bmm(pw, halves(u))
        n *= 2
    um2b = halves(u)
    y = sg[:, C:2 * C] + bmm(a_rkb.astype(BF16), jnp.concatenate([vm2, um2b], axis=1))
    uv = jnp.concatenate([u, v_p], axis=1).astype(BF16)
    bkh = (jnp.concatenate([bt, kt], axis=1) * pend2).astype(BF16)
    upd = lax.dot_general(uv, bkh, (((1,), (1,)), ((0,), (0,))), preferred_element_type=F32)
    st_ref[...] = g0 * pend2 + jnp.where(same_head, upd, 0.0)

    inv_n = 1.0 / N
    yc = y - head_sum(y) * inv_n
    var = head_sum(yc * yc) * inv_n
    yn = yc * lax.rsqrt(var + GN_EPS) * per_pair(lng_ref) + per_pair(lnb_ref)
    bonus = head_sum(r_p * k2_p * per_pair(rk_ref)) * v_p
    out = (yn + bonus) * pairs(gate)
    for b in range(nb):
        for j in range(HP):
            o_ref[b, :, j * LANES:(j + 1) * LANES] = out[b * HP + j].astype(BF16)


def _rwkv(z, B, T, mu, w0, a0, k_k, k_a, r_k, ln_g, ln_b, w_up, a_up):
    C = RW_CHUNK
    W = RW_WIDTH
    z3 = z.reshape(B, T, z.shape[1])
    row = lambda a: a.reshape(1, -1).astype(F32)
    mu_r, mu_k, mu_v = mu[0:W], mu[W:2 * W], mu[2 * W:3 * W]
    mu_wa = mu[3 * W:3 * W + 2 * RW_LORA]
    zspec = lambda col: pl.BlockSpec((B, C, W), lambda c: (0, c, col // W))
    pspec = lambda width: pl.BlockSpec((1, width), lambda c: (0, 0))
    npairs = RW_HEADS // 2
    prow = lambda a: a.reshape(npairs, 1, LANES).astype(F32)
    ppspec = pl.BlockSpec((npairs, 1, LANES), lambda c: (0, 0, 0))
    wspec = pl.BlockSpec((RW_LORA, W), lambda c: (0, 0))
    out = pl.pallas_call(
        functools.partial(_rwkv_kernel, nb=B),
        grid=(T // C,),
        in_specs=[zspec(COL_R), zspec(COL_K), zspec(COL_V), zspec(COL_GRW),
                  pl.BlockSpec((B, C, LANES), lambda c: (0, c, COL_WA // LANES)),
                  pspec(W), pspec(W), pspec(W), pspec(LANES),
                  pspec(W), pspec(W), pspec(W), pspec(W), ppspec, ppspec, ppspec,
                  wspec, wspec],
        out_specs=pl.BlockSpec((B, C, W), lambda c: (0, c, 0)),
        out_shape=jax.ShapeDtypeStruct((B, T, W), BF16),
        scratch_shapes=[pltpu.VMEM((B, 1, W), F32), pltpu.VMEM((B, 1, W), F32), pltpu.VMEM((B, 1, W), F32),
                        pltpu.VMEM((B, 1, LANES), F32),
                        pltpu.VMEM((B * npairs, LANES, LANES), F32)],
        compiler_params=pltpu.CompilerParams(dimension_semantics=("arbitrary",),
                                             vmem_limit_bytes=VMEM_LIMIT),
        name="rwkv",
    )(z3, z3, z3, z3, z3,
      row(mu_r), row(mu_k), row(mu_v), row(mu_wa),
      row(w0), row(a0), row(k_k), row(k_a), prow(r_k), prow(ln_g), prow(ln_b),
      w_up.astype(BF16), a_up.astype(BF16))
    return out.reshape(B * T, W)


def _biastab_kernel(rb_ref, o_ref):
    o_ref[...] = jnp.zeros_like(o_ref)
    nb = NUM_BUCKETS // 2
    max_exact = nb // 2
    c = lax.broadcasted_iota(I32, (2 * BQ, BQ), 0)
    r = lax.broadcasted_iota(I32, (2 * BQ, BQ), 1)
    rel = c - BQ - r
    ret = jnp.where(rel > 0, nb, 0)
    n = jnp.abs(rel)
    nf = jnp.maximum(n, 1).astype(F32)
    large = max_exact + (jnp.log(nf / max_exact) / math.log(MAX_DISTANCE / max_exact)
                         * (nb - max_exact)).astype(I32)
    large = jnp.minimum(large, nb - 1) & (NUM_BUCKETS - 1)
    bucket = ret + jnp.where(n < max_exact, n, large)
    for h in range(DS_HEADS):
        far = rb_ref[nb - 1, h]
        acc = jnp.zeros((2 * BQ, BQ), F32)
        for b in range(NUM_BUCKETS):
            acc = jnp.where(bucket == b, rb_ref[b, h] - far, acc)
        o_ref[SK:SK + 2 * BQ, h * BQ:(h + 1) * BQ] = acc * LOG2E


def _biastab(rel_bias):
    return pl.pallas_call(
        _biastab_kernel,
        in_specs=[pl.BlockSpec(memory_space=pltpu.SMEM)],
        out_specs=pl.BlockSpec(memory_space=pltpu.VMEM),
        out_shape=jax.ShapeDtypeStruct((BIAS_ROWS, DS_HEADS * BQ), F32),
        compiler_params=pltpu.CompilerParams(vmem_limit_bytes=VMEM_LIMIT),
        name="biastab",
    )(rel_bias.astype(F32))


def _fold_rows(x, op):
    n = x.shape[0] // SUBLANES
    accs = [x[j * SUBLANES:(j + 1) * SUBLANES] for j in range(min(4, n))]
    for j in range(4, n):
        accs[j % 4] = op(accs[j % 4], x[j * SUBLANES:(j + 1) * SUBLANES])
    while len(accs) > 1:
        accs = [op(accs[j], accs[j + 1]) for j in range(0, len(accs) - 1, 2)] + (
            [accs[-1]] if len(accs) % 2 else [])
    return accs[0]


def _dsa_kernel(ql_ref, kx_ref, gds_ref, kid_ref, ckv_ref, ckvt_ref, qg_ref, wq_ref, wuk_ref, wuvt_ref,
                tab_ref, o_ref,
                sc_ref, pl_ref, cand_ref, lgt_ref, acc_ref, m_ref, qat_ref, qit_ref, w_ref, out_ref, *, topk):
    i = pl.program_id(1)
    q0 = i * QB
    ntile = jnp.right_shift(q0 + (QB + SK - 1), SK.bit_length() - 1)
    R = DS_KV_RANK
    GW = DS_HEADS * QB

    ql = ql_ref[...]
    ms = jnp.mean(ql * ql, axis=-1, keepdims=True)
    qn = (ql * lax.rsqrt(ms + NORM_EPS) * qg_ref[...]).astype(BF16)
    qt = _dot(qn, wq_ref[...]).T
    for h in range(DS_HEADS):
        qh = qt[h * DS_HEAD:(h + 1) * DS_HEAD, :].astype(BF16)
        qat_ref[:, h * QB:(h + 1) * QB] = (_dot(wuk_ref[h], qh) * (DS_HEAD ** -0.5 * LOG2E)).astype(BF16)
    for pr in range(IDX_HEADS // 2):
        base = DS_WIDTH + 2 * pr * IDX_DIM
        qit_ref[pr] = jnp.concatenate([qt[base:base + IDX_DIM, :],
                                       qt[base + IDX_DIM:base + 2 * IDX_DIM, :]], axis=1).astype(BF16)
    w_ref[...] = kx_ref[...].T[IDX_DIM:IDX_DIM + IDX_HEADS, :] * (IDX_HEADS ** -0.5 * IDX_DIM ** -0.5)

    lanei = lax.broadcasted_iota(I32, (1, QB), 1)
    csh = CHUNK.bit_length() - 1
    limit = jnp.left_shift(jnp.right_shift(q0 + lanei, csh) + 1, csh)
    rowi = lax.broadcasted_iota(I32, (SK, QB), 0)

    def score_tile(kt, carry):
        off = pl.multiple_of(kt * SK, SK)
        kid = kid_ref[pl.ds(off, SK), :]
        s = jnp.zeros((SK, QB), F32)
        for pr in range(IDX_HEADS // 2):
            lg = _dot(kid, qit_ref[pr])
            s = s + w_ref[2 * pr:2 * pr + 1, :] * jnp.maximum(lg[:, 0:QB], 0.0)
            s = s + w_ref[2 * pr + 1:2 * pr + 2, :] * jnp.maximum(lg[:, QB:2 * QB], 0.0)
        adm = (off + rowi) < limit
        sc_ref[pl.ds(off, SK), :] = jnp.where(adm, s, -jnp.inf)
        return carry

    lax.fori_loop(0, ntile, score_tile, 0)

    def key_to_f32(key):
        return pltpu.bitcast(jnp.where(key < 0, key ^ 0x7FFFFFFF, key), F32)

    def count(pred):
        def body(kt, acc):
            off = pl.multiple_of(kt * SK, SK)
            sc = sc_ref[pl.ds(off, SK), :]
            return acc + _fold_rows(jnp.where(pred(sc, off), 1, 0).astype(I32), jnp.add)
        acc = lax.fori_loop(0, ntile, body, jnp.zeros((SUBLANES, QB), I32))
        return jnp.sum(acc, axis=0, keepdims=True)

    def bit_step(it, carry):
        lo, cnt_lo = carry
        cand = lo + jnp.left_shift(jnp.int32(1), 31 - it)
        cf = key_to_f32(cand)
        cnt = count(lambda sc, off: sc >= cf)
        take = cnt >= topk
        return jnp.where(take, cand, lo), jnp.where(take, cnt, cnt_lo)

    def all_settled(cnt_lo):
        return jnp.min(jnp.where((cnt_lo == topk) | (limit < topk), 1, 0))

    def slow_search():
        lo, cnt_lo = lax.fori_loop(0, SEARCH_MIN_BITS, bit_step,
                                   (jnp.full((1, QB), INT_MIN, I32), jnp.full((1, QB), 2 ** 30, I32)))

        def more_bits(c):
            it, lo, cnt_lo, _ = c
            lo, cnt_lo = lax.fori_loop(it, it + SEARCH_STEP_BITS, bit_step, (lo, cnt_lo))
            return it + SEARCH_STEP_BITS, lo, cnt_lo, all_settled(cnt_lo)

        _, lo, _, settled = lax.while_loop(lambda c: (c[0] < 32) & (c[3] == 0), more_bits,
                                           (jnp.int32(SEARCH_MIN_BITS), lo, cnt_lo, all_settled(cnt_lo)))
        return lo, settled

    ngrp_max = sc_ref.shape[0] // PLANE_ROWS
    ngrp = ntile * (SK // PLANE_ROWS)

    @pl.when(i == 0)
    def _():
        pl_ref[...] = jnp.zeros_like(pl_ref)

    cand_ref[...] = jnp.zeros_like(cand_ref)

    def to_planes(g, c):
        base = pl.multiple_of(g * PLANE_ROWS, PLANE_ROWS)
        words = []
        for j in range(32):
            bits = pltpu.bitcast(sc_ref[pl.ds(base + SUBLANES * j, SUBLANES), :], I32)
            words.append(jnp.where(bits < 0, ~bits, bits | INT_MIN))
        j, m = 16, 0x0000FFFF
        while j:
            k = 0
            while k < 32:
                t = (words[k] ^ lax.shift_right_logical(words[k + j], j)) & m
                words[k] = words[k] ^ t
                words[k + j] = words[k + j] ^ lax.shift_left(t, j)
                k = (k + j + 1) & ~j
            j >>= 1
            m = (m ^ (m << j)) & 0xFFFFFFFF
        for b in range(32):
            pl_ref[pl.ds(base + SUBLANES * b, SUBLANES), :] = words[b]
        cand_ref[pl.ds(pl.multiple_of(g * SUBLANES, SUBLANES), SUBLANES), :] = jnp.full((SUBLANES, QB), -1, I32)
        return c

    lax.fori_loop(0, ngrp, to_planes, 0)

    def radix_step(it, c):
        above, tb = c
        prow = pl.multiple_of(it * SUBLANES, SUBLANES)
        acc = jnp.zeros((SUBLANES, QB), I32)
        for g in range(ngrp_max):
            plane = pl_ref[pl.ds(g * PLANE_ROWS + prow, SUBLANES), :]
            acc = acc + lax.population_count(cand_ref[g * SUBLANES:(g + 1) * SUBLANES, :] & plane)
        ones = jnp.sum(acc, axis=0, keepdims=True)
        take = above + ones >= topk
        for g in range(ngrp_max):
            plane = pl_ref[pl.ds(g * PLANE_ROWS + prow, SUBLANES), :]
            cand = cand_ref[g * SUBLANES:(g + 1) * SUBLANES, :]
            cand_ref[g * SUBLANES:(g + 1) * SUBLANES, :] = jnp.where(take, cand & plane, cand & ~plane)
        return (jnp.where(take, above, above + ones),
                jnp.where(take, tb | lax.shift_left(jnp.int32(1), 31 - it), tb))

    _, tbits = lax.fori_loop(0, 32, radix_step, (jnp.zeros((1, QB), I32), jnp.zeros((1, QB), I32)))
    lo_fast = tbits ^ INT_MIN
    thr_fast = key_to_f32(jnp.maximum(lo_fast, KEY_NEG_INF))
    fast_ok = all_settled(count(lambda sc, off: sc >= thr_fast))
    lo, settled = lax.cond(fast_ok == 1, lambda: (lo_fast, fast_ok), slow_search)
    has_thr = lo > KEY_NEG_INF
    thr = key_to_f32(jnp.maximum(lo, KEY_NEG_INF))
    nbits = max(1, (sc_ref.shape[0] - 1).bit_length())
    take_all_ties = jnp.full((1, QB), sc_ref.shape[0], I32)

    def tie_cut():
        cnt_gt = count(lambda sc, off: sc > thr)
        cnt_eq = count(lambda sc, off: sc == thr)
        tied = (cnt_gt + cnt_eq > topk) & has_thr

        def search_cut():
            def idx_step(it, m):
                cand = m + jnp.left_shift(jnp.int32(1), nbits - 1 - it)
                cnt = cnt_gt + count(lambda sc, off: (sc == thr) & ((off + rowi) < cand))
                return jnp.where(cnt < topk, cand, m)
            return lax.fori_loop(0, nbits, idx_step, jnp.zeros((1, QB), I32))

        return lax.cond(jnp.max(jnp.where(tied, 1, 0)) > 0, search_cut, lambda: take_all_ties)

    cut = lax.cond(settled == 1, lambda: take_all_ties, tie_cut)
    cut = jnp.where(has_thr, cut, -1)

    def selection_mask(off):
        sc = sc_ref[pl.ds(off, SK), :]
        sel = (sc > thr) | ((sc == thr) & ((off + rowi) <= cut))
        return jnp.where(sel, 0.0, MASK_NEG).astype(F32)

    m_ref[...] = jnp.full_like(m_ref, MASK_NEG)
    acc_ref[...] = jnp.zeros_like(acc_ref)
    near_lo = q0 - BQ

    def attend(off, bias_off):
        rows = pl.ds(off, SK)
        s = _dot(ckv_ref[rows, :], qat_ref[...])
        mk = selection_mask(off)
        m_old = m_ref[...]
        tmax = []
        for h in range(DS_HEADS):
            cs = slice(h * QB, (h + 1) * QB)
            t = s[:, cs] + mk
            if bias_off is not None:
                t = t + jnp.concatenate(
                    [tab_ref[pl.ds(pl.multiple_of(jnp.maximum(bias_off - j * BQ, 0), BQ), SK),
                             h * BQ:(h + 1) * BQ] for j in range(QB // BQ)], axis=1)
            lgt_ref[:, cs] = t
            tmax.append(jnp.max(_fold_rows(t, jnp.maximum), axis=0, keepdims=True))
        m_new = jnp.maximum(m_old, jnp.concatenate(tmax, axis=1))
        m_ref[...] = m_new
        pr = jnp.exp2(lgt_ref[...] - m_new).astype(BF16)
        acc_ref[...] = acc_ref[...] * jnp.exp2(m_old - m_new) + _dot(ckvt_ref[:, rows], pr)

    def far_tile(kt, c):
        attend(pl.multiple_of(kt * SK, SK), None)
        return c

    def edge_tile(kt, c):
        off = pl.multiple_of(kt * SK, SK)
        attend(off, SK + off - near_lo)
        return c

    nfar = jnp.maximum(ntile - 2, 0)
    lax.fori_loop(0, nfar, far_tile, 0)
    lax.fori_loop(nfar, ntile, edge_tile, 0)

    o_lat = acc_ref[0:R, :] * (1.0 / acc_ref[R:R + 1, :])
    for h in range(DS_HEADS):
        out_ref[h * DS_HEAD:(h + 1) * DS_HEAD, :] = _dot(
            wuvt_ref[h], o_lat[:, h * QB:(h + 1) * QB].astype(BF16))

    g = gds_ref[...]
    o_ref[...] = (out_ref[...].T * (g * _sigmoid(g))).astype(BF16)


def _dsa(z, ckv, kid, B, T, q_norm_g, w_uq, w_uk, w_uv, iw_q, tab):
    nq = T // QB
    topk = min(TOPK_MAX, T // 4)
    R = DS_KV_RANK
    wq = jnp.concatenate([w_uq, iw_q], axis=1).astype(BF16)
    wuk_h = jnp.transpose(w_uk, (1, 0, 2)).astype(BF16)
    wuv_t = jnp.transpose(w_uv, (1, 2, 0)).astype(BF16)
    ckv_t = jnp.concatenate([jnp.swapaxes(ckv.reshape(B, T, R), 1, 2),
                             jnp.ones((B, ONES_ROWS, T), BF16)], axis=1).reshape(B * (R + ONES_ROWS), T)
    const2 = lambda b, i: (0, 0)
    const3 = lambda b, i: (0, 0, 0)
    resident = dict(pipeline_mode=pl.Buffered(1))
    return pl.pallas_call(
        functools.partial(_dsa_kernel, topk=topk),
        grid=(B, nq),
        in_specs=[pl.BlockSpec((QB, DS_Q_RANK), lambda b, i: (b * nq + i, COL_Q // DS_Q_RANK)),
                  pl.BlockSpec((QB, LANES), lambda b, i: (b * nq + i, COL_KX // LANES)),
                  pl.BlockSpec((QB, DS_WIDTH), lambda b, i: (b * nq + i, COL_GDS // DS_WIDTH)),
                  pl.BlockSpec((T, IDX_DIM), lambda b, i: (b, 0), **resident),
                  pl.BlockSpec((T, R), lambda b, i: (b, 0), **resident),
                  pl.BlockSpec((R + ONES_ROWS, T), lambda b, i: (b, 0), **resident),
                  pl.BlockSpec((1, DS_Q_RANK), const2),
                  pl.BlockSpec((DS_Q_RANK, 2 * DS_WIDTH), const2, **resident),
                  pl.BlockSpec((DS_HEADS, R, DS_HEAD), const3, **resident),
                  pl.BlockSpec((DS_HEADS, DS_HEAD, R), const3, **resident),
                  pl.BlockSpec((BIAS_ROWS, DS_HEADS * BQ), const2, **resident)],
        out_specs=pl.BlockSpec((QB, DS_WIDTH), lambda b, i: (b * nq + i, 0)),
        out_shape=jax.ShapeDtypeStruct((B * T, DS_WIDTH), BF16),
        scratch_shapes=[pltpu.VMEM((T, QB), F32),
                        pltpu.VMEM((T, QB), I32),
                        pltpu.VMEM((T // PLANE_ROWS * SUBLANES, QB), I32),
                        pltpu.VMEM((SK, DS_HEADS * QB), F32),
                        pltpu.VMEM((R + ONES_ROWS, DS_HEADS * QB), F32),
                        pltpu.VMEM((1, DS_HEADS * QB), F32),
                        pltpu.VMEM((R, DS_HEADS * QB), BF16),
                        pltpu.VMEM((IDX_HEADS // 2, IDX_DIM, 2 * QB), BF16),
                        pltpu.VMEM((IDX_HEADS, QB), F32),
                        pltpu.VMEM((DS_WIDTH, QB), F32)],
        compiler_params=pltpu.CompilerParams(dimension_semantics=("parallel", "arbitrary"),
                                             vmem_limit_bytes=VMEM_LIMIT),
        name="dsa",
    )(z, z, z, kid, ckv, ckv_t, q_norm_g.reshape(1, -1).astype(F32), wq, wuk_h, wuv_t, tab)


def _post_kernel(x_ref, a1_ref, a2_ref, p_ref, w1_ref, w2_ref, pw_ref, gw_ref, fg_ref, o_ref):
    h = x_ref[...] + _dot(a1_ref[...], w1_ref[...]) + _dot(a2_ref[...], w2_ref[...])
    e = _dot(p_ref[...].astype(BF16), pw_ref[...])
    gate = _sigmoid(_dot(h.astype(BF16), gw_ref[...]))
    h2 = h + e * gate
    ms = jnp.mean(h2 * h2, axis=-1, keepdims=True)
    o_ref[...] = h2 * lax.rsqrt(ms + NORM_EPS) * fg_ref[...]


def _post(x2, o_rw, o_ds, p2, w_out, ple_w, gate_w, final_g, tm=256):
    m, d = x2.shape
    kh = o_rw.shape[1]
    pd = p2.shape[1]
    resident = dict(pipeline_mode=pl.Buffered(1))
    return pl.pallas_call(
        _post_kernel,
        grid=(m // tm,),
        in_specs=[pl.BlockSpec((tm, d), lambda i: (i, 0)),
                  pl.BlockSpec((tm, kh), lambda i: (i, 0)),
                  pl.BlockSpec((tm, kh), lambda i: (i, 0)),
                  pl.BlockSpec((tm, pd), lambda i: (i, 0)),
                  pl.BlockSpec((kh, d), lambda i: (0, 0), **resident),
                  pl.BlockSpec((kh, d), lambda i: (1, 0), **resident),
                  pl.BlockSpec((pd, d), lambda i: (0, 0), **resident),
                  pl.BlockSpec((d, d), lambda i: (0, 0), **resident),
                  pl.BlockSpec((1, d), lambda i: (0, 0))],
        out_specs=pl.BlockSpec((tm, d), lambda i: (i, 0)),
        out_shape=jax.ShapeDtypeStruct((m, d), F32),
        compiler_params=pltpu.CompilerParams(dimension_semantics=("parallel",),
                                             vmem_limit_bytes=VMEM_LIMIT),
        name="post",
    )(x2, o_rw, o_ds, p2, w_out, w_out, ple_w, gate_w, final_g)


def _split_w_in(w):
    w = w.astype(BF16)
    return w, w[:, w.shape[1] - DS_WIDTH:]


def kernel(x, p, w_in, norm_g, rw_mu, rw_w0, rw_w_up, rw_a0, rw_a_up, rw_k_k, rw_k_a, rw_r_k, rw_ln_g, rw_ln_b, ds_q_norm_g, ds_kv_norm_g, idx_k_norm_g, ds_w_uq, ds_w_uk, ds_w_uv, idx_w_q, rel_bias, w_out, ple_w, ple_gate_w, final_g):
    B, T, D = x.shape
    depth = w_in.shape[0]
    assert depth == 1 and T % SK == 0 and T % QB == 0 and T % RW_CHUNK == 0 and (B * T) % 256 == 0
    h = x.reshape(B * T, D)
    tab = _biastab(rel_bias)
    for i in range(depth):
        z, ckv, kid = _inproj(h, norm_g[i].reshape(1, D), *_split_w_in(w_in[i]),
                              ds_kv_norm_g[i].reshape(1, -1), idx_k_norm_g[i].reshape(1, -1))
        o_rw = _rwkv(z, B, T, rw_mu[i], rw_w0[i], rw_a0[i], rw_k_k[i], rw_k_a[i],
                     rw_r_k[i].reshape(-1), rw_ln_g[i], rw_ln_b[i], rw_w_up[i], rw_a_up[i])
        o_ds = _dsa(z, ckv, kid, B, T, ds_q_norm_g[i], ds_w_uq[i], ds_w_uk[i], ds_w_uv[i],
                    idx_w_q[i], tab)
        h = _post(h, o_rw, o_ds, p[i].reshape(B * T, -1), w_out[i].astype(BF16),
                  ple_w[i].astype(BF16), ple_gate_w[i].astype(BF16), final_g.reshape(1, D))
    return h.reshape(B, T, D)
```

```python
import functools
import math

import jax
import jax.numpy as jnp
from jax import lax
from jax.experimental import pallas as pl
from jax.experimental.pallas import tpu as pltpu

F32 = jnp.float32
BF16 = jnp.bfloat16
I32 = jnp.int32

RW_WIDTH = 1024
RW_HEAD = 64
RW_HEADS = 16
RW_LORA = 64
DS_WIDTH = 1024
DS_HEAD = 64
DS_HEADS = 16
DS_Q_RANK = 384
DS_KV_RANK = 256
IDX_HEADS = 16
IDX_DIM = 64
TOPK_MAX = 256
CHUNK = 64
NUM_BUCKETS = 32
MAX_DISTANCE = 128
NORM_EPS = 1e-6
GN_EPS = 64e-5

COL_R, COL_K, COL_V, COL_GRW, COL_GDS = 0, 1024, 2048, 3072, 4096
COL_KV = 5120
COL_Q = 5376
COL_WA = 5760
COL_KX = 5888
Z_WIDTH = 6016

LANES = 128
SUBLANES = 8
QB = 256
BQ = 128
SK = 512
BIAS_ROWS = 2 * SK + 2 * BQ
RW_CHUNK = 64
INT_MIN = -2 ** 31
KEY_NEG_INF = -2139095041
PLANE_ROWS = 32 * SUBLANES
SEARCH_MIN_BITS = 23
SEARCH_STEP_BITS = 3
assert (32 - SEARCH_MIN_BITS) % SEARCH_STEP_BITS == 0
MASK_NEG = -1e30
LOG2E = 1.4426950408889634
ONES_ROWS = 16
VMEM_LIMIT = 52 * 1024 * 1024


def _sigmoid(x):
    return 1.0 / (1.0 + jnp.exp(-x))


def _dot(a, b):
    return jnp.dot(a, b, preferred_element_type=F32)


def _dot_nt(a, b):
    return lax.dot_general(a, b, (((1,), (1,)), ((), ())), preferred_element_type=F32)


def _dot_tn(a, b):
    return lax.dot_general(a, b, (((0,), (0,)), ((), ())), preferred_element_type=F32)


def _inproj_plan():
    s_wa = 3 * RW_WIDTH
    s_grw = s_wa + 2 * RW_LORA
    s_q = s_grw + RW_WIDTH
    s_kv = s_q + DS_Q_RANK
    s_kx = s_kv + DS_KV_RANK
    return [(COL_R, 0, 3 * RW_WIDTH), (COL_GRW, s_grw, RW_WIDTH), (COL_GDS, None, DS_WIDTH),
            (COL_KV, s_kv, DS_KV_RANK), (COL_Q, s_q, DS_Q_RANK), (COL_WA, s_wa, LANES),
            (COL_KX, s_kx, LANES)]


def _inproj_kernel(x_ref, g_ref, w_ref, wg_ref, gkv_ref, gik_ref, o_ref, ckv_ref, kid_ref, *, tn):
    x = x_ref[...]
    ms = jnp.mean(x * x, axis=-1, keepdims=True)
    xn = (x * lax.rsqrt(ms + NORM_EPS) * g_ref[...]).astype(BF16)
    for dst, src, width in _inproj_plan():
        for j in range(0, width, tn):
            w = min(tn, width - j)
            wt = wg_ref[:, j:j + w] if src is None else w_ref[:, src + j:src + j + w]
            o_ref[:, dst + j:dst + j + w] = _dot(xn, wt)
    kv = o_ref[:, COL_KV:COL_KV + DS_KV_RANK]
    ckv_ref[...] = (kv * lax.rsqrt(jnp.mean(kv * kv, axis=-1, keepdims=True) + NORM_EPS)
                    * gkv_ref[...]).astype(BF16)
    ki = o_ref[:, COL_KX:COL_KX + IDX_DIM]
    kid_ref[...] = (ki * lax.rsqrt(jnp.mean(ki * ki, axis=-1, keepdims=True) + NORM_EPS)
                    * gik_ref[...]).astype(BF16)


def _inproj(x2, g, w, w_gds, gkv, gik, tm=256, tn=768):
    m, d = x2.shape
    resident = dict(pipeline_mode=pl.Buffered(1))
    return pl.pallas_call(
        functools.partial(_inproj_kernel, tn=tn),
        grid=(m // tm,),
        in_specs=[pl.BlockSpec((tm, d), lambda i: (i, 0)),
                  pl.BlockSpec((1, d), lambda i: (0, 0)),
                  pl.BlockSpec(w.shape, lambda i: (0, 0), **resident),
                  pl.BlockSpec(w_gds.shape, lambda i: (0, 0), **resident),
                  pl.BlockSpec((1, DS_KV_RANK), lambda i: (0, 0)),
                  pl.BlockSpec((1, IDX_DIM), lambda i: (0, 0))],
        out_specs=[pl.BlockSpec((tm, Z_WIDTH), lambda i: (i, 0)),
                   pl.BlockSpec((tm, DS_KV_RANK), lambda i: (i, 0)),
                   pl.BlockSpec((tm, IDX_DIM), lambda i: (i, 0))],
        out_shape=[jax.ShapeDtypeStruct((m, Z_WIDTH), F32),
                   jax.ShapeDtypeStruct((m, DS_KV_RANK), BF16),
                   jax.ShapeDtypeStruct((m, IDX_DIM), BF16)],
        compiler_params=pltpu.CompilerParams(dimension_semantics=("parallel",),
                                             vmem_limit_bytes=VMEM_LIMIT),
        name="inproj",
    )(x2, g, w, w_gds, gkv, gik)


def _rwkv_kernel(r_ref, k_ref, v_ref, g_ref, wa_ref,
                 mur_ref, muk_ref, muv_ref, muwa_ref,
                 w0_ref, a0_ref, kk_ref, ka_ref, rk_ref, lng_ref, lnb_ref,
                 wup_ref, aup_ref,
                 o_ref,
                 pr_ref, pk_ref, pv_ref, pwa_ref, st_ref, *, nb):
    C = RW_CHUNK
    N = RW_HEAD

    @pl.when(pl.program_id(0) == 0)
    def _():
        pr_ref[...] = jnp.zeros_like(pr_ref)
        pk_ref[...] = jnp.zeros_like(pk_ref)
        pv_ref[...] = jnp.zeros_like(pv_ref)
        pwa_ref[...] = jnp.zeros_like(pwa_ref)
        st_ref[...] = jnp.zeros_like(st_ref)

    row = lax.broadcasted_iota(I32, (nb * C, 1), 0)

    def shift(ref, prev_ref, mu_ref):
        z = ref[...].reshape(nb * C, ref.shape[2])
        zp = pltpu.roll(z, 1, 0)
        for b in range(nb):
            zp = jnp.where(row == b * C, prev_ref[b], zp)
            prev_ref[b] = z[(b + 1) * C - 1:(b + 1) * C, :]
        return z + mu_ref[...] * (zp - z)

    r = shift(r_ref, pr_ref, mur_ref)
    k = shift(k_ref, pk_ref, muk_ref)
    v = shift(v_ref, pv_ref, muv_ref)
    wa = shift(wa_ref, pwa_ref, muwa_ref)
    wd = wa[:, 0:RW_LORA]
    ad = wa[:, RW_LORA:2 * RW_LORA]

    wl = w0_ref[...] + _dot(jnp.tanh(wd).astype(BF16), wup_ref[...])
    nwl = -wl
    softplus = jnp.maximum(nwl, 0.0) + jnp.log(1.0 + jnp.exp(-jnp.abs(nwl)))
    w_log = -softplus - 0.5
    lw = -jnp.exp(w_log)
    a = _sigmoid(a0_ref[...] + _dot(ad.astype(BF16), aup_ref[...]))
    kk = k * kk_ref[...]
    k2 = k * (1.0 + (a - 1.0) * ka_ref[...])

    ti = lax.broadcasted_iota(I32, (C, C), 0)
    tj = lax.broadcasted_iota(I32, (C, C), 1)
    incl = ti >= tj
    strict = ti > tj
    tri = jnp.where(incl, 1.0, 0.0).astype(F32)

    def per_row(mat, x):
        hi = x.astype(BF16)
        r1 = x - hi.astype(F32)
        mid = r1.astype(BF16)
        lo = (r1 - mid.astype(F32)).astype(BF16)
        mb = mat.astype(BF16)
        return jnp.concatenate(
            [_dot(mb, hi[b * C:(b + 1) * C]) + _dot(mb, mid[b * C:(b + 1) * C]) + _dot(mb, lo[b * C:(b + 1) * C])
             for b in range(nb)], axis=0)

    cum = per_row(tri, lw)
    p = jnp.exp(cum)
    pinv = jnp.exp(-cum)
    pprev = jnp.exp(cum - lw)
    tot = per_row(jnp.ones((C, C), F32), lw)
    pend = jnp.exp(tot)

    g = g_ref[...].reshape(nb * C, RW_WIDTH)
    gate = g * _sigmoid(g)

    HP = RW_HEADS // 2
    NP = nb * HP

    def pairs(x):
        return jnp.stack([x[b * C:(b + 1) * C, j * LANES:(j + 1) * LANES]
                          for b in range(nb) for j in range(HP)], axis=0)

    def per_pair(ref):
        return jnp.concatenate([ref[...]] * nb, axis=0)

    lane = lax.broadcasted_iota(I32, (1, 1, LANES), 2)
    m_lo = jnp.where(lane < N, 1.0, 0.0).astype(BF16)
    m_hi = jnp.where(lane < N, 0.0, 1.0).astype(BF16)
    bi = lax.broadcasted_iota(I32, (LANES, LANES), 0)
    bj = lax.broadcasted_iota(I32, (LANES, LANES), 1)
    same_head = (bi < N) == (bj < N)
    ones_bd = jnp.where(same_head, 1.0, 0.0).astype(BF16)

    def head_sum(x):
        return _dot(x.reshape(NP * C, LANES).astype(BF16), ones_bd).reshape(NP, C, LANES)

    def halves(x):
        xb = x.astype(BF16)
        return jnp.concatenate([xb * m_lo, xb * m_hi], axis=1)

    def bmm(x, y):
        return lax.dot_general(x, y, (((2,), (1,)), ((0,), (0,))), preferred_element_type=F32)

    def bmm_nt(x, y):
        return lax.dot_general(x, y, (((2,), (2,)), ((0,), (0,))), preferred_element_type=F32)

    def block_mask(nblk, cmp):
        wi = lax.broadcasted_iota(I32, (C, nblk * C), 0)
        wj = lax.broadcasted_iota(I32, (C, nblk * C), 1) & (C - 1)
        return cmp(wi, wj)

    r_p, k2_p, v_p, a_p = pairs(r), pairs(k2), pairs(v), pairs(a)
    p_p, pinv_p, pprev_p = pairs(p), pairs(pinv), pairs(pprev)
    kk_p = pairs(kk)
    kkn = kk_p * lax.rsqrt(jnp.maximum(head_sum(kk_p * kk_p), 1e-24))
    at = (-kkn) * pprev_p
    bt = (kkn * a_p) * pinv_p
    kt = k2_p * pinv_p
    rt = r_p * p_p
    pend_p = pairs(pend)
    pend2 = jnp.concatenate([pend_p, pend_p], axis=1)

    lhs2 = jnp.concatenate([at, rt], axis=1).astype(BF16)
    rhs4 = jnp.concatenate([halves(kt), halves(bt)], axis=1)
    gc = bmm_nt(lhs2, rhs4)
    strict2 = block_mask(2, lambda i_, j_: i_ > j_)
    incl4 = block_mask(4, lambda i_, j_: i_ >= j_)
    a_ak = jnp.where(strict2, gc[:, 0:C, 0:2 * C], 0.0)
    nmat = jnp.where(strict2, gc[:, 0:C, 2 * C:4 * C], 0.0)
    a_rkb = jnp.where(incl4, gc[:, C:2 * C, :], 0.0)

    g0 = st_ref[...]
    sg = bmm_nt(lhs2, g0.astype(BF16))
    vm2 = halves(v_p)
    u = sg[:, 0:C] + bmm(a_ak.astype(BF16), vm2)
    pw = nmat.astype(BF16)
    u = u + bmm(pw, halves(u))
    n = 1
    while 2 * n < C:
        pw = bmm(pw, halves(pw)).astype(BF16)
        u = u + bmm(pw, halves(u))
        n *= 2
    um2b = halves(u)
    y = sg[:, C:2 * C] + bmm(a_rkb.astype(BF16), jnp.concatenate([vm2, um2b], axis=1))
    uv = jnp.concatenate([u, v_p], axis=1).astype(BF16)
    bkh = (jnp.concatenate([bt, kt], axis=1) * pend2).astype(BF16)
    upd = lax.dot_general(uv, bkh, (((1,), (1,)), ((0,), (0,))), preferred_element_type=F32)
    st_ref[...] = g0 * pend2 + jnp.where(same_head, upd, 0.0)

    inv_n = 1.0 / N
    yc = y - head_sum(y) * inv_n
    var = head_sum(yc * yc) * inv_n
    yn = yc * lax.rsqrt(var + GN_EPS) * per_pair(lng_ref) + per_pair(lnb_ref)
    bonus = head_sum(r_p * k2_p * per_pair(rk_ref)) * v_p
    out = (yn + bonus) * pairs(gate)
    for b in range(nb):
        for j in range(HP):
            o_ref[b, :, j * LANES:(j + 1) * LANES] = out[b * HP + j].astype(BF16)


def _rwkv(z, B, T, mu, w0, a0, k_k, k_a, r_k, ln_g, ln_b, w_up, a_up):
    C = RW_CHUNK
    W = RW_WIDTH
    z3 = z.reshape(B, T, z.shape[1])
    row = lambda a: a.reshape(1, -1).astype(F32)
    mu_r, mu_k, mu_v = mu[0:W], mu[W:2 * W], mu[2 * W:3 * W]
    mu_wa = mu[3 * W:3 * W + 2 * RW_LORA]
    zspec = lambda col: pl.BlockSpec((B, C, W), lambda c: (0, c, col // W))
    pspec = lambda width: pl.BlockSpec((1, width), lambda c: (0, 0))
    npairs = RW_HEADS // 2
    prow = lambda a: a.reshape(npairs, 1, LANES).astype(F32)
    ppspec = pl.BlockSpec((npairs, 1, LANES), lambda c: (0, 0, 0))
    wspec = pl.BlockSpec((RW_LORA, W), lambda c: (0, 0))
    out = pl.pallas_call(
        functools.partial(_rwkv_kernel, nb=B),
        grid=(T // C,),
        in_specs=[zspec(COL_R), zspec(COL_K), zspec(COL_V), zspec(COL_GRW),
                  pl.BlockSpec((B, C, LANES), lambda c: (0, c, COL_WA // LANES)),
                  pspec(W), pspec(W), pspec(W), pspec(LANES),
                  pspec(W), pspec(W), pspec(W), pspec(W), ppspec, ppspec, ppspec,
                  wspec, wspec],
        out_specs=pl.BlockSpec((B, C, W), lambda c: (0, c, 0)),
        out_shape=jax.ShapeDtypeStruct((B, T, W), BF16),
        scratch_shapes=[pltpu.VMEM((B, 1, W), F32), pltpu.VMEM((B, 1, W), F32), pltpu.VMEM((B, 1, W), F32),
                        pltpu.VMEM((B, 1, LANES), F32),
                        pltpu.VMEM((B * npairs, LANES, LANES), F32)],
        compiler_params=pltpu.CompilerParams(dimension_semantics=("arbitrary",),
                                             vmem_limit_bytes=VMEM_LIMIT),
        name="rwkv",
    )(z3, z3, z3, z3, z3,
      row(mu_r), row(mu_k), row(mu_v), row(mu_wa),
      row(w0), row(a0), row(k_k), row(k_a), prow(r_k), prow(ln_g), prow(ln_b),
      w_up.astype(BF16), a_up.astype(BF16))
    return out.reshape(B * T, W)


def _biastab_kernel(rb_ref, o_ref):
    o_ref[...] = jnp.zeros_like(o_ref)
    nb = NUM_BUCKETS // 2
    max_exact = nb // 2
    c = lax.broadcasted_iota(I32, (2 * BQ, BQ), 0)
    r = lax.broadcasted_iota(I32, (2 * BQ, BQ), 1)
    rel = c - BQ - r
    ret = jnp.where(rel > 0, nb, 0)
    n = jnp.abs(rel)
    nf = jnp.maximum(n, 1).astype(F32)
    large = max_exact + (jnp.log(nf / max_exact) / math.log(MAX_DISTANCE / max_exact)
                         * (nb - max_exact)).astype(I32)
    large = jnp.minimum(large, nb - 1) & (NUM_BUCKETS - 1)
    bucket = ret + jnp.where(n < max_exact, n, large)
    for h in range(DS_HEADS):
        far = rb_ref[nb - 1, h]
        acc = jnp.zeros((2 * BQ, BQ), F32)
        for b in range(NUM_BUCKETS):
            acc = jnp.where(bucket == b, rb_ref[b, h] - far, acc)
        o_ref[SK:SK + 2 * BQ, h * BQ:(h + 1) * BQ] = acc * LOG2E


def _biastab(rel_bias):
    return pl.pallas_call(
        _biastab_kernel,
        in_specs=[pl.BlockSpec(memory_space=pltpu.SMEM)],
        out_specs=pl.BlockSpec(memory_space=pltpu.VMEM),
        out_shape=jax.ShapeDtypeStruct((BIAS_ROWS, DS_HEADS * BQ), F32),
        compiler_params=pltpu.CompilerParams(vmem_limit_bytes=VMEM_LIMIT),
        name="biastab",
    )(rel_bias.astype(F32))


def _fold_rows(x, op):
    n = x.shape[0] // SUBLANES
    accs = [x[j * SUBLANES:(j + 1) * SUBLANES] for j in range(min(4, n))]
    for j in range(4, n):
        accs[j % 4] = op(accs[j % 4], x[j * SUBLANES:(j + 1) * SUBLANES])
    while len(accs) > 1:
        accs = [op(accs[j], accs[j + 1]) for j in range(0, len(accs) - 1, 2)] + (
            [accs[-1]] if len(accs) % 2 else [])
    return accs[0]


def _dsa_kernel(ql_ref, kx_ref, gds_ref, kid_ref, ckv_ref, ckvt_ref, qg_ref, wq_ref, wuk_ref, wuvt_ref,
                tab_ref, o_ref,
                sc_ref, pl_ref, cand_ref, lgt_ref, acc_ref, m_ref, qat_ref, qit_ref, w_ref, out_ref, *, topk):
    i = pl.program_id(1)
    q0 = i * QB
    ntile = jnp.right_shift(q0 + (QB + SK - 1), SK.bit_length() - 1)
    R = DS_KV_RANK
    GW = DS_HEADS * QB

    ql = ql_ref[...]
    ms = jnp.mean(ql * ql, axis=-1, keepdims=True)
    qn = (ql * lax.rsqrt(ms + NORM_EPS) * qg_ref[...]).astype(BF16)
    qt = _dot(qn, wq_ref[...]).T
    for h in range(DS_HEADS):
        qh = qt[h * DS_HEAD:(h + 1) * DS_HEAD, :].astype(BF16)
        qat_ref[:, h * QB:(h + 1) * QB] = (_dot(wuk_ref[h], qh) * (DS_HEAD ** -0.5 * LOG2E)).astype(BF16)
    for pr in range(IDX_HEADS // 2):
        base = DS_WIDTH + 2 * pr * IDX_DIM
        qit_ref[pr] = jnp.concatenate([qt[base:base + IDX_DIM, :],
                                       qt[base + IDX_DIM:base + 2 * IDX_DIM, :]], axis=1).astype(BF16)
    w_ref[...] = kx_ref[...].T[IDX_DIM:IDX_DIM + IDX_HEADS, :] * (IDX_HEADS ** -0.5 * IDX_DIM ** -0.5)

    lanei = lax.broadcasted_iota(I32, (1, QB), 1)
    csh = CHUNK.bit_length() - 1
    limit = jnp.left_shift(jnp.right_shift(q0 + lanei, csh) + 1, csh)
    rowi = lax.broadcasted_iota(I32, (SK, QB), 0)

    def score_tile(kt, carry):
        off = pl.multiple_of(kt * SK, SK)
        kid = kid_ref[pl.ds(off, SK), :]
        s = jnp.zeros((SK, QB), F32)
        for pr in range(IDX_HEADS // 2):
            lg = _dot(kid, qit_ref[pr])
            s = s + w_ref[2 * pr:2 * pr + 1, :] * jnp.maximum(lg[:, 0:QB], 0.0)
            s = s + w_ref[2 * pr + 1:2 * pr + 2, :] * jnp.maximum(lg[:, QB:2 * QB], 0.0)
        adm = (off + rowi) < limit
        sc_ref[pl.ds(off, SK), :] = jnp.where(adm, s, -jnp.inf)
        return carry

    lax.fori_loop(0, ntile, score_tile, 0)

    def key_to_f32(key):
        return pltpu.bitcast(jnp.where(key < 0, key ^ 0x7FFFFFFF, key), F32)

    def count(pred):
        def body(kt, acc):
            off = pl.multiple_of(kt * SK, SK)
            sc = sc_ref[pl.ds(off, SK), :]
            return acc + _fold_rows(jnp.where(pred(sc, off), 1, 0).astype(I32), jnp.add)
        acc = lax.fori_loop(0, ntile, body, jnp.zeros((SUBLANES, QB), I32))
        return jnp.sum(acc, axis=0, keepdims=True)

    def bit_step(it, carry):
        lo, cnt_lo = carry
        cand = lo + jnp.left_shift(jnp.int32(1), 31 - it)
        cf = key_to_f32(cand)
        cnt = count(lambda sc, off: sc >= cf)
        take = cnt >= topk
        return jnp.where(take, cand, lo), jnp.where(take, cnt, cnt_lo)

    def all_settled(cnt_lo):
        return jnp.min(jnp.where((cnt_lo == topk) | (limit < topk), 1, 0))

    def slow_search():
        lo, cnt_lo = lax.fori_loop(0, SEARCH_MIN_BITS, bit_step,
                                   (jnp.full((1, QB), INT_MIN, I32), jnp.full((1, QB), 2 ** 30, I32)))

        def more_bits(c):
            it, lo, cnt_lo, _ = c
            lo, cnt_lo = lax.fori_loop(it, it + SEARCH_STEP_BITS, bit_step, (lo, cnt_lo))
            return it + SEARCH_STEP_BITS, lo, cnt_lo, all_settled(cnt_lo)

        _, lo, _, settled = lax.while_loop(lambda c: (c[0] < 32) & (c[3] == 0), more_bits,
                                           (jnp.int32(SEARCH_MIN_BITS), lo, cnt_lo, all_settled(cnt_lo)))
        return lo, settled

    ngrp_max = sc_ref.shape[0] // PLANE_ROWS
    ngrp = ntile * (SK // PLANE_ROWS)

    @pl.when(i == 0)
    def _():
        pl_ref[...] = jnp.zeros_like(pl_ref)

    cand_ref[...] = jnp.zeros_like(cand_ref)

    def to_planes(g, c):
        base = pl.multiple_of(g * PLANE_ROWS, PLANE_ROWS)
        words = []
        for j in range(32):
            bits = pltpu.bitcast(sc_ref[pl.ds(base + SUBLANES * j, SUBLANES), :], I32)
            words.append(jnp.where(bits < 0, ~bits, bits | INT_MIN))
        j, m = 16, 0x0000FFFF
        while j:
            k = 0
            while k < 32:
                t = (words[k] ^ lax.shift_right_logical(words[k + j], j)) & m
                words[k] = words[k] ^ t
                words[k + j] = words[k + j] ^ lax.shift_left(t, j)
                k = (k + j + 1) & ~j
            j >>= 1
            m = (m ^ (m << j)) & 0xFFFFFFFF
        for b in range(32):
            pl_ref[pl.ds(base + SUBLANES * b, SUBLANES), :] = words[b]
        cand_ref[pl.ds(pl.multiple_of(g * SUBLANES, SUBLANES), SUBLANES), :] = jnp.full((SUBLANES, QB), -1, I32)
        return c

    lax.fori_loop(0, ngrp, to_planes, 0)

    def radix_step(it, c):
        above, tb = c
        prow = pl.multiple_of(it * SUBLANES, SUBLANES)
        acc = jnp.zeros((SUBLANES, QB), I32)
        for g in range(ngrp_max):
            plane = pl_ref[pl.ds(g * PLANE_ROWS + prow, SUBLANES), :]
            acc = acc + lax.population_count(cand_ref[g * SUBLANES:(g + 1) * SUBLANES, :] & plane)
        ones = jnp.sum(acc, axis=0, keepdims=True)
        take = above + ones >= topk
        for g in range(ngrp_max):
            plane = pl_ref[pl.ds(g * PLANE_ROWS + prow, SUBLANES), :]
            cand = cand_ref[g * SUBLANES:(g + 1) * SUBLANES, :]
            cand_ref[g * SUBLANES:(g + 1) * SUBLANES, :] = jnp.where(take, cand & plane, cand & ~plane)
        return (jnp.where(take, above, above + ones),
                jnp.where(take, tb | lax.shift_left(jnp.int32(1), 31 - it), tb))

    _, tbits = lax.fori_loop(0, 32, radix_step, (jnp.zeros((1, QB), I32), jnp.zeros((1, QB), I32)))
    lo_fast = tbits ^ INT_MIN
    thr_fast = key_to_f32(jnp.maximum(lo_fast, KEY_NEG_INF))
    fast_ok = all_settled(count(lambda sc, off: sc >= thr_fast))
    lo, settled = lax.cond(fast_ok == 1, lambda: (lo_fast, fast_ok), slow_search)
    has_thr = lo > KEY_NEG_INF
    thr = key_to_f32(jnp.maximum(lo, KEY_NEG_INF))
    nbits = max(1, (sc_ref.shape[0] - 1).bit_length())
    take_all_ties = jnp.full((1, QB), sc_ref.shape[0], I32)

    def tie_cut():
        cnt_gt = count(lambda sc, off: sc > thr)
        cnt_eq = count(lambda sc, off: sc == thr)
        tied = (cnt_gt + cnt_eq > topk) & has_thr

        def search_cut():
            def idx_step(it, m):
                cand = m + jnp.left_shift(jnp.int32(1), nbits - 1 - it)
                cnt = cnt_gt + count(lambda sc, off: (sc == thr) & ((off + rowi) < cand))
                return jnp.where(cnt < topk, cand, m)
            return lax.fori_loop(0, nbits, idx_step, jnp.zeros((1, QB), I32))

        return lax.cond(jnp.max(jnp.where(tied, 1, 0)) > 0, search_cut, lambda: take_all_ties)

    cut = lax.cond(settled == 1, lambda: take_all_ties, tie_cut)
    cut = jnp.where(has_thr, cut, -1)

    def selection_mask(off):
        sc = sc_ref[pl.ds(off, SK), :]
        sel = (sc > thr) | ((sc == thr) & ((off + rowi) <= cut))
        return jnp.where(sel, 0.0, MASK_NEG).astype(F32)

    m_ref[...] = jnp.full_like(m_ref, MASK_NEG)
    acc_ref[...] = jnp.zeros_like(acc_ref)
    near_lo = q0 - BQ

    def attend(off, bias_off):
        rows = pl.ds(off, SK)
        s = _dot(ckv_ref[rows, :], qat_ref[...])
        mk = selection_mask(off)
        m_old = m_ref[...]
        tmax = []
        for h in range(DS_HEADS):
            cs = slice(h * QB, (h + 1) * QB)
            t = s[:, cs] + mk
            if bias_off is not None:
                t = t + jnp.concatenate(
                    [tab_ref[pl.ds(pl.multiple_of(jnp.maximum(bias_off - j * BQ, 0), BQ), SK),
                             h * BQ:(h + 1) * BQ] for j in range(QB // BQ)], axis=1)
            lgt_ref[:, cs] = t
            tmax.append(jnp.max(_fold_rows(t, jnp.maximum), axis=0, keepdims=True))
        m_new = jnp.maximum(m_old, jnp.concatenate(tmax, axis=1))
        m_ref[...] = m_new
        pr = jnp.exp2(lgt_ref[...] - m_new).astype(BF16)
        acc_ref[...] = acc_ref[...] * jnp.exp2(m_old - m_new) + _dot(ckvt_ref[:, rows], pr)

    def far_tile(kt, c):
        attend(pl.multiple_of(kt * SK, SK), None)
        return c

    def edge_tile(kt, c):
        off = pl.multiple_of(kt * SK, SK)
        attend(off, SK + off - near_lo)
        return c

    nfar = jnp.maximum(ntile - 2, 0)
    lax.fori_loop(0, nfar, far_tile, 0)
    lax.fori_loop(nfar, ntile, edge_tile, 0)

    o_lat = acc_ref[0:R, :] * (1.0 / acc_ref[R:R + 1, :])
    for h in range(DS_HEADS):
        out_ref[h * DS_HEAD:(h + 1) * DS_HEAD, :] = _dot(
            wuvt_ref[h], o_lat[:, h * QB:(h + 1) * QB].astype(BF16))

    g = gds_ref[...]
    o_ref[...] = (out_ref[...].T * (g * _sigmoid(g))).astype(BF16)


def _dsa(z, ckv, kid, B, T, q_norm_g, w_uq, w_uk, w_uv, iw_q, tab):
    nq = T // QB
    topk = min(TOPK_MAX, T // 4)
    R = DS_KV_RANK
    wq = jnp.concatenate([w_uq, iw_q], axis=1).astype(BF16)
    wuk_h = jnp.transpose(w_uk, (1, 0, 2)).astype(BF16)
    wuv_t = jnp.transpose(w_uv, (1, 2, 0)).astype(BF16)
    ckv_t = jnp.concatenate([jnp.swapaxes(ckv.reshape(B, T, R), 1, 2),
                             jnp.ones((B, ONES_ROWS, T), BF16)], axis=1).reshape(B * (R + ONES_ROWS), T)
    const2 = lambda b, i: (0, 0)
    const3 = lambda b, i: (0, 0, 0)
    resident = dict(pipeline_mode=pl.Buffered(1))
    return pl.pallas_call(
        functools.partial(_dsa_kernel, topk=topk),
        grid=(B, nq),
        in_specs=[pl.BlockSpec((QB, DS_Q_RANK), lambda b, i: (b * nq + i, COL_Q // DS_Q_RANK)),
                  pl.BlockSpec((QB, LANES), lambda b, i: (b * nq + i, COL_KX // LANES)),
                  pl.BlockSpec((QB, DS_WIDTH), lambda b, i: (b * nq + i, COL_GDS // DS_WIDTH)),
                  pl.BlockSpec((T, IDX_DIM), lambda b, i: (b, 0), **resident),
                  pl.BlockSpec((T, R), lambda b, i: (b, 0), **resident),
                  pl.BlockSpec((R + ONES_ROWS, T), lambda b, i: (b, 0), **resident),
                  pl.BlockSpec((1, DS_Q_RANK), const2),
                  pl.BlockSpec((DS_Q_RANK, 2 * DS_WIDTH), const2, **resident),
                  pl.BlockSpec((DS_HEADS, R, DS_HEAD), const3, **resident),
                  pl.BlockSpec((DS_HEADS, DS_HEAD, R), const3, **resident),
                  pl.BlockSpec((BIAS_ROWS, DS_HEADS * BQ), const2, **resident)],
        out_specs=pl.BlockSpec((QB, DS_WIDTH), lambda b, i: (b * nq + i, 0)),
        out_shape=jax.ShapeDtypeStruct((B * T, DS_WIDTH), BF16),
        scratch_shapes=[pltpu.VMEM((T, QB), F32),
                        pltpu.VMEM((T, QB), I32),
                        pltpu.VMEM((T // PLANE_ROWS * SUBLANES, QB), I32),
                        pltpu.VMEM((SK, DS_HEADS * QB), F32),
                        pltpu.VMEM((R + ONES_ROWS, DS_HEADS * QB), F32),
                        pltpu.VMEM((1, DS_HEADS * QB), F32),
                        pltpu.VMEM((R, DS_HEADS * QB), BF16),
                        pltpu.VMEM((IDX_HEADS // 2, IDX_DIM, 2 * QB), BF16),
                        pltpu.VMEM((IDX_HEADS, QB), F32),
                        pltpu.VMEM((DS_WIDTH, QB), F32)],
        compiler_params=pltpu.CompilerParams(dimension_semantics=("parallel", "arbitrary"),
                                             vmem_limit_bytes=VMEM_LIMIT),
        name="dsa",
    )(z, z, z, kid, ckv, ckv_t, q_norm_g.reshape(1, -1).astype(F32), wq, wuk_h, wuv_t, tab)


def _post_kernel(x_ref, a1_ref, a2_ref, p_ref, w1_ref, w2_ref, pw_ref, gw_ref, fg_ref, o_ref):
    h = x_ref[...] + _dot(a1_ref[...], w1_ref[...]) + _dot(a2_ref[...], w2_ref[...])
    e = _dot(p_ref[...].astype(BF16), pw_ref[...])
    gate = _sigmoid(_dot(h.astype(BF16), gw_ref[...]))
    h2 = h + e * gate
    ms = jnp.mean(h2 * h2, axis=-1, keepdims=True)
    o_ref[...] = h2 * lax.rsqrt(ms + NORM_EPS) * fg_ref[...]


def _post(x2, o_rw, o_ds, p2, w_out, ple_w, gate_w, final_g, tm=512):
    m, d = x2.shape
    kh = o_rw.shape[1]
    pd = p2.shape[1]
    resident = dict(pipeline_mode=pl.Buffered(1))
    return pl.pallas_call(
        _post_kernel,
        grid=(m // tm,),
        in_specs=[pl.BlockSpec((tm, d), lambda i: (i, 0)),
                  pl.BlockSpec((tm, kh), lambda i: (i, 0)),
                  pl.BlockSpec((tm, kh), lambda i: (i, 0)),
                  pl.BlockSpec((tm, pd), lambda i: (i, 0)),
                  pl.BlockSpec((kh, d), lambda i: (0, 0), **resident),
                  pl.BlockSpec((kh, d), lambda i: (1, 0), **resident),
                  pl.BlockSpec((pd, d), lambda i: (0, 0), **resident),
                  pl.BlockSpec((d, d), lambda i: (0, 0), **resident),
                  pl.BlockSpec((1, d), lambda i: (0, 0))],
        out_specs=pl.BlockSpec((tm, d), lambda i: (i, 0)),
        out_shape=jax.ShapeDtypeStruct((m, d), F32),
        compiler_params=pltpu.CompilerParams(dimension_semantics=("parallel",),
                                             vmem_limit_bytes=VMEM_LIMIT),
        name="post",
    )(x2, o_rw, o_ds, p2, w_out, w_out, ple_w, gate_w, final_g)


def _split_w_in(w):
    w = w.astype(BF16)
    return w, w[:, w.shape[1] - DS_WIDTH:]


def kernel(x, p, w_in, norm_g, rw_mu, rw_w0, rw_w_up, rw_a0, rw_a_up, rw_k_k, rw_k_a, rw_r_k, rw_ln_g, rw_ln_b, ds_q_norm_g, ds_kv_norm_g, idx_k_norm_g, ds_w_uq, ds_w_uk, ds_w_uv, idx_w_q, rel_bias, w_out, ple_w, ple_gate_w, final_g):
    B, T, D = x.shape
    depth = w_in.shape[0]
    assert depth == 1 and T % SK == 0 and T % QB == 0 and T % RW_CHUNK == 0 and (B * T) % 512 == 0
    h = x.reshape(B * T, D)
    tab = _biastab(rel_bias)
    for i in range(depth):
        z, ckv, kid = _inproj(h, norm_g[i].reshape(1, D), *_split_w_in(w_in[i]),
                              ds_kv_norm_g[i].reshape(1, -1), idx_k_norm_g[i].reshape(1, -1))
        o_rw = _rwkv(z, B, T, rw_mu[i], rw_w0[i], rw_a0[i], rw_k_k[i], rw_k_a[i],
                     rw_r_k[i].reshape(-1), rw_ln_g[i], rw_ln_b[i], rw_w_up[i], rw_a_up[i])
        o_ds = _dsa(z, ckv, kid, B, T, ds_q_norm_g[i], ds_w_uq[i], ds_w_uk[i], ds_w_uv[i],
                    idx_w_q[i], tab)
        h = _post(h, o_rw, o_ds, p[i].reshape(B * T, -1), w_out[i].astype(BF16),
                  ple_w[i].astype(BF16), ple_gate_w[i].astype(BF16), final_g.reshape(1, D))
    return h.reshape(B, T, D)
```

```python
import functools
import math

import jax
import jax.numpy as jnp
from jax import lax
from jax.experimental import pallas as pl
from jax.experimental.pallas import tpu as pltpu

F32 = jnp.float32
BF16 = jnp.bfloat16
I32 = jnp.int32

RW_WIDTH = 1024
RW_HEAD = 64
RW_HEADS = 16
RW_LORA = 64
DS_WIDTH = 1024
DS_HEAD = 64
DS_HEADS = 16
DS_Q_RANK = 384
DS_KV_RANK = 256
IDX_HEADS = 16
IDX_DIM = 64
TOPK_MAX = 256
CHUNK = 64
NUM_BUCKETS = 32
MAX_DISTANCE = 128
NORM_EPS = 1e-6
GN_EPS = 64e-5

COL_R, COL_K, COL_V, COL_GRW, COL_GDS = 0, 1024, 2048, 3072, 4096
COL_KV = 5120
COL_Q = 5376
COL_WA = 5760
COL_KX = 5888
Z_WIDTH = 6016

LANES = 128
SUBLANES = 8
QB = 256
BQ = 128
SK = 512
BIAS_ROWS = 2 * SK + 2 * BQ
RW_CHUNK = 64
INT_MIN = -2 ** 31
KEY_NEG_INF = -2139095041
PLANE_ROWS = 32 * SUBLANES
SEARCH_MIN_BITS = 23
SEARCH_STEP_BITS = 3
assert (32 - SEARCH_MIN_BITS) % SEARCH_STEP_BITS == 0
MASK_NEG = -1e30
LOG2E = 1.4426950408889634
ONES_ROWS = 16
VMEM_LIMIT = 52 * 1024 * 1024


def _sigmoid(x):
    return 1.0 / (1.0 + jnp.exp(-x))


def _dot(a, b):
    return jnp.dot(a, b, preferred_element_type=F32)


def _inproj_plan():
    s_wa = 3 * RW_WIDTH
    s_grw = s_wa + 2 * RW_LORA
    s_q = s_grw + RW_WIDTH
    s_kv = s_q + DS_Q_RANK
    s_kx = s_kv + DS_KV_RANK
    return [(COL_R, 0, 3 * RW_WIDTH), (COL_GRW, s_grw, RW_WIDTH), (COL_GDS, None, DS_WIDTH),
            (COL_KV, s_kv, DS_KV_RANK), (COL_Q, s_q, DS_Q_RANK), (COL_WA, s_wa, LANES),
            (COL_KX, s_kx, LANES)]


def _inproj_kernel(x_ref, g_ref, w_ref, wg_ref, gkv_ref, gik_ref, o_ref, ckv_ref, kid_ref, *, tn):
    x = x_ref[...]
    ms = jnp.mean(x * x, axis=-1, keepdims=True)
    xn = (x * lax.rsqrt(ms + NORM_EPS) * g_ref[...]).astype(BF16)
    for dst, src, width in _inproj_plan():
        for j in range(0, width, tn):
            w = min(tn, width - j)
            wt = wg_ref[:, j:j + w] if src is None else w_ref[:, src + j:src + j + w]
            o_ref[:, dst + j:dst + j + w] = _dot(xn, wt)
    kv = o_ref[:, COL_KV:COL_KV + DS_KV_RANK]
    ckv_ref[...] = (kv * lax.rsqrt(jnp.mean(kv * kv, axis=-1, keepdims=True) + NORM_EPS)
                    * gkv_ref[...]).astype(BF16)
    ki = o_ref[:, COL_KX:COL_KX + IDX_DIM]
    kid_ref[...] = (ki * lax.rsqrt(jnp.mean(ki * ki, axis=-1, keepdims=True) + NORM_EPS)
                    * gik_ref[...]).astype(BF16)


def _inproj(x2, g, w, w_gds, gkv, gik, tm=256, tn=768):
    m, d = x2.shape
    resident = dict(pipeline_mode=pl.Buffered(1))
    return pl.pallas_call(
        functools.partial(_inproj_kernel, tn=tn),
        grid=(m // tm,),
        in_specs=[pl.BlockSpec((tm, d), lambda i: (i, 0)),
                  pl.BlockSpec((1, d), lambda i: (0, 0)),
                  pl.BlockSpec(w.shape, lambda i: (0, 0), **resident),
                  pl.BlockSpec(w_gds.shape, lambda i: (0, 0), **resident),
                  pl.BlockSpec((1, DS_KV_RANK), lambda i: (0, 0)),
                  pl.BlockSpec((1, IDX_DIM), lambda i: (0, 0))],
        out_specs=[pl.BlockSpec((tm, Z_WIDTH), lambda i: (i, 0)),
                   pl.BlockSpec((tm, DS_KV_RANK), lambda i: (i, 0)),
                   pl.BlockSpec((tm, IDX_DIM), lambda i: (i, 0))],
        out_shape=[jax.ShapeDtypeStruct((m, Z_WIDTH), F32),
                   jax.ShapeDtypeStruct((m, DS_KV_RANK), BF16),
                   jax.ShapeDtypeStruct((m, IDX_DIM), BF16)],
        compiler_params=pltpu.CompilerParams(dimension_semantics=("parallel",),
                                             vmem_limit_bytes=VMEM_LIMIT),
        name="inproj",
    )(x2, g, w, w_gds, gkv, gik)


def _rwkv_kernel(r_ref, k_ref, v_ref, g_ref, wa_ref,
                 mur_ref, muk_ref, muv_ref, muwa_ref,
                 w0_ref, a0_ref, kk_ref, ka_ref, rk_ref, lng_ref, lnb_ref,
                 wup_ref, aup_ref,
                 o_ref,
                 pr_ref, pk_ref, pv_ref, pwa_ref, st_ref, *, nb):
    C = RW_CHUNK
    N = RW_HEAD

    @pl.when(pl.program_id(0) == 0)
    def _():
        pr_ref[...] = jnp.zeros_like(pr_ref)
        pk_ref[...] = jnp.zeros_like(pk_ref)
        pv_ref[...] = jnp.zeros_like(pv_ref)
        pwa_ref[...] = jnp.zeros_like(pwa_ref)
        st_ref[...] = jnp.zeros_like(st_ref)

    row = lax.broadcasted_iota(I32, (nb * C, 1), 0)

    def shift(ref, prev_ref, mu_ref):
        z = ref[...].reshape(nb * C, ref.shape[2])
        zp = pltpu.roll(z, 1, 0)
        for b in range(nb):
            zp = jnp.where(row == b * C, prev_ref[b], zp)
            prev_ref[b] = z[(b + 1) * C - 1:(b + 1) * C, :]
        return z + mu_ref[...] * (zp - z)

    r = shift(r_ref, pr_ref, mur_ref)
    k = shift(k_ref, pk_ref, muk_ref)
    v = shift(v_ref, pv_ref, muv_ref)
    wa = shift(wa_ref, pwa_ref, muwa_ref)
    wd = wa[:, 0:RW_LORA]
    ad = wa[:, RW_LORA:2 * RW_LORA]

    wl = w0_ref[...] + _dot(jnp.tanh(wd).astype(BF16), wup_ref[...])
    nwl = -wl
    softplus = jnp.maximum(nwl, 0.0) + jnp.log(1.0 + jnp.exp(-jnp.abs(nwl)))
    w_log = -softplus - 0.5
    lw = -jnp.exp(w_log)
    a = _sigmoid(a0_ref[...] + _dot(ad.astype(BF16), aup_ref[...]))
    kk = k * kk_ref[...]
    k2 = k * (1.0 + (a - 1.0) * ka_ref[...])

    ti = lax.broadcasted_iota(I32, (C, C), 0)
    tj = lax.broadcasted_iota(I32, (C, C), 1)
    tri = jnp.where(ti >= tj, 1.0, 0.0).astype(F32)

    def per_row(mat, x):
        hi = x.astype(BF16)
        r1 = x - hi.astype(F32)
        mid = r1.astype(BF16)
        lo = (r1 - mid.astype(F32)).astype(BF16)
        mb = mat.astype(BF16)
        return jnp.concatenate(
            [_dot(mb, hi[b * C:(b + 1) * C]) + _dot(mb, mid[b * C:(b + 1) * C]) + _dot(mb, lo[b * C:(b + 1) * C])
             for b in range(nb)], axis=0)

    cum = per_row(tri, lw)
    p = jnp.exp(cum)
    pinv = jnp.exp(-cum)
    pprev = jnp.exp(cum - lw)
    tot = per_row(jnp.ones((C, C), F32), lw)
    pend = jnp.exp(tot)

    g = g_ref[...].reshape(nb * C, RW_WIDTH)
    gate = g * _sigmoid(g)

    HP = RW_HEADS // 2
    NP = nb * HP

    def pairs(x):
        return jnp.stack([x[b * C:(b + 1) * C, j * LANES:(j + 1) * LANES]
                          for b in range(nb) for j in range(HP)], axis=0)

    def per_pair(ref):
        return jnp.concatenate([ref[...]] * nb, axis=0)

    lane = lax.broadcasted_iota(I32, (1, 1, LANES), 2)
    m_lo = jnp.where(lane < N, 1.0, 0.0).astype(BF16)
    m_hi = jnp.where(lane < N, 0.0, 1.0).astype(BF16)
    bi = lax.broadcasted_iota(I32, (LANES, LANES), 0)
    bj = lax.broadcasted_iota(I32, (LANES, LANES), 1)
    same_head = (bi < N) == (bj < N)
    ones_bd = jnp.where(same_head, 1.0, 0.0).astype(BF16)

    def head_sum(x):
        return _dot(x.reshape(NP * C, LANES).astype(BF16), ones_bd).reshape(NP, C, LANES)

    def halves(x):
        xb = x.astype(BF16)
        return jnp.concatenate([xb * m_lo, xb * m_hi], axis=1)

    def bmm(x, y):
        return lax.dot_general(x, y, (((2,), (1,)), ((0,), (0,))), preferred_element_type=F32)

    def bmm_nt(x, y):
        return lax.dot_general(x, y, (((2,), (2,)), ((0,), (0,))), preferred_element_type=F32)

    def block_mask(nblk, cmp):
        wi = lax.broadcasted_iota(I32, (C, nblk * C), 0)
        wj = lax.broadcasted_iota(I32, (C, nblk * C), 1) & (C - 1)
        return cmp(wi, wj)

    r_p, k2_p, v_p, a_p = pairs(r), pairs(k2), pairs(v), pairs(a)
    p_p, pinv_p, pprev_p = pairs(p), pairs(pinv), pairs(pprev)
    kk_p = pairs(kk)
    kkn = kk_p * lax.rsqrt(jnp.maximum(head_sum(kk_p * kk_p), 1e-24))
    at = (-kkn) * pprev_p
    bt = (kkn * a_p) * pinv_p
    kt = k2_p * pinv_p
    rt = r_p * p_p
    pend_p = pairs(pend)
    pend2 = jnp.concatenate([pend_p, pend_p], axis=1)

    lhs2 = jnp.concatenate([at, rt], axis=1).astype(BF16)
    rhs4 = jnp.concatenate([halves(kt), halves(bt)], axis=1)
    gc = bmm_nt(lhs2, rhs4)
    strict2 = block_mask(2, lambda i_, j_: i_ > j_)
    incl4 = block_mask(4, lambda i_, j_: i_ >= j_)
    a_ak = jnp.where(strict2, gc[:, 0:C, 0:2 * C], 0.0)
    nmat = jnp.where(strict2, gc[:, 0:C, 2 * C:4 * C], 0.0)
    a_rkb = jnp.where(incl4, gc[:, C:2 * C, :], 0.0)

    g0 = st_ref[...]
    sg = bmm_nt(lhs2, g0.astype(BF16))
    vm2 = halves(v_p)
    u = sg[:, 0:C] + bmm(a_ak.astype(BF16), vm2)
    pw = nmat.astype(BF16)
    u = u + bmm(pw, halves(u))
    n = 1
    while 2 * n < C:
        pw = bmm(pw, halves(pw)).astype(BF16)
        u = u + bmm(pw, halves(u))
        n *= 2
    um2b = halves(u)
    y = sg[:, C:2 * C] + bmm(a_rkb.astype(BF16), jnp.concatenate([vm2, um2b], axis=1))
    uv = jnp.concatenate([u, v_p], axis=1).astype(BF16)
    bkh = (jnp.concatenate([bt, kt], axis=1) * pend2).astype(BF16)
    upd = lax.dot_general(uv, bkh, (((1,), (1,)), ((0,), (0,))), preferred_element_type=F32)
    st_ref[...] = g0 * pend2 + jnp.where(same_head, upd, 0.0)

    inv_n = 1.0 / N
    yc = y - head_sum(y) * inv_n
    var = head_sum(yc * yc) * inv_n
    yn = yc * lax.rsqrt(var + GN_EPS) * per_pair(lng_ref) + per_pair(lnb_ref)
    bonus = head_sum(r_p * k2_p * per_pair(rk_ref)) * v_p
    out = (yn + bonus) * pairs(gate)
    for b in range(nb):
        for j in range(HP):
            o_ref[b, :, j * LANES:(j + 1) * LANES] = out[b * HP + j].astype(BF16)


def _rwkv(z, B, T, mu, w0, a0, k_k, k_a, r_k, ln_g, ln_b, w_up, a_up):
    C = RW_CHUNK
    W = RW_WIDTH
    z3 = z.reshape(B, T, z.shape[1])
    row = lambda a: a.reshape(1, -1).astype(F32)
    mu_r, mu_k, mu_v = mu[0:W], mu[W:2 * W], mu[2 * W:3 * W]
    mu_wa = mu[3 * W:3 * W + 2 * RW_LORA]
    zspec = lambda col: pl.BlockSpec((B, C, W), lambda c: (0, c, col // W))
    pspec = lambda width: pl.BlockSpec((1, width), lambda c: (0, 0))
    npairs = RW_HEADS // 2
    prow = lambda a: a.reshape(npairs, 1, LANES).astype(F32)
    ppspec = pl.BlockSpec((npairs, 1, LANES), lambda c: (0, 0, 0))
    wspec = pl.BlockSpec((RW_LORA, W), lambda c: (0, 0))
    out = pl.pallas_call(
        functools.partial(_rwkv_kernel, nb=B),
        grid=(T // C,),
        in_specs=[zspec(COL_R), zspec(COL_K), zspec(COL_V), zspec(COL_GRW),
                  pl.BlockSpec((B, C, LANES), lambda c: (0, c, COL_WA // LANES)),
                  pspec(W), pspec(W), pspec(W), pspec(LANES),
                  pspec(W), pspec(W), pspec(W), pspec(W), ppspec, ppspec, ppspec,
                  wspec, wspec],
        out_specs=pl.BlockSpec((B, C, W), lambda c: (0, c, 0)),
        out_shape=jax.ShapeDtypeStruct((B, T, W), BF16),
        scratch_shapes=[pltpu.VMEM((B, 1, W), F32), pltpu.VMEM((B, 1, W), F32), pltpu.VMEM((B, 1, W), F32),
                        pltpu.VMEM((B, 1, LANES), F32),
                        pltpu.VMEM((B * npairs, LANES, LANES), F32)],
        compiler_params=pltpu.CompilerParams(dimension_semantics=("arbitrary",),
                                             vmem_limit_bytes=VMEM_LIMIT),
        name="rwkv",
    )(z3, z3, z3, z3, z3,
      row(mu_r), row(mu_k), row(mu_v), row(mu_wa),
      row(w0), row(a0), row(k_k), row(k_a), prow(r_k), prow(ln_g), prow(ln_b),
      w_up.astype(BF16), a_up.astype(BF16))
    return out.reshape(B * T, W)


def _biastab_kernel(rb_ref, o_ref):
    o_ref[...] = jnp.zeros_like(o_ref)
    nb = NUM_BUCKETS // 2
    max_exact = nb // 2
    c = lax.broadcasted_iota(I32, (2 * BQ, BQ), 0)
    r = lax.broadcasted_iota(I32, (2 * BQ, BQ), 1)
    rel = c - BQ - r
    ret = jnp.where(rel > 0, nb, 0)
    n = jnp.abs(rel)
    nf = jnp.maximum(n, 1).astype(F32)
    large = max_exact + (jnp.log(nf / max_exact) / math.log(MAX_DISTANCE / max_exact)
                         * (nb - max_exact)).astype(I32)
    large = jnp.minimum(large, nb - 1) & (NUM_BUCKETS - 1)
    bucket = ret + jnp.where(n < max_exact, n, large)
    for h in range(DS_HEADS):
        far = rb_ref[nb - 1, h]
        acc = jnp.zeros((2 * BQ, BQ), F32)
        for b in range(NUM_BUCKETS):
            acc = jnp.where(bucket == b, rb_ref[b, h] - far, acc)
        o_ref[SK:SK + 2 * BQ, h * BQ:(h + 1) * BQ] = acc * LOG2E


def _biastab(rel_bias):
    return pl.pallas_call(
        _biastab_kernel,
        in_specs=[pl.BlockSpec(memory_space=pltpu.SMEM)],
        out_specs=pl.BlockSpec(memory_space=pltpu.VMEM),
        out_shape=jax.ShapeDtypeStruct((BIAS_ROWS, DS_HEADS * BQ), F32),
        compiler_params=pltpu.CompilerParams(vmem_limit_bytes=VMEM_LIMIT),
        name="biastab",
    )(rel_bias.astype(F32))


def _fold_rows(x, op):
    n = x.shape[0] // SUBLANES
    accs = [x[j * SUBLANES:(j + 1) * SUBLANES] for j in range(min(4, n))]
    for j in range(4, n):
        accs[j % 4] = op(accs[j % 4], x[j * SUBLANES:(j + 1) * SUBLANES])
    while len(accs) > 1:
        accs = [op(accs[j], accs[j + 1]) for j in range(0, len(accs) - 1, 2)] + (
            [accs[-1]] if len(accs) % 2 else [])
    return accs[0]


def _dsa_kernel(ql_ref, kx_ref, gds_ref, kid_ref, ckv_ref, ckvt_ref, qg_ref, wq_ref, wuk_ref, wuvt_ref,
                tab_ref, o_ref,
                sc_ref, pl_ref, cand_ref, lgt_ref, acc_ref, m_ref, qat_ref, qit_ref, w_ref, out_ref, *, topk):
    i = pl.program_id(1)
    q0 = i * QB
    ntile = jnp.right_shift(q0 + (QB + SK - 1), SK.bit_length() - 1)
    R = DS_KV_RANK
    GW = DS_HEADS * QB

    ql = ql_ref[...]
    ms = jnp.mean(ql * ql, axis=-1, keepdims=True)
    qn = (ql * lax.rsqrt(ms + NORM_EPS) * qg_ref[...]).astype(BF16)
    qt = _dot(qn, wq_ref[...]).T
    for h in range(DS_HEADS):
        qh = qt[h * DS_HEAD:(h + 1) * DS_HEAD, :].astype(BF16)
        qat_ref[:, h * QB:(h + 1) * QB] = (_dot(wuk_ref[h], qh) * (DS_HEAD ** -0.5 * LOG2E)).astype(BF16)
    for pr in range(IDX_HEADS // 2):
        base = DS_WIDTH + 2 * pr * IDX_DIM
        qit_ref[pr] = jnp.concatenate([qt[base:base + IDX_DIM, :],
                                       qt[base + IDX_DIM:base + 2 * IDX_DIM, :]], axis=1).astype(BF16)
    w_ref[...] = kx_ref[...].T[IDX_DIM:IDX_DIM + IDX_HEADS, :] * (IDX_HEADS ** -0.5 * IDX_DIM ** -0.5)

    lanei = lax.broadcasted_iota(I32, (1, QB), 1)
    csh = CHUNK.bit_length() - 1
    limit = jnp.left_shift(jnp.right_shift(q0 + lanei, csh) + 1, csh)
    rowi = lax.broadcasted_iota(I32, (SK, QB), 0)

    def score_tile(kt, carry):
        off = pl.multiple_of(kt * SK, SK)
        kid = kid_ref[pl.ds(off, SK), :]
        s = jnp.zeros((SK, QB), F32)
        for pr in range(IDX_HEADS // 2):
            lg = _dot(kid, qit_ref[pr])
            s = s + w_ref[2 * pr:2 * pr + 1, :] * jnp.maximum(lg[:, 0:QB], 0.0)
            s = s + w_ref[2 * pr + 1:2 * pr + 2, :] * jnp.maximum(lg[:, QB:2 * QB], 0.0)
        adm = (off + rowi) < limit
        sc_ref[pl.ds(off, SK), :] = jnp.where(adm, s, -jnp.inf)
        return carry

    lax.fori_loop(0, ntile, score_tile, 0)

    def key_to_f32(key):
        return pltpu.bitcast(jnp.where(key < 0, key ^ 0x7FFFFFFF, key), F32)

    def count(pred):
        def body(kt, acc):
            off = pl.multiple_of(kt * SK, SK)
            sc = sc_ref[pl.ds(off, SK), :]
            return acc + _fold_rows(jnp.where(pred(sc, off), 1, 0).astype(I32), jnp.add)
        acc = lax.fori_loop(0, ntile, body, jnp.zeros((SUBLANES, QB), I32))
        return jnp.sum(acc, axis=0, keepdims=True)

    def bit_step(it, carry):
        lo, cnt_lo = carry
        cand = lo + jnp.left_shift(jnp.int32(1), 31 - it)
        cf = key_to_f32(cand)
        cnt = count(lambda sc, off: sc >= cf)
        take = cnt >= topk
        return jnp.where(take, cand, lo), jnp.where(take, cnt, cnt_lo)

    def all_settled(cnt_lo):
        return jnp.min(jnp.where((cnt_lo == topk) | (limit < topk), 1, 0))

    def slow_search():
        lo, cnt_lo = lax.fori_loop(0, SEARCH_MIN_BITS, bit_step,
                                   (jnp.full((1, QB), INT_MIN, I32), jnp.full((1, QB), 2 ** 30, I32)))

        def more_bits(c):
            it, lo, cnt_lo, _ = c
            lo, cnt_lo = lax.fori_loop(it, it + SEARCH_STEP_BITS, bit_step, (lo, cnt_lo))
            return it + SEARCH_STEP_BITS, lo, cnt_lo, all_settled(cnt_lo)

        _, lo, _, settled = lax.while_loop(lambda c: (c[0] < 32) & (c[3] == 0), more_bits,
                                           (jnp.int32(SEARCH_MIN_BITS), lo, cnt_lo, all_settled(cnt_lo)))
        return lo, settled

    ngrp_max = sc_ref.shape[0] // PLANE_ROWS
    ngrp = ntile * (SK // PLANE_ROWS)

    @pl.when(i == 0)
    def _():
        pl_ref[...] = jnp.zeros_like(pl_ref)

    cand_ref[...] = jnp.zeros_like(cand_ref)

    def to_planes(g, c):
        base = pl.multiple_of(g * PLANE_ROWS, PLANE_ROWS)
        words = []
        for j in range(32):
            bits = pltpu.bitcast(sc_ref[pl.ds(base + SUBLANES * j, SUBLANES), :], I32)
            words.append(jnp.where(bits < 0, ~bits, bits | INT_MIN))
        j, m = 16, 0x0000FFFF
        while j:
            k = 0
            while k < 32:
                t = (words[k] ^ lax.shift_right_logical(words[k + j], j)) & m
                words[k] = words[k] ^ t
                words[k + j] = words[k + j] ^ lax.shift_left(t, j)
                k = (k + j + 1) & ~j
            j >>= 1
            m = (m ^ (m << j)) & 0xFFFFFFFF
        for b in range(32):
            pl_ref[pl.ds(base + SUBLANES * b, SUBLANES), :] = words[b]
        cand_ref[pl.ds(pl.multiple_of(g * SUBLANES, SUBLANES), SUBLANES), :] = jnp.full((SUBLANES, QB), -1, I32)
        return c

    lax.fori_loop(0, ngrp, to_planes, 0)

    def radix_step(it, c):
        above, tb = c
        prow = pl.multiple_of(it * SUBLANES, SUBLANES)
        acc = jnp.zeros((SUBLANES, QB), I32)
        for g in range(ngrp_max):
            plane = pl_ref[pl.ds(g * PLANE_ROWS + prow, SUBLANES), :]
            acc = acc + lax.population_count(cand_ref[g * SUBLANES:(g + 1) * SUBLANES, :] & plane)
        ones = jnp.sum(acc, axis=0, keepdims=True)
        take = above + ones >= topk
        for g in range(ngrp_max):
            plane = pl_ref[pl.ds(g * PLANE_ROWS + prow, SUBLANES), :]
            cand = cand_ref[g * SUBLANES:(g + 1) * SUBLANES, :]
            cand_ref[g * SUBLANES:(g + 1) * SUBLANES, :] = jnp.where(take, cand & plane, cand & ~plane)
        return (jnp.where(take, above, above + ones),
                jnp.where(take, tb | lax.shift_left(jnp.int32(1), 31 - it), tb))

    _, tbits = lax.fori_loop(0, 32, radix_step, (jnp.zeros((1, QB), I32), jnp.zeros((1, QB), I32)))
    lo_fast = tbits ^ INT_MIN
    thr_fast = key_to_f32(jnp.maximum(lo_fast, KEY_NEG_INF))
    fast_ok = all_settled(count(lambda sc, off: sc >= thr_fast))
    lo, settled = lax.cond(fast_ok == 1, lambda: (lo_fast, fast_ok), slow_search)
    has_thr = lo > KEY_NEG_INF
    thr = key_to_f32(jnp.maximum(lo, KEY_NEG_INF))
    nbits = max(1, (sc_ref.shape[0] - 1).bit_length())
    take_all_ties = jnp.full((1, QB), sc_ref.shape[0], I32)

    def tie_cut():
        cnt_gt = count(lambda sc, off: sc > thr)
        cnt_eq = count(lambda sc, off: sc == thr)
        tied = (cnt_gt + cnt_eq > topk) & has_thr

        def search_cut():
            def idx_step(it, m):
                cand = m + jnp.left_shift(jnp.int32(1), nbits - 1 - it)
                cnt = cnt_gt + count(lambda sc, off: (sc == thr) & ((off + rowi) < cand))
                return jnp.where(cnt < topk, cand, m)
            return lax.fori_loop(0, nbits, idx_step, jnp.zeros((1, QB), I32))

        return lax.cond(jnp.max(jnp.where(tied, 1, 0)) > 0, search_cut, lambda: take_all_ties)

    cut = lax.cond(settled == 1, lambda: take_all_ties, tie_cut)
    cut = jnp.where(has_thr, cut, -1)

    def selection_mask(off):
        sc = sc_ref[pl.ds(off, SK), :]
        sel = (sc > thr) | ((sc == thr) & ((off + rowi) <= cut))
        return jnp.where(sel, 0.0, MASK_NEG).astype(F32)

    near_lo = q0 - BQ

    def attend(off, bias_off, first=False):
        rows = pl.ds(off, SK)
        s = _dot(ckv_ref[rows, :], qat_ref[...])
        mk = selection_mask(off)
        m_old = jnp.full((1, GW), MASK_NEG, F32) if first else m_ref[...]
        tmax = []
        for h in range(DS_HEADS):
            cs = slice(h * QB, (h + 1) * QB)
            t = s[:, cs] + mk
            if bias_off is not None:
                t = t + jnp.concatenate(
                    [tab_ref[pl.ds(pl.multiple_of(jnp.maximum(bias_off - j * BQ, 0), BQ), SK),
                             h * BQ:(h + 1) * BQ] for j in range(QB // BQ)], axis=1)
            lgt_ref[:, cs] = t
            tmax.append(jnp.max(_fold_rows(t, jnp.maximum), axis=0, keepdims=True))
        m_new = jnp.maximum(m_old, jnp.concatenate(tmax, axis=1))
        m_ref[...] = m_new
        pr = jnp.exp2(lgt_ref[...] - m_new).astype(BF16)
        pv = _dot(ckvt_ref[:, rows], pr)
        acc_ref[...] = pv if first else acc_ref[...] * jnp.exp2(m_old - m_new) + pv

    def far_tile(kt, c):
        attend(pl.multiple_of(kt * SK, SK), None)
        return c

    def edge_tile(kt, c):
        off = pl.multiple_of(kt * SK, SK)
        attend(off, SK + off - near_lo)
        return c

    nfar = jnp.maximum(ntile - 2, 0)
    attend(0, SK - near_lo, first=True)
    lax.fori_loop(1, nfar, far_tile, 0)
    lax.fori_loop(jnp.maximum(nfar, 1), ntile, edge_tile, 0)

    o_lat = acc_ref[0:R, :] * (1.0 / acc_ref[R:R + 1, :])
    for h in range(DS_HEADS):
        out_ref[h * DS_HEAD:(h + 1) * DS_HEAD, :] = _dot(
            wuvt_ref[h], o_lat[:, h * QB:(h + 1) * QB].astype(BF16))

    g = gds_ref[...]
    o_ref[...] = (out_ref[...].T * (g * _sigmoid(g))).astype(BF16)


def _dsa(z, ckv, kid, B, T, q_norm_g, w_uq, w_uk, w_uv, iw_q, tab):
    nq = T // QB
    topk = min(TOPK_MAX, T // 4)
    R = DS_KV_RANK
    wq = jnp.concatenate([w_uq, iw_q], axis=1).astype(BF16)
    wuk_h = jnp.transpose(w_uk, (1, 0, 2)).astype(BF16)
    wuv_t = jnp.transpose(w_uv, (1, 2, 0)).astype(BF16)
    ckv_t = jnp.concatenate([jnp.swapaxes(ckv.reshape(B, T, R), 1, 2),
                             jnp.ones((B, ONES_ROWS, T), BF16)], axis=1).reshape(B * (R + ONES_ROWS), T)
    const2 = lambda b, i: (0, 0)
    const3 = lambda b, i: (0, 0, 0)
    resident = dict(pipeline_mode=pl.Buffered(1))
    return pl.pallas_call(
        functools.partial(_dsa_kernel, topk=topk),
        grid=(B, nq),
        in_specs=[pl.BlockSpec((QB, DS_Q_RANK), lambda b, i: (b * nq + i, COL_Q // DS_Q_RANK)),
                  pl.BlockSpec((QB, LANES), lambda b, i: (b * nq + i, COL_KX // LANES)),
                  pl.BlockSpec((QB, DS_WIDTH), lambda b, i: (b * nq + i, COL_GDS // DS_WIDTH)),
                  pl.BlockSpec((T, IDX_DIM), lambda b, i: (b, 0), **resident),
                  pl.BlockSpec((T, R), lambda b, i: (b, 0), **resident),
                  pl.BlockSpec((R + ONES_ROWS, T), lambda b, i: (b, 0), **resident),
                  pl.BlockSpec((1, DS_Q_RANK), const2),
                  pl.BlockSpec((DS_Q_RANK, 2 * DS_WIDTH), const2, **resident),
                  pl.BlockSpec((DS_HEADS, R, DS_HEAD), const3, **resident),
                  pl.BlockSpec((DS_HEADS, DS_HEAD, R), const3, **resident),
                  pl.BlockSpec((BIAS_ROWS, DS_HEADS * BQ), const2, **resident)],
        out_specs=pl.BlockSpec((QB, DS_WIDTH), lambda b, i: (b * nq + i, 0)),
        out_shape=jax.ShapeDtypeStruct((B * T, DS_WIDTH), BF16),
        scratch_shapes=[pltpu.VMEM((T, QB), F32),
                        pltpu.VMEM((T, QB), I32),
                        pltpu.VMEM((T // PLANE_ROWS * SUBLANES, QB), I32),
                        pltpu.VMEM((SK, DS_HEADS * QB), F32),
                        pltpu.VMEM((R + ONES_ROWS, DS_HEADS * QB), F32),
                        pltpu.VMEM((1, DS_HEADS * QB), F32),
                        pltpu.VMEM((R, DS_HEADS * QB), BF16),
                        pltpu.VMEM((IDX_HEADS // 2, IDX_DIM, 2 * QB), BF16),
                        pltpu.VMEM((IDX_HEADS, QB), F32),
                        pltpu.VMEM((DS_WIDTH, QB), F32)],
        compiler_params=pltpu.CompilerParams(dimension_semantics=("parallel", "arbitrary"),
                                             vmem_limit_bytes=VMEM_LIMIT),
        name="dsa",
    )(z, z, z, kid, ckv, ckv_t, q_norm_g.reshape(1, -1).astype(F32), wq, wuk_h, wuv_t, tab)


def _post_kernel(x_ref, a1_ref, a2_ref, p_ref, w1_ref, w2_ref, pw_ref, gw_ref, fg_ref, o_ref):
    h = x_ref[...] + _dot(a1_ref[...], w1_ref[...]) + _dot(a2_ref[...], w2_ref[...])
    e = _dot(p_ref[...].astype(BF16), pw_ref[...])
    gate = _sigmoid(_dot(h.astype(BF16), gw_ref[...]))
    h2 = h + e * gate
    ms = jnp.mean(h2 * h2, axis=-1, keepdims=True)
    o_ref[...] = h2 * lax.rsqrt(ms + NORM_EPS) * fg_ref[...]


def _post(x2, o_rw, o_ds, p2, w_out, ple_w, gate_w, final_g, tm=512):
    m, d = x2.shape
    kh = o_rw.shape[1]
    pd = p2.shape[1]
    resident = dict(pipeline_mode=pl.Buffered(1))
    return pl.pallas_call(
        _post_kernel,
        grid=(m // tm,),
        in_specs=[pl.BlockSpec((tm, d), lambda i: (i, 0)),
                  pl.BlockSpec((tm, kh), lambda i: (i, 0)),
                  pl.BlockSpec((tm, kh), lambda i: (i, 0)),
                  pl.BlockSpec((tm, pd), lambda i: (i, 0)),
                  pl.BlockSpec((kh, d), lambda i: (0, 0), **resident),
                  pl.BlockSpec((kh, d), lambda i: (1, 0), **resident),
                  pl.BlockSpec((pd, d), lambda i: (0, 0), **resident),
                  pl.BlockSpec((d, d), lambda i: (0, 0), **resident),
                  pl.BlockSpec((1, d), lambda i: (0, 0))],
        out_specs=pl.BlockSpec((tm, d), lambda i: (i, 0)),
        out_shape=jax.ShapeDtypeStruct((m, d), F32),
        compiler_params=pltpu.CompilerParams(dimension_semantics=("parallel",),
                                             vmem_limit_bytes=VMEM_LIMIT),
        name="post",
    )(x2, o_rw, o_ds, p2, w_out, w_out, ple_w, gate_w, final_g)


def _split_w_in(w):
    w = w.astype(BF16)
    return w, w[:, w.shape[1] - DS_WIDTH:]


def kernel(x, p, w_in, norm_g, rw_mu, rw_w0, rw_w_up, rw_a0, rw_a_up, rw_k_k, rw_k_a, rw_r_k, rw_ln_g, rw_ln_b, ds_q_norm_g, ds_kv_norm_g, idx_k_norm_g, ds_w_uq, ds_w_uk, ds_w_uv, idx_w_q, rel_bias, w_out, ple_w, ple_gate_w, final_g):
    B, T, D = x.shape
    depth = w_in.shape[0]
    assert depth == 1 and T % SK == 0 and T % QB == 0 and T % RW_CHUNK == 0 and (B * T) % 512 == 0
    h = x.reshape(B * T, D)
    tab = _biastab(rel_bias)
    for i in range(depth):
        z, ckv, kid = _inproj(h, norm_g[i].reshape(1, D), *_split_w_in(w_in[i]),
                              ds_kv_norm_g[i].reshape(1, -1), idx_k_norm_g[i].reshape(1, -1))
        o_rw = _rwkv(z, B, T, rw_mu[i], rw_w0[i], rw_a0[i], rw_k_k[i], rw_k_a[i],
                     rw_r_k[i].reshape(-1), rw_ln_g[i], rw_ln_b[i], rw_w_up[i], rw_a_up[i])
        o_ds = _dsa(z, ckv, kid, B, T, ds_q_norm_g[i], ds_w_uq[i], ds_w_uk[i], ds_w_uv[i],
                    idx_w_q[i], tab)
        h = _post(h, o_rw, o_ds, p[i].reshape(B * T, -1), w_out[i].astype(BF16),
                  ple_w[i].astype(BF16), ple_gate_w[i].astype(BF16), final_g.reshape(1, D))
    return h.reshape(B, T, D)
```

```python
import functools
import math

import jax
import jax.numpy as jnp
from jax import lax
from jax.experimental import pallas as pl
from jax.experimental.pallas import tpu as pltpu

F32 = jnp.float32
BF16 = jnp.bfloat16
I32 = jnp.int32

RW_WIDTH = 1024
RW_HEAD = 64
RW_HEADS = 16
RW_LORA = 64
DS_WIDTH = 1024
DS_HEAD = 64
DS_HEADS = 16
DS_Q_RANK = 384
DS_KV_RANK = 256
IDX_HEADS = 16
IDX_DIM = 64
TOPK_MAX = 256
CHUNK = 64
NUM_BUCKETS = 32
MAX_DISTANCE = 128
NORM_EPS = 1e-6
GN_EPS = 64e-5

COL_R, COL_K, COL_V, COL_GRW, COL_GDS = 0, 1024, 2048, 3072, 4096
COL_KV = 5120
COL_Q = 5376
COL_WA = 5760
COL_KX = 5888
Z_WIDTH = 6016

LANES = 128
SUBLANES = 8
QB = 256
BQ = 128
SK = 512
BIAS_ROWS = 2 * SK + 2 * BQ
RW_CHUNK = 64
INT_MIN = -2 ** 31
KEY_NEG_INF = -2139095041
PLANE_ROWS = 32 * SUBLANES
SEARCH_MIN_BITS = 23
SEARCH_STEP_BITS = 3
assert (32 - SEARCH_MIN_BITS) % SEARCH_STEP_BITS == 0
MASK_NEG = -1e30
LOG2E = 1.4426950408889634
ONES_ROWS = 16
VMEM_LIMIT = 52 * 1024 * 1024


def _sigmoid(x):
    return 1.0 / (1.0 + jnp.exp(-x))


def _dot(a, b):
    return jnp.dot(a, b, preferred_element_type=F32)


def _inproj_plan():
    s_wa = 3 * RW_WIDTH
    s_grw = s_wa + 2 * RW_LORA
    s_q = s_grw + RW_WIDTH
    s_kv = s_q + DS_Q_RANK
    s_kx = s_kv + DS_KV_RANK
    return [(COL_R, 0, 3 * RW_WIDTH), (COL_GRW, s_grw, RW_WIDTH), (COL_GDS, None, DS_WIDTH),
            (COL_KV, s_kv, DS_KV_RANK), (COL_Q, s_q, DS_Q_RANK), (COL_WA, s_wa, LANES),
            (COL_KX, s_kx, LANES)]


def _inproj_kernel(x_ref, g_ref, w_ref, wg_ref, gkv_ref, gik_ref, o_ref, ckv_ref, kid_ref, *, tn):
    x = x_ref[...]
    ms = jnp.mean(x * x, axis=-1, keepdims=True)
    xn = (x * lax.rsqrt(ms + NORM_EPS) * g_ref[...]).astype(BF16)
    for dst, src, width in _inproj_plan():
        for j in range(0, width, tn):
            w = min(tn, width - j)
            wt = wg_ref[:, j:j + w] if src is None else w_ref[:, src + j:src + j + w]
            o_ref[:, dst + j:dst + j + w] = _dot(xn, wt)
    kv = o_ref[:, COL_KV:COL_KV + DS_KV_RANK]
    ckv_ref[...] = (kv * lax.rsqrt(jnp.mean(kv * kv, axis=-1, keepdims=True) + NORM_EPS)
                    * gkv_ref[...]).astype(BF16)
    ki = o_ref[:, COL_KX:COL_KX + IDX_DIM]
    kid_ref[...] = (ki * lax.rsqrt(jnp.mean(ki * ki, axis=-1, keepdims=True) + NORM_EPS)
                    * gik_ref[...]).astype(BF16)


def _inproj(x2, g, w, w_gds, gkv, gik, tm=256, tn=768):
    m, d = x2.shape
    resident = dict(pipeline_mode=pl.Buffered(1))
    return pl.pallas_call(
        functools.partial(_inproj_kernel, tn=tn),
        grid=(m // tm,),
        in_specs=[pl.BlockSpec((tm, d), lambda i: (i, 0)),
                  pl.BlockSpec((1, d), lambda i: (0, 0)),
                  pl.BlockSpec(w.shape, lambda i: (0, 0), **resident),
                  pl.BlockSpec(w_gds.shape, lambda i: (0, 0), **resident),
                  pl.BlockSpec((1, DS_KV_RANK), lambda i: (0, 0)),
                  pl.BlockSpec((1, IDX_DIM), lambda i: (0, 0))],
        out_specs=[pl.BlockSpec((tm, Z_WIDTH), lambda i: (i, 0)),
                   pl.BlockSpec((tm, DS_KV_RANK), lambda i: (i, 0)),
                   pl.BlockSpec((tm, IDX_DIM), lambda i: (i, 0))],
        out_shape=[jax.ShapeDtypeStruct((m, Z_WIDTH), F32),
                   jax.ShapeDtypeStruct((m, DS_KV_RANK), BF16),
                   jax.ShapeDtypeStruct((m, IDX_DIM), BF16)],
        compiler_params=pltpu.CompilerParams(dimension_semantics=("parallel",),
                                             vmem_limit_bytes=VMEM_LIMIT),
        name="inproj",
    )(x2, g, w, w_gds, gkv, gik)


def _rwkv_kernel(r_ref, k_ref, v_ref, g_ref, wa_ref,
                 mur_ref, muk_ref, muv_ref, muwa_ref,
                 w0_ref, a0_ref, kk_ref, ka_ref, rk_ref, lng_ref, lnb_ref,
                 wup_ref, aup_ref,
                 o_ref,
                 pr_ref, pk_ref, pv_ref, pwa_ref, st_ref, *, nb):
    C = RW_CHUNK
    N = RW_HEAD

    @pl.when(pl.program_id(0) == 0)
    def _():
        pr_ref[...] = jnp.zeros_like(pr_ref)
        pk_ref[...] = jnp.zeros_like(pk_ref)
        pv_ref[...] = jnp.zeros_like(pv_ref)
        pwa_ref[...] = jnp.zeros_like(pwa_ref)
        st_ref[...] = jnp.zeros_like(st_ref)

    row = lax.broadcasted_iota(I32, (SUBLANES, 1), 0)

    def shift(ref, prev_ref, mu_ref):
        z = ref[...].reshape(nb * C, ref.shape[2])
        zp = pltpu.roll(z, 1, 0)
        parts = []
        for b in range(nb):
            parts.append(jnp.where(row == 0, prev_ref[b], zp[b * C:b * C + SUBLANES]))
            parts.append(zp[b * C + SUBLANES:(b + 1) * C])
            prev_ref[b] = z[(b + 1) * C - 1:(b + 1) * C, :]
        zp = jnp.concatenate(parts, axis=0)
        return z + mu_ref[...] * (zp - z)

    r = shift(r_ref, pr_ref, mur_ref)
    k = shift(k_ref, pk_ref, muk_ref)
    v = shift(v_ref, pv_ref, muv_ref)
    wa = shift(wa_ref, pwa_ref, muwa_ref)
    wd = wa[:, 0:RW_LORA]
    ad = wa[:, RW_LORA:2 * RW_LORA]

    wl = w0_ref[...] + _dot(jnp.tanh(wd).astype(BF16), wup_ref[...])
    lw = -math.exp(-0.5) * _sigmoid(wl)
    a = _sigmoid(a0_ref[...] + _dot(ad.astype(BF16), aup_ref[...]))
    kk = k * kk_ref[...]
    k2 = k * (1.0 + (a - 1.0) * ka_ref[...])

    ti = lax.broadcasted_iota(I32, (C, C), 0)
    tj = lax.broadcasted_iota(I32, (C, C), 1)
    tri = jnp.where(ti >= tj, 1.0, 0.0).astype(F32)

    def per_row(mat, x):
        hi = x.astype(BF16)
        r1 = x - hi.astype(F32)
        mid = r1.astype(BF16)
        lo = (r1 - mid.astype(F32)).astype(BF16)
        mb = mat.astype(BF16)
        return jnp.concatenate(
            [_dot(mb, hi[b * C:(b + 1) * C]) + _dot(mb, mid[b * C:(b + 1) * C]) + _dot(mb, lo[b * C:(b + 1) * C])
             for b in range(nb)], axis=0)

    cum = per_row(tri, lw)
    p = jnp.exp(cum)
    pinv = jnp.exp(-cum)
    pprev = jnp.exp(cum - lw)
    tot = per_row(jnp.ones((C, C), F32), lw)
    pend = jnp.exp(tot)

    g = g_ref[...].reshape(nb * C, RW_WIDTH)
    gate = g * _sigmoid(g)

    HP = RW_HEADS // 2
    NP = nb * HP

    def pairs(x):
        return jnp.stack([x[b * C:(b + 1) * C, j * LANES:(j + 1) * LANES]
                          for b in range(nb) for j in range(HP)], axis=0)

    def per_pair(ref):
        return jnp.concatenate([ref[...]] * nb, axis=0)

    lane = lax.broadcasted_iota(I32, (1, 1, LANES), 2)
    m_lo = jnp.where(lane < N, 1.0, 0.0).astype(BF16)
    m_hi = jnp.where(lane < N, 0.0, 1.0).astype(BF16)
    bi = lax.broadcasted_iota(I32, (LANES, LANES), 0)
    bj = lax.broadcasted_iota(I32, (LANES, LANES), 1)
    same_head = (bi < N) == (bj < N)
    ones_bd = jnp.where(same_head, 1.0, 0.0).astype(BF16)

    def head_sum(x):
        return _dot(x.reshape(NP * C, LANES).astype(BF16), ones_bd).reshape(NP, C, LANES)

    def halves(x):
        xb = x.astype(BF16)
        return jnp.concatenate([xb * m_lo, xb * m_hi], axis=1)

    def bmm(x, y):
        return lax.dot_general(x, y, (((2,), (1,)), ((0,), (0,))), preferred_element_type=F32)

    def bmm_nt(x, y):
        return lax.dot_general(x, y, (((2,), (2,)), ((0,), (0,))), preferred_element_type=F32)

    def block_mask(nblk, cmp):
        wi = lax.broadcasted_iota(I32, (C, nblk * C), 0)
        wj = lax.broadcasted_iota(I32, (C, nblk * C), 1) & (C - 1)
        return cmp(wi, wj)

    r_p, k2_p, v_p, a_p = pairs(r), pairs(k2), pairs(v), pairs(a)
    p_p, pinv_p, pprev_p = pairs(p), pairs(pinv), pairs(pprev)
    kk_p = pairs(kk)
    kkn = kk_p * lax.rsqrt(jnp.maximum(head_sum(kk_p * kk_p), 1e-24))
    at = (-kkn) * pprev_p
    bt = (kkn * a_p) * pinv_p
    kt = k2_p * pinv_p
    rt = r_p * p_p
    pend_p = pairs(pend)
    pend2 = jnp.concatenate([pend_p, pend_p], axis=1)

    lhs2 = jnp.concatenate([at, rt], axis=1).astype(BF16)
    rhs4 = jnp.concatenate([halves(kt), halves(bt)], axis=1)
    gc = bmm_nt(lhs2, rhs4)
    strict2 = block_mask(2, lambda i_, j_: i_ > j_)
    incl4 = block_mask(4, lambda i_, j_: i_ >= j_)
    a_ak = jnp.where(strict2, gc[:, 0:C, 0:2 * C], 0.0)
    nmat = jnp.where(strict2, gc[:, 0:C, 2 * C:4 * C], 0.0)
    a_rkb = jnp.where(incl4, gc[:, C:2 * C, :], 0.0)

    g0 = st_ref[...]
    sg = bmm_nt(lhs2, g0.astype(BF16))
    vm2 = halves(v_p)
    u = sg[:, 0:C] + bmm(a_ak.astype(BF16), vm2)
    pw = nmat.astype(BF16)
    u = u + bmm(pw, halves(u))
    n = 1
    while 2 * n < C:
        pw = bmm(pw, halves(pw)).astype(BF16)
        u = u + bmm(pw, halves(u))
        n *= 2
    um2b = halves(u)
    y = sg[:, C:2 * C] + bmm(a_rkb.astype(BF16), jnp.concatenate([vm2, um2b], axis=1))
    uv = jnp.concatenate([u, v_p], axis=1).astype(BF16)
    bkh = (jnp.concatenate([bt, kt], axis=1) * pend2).astype(BF16)
    upd = lax.dot_general(uv, bkh, (((1,), (1,)), ((0,), (0,))), preferred_element_type=F32)
    st_ref[...] = g0 * pend2 + jnp.where(same_head, upd, 0.0)

    inv_n = 1.0 / N
    yc = y - head_sum(y) * inv_n
    var = head_sum(yc * yc) * inv_n
    yn = yc * lax.rsqrt(var + GN_EPS) * per_pair(lng_ref) + per_pair(lnb_ref)
    bonus = head_sum(r_p * k2_p * per_pair(rk_ref)) * v_p
    out = (yn + bonus) * pairs(gate)
    for b in range(nb):
        for j in range(HP):
            o_ref[b, :, j * LANES:(j + 1) * LANES] = out[b * HP + j].astype(BF16)


def _rwkv(z, B, T, mu, w0, a0, k_k, k_a, r_k, ln_g, ln_b, w_up, a_up):
    C = RW_CHUNK
    W = RW_WIDTH
    z3 = z.reshape(B, T, z.shape[1])
    row = lambda a: a.reshape(1, -1).astype(F32)
    mu_r, mu_k, mu_v = mu[0:W], mu[W:2 * W], mu[2 * W:3 * W]
    mu_wa = mu[3 * W:3 * W + 2 * RW_LORA]
    zspec = lambda col: pl.BlockSpec((B, C, W), lambda c: (0, c, col // W))
    pspec = lambda width: pl.BlockSpec((1, width), lambda c: (0, 0))
    npairs = RW_HEADS // 2
    prow = lambda a: a.reshape(npairs, 1, LANES).astype(F32)
    ppspec = pl.BlockSpec((npairs, 1, LANES), lambda c: (0, 0, 0))
    wspec = pl.BlockSpec((RW_LORA, W), lambda c: (0, 0))
    out = pl.pallas_call(
        functools.partial(_rwkv_kernel, nb=B),
        grid=(T // C,),
        in_specs=[zspec(COL_R), zspec(COL_K), zspec(COL_V), zspec(COL_GRW),
                  pl.BlockSpec((B, C, LANES), lambda c: (0, c, COL_WA // LANES)),
                  pspec(W), pspec(W), pspec(W), pspec(LANES),
                  pspec(W), pspec(W), pspec(W), pspec(W), ppspec, ppspec, ppspec,
                  wspec, wspec],
        out_specs=pl.BlockSpec((B, C, W), lambda c: (0, c, 0)),
        out_shape=jax.ShapeDtypeStruct((B, T, W), BF16),
        scratch_shapes=[pltpu.VMEM((B, 1, W), F32), pltpu.VMEM((B, 1, W), F32), pltpu.VMEM((B, 1, W), F32),
                        pltpu.VMEM((B, 1, LANES), F32),
                        pltpu.VMEM((B * npairs, LANES, LANES), F32)],
        compiler_params=pltpu.CompilerParams(dimension_semantics=("arbitrary",),
                                             vmem_limit_bytes=VMEM_LIMIT),
        name="rwkv",
    )(z3, z3, z3, z3, z3,
      row(mu_r), row(mu_k), row(mu_v), row(mu_wa),
      row(w0), row(a0), row(k_k), row(k_a), prow(r_k), prow(ln_g), prow(ln_b),
      w_up.astype(BF16), a_up.astype(BF16))
    return out.reshape(B * T, W)


def _biastab_kernel(rb_ref, o_ref):
    o_ref[...] = jnp.zeros_like(o_ref)
    nb = NUM_BUCKETS // 2
    max_exact = nb // 2
    c = lax.broadcasted_iota(I32, (2 * BQ, BQ), 0)
    r = lax.broadcasted_iota(I32, (2 * BQ, BQ), 1)
    rel = c - BQ - r
    ret = jnp.where(rel > 0, nb, 0)
    n = jnp.abs(rel)
    nf = jnp.maximum(n, 1).astype(F32)
    large = max_exact + (jnp.log(nf / max_exact) / math.log(MAX_DISTANCE / max_exact)
                         * (nb - max_exact)).astype(I32)
    large = jnp.minimum(large, nb - 1) & (NUM_BUCKETS - 1)
    bucket = ret + jnp.where(n < max_exact, n, large)
    for h in range(DS_HEADS):
        far = rb_ref[nb - 1, h]
        acc = jnp.zeros((2 * BQ, BQ), F32)
        for b in range(NUM_BUCKETS):
            acc = jnp.where(bucket == b, rb_ref[b, h] - far, acc)
        o_ref[SK:SK + 2 * BQ, h * BQ:(h + 1) * BQ] = acc * LOG2E


def _biastab(rel_bias):
    return pl.pallas_call(
        _biastab_kernel,
        in_specs=[pl.BlockSpec(memory_space=pltpu.SMEM)],
        out_specs=pl.BlockSpec(memory_space=pltpu.VMEM),
        out_shape=jax.ShapeDtypeStruct((BIAS_ROWS, DS_HEADS * BQ), F32),
        compiler_params=pltpu.CompilerParams(vmem_limit_bytes=VMEM_LIMIT),
        name="biastab",
    )(rel_bias.astype(F32))


def _fold_rows(x, op):
    n = x.shape[0] // SUBLANES
    accs = [x[j * SUBLANES:(j + 1) * SUBLANES] for j in range(min(4, n))]
    for j in range(4, n):
        accs[j % 4] = op(accs[j % 4], x[j * SUBLANES:(j + 1) * SUBLANES])
    while len(accs) > 1:
        accs = [op(accs[j], accs[j + 1]) for j in range(0, len(accs) - 1, 2)] + (
            [accs[-1]] if len(accs) % 2 else [])
    return accs[0]


def _dsa_kernel(ql_ref, kx_ref, gds_ref, kid_ref, ckv_ref, ckvt_ref, qg_ref, wq_ref, wuk_ref, wuvt_ref,
                tab_ref, o_ref,
                sc_ref, pl_ref, cand_ref, lgt_ref, acc_ref, m_ref, qat_ref, qit_ref, w_ref, out_ref, *, topk):
    i = pl.program_id(1)
    q0 = i * QB
    ntile = jnp.right_shift(q0 + (QB + SK - 1), SK.bit_length() - 1)
    R = DS_KV_RANK
    GW = DS_HEADS * QB

    ql = ql_ref[...]
    ms = jnp.mean(ql * ql, axis=-1, keepdims=True)
    qn = (ql * lax.rsqrt(ms + NORM_EPS) * qg_ref[...]).astype(BF16)
    qt = _dot(qn, wq_ref[...]).T
    for h in range(DS_HEADS):
        qh = qt[h * DS_HEAD:(h + 1) * DS_HEAD, :].astype(BF16)
        qat_ref[:, h * QB:(h + 1) * QB] = (_dot(wuk_ref[h], qh) * (DS_HEAD ** -0.5 * LOG2E)).astype(BF16)
    for pr in range(IDX_HEADS // 2):
        base = DS_WIDTH + 2 * pr * IDX_DIM
        qit_ref[pr] = jnp.concatenate([qt[base:base + IDX_DIM, :],
                                       qt[base + IDX_DIM:base + 2 * IDX_DIM, :]], axis=1).astype(BF16)
    w_ref[...] = kx_ref[...].T[IDX_DIM:IDX_DIM + IDX_HEADS, :] * (IDX_HEADS ** -0.5 * IDX_DIM ** -0.5)

    lanei = lax.broadcasted_iota(I32, (1, QB), 1)
    csh = CHUNK.bit_length() - 1
    limit = jnp.left_shift(jnp.right_shift(q0 + lanei, csh) + 1, csh)
    rowi = lax.broadcasted_iota(I32, (SK, QB), 0)

    def score_tile(kt, carry):
        off = pl.multiple_of(kt * SK, SK)
        kid = kid_ref[pl.ds(off, SK), :]
        s = jnp.zeros((SK, QB), F32)
        for pr in range(IDX_HEADS // 2):
            lg = _dot(kid, qit_ref[pr])
            s = s + w_ref[2 * pr:2 * pr + 1, :] * jnp.maximum(lg[:, 0:QB], 0.0)
            s = s + w_ref[2 * pr + 1:2 * pr + 2, :] * jnp.maximum(lg[:, QB:2 * QB], 0.0)
        adm = (off + rowi) < limit
        sc_ref[pl.ds(off, SK), :] = jnp.where(adm, s, -jnp.inf)
        return carry

    lax.fori_loop(0, ntile, score_tile, 0)

    def key_to_f32(key):
        return pltpu.bitcast(jnp.where(key < 0, key ^ 0x7FFFFFFF, key), F32)

    def count(pred):
        def body(kt, acc):
            off = pl.multiple_of(kt * SK, SK)
            sc = sc_ref[pl.ds(off, SK), :]
            return acc + _fold_rows(jnp.where(pred(sc, off), 1, 0).astype(I32), jnp.add)
        acc = lax.fori_loop(0, ntile, body, jnp.zeros((SUBLANES, QB), I32))
        return jnp.sum(acc, axis=0, keepdims=True)

    def bit_step(it, carry):
        lo, cnt_lo = carry
        cand = lo + jnp.left_shift(jnp.int32(1), 31 - it)
        cf = key_to_f32(cand)
        cnt = count(lambda sc, off: sc >= cf)
        take = cnt >= topk
        return jnp.where(take, cand, lo), jnp.where(take, cnt, cnt_lo)

    def all_settled(cnt_lo):
        return jnp.min(jnp.where((cnt_lo == topk) | (limit < topk), 1, 0))

    def slow_search():
        lo, cnt_lo = lax.fori_loop(0, SEARCH_MIN_BITS, bit_step,
                                   (jnp.full((1, QB), INT_MIN, I32), jnp.full((1, QB), 2 ** 30, I32)))

        def more_bits(c):
            it, lo, cnt_lo, _ = c
            lo, cnt_lo = lax.fori_loop(it, it + SEARCH_STEP_BITS, bit_step, (lo, cnt_lo))
            return it + SEARCH_STEP_BITS, lo, cnt_lo, all_settled(cnt_lo)

        _, lo, _, settled = lax.while_loop(lambda c: (c[0] < 32) & (c[3] == 0), more_bits,
                                           (jnp.int32(SEARCH_MIN_BITS), lo, cnt_lo, all_settled(cnt_lo)))
        return lo, settled

    ngrp_max = sc_ref.shape[0] // PLANE_ROWS
    ngrp = ntile * (SK // PLANE_ROWS)

    @pl.when(i == 0)
    def _():
        pl_ref[...] = jnp.zeros_like(pl_ref)

    cand_ref[...] = jnp.zeros_like(cand_ref)

    def to_planes(g, c):
        base = pl.multiple_of(g * PLANE_ROWS, PLANE_ROWS)
        words = []
        for j in range(32):
            bits = pltpu.bitcast(sc_ref[pl.ds(base + SUBLANES * j, SUBLANES), :], I32)
            words.append(jnp.where(bits < 0, ~bits, bits | INT_MIN))
        j, m = 16, 0x0000FFFF
        while j:
            k = 0
            while k < 32:
                t = (words[k] ^ lax.shift_right_logical(words[k + j], j)) & m
                words[k] = words[k] ^ t
                words[k + j] = words[k + j] ^ lax.shift_left(t, j)
                k = (k + j + 1) & ~j
            j >>= 1
            m = (m ^ (m << j)) & 0xFFFFFFFF
        for b in range(32):
            pl_ref[pl.ds(base + SUBLANES * b, SUBLANES), :] = words[b]
        cand_ref[pl.ds(pl.multiple_of(g * SUBLANES, SUBLANES), SUBLANES), :] = jnp.full((SUBLANES, QB), -1, I32)
        return c

    lax.fori_loop(0, ngrp, to_planes, 0)

    def radix_step(it, c):
        above, tb = c
        prow = pl.multiple_of(it * SUBLANES, SUBLANES)
        acc = jnp.zeros((SUBLANES, QB), I32)
        for g in range(ngrp_max):
            plane = pl_ref[pl.ds(g * PLANE_ROWS + prow, SUBLANES), :]
            acc = acc + lax.population_count(cand_ref[g * SUBLANES:(g + 1) * SUBLANES, :] & plane)
        ones = jnp.sum(acc, axis=0, keepdims=True)
        take = above + ones >= topk
        for g in range(ngrp_max):
            plane = pl_ref[pl.ds(g * PLANE_ROWS + prow, SUBLANES), :]
            cand = cand_ref[g * SUBLANES:(g + 1) * SUBLANES, :]
            cand_ref[g * SUBLANES:(g + 1) * SUBLANES, :] = jnp.where(take, cand & plane, cand & ~plane)
        return (jnp.where(take, above, above + ones),
                jnp.where(take, tb | lax.shift_left(jnp.int32(1), 31 - it), tb))

    _, tbits = lax.fori_loop(0, 32, radix_step, (jnp.zeros((1, QB), I32), jnp.zeros((1, QB), I32)))
    lo_fast = tbits ^ INT_MIN
    thr_fast = key_to_f32(jnp.maximum(lo_fast, KEY_NEG_INF))
    fast_ok = all_settled(count(lambda sc, off: sc >= thr_fast))
    lo, settled = lax.cond(fast_ok == 1, lambda: (lo_fast, fast_ok), slow_search)
    has_thr = lo > KEY_NEG_INF
    thr = key_to_f32(jnp.maximum(lo, KEY_NEG_INF))
    nbits = max(1, (sc_ref.shape[0] - 1).bit_length())
    take_all_ties = jnp.full((1, QB), sc_ref.shape[0], I32)

    def tie_cut():
        cnt_gt = count(lambda sc, off: sc > thr)
        cnt_eq = count(lambda sc, off: sc == thr)
        tied = (cnt_gt + cnt_eq > topk) & has_thr

        def search_cut():
            def idx_step(it, m):
                cand = m + jnp.left_shift(jnp.int32(1), nbits - 1 - it)
                cnt = cnt_gt + count(lambda sc, off: (sc == thr) & ((off + rowi) < cand))
                return jnp.where(cnt < topk, cand, m)
            return lax.fori_loop(0, nbits, idx_step, jnp.zeros((1, QB), I32))

        return lax.cond(jnp.max(jnp.where(tied, 1, 0)) > 0, search_cut, lambda: take_all_ties)

    cut = lax.cond(settled == 1, lambda: take_all_ties, tie_cut)
    cut = jnp.where(has_thr, cut, -1)

    def selection_mask(off):
        sc = sc_ref[pl.ds(off, SK), :]
        sel = (sc > thr) | ((sc == thr) & ((off + rowi) <= cut))
        return jnp.where(sel, 0.0, MASK_NEG).astype(F32)

    near_lo = q0 - BQ

    def attend(off, bias_off, first=False):
        rows = pl.ds(off, SK)
        s = _dot(ckv_ref[rows, :], qat_ref[...])
        mk = selection_mask(off)
        m_old = jnp.full((1, GW), MASK_NEG, F32) if first else m_ref[...]
        tmax = []
        for h in range(DS_HEADS):
            cs = slice(h * QB, (h + 1) * QB)
            t = s[:, cs] + mk
            if bias_off is not None:
                t = t + jnp.concatenate(
                    [tab_ref[pl.ds(pl.multiple_of(jnp.maximum(bias_off - j * BQ, 0), BQ), SK),
                             h * BQ:(h + 1) * BQ] for j in range(QB // BQ)], axis=1)
            lgt_ref[:, cs] = t
            tmax.append(jnp.max(_fold_rows(t, jnp.maximum), axis=0, keepdims=True))
        m_new = jnp.maximum(m_old, jnp.concatenate(tmax, axis=1))
        m_ref[...] = m_new
        pr = jnp.exp2(lgt_ref[...] - m_new).astype(BF16)
        pv = _dot(ckvt_ref[:, rows], pr)
        acc_ref[...] = pv if first else acc_ref[...] * jnp.exp2(m_old - m_new) + pv

    def far_tile(kt, c):
        attend(pl.multiple_of(kt * SK, SK), None)
        return c

    def edge_tile(kt, c):
        off = pl.multiple_of(kt * SK, SK)
        attend(off, SK + off - near_lo)
        return c

    nfar = jnp.maximum(ntile - 2, 0)
    attend(0, SK - near_lo, first=True)
    lax.fori_loop(1, nfar, far_tile, 0)
    lax.fori_loop(jnp.maximum(nfar, 1), ntile, edge_tile, 0)

    o_lat = acc_ref[0:R, :] * (1.0 / acc_ref[R:R + 1, :])
    for h in range(DS_HEADS):
        out_ref[h * DS_HEAD:(h + 1) * DS_HEAD, :] = _dot(
            wuvt_ref[h], o_lat[:, h * QB:(h + 1) * QB].astype(BF16))

    g = gds_ref[...]
    o_ref[...] = (out_ref[...].T * (g * _sigmoid(g))).astype(BF16)


def _dsa(z, ckv, kid, B, T, q_norm_g, w_uq, w_uk, w_uv, iw_q, tab):
    nq = T // QB
    topk = min(TOPK_MAX, T // 4)
    R = DS_KV_RANK
    wq = jnp.concatenate([w_uq, iw_q], axis=1).astype(BF16)
    wuk_h = jnp.transpose(w_uk, (1, 0, 2)).astype(BF16)
    wuv_t = jnp.transpose(w_uv, (1, 2, 0)).astype(BF16)
    ckv_t = jnp.concatenate([jnp.swapaxes(ckv.reshape(B, T, R), 1, 2),
                             jnp.ones((B, ONES_ROWS, T), BF16)], axis=1).reshape(B * (R + ONES_ROWS), T)
    const2 = lambda b, i: (0, 0)
    const3 = lambda b, i: (0, 0, 0)
    resident = dict(pipeline_mode=pl.Buffered(1))
    return pl.pallas_call(
        functools.partial(_dsa_kernel, topk=topk),
        grid=(B, nq),
        in_specs=[pl.BlockSpec((QB, DS_Q_RANK), lambda b, i: (b * nq + i, COL_Q // DS_Q_RANK)),
                  pl.BlockSpec((QB, LANES), lambda b, i: (b * nq + i, COL_KX // LANES)),
                  pl.BlockSpec((QB, DS_WIDTH), lambda b, i: (b * nq + i, COL_GDS // DS_WIDTH)),
                  pl.BlockSpec((T, IDX_DIM), lambda b, i: (b, 0), **resident),
                  pl.BlockSpec((T, R), lambda b, i: (b, 0), **resident),
                  pl.BlockSpec((R + ONES_ROWS, T), lambda b, i: (b, 0), **resident),
                  pl.BlockSpec((1, DS_Q_RANK), const2),
                  pl.BlockSpec((DS_Q_RANK, 2 * DS_WIDTH), const2, **resident),
                  pl.BlockSpec((DS_HEADS, R, DS_HEAD), const3, **resident),
                  pl.BlockSpec((DS_HEADS, DS_HEAD, R), const3, **resident),
                  pl.BlockSpec((BIAS_ROWS, DS_HEADS * BQ), const2, **resident)],
        out_specs=pl.BlockSpec((QB, DS_WIDTH), lambda b, i: (b * nq + i, 0)),
        out_shape=jax.ShapeDtypeStruct((B * T, DS_WIDTH), BF16),
        scratch_shapes=[pltpu.VMEM((T, QB), F32),
                        pltpu.VMEM((T, QB), I32),
                        pltpu.VMEM((T // PLANE_ROWS * SUBLANES, QB), I32),
                        pltpu.VMEM((SK, DS_HEADS * QB), F32),
                        pltpu.VMEM((R + ONES_ROWS, DS_HEADS * QB), F32),
                        pltpu.VMEM((1, DS_HEADS * QB), F32),
                        pltpu.VMEM((R, DS_HEADS * QB), BF16),
                        pltpu.VMEM((IDX_HEADS // 2, IDX_DIM, 2 * QB), BF16),
                        pltpu.VMEM((IDX_HEADS, QB), F32),
                        pltpu.VMEM((DS_WIDTH, QB), F32)],
        compiler_params=pltpu.CompilerParams(dimension_semantics=("parallel", "arbitrary"),
                                             vmem_limit_bytes=VMEM_LIMIT),
        name="dsa",
    )(z, z, z, kid, ckv, ckv_t, q_norm_g.reshape(1, -1).astype(F32), wq, wuk_h, wuv_t, tab)


def _post_kernel(x_ref, a1_ref, a2_ref, p_ref, w1_ref, w2_ref, pw_ref, gw_ref, fg_ref, o_ref):
    h = x_ref[...] + _dot(a1_ref[...], w1_ref[...]) + _dot(a2_ref[...], w2_ref[...])
    e = _dot(p_ref[...].astype(BF16), pw_ref[...])
    gate = _sigmoid(_dot(h.astype(BF16), gw_ref[...]))
    h2 = h + e * gate
    ms = jnp.mean(h2 * h2, axis=-1, keepdims=True)
    o_ref[...] = h2 * lax.rsqrt(ms + NORM_EPS) * fg_ref[...]


def _post(x2, o_rw, o_ds, p2, w_out, ple_w, gate_w, final_g, tm=512):
    m, d = x2.shape
    kh = o_rw.shape[1]
    pd = p2.shape[1]
    resident = dict(pipeline_mode=pl.Buffered(1))
    return pl.pallas_call(
        _post_kernel,
        grid=(m // tm,),
        in_specs=[pl.BlockSpec((tm, d), lambda i: (i, 0)),
                  pl.BlockSpec((tm, kh), lambda i: (i, 0)),
                  pl.BlockSpec((tm, kh), lambda i: (i, 0)),
                  pl.BlockSpec((tm, pd), lambda i: (i, 0)),
                  pl.BlockSpec((kh, d), lambda i: (0, 0), **resident),
                  pl.BlockSpec((kh, d), lambda i: (1, 0), **resident),
                  pl.BlockSpec((pd, d), lambda i: (0, 0), **resident),
                  pl.BlockSpec((d, d), lambda i: (0, 0), **resident),
                  pl.BlockSpec((1, d), lambda i: (0, 0))],
        out_specs=pl.BlockSpec((tm, d), lambda i: (i, 0)),
        out_shape=jax.ShapeDtypeStruct((m, d), F32),
        compiler_params=pltpu.CompilerParams(dimension_semantics=("parallel",),
                                             vmem_limit_bytes=VMEM_LIMIT),
        name="post",
    )(x2, o_rw, o_ds, p2, w_out, w_out, ple_w, gate_w, final_g)


def _split_w_in(w):
    w = w.astype(BF16)
    return w, w[:, w.shape[1] - DS_WIDTH:]


def kernel(x, p, w_in, norm_g, rw_mu, rw_w0, rw_w_up, rw_a0, rw_a_up, rw_k_k, rw_k_a, rw_r_k, rw_ln_g, rw_ln_b, ds_q_norm_g, ds_kv_norm_g, idx_k_norm_g, ds_w_uq, ds_w_uk, ds_w_uv, idx_w_q, rel_bias, w_out, ple_w, ple_gate_w, final_g):
    B, T, D = x.shape
    depth = w_in.shape[0]
    assert depth == 1 and T % SK == 0 and T % QB == 0 and T % RW_CHUNK == 0 and (B * T) % 512 == 0
    h = x.reshape(B * T, D)
    tab = _biastab(rel_bias)
    for i in range(depth):
        z, ckv, kid = _inproj(h, norm_g[i].reshape(1, D), *_split_w_in(w_in[i]),
                              ds_kv_norm_g[i].reshape(1, -1), idx_k_norm_g[i].reshape(1, -1))
        o_rw = _rwkv(z, B, T, rw_mu[i], rw_w0[i], rw_a0[i], rw_k_k[i], rw_k_a[i],
                     rw_r_k[i].reshape(-1), rw_ln_g[i], rw_ln_b[i], rw_w_up[i], rw_a_up[i])
        o_ds = _dsa(z, ckv, kid, B, T, ds_q_norm_g[i], ds_w_uq[i], ds_w_uk[i], ds_w_uv[i],
                    idx_w_q[i], tab)
        h = _post(h, o_rw, o_ds, p[i].reshape(B * T, -1), w_out[i].astype(BF16),
                  ple_w[i].astype(BF16), ple_gate_w[i].astype(BF16), final_g.reshape(1, D))
    return h.reshape(B, T, D)
```

```python
import functools
import math

import jax
import jax.numpy as jnp
from jax import lax
from jax.experimental import pallas as pl
from jax.experimental.pallas import tpu as pltpu

F32 = jnp.float32
BF16 = jnp.bfloat16
I32 = jnp.int32

RW_WIDTH = 1024
RW_HEAD = 64
RW_HEADS = 16
RW_LORA = 64
DS_WIDTH = 1024
DS_HEAD = 64
DS_HEADS = 16
DS_Q_RANK = 384
DS_KV_RANK = 256
IDX_HEADS = 16
IDX_DIM = 64
TOPK_MAX = 256
CHUNK = 64
NUM_BUCKETS = 32
MAX_DISTANCE = 128
NORM_EPS = 1e-6
GN_EPS = 64e-5

COL_R, COL_K, COL_V, COL_GRW, COL_GDS = 0, 1024, 2048, 3072, 4096
COL_KV = 5120
COL_Q = 5376
COL_WA = 5760
COL_KX = 5888
Z_WIDTH = 6016

LANES = 128
SUBLANES = 8
QB = 256
BQ = 128
SK = 512
BIAS_ROWS = 2 * SK + 2 * BQ
RW_CHUNK = 64
INT_MIN = -2 ** 31
KEY_NEG_INF = -2139095041
PLANE_ROWS = 32 * SUBLANES
SEARCH_MIN_BITS = 23
SEARCH_STEP_BITS = 3
assert (32 - SEARCH_MIN_BITS) % SEARCH_STEP_BITS == 0
MASK_NEG = -1e30
LOG2E = 1.4426950408889634
ONES_ROWS = 16
VMEM_LIMIT = 52 * 1024 * 1024


def _sigmoid(x):
    return 1.0 / (1.0 + jnp.exp(-x))


def _dot(a, b):
    return jnp.dot(a, b, preferred_element_type=F32)


def _inproj_plan():
    s_wa = 3 * RW_WIDTH
    s_grw = s_wa + 2 * RW_LORA
    s_q = s_grw + RW_WIDTH
    s_kv = s_q + DS_Q_RANK
    s_kx = s_kv + DS_KV_RANK
    return [(COL_R, 0, 3 * RW_WIDTH), (COL_GRW, s_grw, RW_WIDTH), (COL_GDS, None, DS_WIDTH),
            (COL_KV, s_kv, DS_KV_RANK), (COL_Q, s_q, DS_Q_RANK), (COL_WA, s_wa, LANES),
            (COL_KX, s_kx, LANES)]


def _inproj_kernel(x_ref, g_ref, w_ref, wg_ref, gkv_ref, gik_ref, o_ref, ckv_ref, kid_ref, *, tn):
    x = x_ref[...]
    ms = jnp.mean(x * x, axis=-1, keepdims=True)
    xn = (x * lax.rsqrt(ms + NORM_EPS) * g_ref[...]).astype(BF16)
    for dst, src, width in _inproj_plan():
        for j in range(0, width, tn):
            w = min(tn, width - j)
            wt = wg_ref[:, j:j + w] if src is None else w_ref[:, src + j:src + j + w]
            o_ref[:, dst + j:dst + j + w] = _dot(xn, wt)
    kv = o_ref[:, COL_KV:COL_KV + DS_KV_RANK]
    ckv_ref[...] = (kv * lax.rsqrt(jnp.mean(kv * kv, axis=-1, keepdims=True) + NORM_EPS)
                    * gkv_ref[...]).astype(BF16)
    ki = o_ref[:, COL_KX:COL_KX + IDX_DIM]
    kid_ref[...] = (ki * lax.rsqrt(jnp.mean(ki * ki, axis=-1, keepdims=True) + NORM_EPS)
                    * gik_ref[...]).astype(BF16)


def _inproj(x2, g, w, w_gds, gkv, gik, tm=256, tn=768):
    m, d = x2.shape
    resident = dict(pipeline_mode=pl.Buffered(1))
    return pl.pallas_call(
        functools.partial(_inproj_kernel, tn=tn),
        grid=(m // tm,),
        in_specs=[pl.BlockSpec((tm, d), lambda i: (i, 0)),
                  pl.BlockSpec((1, d), lambda i: (0, 0)),
                  pl.BlockSpec(w.shape, lambda i: (0, 0), **resident),
                  pl.BlockSpec(w_gds.shape, lambda i: (0, 0), **resident),
                  pl.BlockSpec((1, DS_KV_RANK), lambda i: (0, 0)),
                  pl.BlockSpec((1, IDX_DIM), lambda i: (0, 0))],
        out_specs=[pl.BlockSpec((tm, Z_WIDTH), lambda i: (i, 0)),
                   pl.BlockSpec((tm, DS_KV_RANK), lambda i: (i, 0)),
                   pl.BlockSpec((tm, IDX_DIM), lambda i: (i, 0))],
        out_shape=[jax.ShapeDtypeStruct((m, Z_WIDTH), F32),
                   jax.ShapeDtypeStruct((m, DS_KV_RANK), BF16),
                   jax.ShapeDtypeStruct((m, IDX_DIM), BF16)],
        compiler_params=pltpu.CompilerParams(dimension_semantics=("parallel",),
                                             vmem_limit_bytes=VMEM_LIMIT),
        name="inproj",
    )(x2, g, w, w_gds, gkv, gik)


def _rwkv_kernel(r_ref, k_ref, v_ref, g_ref, wa_ref,
                 mur_ref, muk_ref, muv_ref, muwa_ref,
                 w0_ref, a0_ref, kk_ref, ka_ref, rk_ref, lng_ref, lnb_ref,
                 wup_ref, aup_ref,
                 o_ref,
                 pr_ref, pk_ref, pv_ref, pwa_ref, st_ref, *, nb):
    C = RW_CHUNK
    N = RW_HEAD

    @pl.when(pl.program_id(0) == 0)
    def _():
        pr_ref[...] = jnp.zeros_like(pr_ref)
        pk_ref[...] = jnp.zeros_like(pk_ref)
        pv_ref[...] = jnp.zeros_like(pv_ref)
        pwa_ref[...] = jnp.zeros_like(pwa_ref)
        st_ref[...] = jnp.zeros_like(st_ref)

    row = lax.broadcasted_iota(I32, (SUBLANES, 1), 0)

    def shift(ref, prev_ref, mu_ref):
        z = ref[...].reshape(nb * C, ref.shape[2])
        zp = pltpu.roll(z, 1, 0)
        parts = []
        for b in range(nb):
            parts.append(jnp.where(row == 0, prev_ref[b], zp[b * C:b * C + SUBLANES]))
            parts.append(zp[b * C + SUBLANES:(b + 1) * C])
            prev_ref[b] = z[(b + 1) * C - 1:(b + 1) * C, :]
        zp = jnp.concatenate(parts, axis=0)
        return z + mu_ref[...] * (zp - z)

    r = shift(r_ref, pr_ref, mur_ref)
    k = shift(k_ref, pk_ref, muk_ref)
    v = shift(v_ref, pv_ref, muv_ref)
    wa = shift(wa_ref, pwa_ref, muwa_ref)
    wd = wa[:, 0:RW_LORA]
    ad = wa[:, RW_LORA:2 * RW_LORA]

    wl = w0_ref[...] + _dot(jnp.tanh(wd).astype(BF16), wup_ref[...])
    lw = -math.exp(-0.5) * _sigmoid(wl)
    a = _sigmoid(a0_ref[...] + _dot(ad.astype(BF16), aup_ref[...]))
    kk = k * kk_ref[...]
    k2 = k * (1.0 + (a - 1.0) * ka_ref[...])

    ti = lax.broadcasted_iota(I32, (C, C), 0)
    tj = lax.broadcasted_iota(I32, (C, C), 1)
    tri = jnp.where(ti >= tj, 1.0, 0.0).astype(F32)

    def per_row(mat, x):
        hi = x.astype(BF16)
        r1 = x - hi.astype(F32)
        mid = r1.astype(BF16)
        lo = (r1 - mid.astype(F32)).astype(BF16)
        mb = mat.astype(BF16)
        return jnp.concatenate(
            [_dot(mb, hi[b * C:(b + 1) * C]) + _dot(mb, mid[b * C:(b + 1) * C]) + _dot(mb, lo[b * C:(b + 1) * C])
             for b in range(nb)], axis=0)

    cum = per_row(tri, lw)
    p = jnp.exp(cum)
    pinv = jnp.exp(-cum)
    pprev = jnp.exp(cum - lw)
    tot = per_row(jnp.ones((C, C), F32), lw)
    pend = jnp.exp(tot)

    g = g_ref[...].reshape(nb * C, RW_WIDTH)
    gate = g * _sigmoid(g)

    HP = RW_HEADS // 2
    NP = nb * HP

    def pairs(x):
        return jnp.stack([x[b * C:(b + 1) * C, j * LANES:(j + 1) * LANES]
                          for b in range(nb) for j in range(HP)], axis=0)

    def per_pair(ref):
        return jnp.concatenate([ref[...]] * nb, axis=0)

    lane = lax.broadcasted_iota(I32, (1, 1, LANES), 2)
    m_lo = jnp.where(lane < N, 1.0, 0.0).astype(BF16)
    m_hi = jnp.where(lane < N, 0.0, 1.0).astype(BF16)
    bi = lax.broadcasted_iota(I32, (LANES, LANES), 0)
    bj = lax.broadcasted_iota(I32, (LANES, LANES), 1)
    same_head = (bi < N) == (bj < N)
    ones_bd = jnp.where(same_head, 1.0, 0.0).astype(BF16)

    def head_sum(x):
        return _dot(x.reshape(NP * C, LANES).astype(BF16), ones_bd).reshape(NP, C, LANES)

    def halves(x):
        xb = x.astype(BF16)
        return jnp.concatenate([xb * m_lo, xb * m_hi], axis=1)

    def bmm(x, y):
        return lax.dot_general(x, y, (((2,), (1,)), ((0,), (0,))), preferred_element_type=F32)

    def bmm_nt(x, y):
        return lax.dot_general(x, y, (((2,), (2,)), ((0,), (0,))), preferred_element_type=F32)

    def block_mask(nblk, cmp):
        wi = lax.broadcasted_iota(I32, (C, nblk * C), 0)
        wj = lax.broadcasted_iota(I32, (C, nblk * C), 1) & (C - 1)
        return cmp(wi, wj)

    r_p, k2_p, v_p, a_p = pairs(r), pairs(k2), pairs(v), pairs(a)
    p_p, pinv_p, pprev_p = pairs(p), pairs(pinv), pairs(pprev)
    kk_p = pairs(kk)
    kkn = kk_p * lax.rsqrt(jnp.maximum(head_sum(kk_p * kk_p), 1e-24))
    at = (-kkn) * pprev_p
    bt = (kkn * a_p) * pinv_p
    kt = k2_p * pinv_p
    rt = r_p * p_p
    pend_p = pairs(pend)
    pend2 = jnp.concatenate([pend_p, pend_p], axis=1)

    lhs2 = jnp.concatenate([at, rt], axis=1).astype(BF16)
    rhs4 = jnp.concatenate([halves(kt), halves(bt)], axis=1)
    gc = bmm_nt(lhs2, rhs4)
    strict2 = block_mask(2, lambda i_, j_: i_ > j_)
    incl4 = block_mask(4, lambda i_, j_: i_ >= j_)
    a_ak = jnp.where(strict2, gc[:, 0:C, 0:2 * C], 0.0)
    nmat = jnp.where(strict2, gc[:, 0:C, 2 * C:4 * C], 0.0)
    a_rkb = jnp.where(incl4, gc[:, C:2 * C, :], 0.0)

    g0 = st_ref[...]
    sg = bmm_nt(lhs2, g0.astype(BF16))
    vm2 = halves(v_p)
    u = sg[:, 0:C] + bmm(a_ak.astype(BF16), vm2)
    pw = nmat.astype(BF16)
    u = u + bmm(pw, halves(u))
    n = 1
    while 2 * n < C:
        pw = bmm(pw, halves(pw)).astype(BF16)
        u = u + bmm(pw, halves(u))
        n *= 2
    um2b = halves(u)
    y = sg[:, C:2 * C] + bmm(a_rkb.astype(BF16), jnp.concatenate([vm2, um2b], axis=1))
    uv = jnp.concatenate([u, v_p], axis=1).astype(BF16)
    bkh = (jnp.concatenate([bt, kt], axis=1) * pend2).astype(BF16)
    upd = lax.dot_general(uv, bkh, (((1,), (1,)), ((0,), (0,))), preferred_element_type=F32)
    st_ref[...] = g0 * pend2 + jnp.where(same_head, upd, 0.0)

    inv_n = 1.0 / N
    yc = y - head_sum(y) * inv_n
    var = head_sum(yc * yc) * inv_n
    yn = yc * lax.rsqrt(var + GN_EPS) * per_pair(lng_ref) + per_pair(lnb_ref)
    bonus = head_sum(r_p * k2_p * per_pair(rk_ref)) * v_p
    out = (yn + bonus) * pairs(gate)
    for b in range(nb):
        for j in range(HP):
            o_ref[b, :, j * LANES:(j + 1) * LANES] = out[b * HP + j].astype(BF16)


def _rwkv(z, B, T, mu, w0, a0, k_k, k_a, r_k, ln_g, ln_b, w_up, a_up):
    C = RW_CHUNK
    W = RW_WIDTH
    z3 = z.reshape(B, T, z.shape[1])
    row = lambda a: a.reshape(1, -1).astype(F32)
    mu_r, mu_k, mu_v = mu[0:W], mu[W:2 * W], mu[2 * W:3 * W]
    mu_wa = mu[3 * W:3 * W + 2 * RW_LORA]
    zspec = lambda col: pl.BlockSpec((B, C, W), lambda c: (0, c, col // W))
    pspec = lambda width: pl.BlockSpec((1, width), lambda c: (0, 0))
    npairs = RW_HEADS // 2
    prow = lambda a: a.reshape(npairs, 1, LANES).astype(F32)
    ppspec = pl.BlockSpec((npairs, 1, LANES), lambda c: (0, 0, 0))
    wspec = pl.BlockSpec((RW_LORA, W), lambda c: (0, 0))
    out = pl.pallas_call(
        functools.partial(_rwkv_kernel, nb=B),
        grid=(T // C,),
        in_specs=[zspec(COL_R), zspec(COL_K), zspec(COL_V), zspec(COL_GRW),
                  pl.BlockSpec((B, C, LANES), lambda c: (0, c, COL_WA // LANES)),
                  pspec(W), pspec(W), pspec(W), pspec(LANES),
                  pspec(W), pspec(W), pspec(W), pspec(W), ppspec, ppspec, ppspec,
                  wspec, wspec],
        out_specs=pl.BlockSpec((B, C, W), lambda c: (0, c, 0)),
        out_shape=jax.ShapeDtypeStruct((B, T, W), BF16),
        scratch_shapes=[pltpu.VMEM((B, 1, W), F32), pltpu.VMEM((B, 1, W), F32), pltpu.VMEM((B, 1, W), F32),
                        pltpu.VMEM((B, 1, LANES), F32),
                        pltpu.VMEM((B * npairs, LANES, LANES), F32)],
        compiler_params=pltpu.CompilerParams(dimension_semantics=("arbitrary",),
                                             vmem_limit_bytes=VMEM_LIMIT),
        name="rwkv",
    )(z3, z3, z3, z3, z3,
      row(mu_r), row(mu_k), row(mu_v), row(mu_wa),
      row(w0), row(a0), row(k_k), row(k_a), prow(r_k), prow(ln_g), prow(ln_b),
      w_up.astype(BF16), a_up.astype(BF16))
    return out.reshape(B * T, W)


def _biastab_kernel(rb_ref, o_ref):
    o_ref[...] = jnp.zeros_like(o_ref)
    nb = NUM_BUCKETS // 2
    max_exact = nb // 2
    c = lax.broadcasted_iota(I32, (2 * BQ, BQ), 0)
    r = lax.broadcasted_iota(I32, (2 * BQ, BQ), 1)
    rel = c - BQ - r
    ret = jnp.where(rel > 0, nb, 0)
    n = jnp.abs(rel)
    nf = jnp.maximum(n, 1).astype(F32)
    large = max_exact + (jnp.log(nf / max_exact) / math.log(MAX_DISTANCE / max_exact)
                         * (nb - max_exact)).astype(I32)
    large = jnp.minimum(large, nb - 1) & (NUM_BUCKETS - 1)
    bucket = ret + jnp.where(n < max_exact, n, large)
    for h in range(DS_HEADS):
        far = rb_ref[nb - 1, h]
        acc = jnp.zeros((2 * BQ, BQ), F32)
        for b in range(NUM_BUCKETS):
            acc = jnp.where(bucket == b, rb_ref[b, h] - far, acc)
        o_ref[SK:SK + 2 * BQ, h * BQ:(h + 1) * BQ] = acc * LOG2E


def _biastab(rel_bias):
    return pl.pallas_call(
        _biastab_kernel,
        in_specs=[pl.BlockSpec(memory_space=pltpu.SMEM)],
        out_specs=pl.BlockSpec(memory_space=pltpu.VMEM),
        out_shape=jax.ShapeDtypeStruct((BIAS_ROWS, DS_HEADS * BQ), F32),
        compiler_params=pltpu.CompilerParams(vmem_limit_bytes=VMEM_LIMIT),
        name="biastab",
    )(rel_bias.astype(F32))


def _fold_rows(x, op):
    n = x.shape[0] // SUBLANES
    accs = [x[j * SUBLANES:(j + 1) * SUBLANES] for j in range(min(4, n))]
    for j in range(4, n):
        accs[j % 4] = op(accs[j % 4], x[j * SUBLANES:(j + 1) * SUBLANES])
    while len(accs) > 1:
        accs = [op(accs[j], accs[j + 1]) for j in range(0, len(accs) - 1, 2)] + (
            [accs[-1]] if len(accs) % 2 else [])
    return accs[0]


def _dsa_kernel(ql_ref, kx_ref, gds_ref, kid_ref, ckv_ref, ckvt_ref, qg_ref, wq_ref, wuk_ref, wuvt_ref,
                tab_ref, o_ref,
                sc_ref, pl_ref, cand_ref, lgt_ref, acc_ref, m_ref, qat_ref, qit_ref, w_ref, out_ref, *, topk):
    i = pl.program_id(1)
    q0 = i * QB
    ntile = jnp.right_shift(q0 + (QB + SK - 1), SK.bit_length() - 1)
    R = DS_KV_RANK
    GW = DS_HEADS * QB

    ql = ql_ref[...]
    ms = jnp.mean(ql * ql, axis=-1, keepdims=True)
    qn = (ql * lax.rsqrt(ms + NORM_EPS) * qg_ref[...]).astype(BF16)
    qt = _dot(qn, wq_ref[...]).T
    for h in range(DS_HEADS):
        qh = qt[h * DS_HEAD:(h + 1) * DS_HEAD, :].astype(BF16)
        qat_ref[:, h * QB:(h + 1) * QB] = (_dot(wuk_ref[h], qh) * (DS_HEAD ** -0.5 * LOG2E)).astype(BF16)
    for pr in range(IDX_HEADS // 2):
        base = DS_WIDTH + 2 * pr * IDX_DIM
        qit_ref[pr] = jnp.concatenate([qt[base:base + IDX_DIM, :],
                                       qt[base + IDX_DIM:base + 2 * IDX_DIM, :]], axis=1).astype(BF16)
    w_ref[...] = kx_ref[...].T[IDX_DIM:IDX_DIM + IDX_HEADS, :] * (IDX_HEADS ** -0.5 * IDX_DIM ** -0.5)

    lanei = lax.broadcasted_iota(I32, (1, QB), 1)
    csh = CHUNK.bit_length() - 1
    limit = jnp.left_shift(jnp.right_shift(q0 + lanei, csh) + 1, csh)
    rowi = lax.broadcasted_iota(I32, (SK, QB), 0)

    def score_tile(kt, carry):
        off = pl.multiple_of(kt * SK, SK)
        kid = kid_ref[pl.ds(off, SK), :]
        s = jnp.zeros((SK, QB), F32)
        for pr in range(IDX_HEADS // 2):
            lg = _dot(kid, qit_ref[pr])
            s = s + w_ref[2 * pr:2 * pr + 1, :] * jnp.maximum(lg[:, 0:QB], 0.0)
            s = s + w_ref[2 * pr + 1:2 * pr + 2, :] * jnp.maximum(lg[:, QB:2 * QB], 0.0)
        adm = (off + rowi) < limit
        sc_ref[pl.ds(off, SK), :] = jnp.where(adm, s, -jnp.inf)
        return carry

    lax.fori_loop(0, ntile, score_tile, 0)

    def key_to_f32(key):
        return pltpu.bitcast(jnp.where(key < 0, key ^ 0x7FFFFFFF, key), F32)

    def count(pred):
        def body(kt, acc):
            off = pl.multiple_of(kt * SK, SK)
            sc = sc_ref[pl.ds(off, SK), :]
            return acc + _fold_rows(jnp.where(pred(sc, off), 1, 0).astype(I32), jnp.add)
        acc = lax.fori_loop(0, ntile, body, jnp.zeros((SUBLANES, QB), I32))
        return jnp.sum(acc, axis=0, keepdims=True)

    def bit_step(it, carry):
        lo, cnt_lo = carry
        cand = lo + jnp.left_shift(jnp.int32(1), 31 - it)
        cf = key_to_f32(cand)
        cnt = count(lambda sc, off: sc >= cf)
        take = cnt >= topk
        return jnp.where(take, cand, lo), jnp.where(take, cnt, cnt_lo)

    def all_settled(cnt_lo):
        return jnp.min(jnp.where((cnt_lo == topk) | (limit < topk), 1, 0))

    def slow_search():
        lo, cnt_lo = lax.fori_loop(0, SEARCH_MIN_BITS, bit_step,
                                   (jnp.full((1, QB), INT_MIN, I32), jnp.full((1, QB), 2 ** 30, I32)))

        def more_bits(c):
            it, lo, cnt_lo, _ = c
            lo, cnt_lo = lax.fori_loop(it, it + SEARCH_STEP_BITS, bit_step, (lo, cnt_lo))
            return it + SEARCH_STEP_BITS, lo, cnt_lo, all_settled(cnt_lo)

        _, lo, _, settled = lax.while_loop(lambda c: (c[0] < 32) & (c[3] == 0), more_bits,
                                           (jnp.int32(SEARCH_MIN_BITS), lo, cnt_lo, all_settled(cnt_lo)))
        return lo, settled

    ngrp_max = sc_ref.shape[0] // PLANE_ROWS
    ngrp = ntile * (SK // PLANE_ROWS)

    @pl.when(i == 0)
    def _():
        pl_ref[...] = jnp.zeros_like(pl_ref)

    cand_ref[...] = jnp.zeros_like(cand_ref)

    def to_planes(g, c):
        base = pl.multiple_of(g * PLANE_ROWS, PLANE_ROWS)
        words = []
        for j in range(32):
            bits = pltpu.bitcast(sc_ref[pl.ds(base + SUBLANES * j, SUBLANES), :], I32)
            words.append(jnp.where(bits < 0, ~bits, bits | INT_MIN))
        j, m = 16, 0x0000FFFF
        while j:
            k = 0
            while k < 32:
                t = (words[k] ^ lax.shift_right_logical(words[k + j], j)) & m
                words[k] = words[k] ^ t
                words[k + j] = words[k + j] ^ lax.shift_left(t, j)
                k = (k + j + 1) & ~j
            j >>= 1
            m = (m ^ (m << j)) & 0xFFFFFFFF
        for b in range(32):
            pl_ref[pl.ds(base + SUBLANES * b, SUBLANES), :] = words[b]
        cand_ref[pl.ds(pl.multiple_of(g * SUBLANES, SUBLANES), SUBLANES), :] = jnp.full((SUBLANES, QB), -1, I32)
        return c

    lax.fori_loop(0, ngrp, to_planes, 0)

    def radix_step(it, c):
        above, tb = c
        prow = pl.multiple_of(it * SUBLANES, SUBLANES)
        acc = jnp.zeros((SUBLANES, QB), I32)
        for g in range(ngrp_max):
            plane = pl_ref[pl.ds(g * PLANE_ROWS + prow, SUBLANES), :]
            acc = acc + lax.population_count(cand_ref[g * SUBLANES:(g + 1) * SUBLANES, :] & plane)
        ones = jnp.sum(acc, axis=0, keepdims=True)
        take = above + ones >= topk
        for g in range(ngrp_max):
            plane = pl_ref[pl.ds(g * PLANE_ROWS + prow, SUBLANES), :]
            cand = cand_ref[g * SUBLANES:(g + 1) * SUBLANES, :]
            cand_ref[g * SUBLANES:(g + 1) * SUBLANES, :] = jnp.where(take, cand & plane, cand & ~plane)
        return (jnp.where(take, above, above + ones),
                jnp.where(take, tb | lax.shift_left(jnp.int32(1), 31 - it), tb))

    _, tbits = lax.fori_loop(0, 32, radix_step, (jnp.zeros((1, QB), I32), jnp.zeros((1, QB), I32)))
    lo_fast = tbits ^ INT_MIN
    thr_fast = key_to_f32(jnp.maximum(lo_fast, KEY_NEG_INF))
    fast_ok = all_settled(count(lambda sc, off: sc >= thr_fast))
    lo, settled = lax.cond(fast_ok == 1, lambda: (lo_fast, fast_ok), slow_search)
    has_thr = lo > KEY_NEG_INF
    thr = key_to_f32(jnp.maximum(lo, KEY_NEG_INF))
    nbits = max(1, (sc_ref.shape[0] - 1).bit_length())
    take_all_ties = jnp.full((1, QB), sc_ref.shape[0], I32)

    def tie_cut():
        cnt_gt = count(lambda sc, off: sc > thr)
        cnt_eq = count(lambda sc, off: sc == thr)
        tied = (cnt_gt + cnt_eq > topk) & has_thr

        def search_cut():
            def idx_step(it, m):
                cand = m + jnp.left_shift(jnp.int32(1), nbits - 1 - it)
                cnt = cnt_gt + count(lambda sc, off: (sc == thr) & ((off + rowi) < cand))
                return jnp.where(cnt < topk, cand, m)
            return lax.fori_loop(0, nbits, idx_step, jnp.zeros((1, QB), I32))

        return lax.cond(jnp.max(jnp.where(tied, 1, 0)) > 0, search_cut, lambda: take_all_ties)

    cut = lax.cond(settled == 1, lambda: take_all_ties, tie_cut)
    cut = jnp.where(has_thr, cut, -1)

    def selection_mask(off, nrows):
        sc = sc_ref[pl.ds(off, nrows), :]
        ri = rowi if nrows == SK else lax.broadcasted_iota(I32, (nrows, QB), 0)
        sel = (sc > thr) | ((sc == thr) & ((off + ri) <= cut))
        return jnp.where(sel, 0.0, MASK_NEG).astype(F32)

    near_lo = q0 - BQ

    def attend(off, bias_off, first=False, nrows=SK):
        rows = pl.ds(off, nrows)
        s = _dot(ckv_ref[rows, :], qat_ref[...])
        mk = selection_mask(off, nrows)
        m_old = jnp.full((1, GW), MASK_NEG, F32) if first else m_ref[...]
        tmax = []
        for h in range(DS_HEADS):
            cs = slice(h * QB, (h + 1) * QB)
            t = s[:, cs] + mk
            if bias_off is not None:
                t = t + jnp.concatenate(
                    [tab_ref[pl.ds(pl.multiple_of(jnp.maximum(bias_off - j * BQ, 0), BQ), nrows),
                             h * BQ:(h + 1) * BQ] for j in range(QB // BQ)], axis=1)
            lgt_ref[0:nrows, cs] = t
            tmax.append(jnp.max(_fold_rows(t, jnp.maximum), axis=0, keepdims=True))
        m_new = jnp.maximum(m_old, jnp.concatenate(tmax, axis=1))
        m_ref[...] = m_new
        pr = jnp.exp2(lgt_ref[0:nrows, :] - m_new).astype(BF16)
        pv = _dot(ckvt_ref[:, rows], pr)
        acc_ref[...] = pv if first else acc_ref[...] * jnp.exp2(m_old - m_new) + pv

    def far_tile(kt, c):
        attend(pl.multiple_of(kt * SK, SK), None)
        return c

    def edge_tile(kt, c):
        off = pl.multiple_of(kt * SK, SK)
        attend(off, SK + off - near_lo)
        return c

    nfull = jnp.right_shift(q0 + QB, SK.bit_length() - 1)

    @pl.when(nfull >= 1)
    def _():
        attend(0, SK - near_lo, first=True)
        lax.fori_loop(1, nfull - 1, far_tile, 0)
        lax.fori_loop(jnp.maximum(nfull - 1, 1), nfull, edge_tile, 0)

        @pl.when(nfull < ntile)
        def _():
            off = pl.multiple_of(nfull * SK, SK)
            attend(off, SK + off - near_lo, nrows=QB)

    @pl.when(nfull == 0)
    def _():
        attend(0, SK - near_lo, first=True, nrows=QB)

    o_lat = acc_ref[0:R, :] * (1.0 / acc_ref[R:R + 1, :])
    for h in range(DS_HEADS):
        out_ref[h * DS_HEAD:(h + 1) * DS_HEAD, :] = _dot(
            wuvt_ref[h], o_lat[:, h * QB:(h + 1) * QB].astype(BF16))

    g = gds_ref[...]
    o_ref[...] = (out_ref[...].T * (g * _sigmoid(g))).astype(BF16)


def _dsa(z, ckv, kid, B, T, q_norm_g, w_uq, w_uk, w_uv, iw_q, tab):
    nq = T // QB
    topk = min(TOPK_MAX, T // 4)
    R = DS_KV_RANK
    wq = jnp.concatenate([w_uq, iw_q], axis=1).astype(BF16)
    wuk_h = jnp.transpose(w_uk, (1, 0, 2)).astype(BF16)
    wuv_t = jnp.transpose(w_uv, (1, 2, 0)).astype(BF16)
    ckv_t = jnp.concatenate([jnp.swapaxes(ckv.reshape(B, T, R), 1, 2),
                             jnp.ones((B, ONES_ROWS, T), BF16)], axis=1).reshape(B * (R + ONES_ROWS), T)
    const2 = lambda b, i: (0, 0)
    const3 = lambda b, i: (0, 0, 0)
    resident = dict(pipeline_mode=pl.Buffered(1))
    return pl.pallas_call(
        functools.partial(_dsa_kernel, topk=topk),
        grid=(B, nq),
        in_specs=[pl.BlockSpec((QB, DS_Q_RANK), lambda b, i: (b * nq + i, COL_Q // DS_Q_RANK)),
                  pl.BlockSpec((QB, LANES), lambda b, i: (b * nq + i, COL_KX // LANES)),
                  pl.BlockSpec((QB, DS_WIDTH), lambda b, i: (b * nq + i, COL_GDS // DS_WIDTH)),
                  pl.BlockSpec((T, IDX_DIM), lambda b, i: (b, 0), **resident),
                  pl.BlockSpec((T, R), lambda b, i: (b, 0), **resident),
                  pl.BlockSpec((R + ONES_ROWS, T), lambda b, i: (b, 0), **resident),
                  pl.BlockSpec((1, DS_Q_RANK), const2),
                  pl.BlockSpec((DS_Q_RANK, 2 * DS_WIDTH), const2, **resident),
                  pl.BlockSpec((DS_HEADS, R, DS_HEAD), const3, **resident),
                  pl.BlockSpec((DS_HEADS, DS_HEAD, R), const3, **resident),
                  pl.BlockSpec((BIAS_ROWS, DS_HEADS * BQ), const2, **resident)],
        out_specs=pl.BlockSpec((QB, DS_WIDTH), lambda b, i: (b * nq + i, 0)),
        out_shape=jax.ShapeDtypeStruct((B * T, DS_WIDTH), BF16),
        scratch_shapes=[pltpu.VMEM((T, QB), F32),
                        pltpu.VMEM((T, QB), I32),
                        pltpu.VMEM((T // PLANE_ROWS * SUBLANES, QB), I32),
                        pltpu.VMEM((SK, DS_HEADS * QB), F32),
                        pltpu.VMEM((R + ONES_ROWS, DS_HEADS * QB), F32),
                        pltpu.VMEM((1, DS_HEADS * QB), F32),
                        pltpu.VMEM((R, DS_HEADS * QB), BF16),
                        pltpu.VMEM((IDX_HEADS // 2, IDX_DIM, 2 * QB), BF16),
                        pltpu.VMEM((IDX_HEADS, QB), F32),
                        pltpu.VMEM((DS_WIDTH, QB), F32)],
        compiler_params=pltpu.CompilerParams(dimension_semantics=("parallel", "arbitrary"),
                                             vmem_limit_bytes=VMEM_LIMIT),
        name="dsa",
    )(z, z, z, kid, ckv, ckv_t, q_norm_g.reshape(1, -1).astype(F32), wq, wuk_h, wuv_t, tab)


def _post_kernel(x_ref, a1_ref, a2_ref, p_ref, w1_ref, w2_ref, pw_ref, gw_ref, fg_ref, o_ref):
    h = x_ref[...] + _dot(a1_ref[...], w1_ref[...]) + _dot(a2_ref[...], w2_ref[...])
    e = _dot(p_ref[...].astype(BF16), pw_ref[...])
    gate = _sigmoid(_dot(h.astype(BF16), gw_ref[...]))
    h2 = h + e * gate
    ms = jnp.mean(h2 * h2, axis=-1, keepdims=True)
    o_ref[...] = h2 * lax.rsqrt(ms + NORM_EPS) * fg_ref[...]


def _post(x2, o_rw, o_ds, p2, w_out, ple_w, gate_w, final_g, tm=512):
    m, d = x2.shape
    kh = o_rw.shape[1]
    pd = p2.shape[1]
    resident = dict(pipeline_mode=pl.Buffered(1))
    return pl.pallas_call(
        _post_kernel,
        grid=(m // tm,),
        in_specs=[pl.BlockSpec((tm, d), lambda i: (i, 0)),
                  pl.BlockSpec((tm, kh), lambda i: (i, 0)),
                  pl.BlockSpec((tm, kh), lambda i: (i, 0)),
                  pl.BlockSpec((tm, pd), lambda i: (i, 0)),
                  pl.BlockSpec((kh, d), lambda i: (0, 0), **resident),
                  pl.BlockSpec((kh, d), lambda i: (1, 0), **resident),
                  pl.BlockSpec((pd, d), lambda i: (0, 0), **resident),
                  pl.BlockSpec((d, d), lambda i: (0, 0), **resident),
                  pl.BlockSpec((1, d), lambda i: (0, 0))],
        out_specs=pl.BlockSpec((tm, d), lambda i: (i, 0)),
        out_shape=jax.ShapeDtypeStruct((m, d), F32),
        compiler_params=pltpu.CompilerParams(dimension_semantics=("parallel",),
                                             vmem_limit_bytes=VMEM_LIMIT),
        name="post",
    )(x2, o_rw, o_ds, p2, w_out, w_out, ple_w, gate_w, final_g)


def _split_w_in(w):
    w = w.astype(BF16)
    return w, w[:, w.shape[1] - DS_WIDTH:]


def kernel(x, p, w_in, norm_g, rw_mu, rw_w0, rw_w_up, rw_a0, rw_a_up, rw_k_k, rw_k_a, rw_r_k, rw_ln_g, rw_ln_b, ds_q_norm_g, ds_kv_norm_g, idx_k_norm_g, ds_w_uq, ds_w_uk, ds_w_uv, idx_w_q, rel_bias, w_out, ple_w, ple_gate_w, final_g):
    B, T, D = x.shape
    depth = w_in.shape[0]
    assert depth == 1 and T % SK == 0 and SK == 2 * QB and T % RW_CHUNK == 0 and (B * T) % 512 == 0
    h = x.reshape(B * T, D)
    tab = _biastab(rel_bias)
    for i in range(depth):
        z, ckv, kid = _inproj(h, norm_g[i].reshape(1, D), *_split_w_in(w_in[i]),
                              ds_kv_norm_g[i].reshape(1, -1), idx_k_norm_g[i].reshape(1, -1))
        o_rw = _rwkv(z, B, T, rw_mu[i], rw_w0[i], rw_a0[i], rw_k_k[i], rw_k_a[i],
                     rw_r_k[i].reshape(-1), rw_ln_g[i], rw_ln_b[i], rw_w_up[i], rw_a_up[i])
        o_ds = _dsa(z, ckv, kid, B, T, ds_q_norm_g[i], ds_w_uq[i], ds_w_uk[i], ds_w_uv[i],
                    idx_w_q[i], tab)
        h = _post(h, o_rw, o_ds, p[i].reshape(B * T, -1), w_out[i].astype(BF16),
                  ple_w[i].astype(BF16), ple_gate_w[i].astype(BF16), final_g.reshape(1, D))
    return h.reshape(B, T, D)
```

```python
import functools
import math

import jax
import jax.numpy as jnp
from jax import lax
from jax.experimental import pallas as pl
from jax.experimental.pallas import tpu as pltpu

F32 = jnp.float32
BF16 = jnp.bfloat16
I32 = jnp.int32

RW_WIDTH = 1024
RW_HEAD = 64
RW_HEADS = 16
RW_LORA = 64
DS_WIDTH = 1024
DS_HEAD = 64
DS_HEADS = 16
DS_Q_RANK = 384
DS_KV_RANK = 256
IDX_HEADS = 16
IDX_DIM = 64
TOPK_MAX = 256
CHUNK = 64
NUM_BUCKETS = 32
MAX_DISTANCE = 128
NORM_EPS = 1e-6
GN_EPS = 64e-5

COL_R, COL_K, COL_V, COL_GRW, COL_GDS = 0, 1024, 2048, 3072, 4096
COL_KV = 5120
COL_Q = 5376
COL_WA = 5760
COL_KX = 5888
Z_WIDTH = 6016

LANES = 128
SUBLANES = 8
QB = 256
BQ = 128
SK = 512
BIAS_ROWS = 2 * SK + 2 * BQ
RW_CHUNK = 64
INT_MIN = -2 ** 31
KEY_NEG_INF = -2139095041
PLANE_ROWS = 32 * SUBLANES
SEARCH_MIN_BITS = 23
SEARCH_STEP_BITS = 3
assert (32 - SEARCH_MIN_BITS) % SEARCH_STEP_BITS == 0
MASK_NEG = -1e30
LOG2E = 1.4426950408889634
ONES_ROWS = 16
VMEM_LIMIT = 52 * 1024 * 1024


def _sigmoid(x):
    return 1.0 / (1.0 + jnp.exp(-x))


def _dot(a, b):
    return jnp.dot(a, b, preferred_element_type=F32)


def _inproj_plan():
    s_wa = 3 * RW_WIDTH
    s_grw = s_wa + 2 * RW_LORA
    s_q = s_grw + RW_WIDTH
    s_kv = s_q + DS_Q_RANK
    s_kx = s_kv + DS_KV_RANK
    return [(COL_R, 0, 3 * RW_WIDTH), (COL_GRW, s_grw, RW_WIDTH), (COL_GDS, None, DS_WIDTH),
            (COL_KV, s_kv, DS_KV_RANK), (COL_Q, s_q, DS_Q_RANK), (COL_WA, s_wa, LANES),
            (COL_KX, s_kx, LANES)]


def _inproj_kernel(x_ref, g_ref, w_ref, wg_ref, gkv_ref, gik_ref, o_ref, ckv_ref, kid_ref, *, tn):
    x = x_ref[...]
    ms = jnp.mean(x * x, axis=-1, keepdims=True)
    xn = (x * lax.rsqrt(ms + NORM_EPS) * g_ref[...]).astype(BF16)
    for dst, src, width in _inproj_plan():
        for j in range(0, width, tn):
            w = min(tn, width - j)
            wt = wg_ref[:, j:j + w] if src is None else w_ref[:, src + j:src + j + w]
            o_ref[:, dst + j:dst + j + w] = _dot(xn, wt)
    kv = o_ref[:, COL_KV:COL_KV + DS_KV_RANK]
    ckv_ref[...] = (kv * lax.rsqrt(jnp.mean(kv * kv, axis=-1, keepdims=True) + NORM_EPS)
                    * gkv_ref[...]).astype(BF16)
    ki = o_ref[:, COL_KX:COL_KX + IDX_DIM]
    kid_ref[...] = (ki * lax.rsqrt(jnp.mean(ki * ki, axis=-1, keepdims=True) + NORM_EPS)
                    * gik_ref[...]).astype(BF16)


def _inproj(x2, g, w, w_gds, gkv, gik, tm=256, tn=768):
    m, d = x2.shape
    resident = dict(pipeline_mode=pl.Buffered(1))
    return pl.pallas_call(
        functools.partial(_inproj_kernel, tn=tn),
        grid=(m // tm,),
        in_specs=[pl.BlockSpec((tm, d), lambda i: (i, 0)),
                  pl.BlockSpec((1, d), lambda i: (0, 0)),
                  pl.BlockSpec(w.shape, lambda i: (0, 0), **resident),
                  pl.BlockSpec(w_gds.shape, lambda i: (0, 0), **resident),
                  pl.BlockSpec((1, DS_KV_RANK), lambda i: (0, 0)),
                  pl.BlockSpec((1, IDX_DIM), lambda i: (0, 0))],
        out_specs=[pl.BlockSpec((tm, Z_WIDTH), lambda i: (i, 0)),
                   pl.BlockSpec((tm, DS_KV_RANK), lambda i: (i, 0)),
                   pl.BlockSpec((tm, IDX_DIM), lambda i: (i, 0))],
        out_shape=[jax.ShapeDtypeStruct((m, Z_WIDTH), F32),
                   jax.ShapeDtypeStruct((m, DS_KV_RANK), BF16),
                   jax.ShapeDtypeStruct((m, IDX_DIM), BF16)],
        compiler_params=pltpu.CompilerParams(dimension_semantics=("parallel",),
                                             vmem_limit_bytes=VMEM_LIMIT),
        name="inproj",
    )(x2, g, w, w_gds, gkv, gik)


def _rwkv_kernel(r_ref, k_ref, v_ref, g_ref, wa_ref,
                 mur_ref, muk_ref, muv_ref, muwa_ref,
                 w0_ref, a0_ref, kk_ref, ka_ref, rk_ref, lng_ref, lnb_ref,
                 wup_ref, aup_ref,
                 o_ref,
                 pr_ref, pk_ref, pv_ref, pwa_ref, st_ref, *, nb):
    C = RW_CHUNK
    N = RW_HEAD

    @pl.when(pl.program_id(0) == 0)
    def _():
        pr_ref[...] = jnp.zeros_like(pr_ref)
        pk_ref[...] = jnp.zeros_like(pk_ref)
        pv_ref[...] = jnp.zeros_like(pv_ref)
        pwa_ref[...] = jnp.zeros_like(pwa_ref)
        st_ref[...] = jnp.zeros_like(st_ref)

    row = lax.broadcasted_iota(I32, (SUBLANES, 1), 0)

    def shift(ref, prev_ref, mu_ref):
        z = ref[...].reshape(nb * C, ref.shape[2])
        zp = pltpu.roll(z, 1, 0)
        parts = []
        for b in range(nb):
            parts.append(jnp.where(row == 0, prev_ref[b], zp[b * C:b * C + SUBLANES]))
            parts.append(zp[b * C + SUBLANES:(b + 1) * C])
            prev_ref[b] = z[(b + 1) * C - 1:(b + 1) * C, :]
        zp = jnp.concatenate(parts, axis=0)
        return z + mu_ref[...] * (zp - z)

    r = shift(r_ref, pr_ref, mur_ref)
    k = shift(k_ref, pk_ref, muk_ref)
    v = shift(v_ref, pv_ref, muv_ref)
    wa = shift(wa_ref, pwa_ref, muwa_ref)
    wd = wa[:, 0:RW_LORA]
    ad = wa[:, RW_LORA:2 * RW_LORA]

    wl = w0_ref[...] + _dot(jnp.tanh(wd).astype(BF16), wup_ref[...])
    lw = -math.exp(-0.5) * _sigmoid(wl)
    a = _sigmoid(a0_ref[...] + _dot(ad.astype(BF16), aup_ref[...]))
    kk = k * kk_ref[...]
    k2 = k * (1.0 + (a - 1.0) * ka_ref[...])

    ti = lax.broadcasted_iota(I32, (C, C), 0)
    tj = lax.broadcasted_iota(I32, (C, C), 1)
    tri = jnp.where(ti >= tj, 1.0, 0.0).astype(F32)

    def per_row(mat, x):
        hi = x.astype(BF16)
        r1 = x - hi.astype(F32)
        mid = r1.astype(BF16)
        lo = (r1 - mid.astype(F32)).astype(BF16)
        mb = mat.astype(BF16)
        return jnp.concatenate(
            [_dot(mb, hi[b * C:(b + 1) * C]) + _dot(mb, mid[b * C:(b + 1) * C]) + _dot(mb, lo[b * C:(b + 1) * C])
             for b in range(nb)], axis=0)

    cum = per_row(tri, lw)
    p = jnp.exp(cum)
    pinv = jnp.exp(-cum)
    pprev = jnp.exp(cum - lw)
    tot = per_row(jnp.ones((C, C), F32), lw)
    pend = jnp.exp(tot)

    g = g_ref[...].reshape(nb * C, RW_WIDTH)
    gate = g * _sigmoid(g)

    HP = RW_HEADS // 2
    NP = nb * HP

    def pairs(x):
        return jnp.stack([x[b * C:(b + 1) * C, j * LANES:(j + 1) * LANES]
                          for b in range(nb) for j in range(HP)], axis=0)

    def per_pair(ref):
        return jnp.concatenate([ref[...]] * nb, axis=0)

    lane = lax.broadcasted_iota(I32, (1, 1, LANES), 2)
    m_lo = jnp.where(lane < N, 1.0, 0.0).astype(BF16)
    m_hi = jnp.where(lane < N, 0.0, 1.0).astype(BF16)
    bi = lax.broadcasted_iota(I32, (LANES, LANES), 0)
    bj = lax.broadcasted_iota(I32, (LANES, LANES), 1)
    same_head = (bi < N) == (bj < N)
    ones_bd = jnp.where(same_head, 1.0, 0.0).astype(BF16)

    def head_sum(x):
        return _dot(x.reshape(NP * C, LANES).astype(BF16), ones_bd).reshape(NP, C, LANES)

    def halves(x):
        xb = x.astype(BF16)
        return jnp.concatenate([xb * m_lo, xb * m_hi], axis=1)

    def bmm(x, y):
        return lax.dot_general(x, y, (((2,), (1,)), ((0,), (0,))), preferred_element_type=F32)

    def bmm_nt(x, y):
        return lax.dot_general(x, y, (((2,), (2,)), ((0,), (0,))), preferred_element_type=F32)

    def block_mask(nblk, cmp):
        wi = lax.broadcasted_iota(I32, (C, nblk * C), 0)
        wj = lax.broadcasted_iota(I32, (C, nblk * C), 1) & (C - 1)
        return cmp(wi, wj)

    r_p, k2_p, v_p, a_p = pairs(r), pairs(k2), pairs(v), pairs(a)
    p_p, pinv_p, pprev_p = pairs(p), pairs(pinv), pairs(pprev)
    kk_p = pairs(kk)
    kkn = kk_p * lax.rsqrt(jnp.maximum(head_sum(kk_p * kk_p), 1e-24))
    at = (-kkn) * pprev_p
    bt = (kkn * a_p) * pinv_p
    kt = k2_p * pinv_p
    rt = r_p * p_p
    pend_p = pairs(pend)
    pend2 = jnp.concatenate([pend_p, pend_p], axis=1)

    lhs2 = jnp.concatenate([at, rt], axis=1).astype(BF16)
    rhs4 = jnp.concatenate([halves(kt), halves(bt)], axis=1)
    gc = bmm_nt(lhs2, rhs4)
    strict2 = block_mask(2, lambda i_, j_: i_ > j_)
    incl4 = block_mask(4, lambda i_, j_: i_ >= j_)
    a_ak = jnp.where(strict2, gc[:, 0:C, 0:2 * C], 0.0)
    nmat = jnp.where(strict2, gc[:, 0:C, 2 * C:4 * C], 0.0)
    a_rkb = jnp.where(incl4, gc[:, C:2 * C, :], 0.0)

    g0 = st_ref[...]
    sg = bmm_nt(lhs2, g0.astype(BF16))
    vm2 = halves(v_p)
    u = sg[:, 0:C] + bmm(a_ak.astype(BF16), vm2)
    pw = nmat.astype(BF16)
    u = u + bmm(pw, halves(u))
    n = 1
    while 2 * n < C:
        pw = bmm(pw, halves(pw)).astype(BF16)
        u = u + bmm(pw, halves(u))
        n *= 2
    um2b = halves(u)
    y = sg[:, C:2 * C] + bmm(a_rkb.astype(BF16), jnp.concatenate([vm2, um2b], axis=1))
    uv = jnp.concatenate([u, v_p], axis=1).astype(BF16)
    bkh = (jnp.concatenate([bt, kt], axis=1) * pend2).astype(BF16)
    upd = lax.dot_general(uv, bkh, (((1,), (1,)), ((0,), (0,))), preferred_element_type=F32)
    st_ref[...] = g0 * pend2 + jnp.where(same_head, upd, 0.0)

    inv_n = 1.0 / N
    yc = y - head_sum(y) * inv_n
    var = head_sum(yc * yc) * inv_n
    yn = yc * lax.rsqrt(var + GN_EPS) * per_pair(lng_ref) + per_pair(lnb_ref)
    bonus = head_sum(r_p * k2_p * per_pair(rk_ref)) * v_p
    out = (yn + bonus) * pairs(gate)
    for b in range(nb):
        for j in range(HP):
            o_ref[b, :, j * LANES:(j + 1) * LANES] = out[b * HP + j].astype(BF16)


def _rwkv(z, B, T, mu, w0, a0, k_k, k_a, r_k, ln_g, ln_b, w_up, a_up):
    C = RW_CHUNK
    W = RW_WIDTH
    z3 = z.reshape(B, T, z.shape[1])
    row = lambda a: a.reshape(1, -1).astype(F32)
    mu_r, mu_k, mu_v = mu[0:W], mu[W:2 * W], mu[2 * W:3 * W]
    mu_wa = mu[3 * W:3 * W + 2 * RW_LORA]
    zspec = lambda col: pl.BlockSpec((B, C, W), lambda c: (0, c, col // W))
    pspec = lambda width: pl.BlockSpec((1, width), lambda c: (0, 0))
    npairs = RW_HEADS // 2
    prow = lambda a: a.reshape(npairs, 1, LANES).astype(F32)
    ppspec = pl.BlockSpec((npairs, 1, LANES), lambda c: (0, 0, 0))
    wspec = pl.BlockSpec((RW_LORA, W), lambda c: (0, 0))
    out = pl.pallas_call(
        functools.partial(_rwkv_kernel, nb=B),
        grid=(T // C,),
        in_specs=[zspec(COL_R), zspec(COL_K), zspec(COL_V), zspec(COL_GRW),
                  pl.BlockSpec((B, C, LANES), lambda c: (0, c, COL_WA // LANES)),
                  pspec(W), pspec(W), pspec(W), pspec(LANES),
                  pspec(W), pspec(W), pspec(W), pspec(W), ppspec, ppspec, ppspec,
                  wspec, wspec],
        out_specs=pl.BlockSpec((B, C, W), lambda c: (0, c, 0)),
        out_shape=jax.ShapeDtypeStruct((B, T, W), BF16),
        scratch_shapes=[pltpu.VMEM((B, 1, W), F32), pltpu.VMEM((B, 1, W), F32), pltpu.VMEM((B, 1, W), F32),
                        pltpu.VMEM((B, 1, LANES), F32),
                        pltpu.VMEM((B * npairs, LANES, LANES), F32)],
        compiler_params=pltpu.CompilerParams(dimension_semantics=("arbitrary",),
                                             vmem_limit_bytes=VMEM_LIMIT),
        name="rwkv",
    )(z3, z3, z3, z3, z3,
      row(mu_r), row(mu_k), row(mu_v), row(mu_wa),
      row(w0), row(a0), row(k_k), row(k_a), prow(r_k), prow(ln_g), prow(ln_b),
      w_up.astype(BF16), a_up.astype(BF16))
    return out.reshape(B * T, W)


def _biastab_kernel(rb_ref, o_ref):
    o_ref[...] = jnp.zeros_like(o_ref)
    nb = NUM_BUCKETS // 2
    max_exact = nb // 2
    c = lax.broadcasted_iota(I32, (2 * BQ, BQ), 0)
    r = lax.broadcasted_iota(I32, (2 * BQ, BQ), 1)
    rel = c - BQ - r
    ret = jnp.where(rel > 0, nb, 0)
    n = jnp.abs(rel)
    nf = jnp.maximum(n, 1).astype(F32)
    large = max_exact + (jnp.log(nf / max_exact) / math.log(MAX_DISTANCE / max_exact)
                         * (nb - max_exact)).astype(I32)
    large = jnp.minimum(large, nb - 1) & (NUM_BUCKETS - 1)
    bucket = ret + jnp.where(n < max_exact, n, large)
    for h in range(DS_HEADS):
        far = rb_ref[nb - 1, h]
        acc = jnp.zeros((2 * BQ, BQ), F32)
        for b in range(NUM_BUCKETS):
            acc = jnp.where(bucket == b, rb_ref[b, h] - far, acc)
        o_ref[SK:SK + 2 * BQ, h * BQ:(h + 1) * BQ] = acc * LOG2E


def _biastab(rel_bias):
    return pl.pallas_call(
        _biastab_kernel,
        in_specs=[pl.BlockSpec(memory_space=pltpu.SMEM)],
        out_specs=pl.BlockSpec(memory_space=pltpu.VMEM),
        out_shape=jax.ShapeDtypeStruct((BIAS_ROWS, DS_HEADS * BQ), F32),
        compiler_params=pltpu.CompilerParams(vmem_limit_bytes=VMEM_LIMIT),
        name="biastab",
    )(rel_bias.astype(F32))


def _fold_rows(x, op):
    n = x.shape[0] // SUBLANES
    accs = [x[j * SUBLANES:(j + 1) * SUBLANES] for j in range(min(4, n))]
    for j in range(4, n):
        accs[j % 4] = op(accs[j % 4], x[j * SUBLANES:(j + 1) * SUBLANES])
    while len(accs) > 1:
        accs = [op(accs[j], accs[j + 1]) for j in range(0, len(accs) - 1, 2)] + (
            [accs[-1]] if len(accs) % 2 else [])
    return accs[0]


def _dsa_kernel(ql_ref, kx_ref, gds_ref, kid_ref, ckv_ref, ckvt_ref, qg_ref, wq_ref, wuk_ref, wuvt_ref,
                tab_ref, o_ref,
                sc_ref, pl_ref, cand_ref, lgt_ref, acc_ref, m_ref, qat_ref, qit_ref, w_ref, out_ref, *, topk):
    i = pl.program_id(1)
    q0 = i * QB
    ntile = jnp.right_shift(q0 + (QB + SK - 1), SK.bit_length() - 1)
    R = DS_KV_RANK
    GW = DS_HEADS * QB

    ql = ql_ref[...]
    ms = jnp.mean(ql * ql, axis=-1, keepdims=True)
    qn = (ql * lax.rsqrt(ms + NORM_EPS) * qg_ref[...]).astype(BF16)
    qt = _dot(qn, wq_ref[...]).T
    for h in range(DS_HEADS):
        qh = qt[h * DS_HEAD:(h + 1) * DS_HEAD, :].astype(BF16)
        qat_ref[:, h * QB:(h + 1) * QB] = (_dot(wuk_ref[h], qh) * (DS_HEAD ** -0.5 * LOG2E)).astype(BF16)
    for pr in range(IDX_HEADS // 2):
        base = DS_WIDTH + 2 * pr * IDX_DIM
        qit_ref[pr] = jnp.concatenate([qt[base:base + IDX_DIM, :],
                                       qt[base + IDX_DIM:base + 2 * IDX_DIM, :]], axis=1).astype(BF16)
    w_ref[...] = kx_ref[...].T[IDX_DIM:IDX_DIM + IDX_HEADS, :] * (IDX_HEADS ** -0.5 * IDX_DIM ** -0.5)

    lanei = lax.broadcasted_iota(I32, (1, QB), 1)
    csh = CHUNK.bit_length() - 1
    limit = jnp.left_shift(jnp.right_shift(q0 + lanei, csh) + 1, csh)
    rowi = lax.broadcasted_iota(I32, (SK, QB), 0)
    rowh = lax.broadcasted_iota(I32, (QB, QB), 0)
    nfull = jnp.right_shift(q0 + QB, SK.bit_length() - 1)
    nhalf = jnp.right_shift(q0 + QB, QB.bit_length() - 1)

    def score_rows(off, nrows):
        kid = kid_ref[pl.ds(off, nrows), :]
        s = jnp.zeros((nrows, QB), F32)
        for pr in range(IDX_HEADS // 2):
            lg = _dot(kid, qit_ref[pr])
            s = s + w_ref[2 * pr:2 * pr + 1, :] * jnp.maximum(lg[:, 0:QB], 0.0)
            s = s + w_ref[2 * pr + 1:2 * pr + 2, :] * jnp.maximum(lg[:, QB:2 * QB], 0.0)
        adm = (off + (rowi if nrows == SK else rowh)) < limit
        sc_ref[pl.ds(off, nrows), :] = jnp.where(adm, s, -jnp.inf)

    def score_tile(kt, carry):
        score_rows(pl.multiple_of(kt * SK, SK), SK)
        return carry

    lax.fori_loop(0, nfull, score_tile, 0)

    @pl.when(nfull < ntile)
    def _():
        score_rows(pl.multiple_of(nfull * SK, SK), QB)

    def key_to_f32(key):
        return pltpu.bitcast(jnp.where(key < 0, key ^ 0x7FFFFFFF, key), F32)

    def count(pred):
        def body(kt, acc):
            off = pl.multiple_of(kt * QB, QB)
            sc = sc_ref[pl.ds(off, QB), :]
            return acc + _fold_rows(jnp.where(pred(sc, off), 1, 0).astype(I32), jnp.add)
        acc = lax.fori_loop(0, nhalf, body, jnp.zeros((SUBLANES, QB), I32))
        return jnp.sum(acc, axis=0, keepdims=True)

    def bit_step(it, carry):
        lo, cnt_lo = carry
        cand = lo + jnp.left_shift(jnp.int32(1), 31 - it)
        cf = key_to_f32(cand)
        cnt = count(lambda sc, off: sc >= cf)
        take = cnt >= topk
        return jnp.where(take, cand, lo), jnp.where(take, cnt, cnt_lo)

    def all_settled(cnt_lo):
        return jnp.min(jnp.where((cnt_lo == topk) | (limit < topk), 1, 0))

    def slow_search():
        lo, cnt_lo = lax.fori_loop(0, SEARCH_MIN_BITS, bit_step,
                                   (jnp.full((1, QB), INT_MIN, I32), jnp.full((1, QB), 2 ** 30, I32)))

        def more_bits(c):
            it, lo, cnt_lo, _ = c
            lo, cnt_lo = lax.fori_loop(it, it + SEARCH_STEP_BITS, bit_step, (lo, cnt_lo))
            return it + SEARCH_STEP_BITS, lo, cnt_lo, all_settled(cnt_lo)

        _, lo, _, settled = lax.while_loop(lambda c: (c[0] < 32) & (c[3] == 0), more_bits,
                                           (jnp.int32(SEARCH_MIN_BITS), lo, cnt_lo, all_settled(cnt_lo)))
        return lo, settled

    ngrp_max = sc_ref.shape[0] // PLANE_ROWS
    ngrp = jnp.right_shift(q0 + QB, PLANE_ROWS.bit_length() - 1)

    @pl.when(i == 0)
    def _():
        pl_ref[...] = jnp.zeros_like(pl_ref)

    cand_ref[...] = jnp.zeros_like(cand_ref)

    def to_planes(g, c):
        base = pl.multiple_of(g * PLANE_ROWS, PLANE_ROWS)
        words = []
        for j in range(32):
            bits = pltpu.bitcast(sc_ref[pl.ds(base + SUBLANES * j, SUBLANES), :], I32)
            words.append(jnp.where(bits < 0, ~bits, bits | INT_MIN))
        j, m = 16, 0x0000FFFF
        while j:
            k = 0
            while k < 32:
                t = (words[k] ^ lax.shift_right_logical(words[k + j], j)) & m
                words[k] = words[k] ^ t
                words[k + j] = words[k + j] ^ lax.shift_left(t, j)
                k = (k + j + 1) & ~j
            j >>= 1
            m = (m ^ (m << j)) & 0xFFFFFFFF
        for b in range(32):
            pl_ref[pl.ds(base + SUBLANES * b, SUBLANES), :] = words[b]
        cand_ref[pl.ds(pl.multiple_of(g * SUBLANES, SUBLANES), SUBLANES), :] = jnp.full((SUBLANES, QB), -1, I32)
        return c

    lax.fori_loop(0, ngrp, to_planes, 0)

    def radix_step(it, c):
        above, tb = c
        prow = pl.multiple_of(it * SUBLANES, SUBLANES)
        acc = jnp.zeros((SUBLANES, QB), I32)
        for g in range(ngrp_max):
            plane = pl_ref[pl.ds(g * PLANE_ROWS + prow, SUBLANES), :]
            acc = acc + lax.population_count(cand_ref[g * SUBLANES:(g + 1) * SUBLANES, :] & plane)
        ones = jnp.sum(acc, axis=0, keepdims=True)
        take = above + ones >= topk
        for g in range(ngrp_max):
            plane = pl_ref[pl.ds(g * PLANE_ROWS + prow, SUBLANES), :]
            cand = cand_ref[g * SUBLANES:(g + 1) * SUBLANES, :]
            cand_ref[g * SUBLANES:(g + 1) * SUBLANES, :] = jnp.where(take, cand & plane, cand & ~plane)
        return (jnp.where(take, above, above + ones),
                jnp.where(take, tb | lax.shift_left(jnp.int32(1), 31 - it), tb))

    _, tbits = lax.fori_loop(0, 32, radix_step, (jnp.zeros((1, QB), I32), jnp.zeros((1, QB), I32)))
    lo_fast = tbits ^ INT_MIN
    thr_fast = key_to_f32(jnp.maximum(lo_fast, KEY_NEG_INF))
    fast_ok = all_settled(count(lambda sc, off: sc >= thr_fast))
    lo, settled = lax.cond(fast_ok == 1, lambda: (lo_fast, fast_ok), slow_search)
    has_thr = lo > KEY_NEG_INF
    thr = key_to_f32(jnp.maximum(lo, KEY_NEG_INF))
    nbits = max(1, (sc_ref.shape[0] - 1).bit_length())
    take_all_ties = jnp.full((1, QB), sc_ref.shape[0], I32)

    def tie_cut():
        cnt_gt = count(lambda sc, off: sc > thr)
        cnt_eq = count(lambda sc, off: sc == thr)
        tied = (cnt_gt + cnt_eq > topk) & has_thr

        def search_cut():
            def idx_step(it, m):
                cand = m + jnp.left_shift(jnp.int32(1), nbits - 1 - it)
                cnt = cnt_gt + count(lambda sc, off: (sc == thr) & ((off + rowh) < cand))
                return jnp.where(cnt < topk, cand, m)
            return lax.fori_loop(0, nbits, idx_step, jnp.zeros((1, QB), I32))

        return lax.cond(jnp.max(jnp.where(tied, 1, 0)) > 0, search_cut, lambda: take_all_ties)

    cut = lax.cond(settled == 1, lambda: take_all_ties, tie_cut)
    cut = jnp.where(has_thr, cut, -1)

    def selection_mask(off, nrows):
        sc = sc_ref[pl.ds(off, nrows), :]
        ri = rowi if nrows == SK else rowh
        sel = (sc > thr) | ((sc == thr) & ((off + ri) <= cut))
        return jnp.where(sel, 0.0, MASK_NEG).astype(F32)

    near_lo = q0 - BQ

    def attend(off, bias_off, first=False, nrows=SK):
        rows = pl.ds(off, nrows)
        s = _dot(ckv_ref[rows, :], qat_ref[...])
        mk = selection_mask(off, nrows)
        m_old = jnp.full((1, GW), MASK_NEG, F32) if first else m_ref[...]
        tmax = []
        for h in range(DS_HEADS):
            cs = slice(h * QB, (h + 1) * QB)
            t = s[:, cs] + mk
            if bias_off is not None:
                t = t + jnp.concatenate(
                    [tab_ref[pl.ds(pl.multiple_of(jnp.maximum(bias_off - j * BQ, 0), BQ), nrows),
                             h * BQ:(h + 1) * BQ] for j in range(QB // BQ)], axis=1)
            lgt_ref[0:nrows, cs] = t
            tmax.append(jnp.max(_fold_rows(t, jnp.maximum), axis=0, keepdims=True))
        m_new = jnp.maximum(m_old, jnp.concatenate(tmax, axis=1))
        m_ref[...] = m_new
        pr = jnp.exp2(lgt_ref[0:nrows, :] - m_new).astype(BF16)
        pv = _dot(ckvt_ref[:, rows], pr)
        acc_ref[...] = pv if first else acc_ref[...] * jnp.exp2(m_old - m_new) + pv

    def far_tile(kt, c):
        attend(pl.multiple_of(kt * SK, SK), None)
        return c

    def edge_tile(kt, c):
        off = pl.multiple_of(kt * SK, SK)
        attend(off, SK + off - near_lo)
        return c

    @pl.when(nfull >= 1)
    def _():
        attend(0, SK - near_lo, first=True)
        lax.fori_loop(1, nfull - 1, far_tile, 0)
        lax.fori_loop(jnp.maximum(nfull - 1, 1), nfull, edge_tile, 0)

        @pl.when(nfull < ntile)
        def _():
            off = pl.multiple_of(nfull * SK, SK)
            attend(off, SK + off - near_lo, nrows=QB)

    @pl.when(nfull == 0)
    def _():
        attend(0, SK - near_lo, first=True, nrows=QB)

    o_lat = acc_ref[0:R, :] * (1.0 / acc_ref[R:R + 1, :])
    for h in range(DS_HEADS):
        out_ref[h * DS_HEAD:(h + 1) * DS_HEAD, :] = _dot(
            wuvt_ref[h], o_lat[:, h * QB:(h + 1) * QB].astype(BF16))

    g = gds_ref[...]
    o_ref[...] = (out_ref[...].T * (g * _sigmoid(g))).astype(BF16)


def _dsa(z, ckv, kid, B, T, q_norm_g, w_uq, w_uk, w_uv, iw_q, tab):
    nq = T // QB
    topk = min(TOPK_MAX, T // 4)
    R = DS_KV_RANK
    wq = jnp.concatenate([w_uq, iw_q], axis=1).astype(BF16)
    wuk_h = jnp.transpose(w_uk, (1, 0, 2)).astype(BF16)
    wuv_t = jnp.transpose(w_uv, (1, 2, 0)).astype(BF16)
    ckv_t = jnp.concatenate([jnp.swapaxes(ckv.reshape(B, T, R), 1, 2),
                             jnp.ones((B, ONES_ROWS, T), BF16)], axis=1).reshape(B * (R + ONES_ROWS), T)
    const2 = lambda b, i: (0, 0)
    const3 = lambda b, i: (0, 0, 0)
    resident = dict(pipeline_mode=pl.Buffered(1))
    return pl.pallas_call(
        functools.partial(_dsa_kernel, topk=topk),
        grid=(B, nq),
        in_specs=[pl.BlockSpec((QB, DS_Q_RANK), lambda b, i: (b * nq + i, COL_Q // DS_Q_RANK)),
                  pl.BlockSpec((QB, LANES), lambda b, i: (b * nq + i, COL_KX // LANES)),
                  pl.BlockSpec((QB, DS_WIDTH), lambda b, i: (b * nq + i, COL_GDS // DS_WIDTH)),
                  pl.BlockSpec((T, IDX_DIM), lambda b, i: (b, 0), **resident),
                  pl.BlockSpec((T, R), lambda b, i: (b, 0), **resident),
                  pl.BlockSpec((R + ONES_ROWS, T), lambda b, i: (b, 0), **resident),
                  pl.BlockSpec((1, DS_Q_RANK), const2),
                  pl.BlockSpec((DS_Q_RANK, 2 * DS_WIDTH), const2, **resident),
                  pl.BlockSpec((DS_HEADS, R, DS_HEAD), const3, **resident),
                  pl.BlockSpec((DS_HEADS, DS_HEAD, R), const3, **resident),
                  pl.BlockSpec((BIAS_ROWS, DS_HEADS * BQ), const2, **resident)],
        out_specs=pl.BlockSpec((QB, DS_WIDTH), lambda b, i: (b * nq + i, 0)),
        out_shape=jax.ShapeDtypeStruct((B * T, DS_WIDTH), BF16),
        scratch_shapes=[pltpu.VMEM((T, QB), F32),
                        pltpu.VMEM((T, QB), I32),
                        pltpu.VMEM((T // PLANE_ROWS * SUBLANES, QB), I32),
                        pltpu.VMEM((SK, DS_HEADS * QB), F32),
                        pltpu.VMEM((R + ONES_ROWS, DS_HEADS * QB), F32),
                        pltpu.VMEM((1, DS_HEADS * QB), F32),
                        pltpu.VMEM((R, DS_HEADS * QB), BF16),
                        pltpu.VMEM((IDX_HEADS // 2, IDX_DIM, 2 * QB), BF16),
                        pltpu.VMEM((IDX_HEADS, QB), F32),
                        pltpu.VMEM((DS_WIDTH, QB), F32)],
        compiler_params=pltpu.CompilerParams(dimension_semantics=("parallel", "arbitrary"),
                                             vmem_limit_bytes=VMEM_LIMIT),
        name="dsa",
    )(z, z, z, kid, ckv, ckv_t, q_norm_g.reshape(1, -1).astype(F32), wq, wuk_h, wuv_t, tab)


def _post_kernel(x_ref, a1_ref, a2_ref, p_ref, w1_ref, w2_ref, pw_ref, gw_ref, fg_ref, o_ref):
    h = x_ref[...] + _dot(a1_ref[...], w1_ref[...]) + _dot(a2_ref[...], w2_ref[...])
    e = _dot(p_ref[...].astype(BF16), pw_ref[...])
    gate = _sigmoid(_dot(h.astype(BF16), gw_ref[...]))
    h2 = h + e * gate
    ms = jnp.mean(h2 * h2, axis=-1, keepdims=True)
    o_ref[...] = h2 * lax.rsqrt(ms + NORM_EPS) * fg_ref[...]


def _post(x2, o_rw, o_ds, p2, w_out, ple_w, gate_w, final_g, tm=512):
    m, d = x2.shape
    kh = o_rw.shape[1]
    pd = p2.shape[1]
    resident = dict(pipeline_mode=pl.Buffered(1))
    return pl.pallas_call(
        _post_kernel,
        grid=(m // tm,),
        in_specs=[pl.BlockSpec((tm, d), lambda i: (i, 0)),
                  pl.BlockSpec((tm, kh), lambda i: (i, 0)),
                  pl.BlockSpec((tm, kh), lambda i: (i, 0)),
                  pl.BlockSpec((tm, pd), lambda i: (i, 0)),
                  pl.BlockSpec((kh, d), lambda i: (0, 0), **resident),
                  pl.BlockSpec((kh, d), lambda i: (1, 0), **resident),
                  pl.BlockSpec((pd, d), lambda i: (0, 0), **resident),
                  pl.BlockSpec((d, d), lambda i: (0, 0), **resident),
                  pl.BlockSpec((1, d), lambda i: (0, 0))],
        out_specs=pl.BlockSpec((tm, d), lambda i: (i, 0)),
        out_shape=jax.ShapeDtypeStruct((m, d), F32),
        compiler_params=pltpu.CompilerParams(dimension_semantics=("parallel",),
                                             vmem_limit_bytes=VMEM_LIMIT),
        name="post",
    )(x2, o_rw, o_ds, p2, w_out, w_out, ple_w, gate_w, final_g)


def _split_w_in(w):
    w = w.astype(BF16)
    return w, w[:, w.shape[1] - DS_WIDTH:]


def kernel(x, p, w_in, norm_g, rw_mu, rw_w0, rw_w_up, rw_a0, rw_a_up, rw_k_k, rw_k_a, rw_r_k, rw_ln_g, rw_ln_b, ds_q_norm_g, ds_kv_norm_g, idx_k_norm_g, ds_w_uq, ds_w_uk, ds_w_uv, idx_w_q, rel_bias, w_out, ple_w, ple_gate_w, final_g):
    B, T, D = x.shape
    depth = w_in.shape[0]
    assert depth == 1 and T % SK == 0 and SK == 2 * QB and T % RW_CHUNK == 0 and (B * T) % 512 == 0
    h = x.reshape(B * T, D)
    tab = _biastab(rel_bias)
    for i in range(depth):
        z, ckv, kid = _inproj(h, norm_g[i].reshape(1, D), *_split_w_in(w_in[i]),
                              ds_kv_norm_g[i].reshape(1, -1), idx_k_norm_g[i].reshape(1, -1))
        o_rw = _rwkv(z, B, T, rw_mu[i], rw_w0[i], rw_a0[i], rw_k_k[i], rw_k_a[i],
                     rw_r_k[i].reshape(-1), rw_ln_g[i], rw_ln_b[i], rw_w_up[i], rw_a_up[i])
        o_ds = _dsa(z, ckv, kid, B, T, ds_q_norm_g[i], ds_w_uq[i], ds_w_uk[i], ds_w_uv[i],
                    idx_w_q[i], tab)
        h = _post(h, o_rw, o_ds, p[i].reshape(B * T, -1), w_out[i].astype(BF16),
                  ple_w[i].astype(BF16), ple_gate_w[i].astype(BF16), final_g.reshape(1, D))
    return h.reshape(B, T, D)
```

```python
import functools
import math

import jax
import jax.numpy as jnp
from jax import lax
from jax.experimental import pallas as pl
from jax.experimental.pallas import tpu as pltpu

F32 = jnp.float32
BF16 = jnp.bfloat16
I32 = jnp.int32

RW_WIDTH = 1024
RW_HEAD = 64
RW_HEADS = 16
RW_LORA = 64
DS_WIDTH = 1024
DS_HEAD = 64
DS_HEADS = 16
DS_Q_RANK = 384
DS_KV_RANK = 256
IDX_HEADS = 16
IDX_DIM = 64
TOPK_MAX = 256
CHUNK = 64
NUM_BUCKETS = 32
MAX_DISTANCE = 128
NORM_EPS = 1e-6
GN_EPS = 64e-5

COL_R, COL_K, COL_V, COL_GRW, COL_GDS = 0, 1024, 2048, 3072, 4096
COL_KV = 5120
COL_Q = 5376
COL_WA = 5760
COL_KX = 5888
Z_WIDTH = 6016

LANES = 128
SUBLANES = 8
QB = 256
BQ = 128
SK = 512
BIAS_ROWS = 2 * SK + 2 * BQ
RW_CHUNK = 64
INT_MIN = -2 ** 31
KEY_NEG_INF = -2139095041
PLANE_ROWS = 32 * SUBLANES
SEARCH_MIN_BITS = 23
SEARCH_STEP_BITS = 3
assert (32 - SEARCH_MIN_BITS) % SEARCH_STEP_BITS == 0
MASK_NEG = -1e30
LOG2E = 1.4426950408889634
ONES_ROWS = 16
VMEM_LIMIT = 52 * 1024 * 1024


def _sigmoid(x):
    return 1.0 / (1.0 + jnp.exp(-x))


def _dot(a, b):
    return jnp.dot(a, b, preferred_element_type=F32)


def _inproj_plan():
    s_wa = 3 * RW_WIDTH
    s_grw = s_wa + 2 * RW_LORA
    s_q = s_grw + RW_WIDTH
    s_kv = s_q + DS_Q_RANK
    s_kx = s_kv + DS_KV_RANK
    return [(COL_R, 0, 3 * RW_WIDTH), (COL_GRW, s_grw, RW_WIDTH), (COL_GDS, None, DS_WIDTH),
            (COL_KV, s_kv, DS_KV_RANK), (COL_Q, s_q, DS_Q_RANK), (COL_WA, s_wa, LANES),
            (COL_KX, s_kx, LANES)]


def _inproj_kernel(x_ref, g_ref, w_ref, wg_ref, gkv_ref, gik_ref, o_ref, ckv_ref, kid_ref, *, tn):
    x = x_ref[...]
    ms = jnp.mean(x * x, axis=-1, keepdims=True)
    xn = (x * lax.rsqrt(ms + NORM_EPS) * g_ref[...]).astype(BF16)
    for dst, src, width in _inproj_plan():
        for j in range(0, width, tn):
            w = min(tn, width - j)
            wt = wg_ref[:, j:j + w] if src is None else w_ref[:, src + j:src + j + w]
            o_ref[:, dst + j:dst + j + w] = _dot(xn, wt)
    kv = o_ref[:, COL_KV:COL_KV + DS_KV_RANK]
    ckv_ref[...] = (kv * lax.rsqrt(jnp.mean(kv * kv, axis=-1, keepdims=True) + NORM_EPS)
                    * gkv_ref[...]).astype(BF16)
    ki = o_ref[:, COL_KX:COL_KX + IDX_DIM]
    kid_ref[...] = (ki * lax.rsqrt(jnp.mean(ki * ki, axis=-1, keepdims=True) + NORM_EPS)
                    * gik_ref[...]).astype(BF16)


def _inproj(x2, g, w, w_gds, gkv, gik, tm=256, tn=768):
    m, d = x2.shape
    resident = dict(pipeline_mode=pl.Buffered(1))
    return pl.pallas_call(
        functools.partial(_inproj_kernel, tn=tn),
        grid=(m // tm,),
        in_specs=[pl.BlockSpec((tm, d), lambda i: (i, 0)),
                  pl.BlockSpec((1, d), lambda i: (0, 0)),
                  pl.BlockSpec(w.shape, lambda i: (0, 0), **resident),
                  pl.BlockSpec(w_gds.shape, lambda i: (0, 0), **resident),
                  pl.BlockSpec((1, DS_KV_RANK), lambda i: (0, 0)),
                  pl.BlockSpec((1, IDX_DIM), lambda i: (0, 0))],
        out_specs=[pl.BlockSpec((tm, Z_WIDTH), lambda i: (i, 0)),
                   pl.BlockSpec((tm, DS_KV_RANK), lambda i: (i, 0)),
                   pl.BlockSpec((tm, IDX_DIM), lambda i: (i, 0))],
        out_shape=[jax.ShapeDtypeStruct((m, Z_WIDTH), F32),
                   jax.ShapeDtypeStruct((m, DS_KV_RANK), BF16),
                   jax.ShapeDtypeStruct((m, IDX_DIM), BF16)],
        compiler_params=pltpu.CompilerParams(dimension_semantics=("parallel",),
                                             vmem_limit_bytes=VMEM_LIMIT),
        name="inproj",
    )(x2, g, w, w_gds, gkv, gik)


def _rwkv_kernel(r_ref, k_ref, v_ref, g_ref, wa_ref,
                 mur_ref, muk_ref, muv_ref, muwa_ref,
                 w0_ref, a0_ref, kk_ref, ka_ref, rk_ref, lng_ref, lnb_ref,
                 wup_ref, aup_ref,
                 o_ref,
                 pr_ref, pk_ref, pv_ref, pwa_ref, st_ref, *, nb):
    C = RW_CHUNK
    N = RW_HEAD

    @pl.when(pl.program_id(0) == 0)
    def _():
        pr_ref[...] = jnp.zeros_like(pr_ref)
        pk_ref[...] = jnp.zeros_like(pk_ref)
        pv_ref[...] = jnp.zeros_like(pv_ref)
        pwa_ref[...] = jnp.zeros_like(pwa_ref)
        st_ref[...] = jnp.zeros_like(st_ref)

    row = lax.broadcasted_iota(I32, (SUBLANES, 1), 0)

    def shift(ref, prev_ref, mu_ref):
        z = ref[...].reshape(nb * C, ref.shape[2])
        zp = pltpu.roll(z, 1, 0)
        parts = []
        for b in range(nb):
            parts.append(jnp.where(row == 0, prev_ref[b], zp[b * C:b * C + SUBLANES]))
            parts.append(zp[b * C + SUBLANES:(b + 1) * C])
            prev_ref[b] = z[(b + 1) * C - 1:(b + 1) * C, :]
        zp = jnp.concatenate(parts, axis=0)
        return z + mu_ref[...] * (zp - z)

    r = shift(r_ref, pr_ref, mur_ref)
    k = shift(k_ref, pk_ref, muk_ref)
    v = shift(v_ref, pv_ref, muv_ref)
    wa = shift(wa_ref, pwa_ref, muwa_ref)
    wd = wa[:, 0:RW_LORA]
    ad = wa[:, RW_LORA:2 * RW_LORA]

    wl = w0_ref[...] + _dot(jnp.tanh(wd).astype(BF16), wup_ref[...])
    lw = -math.exp(-0.5) * _sigmoid(wl)
    a = _sigmoid(a0_ref[...] + _dot(ad.astype(BF16), aup_ref[...]))
    kk = k * kk_ref[...]
    k2 = k * (1.0 + (a - 1.0) * ka_ref[...])

    ti = lax.broadcasted_iota(I32, (C, C), 0)
    tj = lax.broadcasted_iota(I32, (C, C), 1)
    tri = jnp.where(ti >= tj, 1.0, 0.0).astype(F32)

    def per_row(mat, x):
        hi = x.astype(BF16)
        r1 = x - hi.astype(F32)
        mid = r1.astype(BF16)
        lo = (r1 - mid.astype(F32)).astype(BF16)
        mb = mat.astype(BF16)
        return jnp.concatenate(
            [_dot(mb, hi[b * C:(b + 1) * C]) + _dot(mb, mid[b * C:(b + 1) * C]) + _dot(mb, lo[b * C:(b + 1) * C])
             for b in range(nb)], axis=0)

    cum = per_row(tri, lw)
    p = jnp.exp(cum)
    pinv = jnp.exp(-cum)
    pprev = jnp.exp(cum - lw)
    tot = per_row(jnp.ones((C, C), F32), lw)
    pend = jnp.exp(tot)

    g = g_ref[...].reshape(nb * C, RW_WIDTH)
    gate = g * _sigmoid(g)

    HP = RW_HEADS // 2
    NP = nb * HP

    def pairs(x):
        return jnp.stack([x[b * C:(b + 1) * C, j * LANES:(j + 1) * LANES]
                          for b in range(nb) for j in range(HP)], axis=0)

    def per_pair(ref):
        return jnp.concatenate([ref[...]] * nb, axis=0)

    lane = lax.broadcasted_iota(I32, (1, 1, LANES), 2)
    m_lo = jnp.where(lane < N, 1.0, 0.0).astype(BF16)
    m_hi = jnp.where(lane < N, 0.0, 1.0).astype(BF16)
    bi = lax.broadcasted_iota(I32, (LANES, LANES), 0)
    bj = lax.broadcasted_iota(I32, (LANES, LANES), 1)
    same_head = (bi < N) == (bj < N)
    ones_bd = jnp.where(same_head, 1.0, 0.0).astype(BF16)

    def head_sum(x):
        return _dot(x.reshape(NP * C, LANES).astype(BF16), ones_bd).reshape(NP, C, LANES)

    def halves(x):
        xb = x.astype(BF16)
        return jnp.concatenate([xb * m_lo, xb * m_hi], axis=1)

    def bmm(x, y):
        return lax.dot_general(x, y, (((2,), (1,)), ((0,), (0,))), preferred_element_type=F32)

    def bmm_nt(x, y):
        return lax.dot_general(x, y, (((2,), (2,)), ((0,), (0,))), preferred_element_type=F32)

    def block_mask(nblk, cmp):
        wi = lax.broadcasted_iota(I32, (C, nblk * C), 0)
        wj = lax.broadcasted_iota(I32, (C, nblk * C), 1) & (C - 1)
        return cmp(wi, wj)

    r_p, k2_p, v_p, a_p = pairs(r), pairs(k2), pairs(v), pairs(a)
    p_p, pinv_p, pprev_p = pairs(p), pairs(pinv), pairs(pprev)
    kk_p = pairs(kk)
    kkn = kk_p * lax.rsqrt(jnp.maximum(head_sum(kk_p * kk_p), 1e-24))
    at = (-kkn) * pprev_p
    bt = (kkn * a_p) * pinv_p
    kt = k2_p * pinv_p
    rt = r_p * p_p
    pend_p = pairs(pend)
    pend2 = jnp.concatenate([pend_p, pend_p], axis=1)

    lhs2 = jnp.concatenate([at, rt], axis=1).astype(BF16)
    rhs4 = jnp.concatenate([halves(kt), halves(bt)], axis=1)
    gc = bmm_nt(lhs2, rhs4)
    strict2 = block_mask(2, lambda i_, j_: i_ > j_)
    incl4 = block_mask(4, lambda i_, j_: i_ >= j_)
    a_ak = jnp.where(strict2, gc[:, 0:C, 0:2 * C], 0.0)
    nmat = jnp.where(strict2, gc[:, 0:C, 2 * C:4 * C], 0.0)
    a_rkb = jnp.where(incl4, gc[:, C:2 * C, :], 0.0)

    g0 = st_ref[...]
    sg = bmm_nt(lhs2, g0.astype(BF16))
    vm2 = halves(v_p)
    u = sg[:, 0:C] + bmm(a_ak.astype(BF16), vm2)
    pw = nmat.astype(BF16)
    u = u + bmm(pw, halves(u))
    n = 1
    while 2 * n < C:
        pw = bmm(pw, halves(pw)).astype(BF16)
        u = u + bmm(pw, halves(u))
        n *= 2
    um2b = halves(u)
    y = sg[:, C:2 * C] + bmm(a_rkb.astype(BF16), jnp.concatenate([vm2, um2b], axis=1))
    uv = jnp.concatenate([u, v_p], axis=1).astype(BF16)
    bkh = (jnp.concatenate([bt, kt], axis=1) * pend2).astype(BF16)
    upd = lax.dot_general(uv, bkh, (((1,), (1,)), ((0,), (0,))), preferred_element_type=F32)
    st_ref[...] = g0 * pend2 + jnp.where(same_head, upd, 0.0)

    inv_n = 1.0 / N
    yc = y - head_sum(y) * inv_n
    var = head_sum(yc * yc) * inv_n
    yn = yc * lax.rsqrt(var + GN_EPS) * per_pair(lng_ref) + per_pair(lnb_ref)
    bonus = head_sum(r_p * k2_p * per_pair(rk_ref)) * v_p
    out = (yn + bonus) * pairs(gate)
    for b in range(nb):
        for j in range(HP):
            o_ref[b, :, j * LANES:(j + 1) * LANES] = out[b * HP + j].astype(BF16)


def _rwkv(z, B, T, mu, w0, a0, k_k, k_a, r_k, ln_g, ln_b, w_up, a_up):
    C = RW_CHUNK
    W = RW_WIDTH
    z3 = z.reshape(B, T, z.shape[1])
    row = lambda a: a.reshape(1, -1).astype(F32)
    mu_r, mu_k, mu_v = mu[0:W], mu[W:2 * W], mu[2 * W:3 * W]
    mu_wa = mu[3 * W:3 * W + 2 * RW_LORA]
    zspec = lambda col: pl.BlockSpec((B, C, W), lambda c: (0, c, col // W))
    pspec = lambda width: pl.BlockSpec((1, width), lambda c: (0, 0))
    npairs = RW_HEADS // 2
    prow = lambda a: a.reshape(npairs, 1, LANES).astype(F32)
    ppspec = pl.BlockSpec((npairs, 1, LANES), lambda c: (0, 0, 0))
    wspec = pl.BlockSpec((RW_LORA, W), lambda c: (0, 0))
    out = pl.pallas_call(
        functools.partial(_rwkv_kernel, nb=B),
        grid=(T // C,),
        in_specs=[zspec(COL_R), zspec(COL_K), zspec(COL_V), zspec(COL_GRW),
                  pl.BlockSpec((B, C, LANES), lambda c: (0, c, COL_WA // LANES)),
                  pspec(W), pspec(W), pspec(W), pspec(LANES),
                  pspec(W), pspec(W), pspec(W), pspec(W), ppspec, ppspec, ppspec,
                  wspec, wspec],
        out_specs=pl.BlockSpec((B, C, W), lambda c: (0, c, 0)),
        out_shape=jax.ShapeDtypeStruct((B, T, W), BF16),
        scratch_shapes=[pltpu.VMEM((B, 1, W), F32), pltpu.VMEM((B, 1, W), F32), pltpu.VMEM((B, 1, W), F32),
                        pltpu.VMEM((B, 1, LANES), F32),
                        pltpu.VMEM((B * npairs, LANES, LANES), F32)],
        compiler_params=pltpu.CompilerParams(dimension_semantics=("arbitrary",),
                                             vmem_limit_bytes=VMEM_LIMIT),
        name="rwkv",
    )(z3, z3, z3, z3, z3,
      row(mu_r), row(mu_k), row(mu_v), row(mu_wa),
      row(w0), row(a0), row(k_k), row(k_a), prow(r_k), prow(ln_g), prow(ln_b),
      w_up.astype(BF16), a_up.astype(BF16))
    return out.reshape(B * T, W)


def _biastab_kernel(rb_ref, o_ref):
    o_ref[...] = jnp.zeros_like(o_ref)
    nb = NUM_BUCKETS // 2
    max_exact = nb // 2
    c = lax.broadcasted_iota(I32, (2 * BQ, BQ), 0)
    r = lax.broadcasted_iota(I32, (2 * BQ, BQ), 1)
    rel = c - BQ - r
    ret = jnp.where(rel > 0, nb, 0)
    n = jnp.abs(rel)
    nf = jnp.maximum(n, 1).astype(F32)
    large = max_exact + (jnp.log(nf / max_exact) / math.log(MAX_DISTANCE / max_exact)
                         * (nb - max_exact)).astype(I32)
    large = jnp.minimum(large, nb - 1) & (NUM_BUCKETS - 1)
    bucket = ret + jnp.where(n < max_exact, n, large)
    for h in range(DS_HEADS):
        far = rb_ref[nb - 1, h]
        acc = jnp.zeros((2 * BQ, BQ), F32)
        for b in range(NUM_BUCKETS):
            acc = jnp.where(bucket == b, rb_ref[b, h] - far, acc)
        o_ref[SK:SK + 2 * BQ, h * BQ:(h + 1) * BQ] = acc * LOG2E


def _biastab(rel_bias):
    return pl.pallas_call(
        _biastab_kernel,
        in_specs=[pl.BlockSpec(memory_space=pltpu.SMEM)],
        out_specs=pl.BlockSpec(memory_space=pltpu.VMEM),
        out_shape=jax.ShapeDtypeStruct((BIAS_ROWS, DS_HEADS * BQ), F32),
        compiler_params=pltpu.CompilerParams(vmem_limit_bytes=VMEM_LIMIT),
        name="biastab",
    )(rel_bias.astype(F32))


def _fold_rows(x, op):
    n = x.shape[0] // SUBLANES
    accs = [x[j * SUBLANES:(j + 1) * SUBLANES] for j in range(min(4, n))]
    for j in range(4, n):
        accs[j % 4] = op(accs[j % 4], x[j * SUBLANES:(j + 1) * SUBLANES])
    while len(accs) > 1:
        accs = [op(accs[j], accs[j + 1]) for j in range(0, len(accs) - 1, 2)] + (
            [accs[-1]] if len(accs) % 2 else [])
    return accs[0]


def _dsa_kernel(ql_ref, kx_ref, gds_ref, kid_ref, ckv_ref, ckvt_ref, qg_ref, wq_ref, wuk_ref, wuvt_ref,
                tab_ref, o_ref,
                sc_ref, pl_ref, cand_ref, lgt_ref, acc_ref, m_ref, qat_ref, qit_ref, w_ref, out_ref, *, topk):
    i = pl.program_id(1)
    q0 = i * QB
    ntile = jnp.right_shift(q0 + (QB + SK - 1), SK.bit_length() - 1)
    R = DS_KV_RANK
    GW = DS_HEADS * QB

    ql = ql_ref[...]
    ms = jnp.mean(ql * ql, axis=-1, keepdims=True)
    qn = (ql * lax.rsqrt(ms + NORM_EPS) * qg_ref[...]).astype(BF16)
    qt = _dot(qn, wq_ref[...]).T
    for h in range(DS_HEADS):
        qh = qt[h * DS_HEAD:(h + 1) * DS_HEAD, :].astype(BF16)
        qat_ref[:, h * QB:(h + 1) * QB] = (_dot(wuk_ref[h], qh) * (DS_HEAD ** -0.5 * LOG2E)).astype(BF16)
    for pr in range(IDX_HEADS // 2):
        base = DS_WIDTH + 2 * pr * IDX_DIM
        qit_ref[pr] = jnp.concatenate([qt[base:base + IDX_DIM, :],
                                       qt[base + IDX_DIM:base + 2 * IDX_DIM, :]], axis=1).astype(BF16)
    w_ref[...] = kx_ref[...].T[IDX_DIM:IDX_DIM + IDX_HEADS, :] * (IDX_HEADS ** -0.5 * IDX_DIM ** -0.5)

    lanei = lax.broadcasted_iota(I32, (1, QB), 1)
    csh = CHUNK.bit_length() - 1
    limit = jnp.left_shift(jnp.right_shift(q0 + lanei, csh) + 1, csh)
    rowi = lax.broadcasted_iota(I32, (SK, QB), 0)
    rowh = lax.broadcasted_iota(I32, (QB, QB), 0)
    nfull = jnp.right_shift(q0 + QB, SK.bit_length() - 1)
    nhalf = jnp.right_shift(q0 + QB, QB.bit_length() - 1)

    def score_rows(off, nrows):
        kid = kid_ref[pl.ds(off, nrows), :]
        s = jnp.zeros((nrows, QB), F32)
        for pr in range(IDX_HEADS // 2):
            lg = _dot(kid, qit_ref[pr])
            s = s + w_ref[2 * pr:2 * pr + 1, :] * jnp.maximum(lg[:, 0:QB], 0.0)
            s = s + w_ref[2 * pr + 1:2 * pr + 2, :] * jnp.maximum(lg[:, QB:2 * QB], 0.0)
        adm = (off + (rowi if nrows == SK else rowh)) < limit
        sc_ref[pl.ds(off, nrows), :] = jnp.where(adm, s, -jnp.inf)

    def score_tile(kt, carry):
        score_rows(pl.multiple_of(kt * SK, SK), SK)
        return carry

    lax.fori_loop(0, nfull, score_tile, 0)

    @pl.when(nfull < ntile)
    def _():
        score_rows(pl.multiple_of(nfull * SK, SK), QB)

    def key_to_f32(key):
        return pltpu.bitcast(jnp.where(key < 0, key ^ 0x7FFFFFFF, key), F32)

    def count(pred):
        def body(kt, acc):
            off = pl.multiple_of(kt * QB, QB)
            sc = sc_ref[pl.ds(off, QB), :]
            return acc + _fold_rows(jnp.where(pred(sc, off), 1, 0).astype(I32), jnp.add)
        acc = lax.fori_loop(0, nhalf, body, jnp.zeros((SUBLANES, QB), I32))
        return jnp.sum(acc, axis=0, keepdims=True)

    def bit_step(it, carry):
        lo, cnt_lo = carry
        cand = lo + jnp.left_shift(jnp.int32(1), 31 - it)
        cf = key_to_f32(cand)
        cnt = count(lambda sc, off: sc >= cf)
        take = cnt >= topk
        return jnp.where(take, cand, lo), jnp.where(take, cnt, cnt_lo)

    def all_settled(cnt_lo):
        return jnp.min(jnp.where((cnt_lo == topk) | (limit < topk), 1, 0))

    def slow_search():
        lo, cnt_lo = lax.fori_loop(0, SEARCH_MIN_BITS, bit_step,
                                   (jnp.full((1, QB), INT_MIN, I32), jnp.full((1, QB), 2 ** 30, I32)))

        def more_bits(c):
            it, lo, cnt_lo, _ = c
            lo, cnt_lo = lax.fori_loop(it, it + SEARCH_STEP_BITS, bit_step, (lo, cnt_lo))
            return it + SEARCH_STEP_BITS, lo, cnt_lo, all_settled(cnt_lo)

        _, lo, _, settled = lax.while_loop(lambda c: (c[0] < 32) & (c[3] == 0), more_bits,
                                           (jnp.int32(SEARCH_MIN_BITS), lo, cnt_lo, all_settled(cnt_lo)))
        return lo, settled

    ngrp_max = sc_ref.shape[0] // PLANE_ROWS
    ngrp = jnp.right_shift(q0 + QB, PLANE_ROWS.bit_length() - 1)

    @pl.when(i == 0)
    def _():
        pl_ref[...] = jnp.zeros_like(pl_ref)

    cand_ref[...] = jnp.zeros_like(cand_ref)

    def to_planes(g, c):
        base = pl.multiple_of(g * PLANE_ROWS, PLANE_ROWS)
        words = []
        for j in range(32):
            bits = pltpu.bitcast(sc_ref[pl.ds(base + SUBLANES * j, SUBLANES), :], I32)
            words.append(jnp.where(bits < 0, ~bits, bits | INT_MIN))
        j, m = 16, 0x0000FFFF
        while j:
            k = 0
            while k < 32:
                t = (words[k] ^ lax.shift_right_logical(words[k + j], j)) & m
                words[k] = words[k] ^ t
                words[k + j] = words[k + j] ^ lax.shift_left(t, j)
                k = (k + j + 1) & ~j
            j >>= 1
            m = (m ^ (m << j)) & 0xFFFFFFFF
        for b in range(32):
            pl_ref[pl.ds(base + SUBLANES * b, SUBLANES), :] = words[b]
        cand_ref[pl.ds(pl.multiple_of(g * SUBLANES, SUBLANES), SUBLANES), :] = jnp.full((SUBLANES, QB), -1, I32)
        return c

    lax.fori_loop(0, ngrp, to_planes, 0)

    def radix_select(ng):
        def radix_step(it, c):
            above, tb = c
            prow = pl.multiple_of(it * SUBLANES, SUBLANES)
            acc = jnp.zeros((SUBLANES, QB), I32)
            for g in range(ng):
                plane = pl_ref[pl.ds(g * PLANE_ROWS + prow, SUBLANES), :]
                acc = acc + lax.population_count(cand_ref[g * SUBLANES:(g + 1) * SUBLANES, :] & plane)
            ones = jnp.sum(acc, axis=0, keepdims=True)
            take = above + ones >= topk
            for g in range(ng):
                plane = pl_ref[pl.ds(g * PLANE_ROWS + prow, SUBLANES), :]
                cand = cand_ref[g * SUBLANES:(g + 1) * SUBLANES, :]
                cand_ref[g * SUBLANES:(g + 1) * SUBLANES, :] = jnp.where(take, cand & plane, cand & ~plane)
            return (jnp.where(take, above, above + ones),
                    jnp.where(take, tb | lax.shift_left(jnp.int32(1), 31 - it), tb))

        return lambda: lax.fori_loop(0, 32, radix_step,
                                     (jnp.zeros((1, QB), I32), jnp.zeros((1, QB), I32)))[1]

    tbits = lax.cond(ngrp <= ngrp_max // 2, radix_select(ngrp_max // 2), radix_select(ngrp_max))
    lo_fast = tbits ^ INT_MIN
    thr_fast = key_to_f32(jnp.maximum(lo_fast, KEY_NEG_INF))
    fast_ok = all_settled(count(lambda sc, off: sc >= thr_fast))
    lo, settled = lax.cond(fast_ok == 1, lambda: (lo_fast, fast_ok), slow_search)
    has_thr = lo > KEY_NEG_INF
    thr = key_to_f32(jnp.maximum(lo, KEY_NEG_INF))
    nbits = max(1, (sc_ref.shape[0] - 1).bit_length())
    take_all_ties = jnp.full((1, QB), sc_ref.shape[0], I32)

    def tie_cut():
        cnt_gt = count(lambda sc, off: sc > thr)
        cnt_eq = count(lambda sc, off: sc == thr)
        tied = (cnt_gt + cnt_eq > topk) & has_thr

        def search_cut():
            def idx_step(it, m):
                cand = m + jnp.left_shift(jnp.int32(1), nbits - 1 - it)
                cnt = cnt_gt + count(lambda sc, off: (sc == thr) & ((off + rowh) < cand))
                return jnp.where(cnt < topk, cand, m)
            return lax.fori_loop(0, nbits, idx_step, jnp.zeros((1, QB), I32))

        return lax.cond(jnp.max(jnp.where(tied, 1, 0)) > 0, search_cut, lambda: take_all_ties)

    cut = lax.cond(settled == 1, lambda: take_all_ties, tie_cut)
    cut = jnp.where(has_thr, cut, -1)

    def selection_mask(off, nrows):
        sc = sc_ref[pl.ds(off, nrows), :]
        ri = rowi if nrows == SK else rowh
        sel = (sc > thr) | ((sc == thr) & ((off + ri) <= cut))
        return jnp.where(sel, 0.0, MASK_NEG).astype(F32)

    near_lo = q0 - BQ

    def attend(off, bias_off, first=False, nrows=SK):
        rows = pl.ds(off, nrows)
        s = _dot(ckv_ref[rows, :], qat_ref[...])
        mk = selection_mask(off, nrows)
        m_old = jnp.full((1, GW), MASK_NEG, F32) if first else m_ref[...]
        tmax = []
        for h in range(DS_HEADS):
            cs = slice(h * QB, (h + 1) * QB)
            t = s[:, cs] + mk
            if bias_off is not None:
                t = t + jnp.concatenate(
                    [tab_ref[pl.ds(pl.multiple_of(jnp.maximum(bias_off - j * BQ, 0), BQ), nrows),
                             h * BQ:(h + 1) * BQ] for j in range(QB // BQ)], axis=1)
            lgt_ref[0:nrows, cs] = t
            tmax.append(jnp.max(_fold_rows(t, jnp.maximum), axis=0, keepdims=True))
        m_new = jnp.maximum(m_old, jnp.concatenate(tmax, axis=1))
        m_ref[...] = m_new
        pr = jnp.exp2(lgt_ref[0:nrows, :] - m_new).astype(BF16)
        pv = _dot(ckvt_ref[:, rows], pr)
        acc_ref[...] = pv if first else acc_ref[...] * jnp.exp2(m_old - m_new) + pv

    def far_tile(kt, c):
        attend(pl.multiple_of(kt * SK, SK), None)
        return c

    def edge_tile(kt, c):
        off = pl.multiple_of(kt * SK, SK)
        attend(off, SK + off - near_lo)
        return c

    @pl.when(nfull >= 1)
    def _():
        attend(0, SK - near_lo, first=True)
        lax.fori_loop(1, nfull - 1, far_tile, 0)
        lax.fori_loop(jnp.maximum(nfull - 1, 1), nfull, edge_tile, 0)

        @pl.when(nfull < ntile)
        def _():
            off = pl.multiple_of(nfull * SK, SK)
            attend(off, SK + off - near_lo, nrows=QB)

    @pl.when(nfull == 0)
    def _():
        attend(0, SK - near_lo, first=True, nrows=QB)

    o_lat = acc_ref[0:R, :] * (1.0 / acc_ref[R:R + 1, :])
    for h in range(DS_HEADS):
        out_ref[h * DS_HEAD:(h + 1) * DS_HEAD, :] = _dot(
            wuvt_ref[h], o_lat[:, h * QB:(h + 1) * QB].astype(BF16))

    g = gds_ref[...]
    o_ref[...] = (out_ref[...].T * (g * _sigmoid(g))).astype(BF16)


def _dsa(z, ckv, kid, B, T, q_norm_g, w_uq, w_uk, w_uv, iw_q, tab):
    nq = T // QB
    topk = min(TOPK_MAX, T // 4)
    R = DS_KV_RANK
    wq = jnp.concatenate([w_uq, iw_q], axis=1).astype(BF16)
    wuk_h = jnp.transpose(w_uk, (1, 0, 2)).astype(BF16)
    wuv_t = jnp.transpose(w_uv, (1, 2, 0)).astype(BF16)
    ckv_t = jnp.concatenate([jnp.swapaxes(ckv.reshape(B, T, R), 1, 2),
                             jnp.ones((B, ONES_ROWS, T), BF16)], axis=1).reshape(B * (R + ONES_ROWS), T)
    const2 = lambda b, i: (0, 0)
    const3 = lambda b, i: (0, 0, 0)
    resident = dict(pipeline_mode=pl.Buffered(1))
    return pl.pallas_call(
        functools.partial(_dsa_kernel, topk=topk),
        grid=(B, nq),
        in_specs=[pl.BlockSpec((QB, DS_Q_RANK), lambda b, i: (b * nq + i, COL_Q // DS_Q_RANK)),
                  pl.BlockSpec((QB, LANES), lambda b, i: (b * nq + i, COL_KX // LANES)),
                  pl.BlockSpec((QB, DS_WIDTH), lambda b, i: (b * nq + i, COL_GDS // DS_WIDTH)),
                  pl.BlockSpec((T, IDX_DIM), lambda b, i: (b, 0), **resident),
                  pl.BlockSpec((T, R), lambda b, i: (b, 0), **resident),
                  pl.BlockSpec((R + ONES_ROWS, T), lambda b, i: (b, 0), **resident),
                  pl.BlockSpec((1, DS_Q_RANK), const2),
                  pl.BlockSpec((DS_Q_RANK, 2 * DS_WIDTH), const2, **resident),
                  pl.BlockSpec((DS_HEADS, R, DS_HEAD), const3, **resident),
                  pl.BlockSpec((DS_HEADS, DS_HEAD, R), const3, **resident),
                  pl.BlockSpec((BIAS_ROWS, DS_HEADS * BQ), const2, **resident)],
        out_specs=pl.BlockSpec((QB, DS_WIDTH), lambda b, i: (b * nq + i, 0)),
        out_shape=jax.ShapeDtypeStruct((B * T, DS_WIDTH), BF16),
        scratch_shapes=[pltpu.VMEM((T, QB), F32),
                        pltpu.VMEM((T, QB), I32),
                        pltpu.VMEM((T // PLANE_ROWS * SUBLANES, QB), I32),
                        pltpu.VMEM((SK, DS_HEADS * QB), F32),
                        pltpu.VMEM((R + ONES_ROWS, DS_HEADS * QB), F32),
                        pltpu.VMEM((1, DS_HEADS * QB), F32),
                        pltpu.VMEM((R, DS_HEADS * QB), BF16),
                        pltpu.VMEM((IDX_HEADS // 2, IDX_DIM, 2 * QB), BF16),
                        pltpu.VMEM((IDX_HEADS, QB), F32),
                        pltpu.VMEM((DS_WIDTH, QB), F32)],
        compiler_params=pltpu.CompilerParams(dimension_semantics=("parallel", "arbitrary"),
                                             vmem_limit_bytes=VMEM_LIMIT),
        name="dsa",
    )(z, z, z, kid, ckv, ckv_t, q_norm_g.reshape(1, -1).astype(F32), wq, wuk_h, wuv_t, tab)


def _post_kernel(x_ref, a1_ref, a2_ref, p_ref, w1_ref, w2_ref, pw_ref, gw_ref, fg_ref, o_ref):
    h = x_ref[...] + _dot(a1_ref[...], w1_ref[...]) + _dot(a2_ref[...], w2_ref[...])
    e = _dot(p_ref[...].astype(BF16), pw_ref[...])
    gate = _sigmoid(_dot(h.astype(BF16), gw_ref[...]))
    h2 = h + e * gate
    ms = jnp.mean(h2 * h2, axis=-1, keepdims=True)
    o_ref[...] = h2 * lax.rsqrt(ms + NORM_EPS) * fg_ref[...]


def _post(x2, o_rw, o_ds, p2, w_out, ple_w, gate_w, final_g, tm=512):
    m, d = x2.shape
    kh = o_rw.shape[1]
    pd = p2.shape[1]
    resident = dict(pipeline_mode=pl.Buffered(1))
    return pl.pallas_call(
        _post_kernel,
        grid=(m // tm,),
        in_specs=[pl.BlockSpec((tm, d), lambda i: (i, 0)),
                  pl.BlockSpec((tm, kh), lambda i: (i, 0)),
                  pl.BlockSpec((tm, kh), lambda i: (i, 0)),
                  pl.BlockSpec((tm, pd), lambda i: (i, 0)),
                  pl.BlockSpec((kh, d), lambda i: (0, 0), **resident),
                  pl.BlockSpec((kh, d), lambda i: (1, 0), **resident),
                  pl.BlockSpec((pd, d), lambda i: (0, 0), **resident),
                  pl.BlockSpec((d, d), lambda i: (0, 0), **resident),
                  pl.BlockSpec((1, d), lambda i: (0, 0))],
        out_specs=pl.BlockSpec((tm, d), lambda i: (i, 0)),
        out_shape=jax.ShapeDtypeStruct((m, d), F32),
        compiler_params=pltpu.CompilerParams(dimension_semantics=("parallel",),
                                             vmem_limit_bytes=VMEM_LIMIT),
        name="post",
    )(x2, o_rw, o_ds, p2, w_out, w_out, ple_w, gate_w, final_g)


def _split_w_in(w):
    w = w.astype(BF16)
    return w, w[:, w.shape[1] - DS_WIDTH:]


def kernel(x, p, w_in, norm_g, rw_mu, rw_w0, rw_w_up, rw_a0, rw_a_up, rw_k_k, rw_k_a, rw_r_k, rw_ln_g, rw_ln_b, ds_q_norm_g, ds_kv_norm_g, idx_k_norm_g, ds_w_uq, ds_w_uk, ds_w_uv, idx_w_q, rel_bias, w_out, ple_w, ple_gate_w, final_g):
    B, T, D = x.shape
    depth = w_in.shape[0]
    assert depth == 1 and T % SK == 0 and SK == 2 * QB and T % RW_CHUNK == 0 and (B * T) % 512 == 0
    h = x.reshape(B * T, D)
    tab = _biastab(rel_bias)
    for i in range(depth):
        z, ckv, kid = _inproj(h, norm_g[i].reshape(1, D), *_split_w_in(w_in[i]),
                              ds_kv_norm_g[i].reshape(1, -1), idx_k_norm_g[i].reshape(1, -1))
        o_rw = _rwkv(z, B, T, rw_mu[i], rw_w0[i], rw_a0[i], rw_k_k[i], rw_k_a[i],
                     rw_r_k[i].reshape(-1), rw_ln_g[i], rw_ln_b[i], rw_w_up[i], rw_a_up[i])
        o_ds = _dsa(z, ckv, kid, B, T, ds_q_norm_g[i], ds_w_uq[i], ds_w_uk[i], ds_w_uv[i],
                    idx_w_q[i], tab)
        h = _post(h, o_rw, o_ds, p[i].reshape(B * T, -1), w_out[i].astype(BF16),
                  ple_w[i].astype(BF16), ple_gate_w[i].astype(BF16), final_g.reshape(1, D))
    return h.reshape(B, T, D)
```

```python
import functools
import math

import jax
import jax.numpy as jnp
from jax import lax
from jax.experimental import pallas as pl
from jax.experimental.pallas import tpu as pltpu

F32 = jnp.float32
BF16 = jnp.bfloat16
I32 = jnp.int32

RW_WIDTH = 1024
RW_HEAD = 64
RW_HEADS = 16
RW_LORA = 64
DS_WIDTH = 1024
DS_HEAD = 64
DS_HEADS = 16
DS_Q_RANK = 384
DS_KV_RANK = 256
IDX_HEADS = 16
IDX_DIM = 64
TOPK_MAX = 256
CHUNK = 64
NUM_BUCKETS = 32
MAX_DISTANCE = 128
NORM_EPS = 1e-6
GN_EPS = 64e-5

COL_R, COL_K, COL_V, COL_GRW, COL_GDS = 0, 1024, 2048, 3072, 4096
COL_KV = 5120
COL_Q = 5376
COL_WA = 5760
COL_KX = 5888
Z_WIDTH = 6016

LANES = 128
SUBLANES = 8
QB = 256
BQ = 128
SK = 512
BIAS_ROWS = 2 * SK + 2 * BQ
RW_CHUNK = 64
INT_MIN = -2 ** 31
KEY_NEG_INF = -2139095041
PLANE_ROWS = 32 * SUBLANES
SEARCH_MIN_BITS = 23
SEARCH_STEP_BITS = 3
assert (32 - SEARCH_MIN_BITS) % SEARCH_STEP_BITS == 0
MASK_NEG = -1e30
LOG2E = 1.4426950408889634
ONES_ROWS = 16
VMEM_LIMIT = 52 * 1024 * 1024


def _sigmoid(x):
    return 1.0 / (1.0 + jnp.exp(-x))


def _dot(a, b):
    return jnp.dot(a, b, preferred_element_type=F32)


def _inproj_plan():
    s_wa = 3 * RW_WIDTH
    s_grw = s_wa + 2 * RW_LORA
    s_q = s_grw + RW_WIDTH
    s_kv = s_q + DS_Q_RANK
    s_kx = s_kv + DS_KV_RANK
    return [(COL_R, 0, 3 * RW_WIDTH), (COL_GRW, s_grw, RW_WIDTH), (COL_GDS, None, DS_WIDTH),
            (COL_KV, s_kv, DS_KV_RANK), (COL_Q, s_q, DS_Q_RANK), (COL_WA, s_wa, LANES),
            (COL_KX, s_kx, LANES)]


def _inproj_kernel(x_ref, g_ref, w_ref, wg_ref, gkv_ref, gik_ref, o_ref, ckv_ref, kid_ref, *, tn):
    x = x_ref[...]
    ms = jnp.mean(x * x, axis=-1, keepdims=True)
    xn = (x * lax.rsqrt(ms + NORM_EPS) * g_ref[...]).astype(BF16)
    for dst, src, width in _inproj_plan():
        for j in range(0, width, tn):
            w = min(tn, width - j)
            wt = wg_ref[:, j:j + w] if src is None else w_ref[:, src + j:src + j + w]
            o_ref[:, dst + j:dst + j + w] = _dot(xn, wt)
    kv = o_ref[:, COL_KV:COL_KV + DS_KV_RANK]
    ckv_ref[...] = (kv * lax.rsqrt(jnp.mean(kv * kv, axis=-1, keepdims=True) + NORM_EPS)
                    * gkv_ref[...]).astype(BF16)
    ki = o_ref[:, COL_KX:COL_KX + IDX_DIM]
    kid_ref[...] = (ki * lax.rsqrt(jnp.mean(ki * ki, axis=-1, keepdims=True) + NORM_EPS)
                    * gik_ref[...]).astype(BF16)


def _inproj(x2, g, w, w_gds, gkv, gik, tm=256, tn=768):
    m, d = x2.shape
    resident = dict(pipeline_mode=pl.Buffered(1))
    return pl.pallas_call(
        functools.partial(_inproj_kernel, tn=tn),
        grid=(m // tm,),
        in_specs=[pl.BlockSpec((tm, d), lambda i: (i, 0)),
                  pl.BlockSpec((1, d), lambda i: (0, 0)),
                  pl.BlockSpec(w.shape, lambda i: (0, 0), **resident),
                  pl.BlockSpec(w_gds.shape, lambda i: (0, 0), **resident),
                  pl.BlockSpec((1, DS_KV_RANK), lambda i: (0, 0)),
                  pl.BlockSpec((1, IDX_DIM), lambda i: (0, 0))],
        out_specs=[pl.BlockSpec((tm, Z_WIDTH), lambda i: (i, 0)),
                   pl.BlockSpec((tm, DS_KV_RANK), lambda i: (i, 0)),
                   pl.BlockSpec((tm, IDX_DIM), lambda i: (i, 0))],
        out_shape=[jax.ShapeDtypeStruct((m, Z_WIDTH), F32),
                   jax.ShapeDtypeStruct((m, DS_KV_RANK), BF16),
                   jax.ShapeDtypeStruct((m, IDX_DIM), BF16)],
        compiler_params=pltpu.CompilerParams(dimension_semantics=("parallel",),
                                             vmem_limit_bytes=VMEM_LIMIT),
        name="inproj",
    )(x2, g, w, w_gds, gkv, gik)


def _rwkv_kernel(r_ref, k_ref, v_ref, g_ref, wa_ref,
                 mur_ref, muk_ref, muv_ref, muwa_ref,
                 w0_ref, a0_ref, kk_ref, ka_ref, rk_ref, lng_ref, lnb_ref,
                 wup_ref, aup_ref,
                 o_ref,
                 pr_ref, pk_ref, pv_ref, pwa_ref, st_ref, *, nb):
    C = RW_CHUNK
    N = RW_HEAD

    @pl.when(pl.program_id(0) == 0)
    def _():
        pr_ref[...] = jnp.zeros_like(pr_ref)
        pk_ref[...] = jnp.zeros_like(pk_ref)
        pv_ref[...] = jnp.zeros_like(pv_ref)
        pwa_ref[...] = jnp.zeros_like(pwa_ref)
        st_ref[...] = jnp.zeros_like(st_ref)

    row = lax.broadcasted_iota(I32, (SUBLANES, 1), 0)

    def shift(ref, prev_ref, mu_ref):
        z = ref[...].reshape(nb * C, ref.shape[2])
        zp = pltpu.roll(z, 1, 0)
        parts = []
        for b in range(nb):
            parts.append(jnp.where(row == 0, prev_ref[b], zp[b * C:b * C + SUBLANES]))
            parts.append(zp[b * C + SUBLANES:(b + 1) * C])
            prev_ref[b] = z[(b + 1) * C - 1:(b + 1) * C, :]
        zp = jnp.concatenate(parts, axis=0)
        return z + mu_ref[...] * (zp - z)

    r = shift(r_ref, pr_ref, mur_ref)
    k = shift(k_ref, pk_ref, muk_ref)
    v = shift(v_ref, pv_ref, muv_ref)
    wa = shift(wa_ref, pwa_ref, muwa_ref)
    wd = wa[:, 0:RW_LORA]
    ad = wa[:, RW_LORA:2 * RW_LORA]

    wl = w0_ref[...] + _dot(jnp.tanh(wd).astype(BF16), wup_ref[...])
    lw = -math.exp(-0.5) * _sigmoid(wl)
    a = _sigmoid(a0_ref[...] + _dot(ad.astype(BF16), aup_ref[...]))
    kk = k * kk_ref[...]
    k2 = k * (1.0 + (a - 1.0) * ka_ref[...])

    ti = lax.broadcasted_iota(I32, (C, C), 0)
    tj = lax.broadcasted_iota(I32, (C, C), 1)
    tri = jnp.where(ti >= tj, 1.0, 0.0).astype(F32)

    def per_row(mat, x):
        hi = x.astype(BF16)
        r1 = x - hi.astype(F32)
        mid = r1.astype(BF16)
        lo = (r1 - mid.astype(F32)).astype(BF16)
        mb = mat.astype(BF16)
        return jnp.concatenate(
            [_dot(mb, hi[b * C:(b + 1) * C]) + _dot(mb, mid[b * C:(b + 1) * C]) + _dot(mb, lo[b * C:(b + 1) * C])
             for b in range(nb)], axis=0)

    cum = per_row(tri, lw)
    p = jnp.exp(cum)
    pinv = jnp.exp(-cum)
    pprev = jnp.exp(cum - lw)
    tot = per_row(jnp.ones((C, C), F32), lw)
    pend = jnp.exp(tot)

    g = g_ref[...].reshape(nb * C, RW_WIDTH)
    gate = g * _sigmoid(g)

    HP = RW_HEADS // 2
    NP = nb * HP

    def pairs(x):
        return jnp.stack([x[b * C:(b + 1) * C, j * LANES:(j + 1) * LANES]
                          for b in range(nb) for j in range(HP)], axis=0)

    def per_pair(ref):
        return jnp.concatenate([ref[...]] * nb, axis=0)

    lane = lax.broadcasted_iota(I32, (1, 1, LANES), 2)
    m_lo = jnp.where(lane < N, 1.0, 0.0).astype(BF16)
    m_hi = jnp.where(lane < N, 0.0, 1.0).astype(BF16)
    bi = lax.broadcasted_iota(I32, (LANES, LANES), 0)
    bj = lax.broadcasted_iota(I32, (LANES, LANES), 1)
    same_head = (bi < N) == (bj < N)
    ones_bd = jnp.where(same_head, 1.0, 0.0).astype(BF16)

    def head_sum(x):
        return _dot(x.reshape(NP * C, LANES).astype(BF16), ones_bd).reshape(NP, C, LANES)

    def halves(x):
        xb = x.astype(BF16)
        return jnp.concatenate([xb * m_lo, xb * m_hi], axis=1)

    def bmm(x, y):
        return lax.dot_general(x, y, (((2,), (1,)), ((0,), (0,))), preferred_element_type=F32)

    def bmm_nt(x, y):
        return lax.dot_general(x, y, (((2,), (2,)), ((0,), (0,))), preferred_element_type=F32)

    def block_mask(nblk, cmp):
        wi = lax.broadcasted_iota(I32, (C, nblk * C), 0)
        wj = lax.broadcasted_iota(I32, (C, nblk * C), 1) & (C - 1)
        return cmp(wi, wj)

    r_p, k2_p, v_p, a_p = pairs(r), pairs(k2), pairs(v), pairs(a)
    p_p, pinv_p, pprev_p = pairs(p), pairs(pinv), pairs(pprev)
    kk_p = pairs(kk)
    kkn = kk_p * lax.rsqrt(jnp.maximum(head_sum(kk_p * kk_p), 1e-24))
    at = (-kkn) * pprev_p
    bt = (kkn * a_p) * pinv_p
    kt = k2_p * pinv_p
    rt = r_p * p_p
    pend_p = pairs(pend)
    pend2 = jnp.concatenate([pend_p, pend_p], axis=1)

    lhs2 = jnp.concatenate([at, rt], axis=1).astype(BF16)
    rhs4 = jnp.concatenate([halves(kt), halves(bt)], axis=1)
    gc = bmm_nt(lhs2, rhs4)
    strict2 = block_mask(2, lambda i_, j_: i_ > j_)
    incl4 = block_mask(4, lambda i_, j_: i_ >= j_)
    a_ak = jnp.where(strict2, gc[:, 0:C, 0:2 * C], 0.0)
    nmat = jnp.where(strict2, gc[:, 0:C, 2 * C:4 * C], 0.0)
    a_rkb = jnp.where(incl4, gc[:, C:2 * C, :], 0.0)

    g0 = st_ref[...]
    sg = bmm_nt(lhs2, g0.astype(BF16))
    vm2 = halves(v_p)
    u = sg[:, 0:C] + bmm(a_ak.astype(BF16), vm2)
    pw = nmat.astype(BF16)
    u = u + bmm(pw, halves(u))
    n = 1
    while 2 * n < C:
        pw = bmm(pw, halves(pw)).astype(BF16)
        u = u + bmm(pw, halves(u))
        n *= 2
    um2b = halves(u)
    y = sg[:, C:2 * C] + bmm(a_rkb.astype(BF16), jnp.concatenate([vm2, um2b], axis=1))
    uv = jnp.concatenate([u, v_p], axis=1).astype(BF16)
    bkh = (jnp.concatenate([bt, kt], axis=1) * pend2).astype(BF16)
    upd = lax.dot_general(uv, bkh, (((1,), (1,)), ((0,), (0,))), preferred_element_type=F32)
    st_ref[...] = g0 * pend2 + jnp.where(same_head, upd, 0.0)

    inv_n = 1.0 / N
    yc = y - head_sum(y) * inv_n
    var = head_sum(yc * yc) * inv_n
    yn = yc * lax.rsqrt(var + GN_EPS) * per_pair(lng_ref) + per_pair(lnb_ref)
    bonus = head_sum(r_p * k2_p * per_pair(rk_ref)) * v_p
    out = (yn + bonus) * pairs(gate)
    for b in range(nb):
        for j in range(HP):
            o_ref[b, :, j * LANES:(j + 1) * LANES] = out[b * HP + j].astype(BF16)


def _rwkv(z, B, T, mu, w0, a0, k_k, k_a, r_k, ln_g, ln_b, w_up, a_up):
    C = RW_CHUNK
    W = RW_WIDTH
    z3 = z.reshape(B, T, z.shape[1])
    row = lambda a: a.reshape(1, -1).astype(F32)
    mu_r, mu_k, mu_v = mu[0:W], mu[W:2 * W], mu[2 * W:3 * W]
    mu_wa = mu[3 * W:3 * W + 2 * RW_LORA]
    zspec = lambda col: pl.BlockSpec((B, C, W), lambda c: (0, c, col // W))
    pspec = lambda width: pl.BlockSpec((1, width), lambda c: (0, 0))
    npairs = RW_HEADS // 2
    prow = lambda a: a.reshape(npairs, 1, LANES).astype(F32)
    ppspec = pl.BlockSpec((npairs, 1, LANES), lambda c: (0, 0, 0))
    wspec = pl.BlockSpec((RW_LORA, W), lambda c: (0, 0))
    out = pl.pallas_call(
        functools.partial(_rwkv_kernel, nb=B),
        grid=(T // C,),
        in_specs=[zspec(COL_R), zspec(COL_K), zspec(COL_V), zspec(COL_GRW),
                  pl.BlockSpec((B, C, LANES), lambda c: (0, c, COL_WA // LANES)),
                  pspec(W), pspec(W), pspec(W), pspec(LANES),
                  pspec(W), pspec(W), pspec(W), pspec(W), ppspec, ppspec, ppspec,
                  wspec, wspec],
        out_specs=pl.BlockSpec((B, C, W), lambda c: (0, c, 0)),
        out_shape=jax.ShapeDtypeStruct((B, T, W), BF16),
        scratch_shapes=[pltpu.VMEM((B, 1, W), F32), pltpu.VMEM((B, 1, W), F32), pltpu.VMEM((B, 1, W), F32),
                        pltpu.VMEM((B, 1, LANES), F32),
                        pltpu.VMEM((B * npairs, LANES, LANES), F32)],
        compiler_params=pltpu.CompilerParams(dimension_semantics=("arbitrary",),
                                             vmem_limit_bytes=VMEM_LIMIT),
        name="rwkv",
    )(z3, z3, z3, z3, z3,
      row(mu_r), row(mu_k), row(mu_v), row(mu_wa),
      row(w0), row(a0), row(k_k), row(k_a), prow(r_k), prow(ln_g), prow(ln_b),
      w_up.astype(BF16), a_up.astype(BF16))
    return out.reshape(B * T, W)


def _biastab_kernel(rb_ref, o_ref):
    o_ref[...] = jnp.zeros_like(o_ref)
    nb = NUM_BUCKETS // 2
    max_exact = nb // 2
    c = lax.broadcasted_iota(I32, (2 * BQ, BQ), 0)
    r = lax.broadcasted_iota(I32, (2 * BQ, BQ), 1)
    rel = c - BQ - r
    ret = jnp.where(rel > 0, nb, 0)
    n = jnp.abs(rel)
    nf = jnp.maximum(n, 1).astype(F32)
    large = max_exact + (jnp.log(nf / max_exact) / math.log(MAX_DISTANCE / max_exact)
                         * (nb - max_exact)).astype(I32)
    large = jnp.minimum(large, nb - 1) & (NUM_BUCKETS - 1)
    bucket = ret + jnp.where(n < max_exact, n, large)
    for h in range(DS_HEADS):
        far = rb_ref[nb - 1, h]
        acc = jnp.zeros((2 * BQ, BQ), F32)
        for b in range(NUM_BUCKETS):
            acc = jnp.where(bucket == b, rb_ref[b, h] - far, acc)
        o_ref[SK:SK + 2 * BQ, h * BQ:(h + 1) * BQ] = acc * LOG2E


def _biastab(rel_bias):
    return pl.pallas_call(
        _biastab_kernel,
        in_specs=[pl.BlockSpec(memory_space=pltpu.SMEM)],
        out_specs=pl.BlockSpec(memory_space=pltpu.VMEM),
        out_shape=jax.ShapeDtypeStruct((BIAS_ROWS, DS_HEADS * BQ), F32),
        compiler_params=pltpu.CompilerParams(vmem_limit_bytes=VMEM_LIMIT),
        name="biastab",
    )(rel_bias.astype(F32))


def _fold_rows(x, op):
    n = x.shape[0] // SUBLANES
    accs = [x[j * SUBLANES:(j + 1) * SUBLANES] for j in range(min(4, n))]
    for j in range(4, n):
        accs[j % 4] = op(accs[j % 4], x[j * SUBLANES:(j + 1) * SUBLANES])
    while len(accs) > 1:
        accs = [op(accs[j], accs[j + 1]) for j in range(0, len(accs) - 1, 2)] + (
            [accs[-1]] if len(accs) % 2 else [])
    return accs[0]


def _dsa_kernel(ql_ref, kx_ref, gds_ref, kid_ref, ckv_ref, ckvt_ref, qg_ref, wq_ref, wuk_ref, wuvt_ref,
                tab_ref, o_ref,
                sc_ref, pl_ref, cand_ref, lgt_ref, acc_ref, m_ref, qat_ref, qit_ref, w_ref, out_ref, *, topk):
    i = pl.program_id(1)
    q0 = i * QB
    ntile = jnp.right_shift(q0 + (QB + SK - 1), SK.bit_length() - 1)
    R = DS_KV_RANK
    GW = DS_HEADS * QB

    ql = ql_ref[...]
    ms = jnp.mean(ql * ql, axis=-1, keepdims=True)
    qn = (ql * lax.rsqrt(ms + NORM_EPS) * qg_ref[...]).astype(BF16)
    qt = _dot(qn, wq_ref[...]).T
    for h in range(DS_HEADS):
        qh = qt[h * DS_HEAD:(h + 1) * DS_HEAD, :].astype(BF16)
        qat_ref[:, h * QB:(h + 1) * QB] = (_dot(wuk_ref[h], qh) * (DS_HEAD ** -0.5 * LOG2E)).astype(BF16)
    for pr in range(IDX_HEADS // 2):
        base = DS_WIDTH + 2 * pr * IDX_DIM
        qit_ref[pr] = jnp.concatenate([qt[base:base + IDX_DIM, :],
                                       qt[base + IDX_DIM:base + 2 * IDX_DIM, :]], axis=1).astype(BF16)
    w_ref[...] = kx_ref[...].T[IDX_DIM:IDX_DIM + IDX_HEADS, :] * (IDX_HEADS ** -0.5 * IDX_DIM ** -0.5)

    lanei = lax.broadcasted_iota(I32, (1, QB), 1)
    csh = CHUNK.bit_length() - 1
    limit = jnp.left_shift(jnp.right_shift(q0 + lanei, csh) + 1, csh)
    rowi = lax.broadcasted_iota(I32, (SK, QB), 0)
    rowh = lax.broadcasted_iota(I32, (QB, QB), 0)
    nfull = jnp.right_shift(q0 + QB, SK.bit_length() - 1)
    nhalf = jnp.right_shift(q0 + QB, QB.bit_length() - 1)

    def score_rows(off, nrows):
        kid = kid_ref[pl.ds(off, nrows), :]
        s = jnp.zeros((nrows, QB), F32)
        for pr in range(IDX_HEADS // 2):
            lg = _dot(kid, qit_ref[pr])
            s = s + w_ref[2 * pr:2 * pr + 1, :] * jnp.maximum(lg[:, 0:QB], 0.0)
            s = s + w_ref[2 * pr + 1:2 * pr + 2, :] * jnp.maximum(lg[:, QB:2 * QB], 0.0)
        adm = (off + (rowi if nrows == SK else rowh)) < limit
        sc_ref[pl.ds(off, nrows), :] = jnp.where(adm, s, -jnp.inf)

    def score_tile(kt, carry):
        score_rows(pl.multiple_of(kt * SK, SK), SK)
        return carry

    lax.fori_loop(0, nfull, score_tile, 0)

    @pl.when(nfull < ntile)
    def _():
        score_rows(pl.multiple_of(nfull * SK, SK), QB)

    def key_to_f32(key):
        return pltpu.bitcast(jnp.where(key < 0, key ^ 0x7FFFFFFF, key), F32)

    def count(pred):
        def body(kt, acc):
            off = pl.multiple_of(kt * QB, QB)
            sc = sc_ref[pl.ds(off, QB), :]
            return acc + _fold_rows(jnp.where(pred(sc, off), 1, 0).astype(I32), jnp.add)
        acc = lax.fori_loop(0, nhalf, body, jnp.zeros((SUBLANES, QB), I32))
        return jnp.sum(acc, axis=0, keepdims=True)

    def bit_step(it, carry):
        lo, cnt_lo = carry
        cand = lo + jnp.left_shift(jnp.int32(1), 31 - it)
        cf = key_to_f32(cand)
        cnt = count(lambda sc, off: sc >= cf)
        take = cnt >= topk
        return jnp.where(take, cand, lo), jnp.where(take, cnt, cnt_lo)

    def all_settled(cnt_lo):
        return jnp.min(jnp.where((cnt_lo == topk) | (limit < topk), 1, 0))

    def slow_search():
        lo, cnt_lo = lax.fori_loop(0, SEARCH_MIN_BITS, bit_step,
                                   (jnp.full((1, QB), INT_MIN, I32), jnp.full((1, QB), 2 ** 30, I32)))

        def more_bits(c):
            it, lo, cnt_lo, _ = c
            lo, cnt_lo = lax.fori_loop(it, it + SEARCH_STEP_BITS, bit_step, (lo, cnt_lo))
            return it + SEARCH_STEP_BITS, lo, cnt_lo, all_settled(cnt_lo)

        _, lo, _, settled = lax.while_loop(lambda c: (c[0] < 32) & (c[3] == 0), more_bits,
                                           (jnp.int32(SEARCH_MIN_BITS), lo, cnt_lo, all_settled(cnt_lo)))
        return lo, settled

    ngrp_max = sc_ref.shape[0] // PLANE_ROWS
    ngrp = jnp.right_shift(q0 + QB, PLANE_ROWS.bit_length() - 1)

    @pl.when(i == 0)
    def _():
        pl_ref[...] = jnp.zeros_like(pl_ref)

    cand_ref[...] = jnp.zeros_like(cand_ref)

    def to_planes(g, c):
        base = pl.multiple_of(g * PLANE_ROWS, PLANE_ROWS)
        words = []
        for j in range(32):
            bits = pltpu.bitcast(sc_ref[pl.ds(base + SUBLANES * j, SUBLANES), :], I32)
            words.append(jnp.where(bits < 0, ~bits, bits | INT_MIN))
        j, m = 16, 0x0000FFFF
        while j:
            k = 0
            while k < 32:
                t = (words[k] ^ lax.shift_right_logical(words[k + j], j)) & m
                words[k] = words[k] ^ t
                words[k + j] = words[k + j] ^ lax.shift_left(t, j)
                k = (k + j + 1) & ~j
            j >>= 1
            m = (m ^ (m << j)) & 0xFFFFFFFF
        for b in range(32):
            pl_ref[pl.ds(base + SUBLANES * b, SUBLANES), :] = words[b]
        cand_ref[pl.ds(pl.multiple_of(g * SUBLANES, SUBLANES), SUBLANES), :] = jnp.full((SUBLANES, QB), -1, I32)
        return c

    lax.fori_loop(0, ngrp, to_planes, 0)

    def radix_select(ng):
        def radix_step(it, c):
            above, tb = c
            prow = pl.multiple_of(it * SUBLANES, SUBLANES)
            acc = jnp.zeros((SUBLANES, QB), I32)
            for g in range(ng):
                plane = pl_ref[pl.ds(g * PLANE_ROWS + prow, SUBLANES), :]
                acc = acc + lax.population_count(cand_ref[g * SUBLANES:(g + 1) * SUBLANES, :] & plane)
            ones = jnp.sum(acc, axis=0, keepdims=True)
            take = above + ones >= topk
            for g in range(ng):
                plane = pl_ref[pl.ds(g * PLANE_ROWS + prow, SUBLANES), :]
                cand = cand_ref[g * SUBLANES:(g + 1) * SUBLANES, :]
                cand_ref[g * SUBLANES:(g + 1) * SUBLANES, :] = jnp.where(take, cand & plane, cand & ~plane)
            return (jnp.where(take, above, above + ones),
                    jnp.where(take, tb | lax.shift_left(jnp.int32(1), 31 - it), tb))

        return lambda: lax.fori_loop(0, 32, radix_step,
                                     (jnp.zeros((1, QB), I32), jnp.zeros((1, QB), I32)))[1]

    tbits = lax.cond(ngrp <= ngrp_max // 2, radix_select(ngrp_max // 2), radix_select(ngrp_max))
    lo_fast = tbits ^ INT_MIN
    thr_fast = key_to_f32(jnp.maximum(lo_fast, KEY_NEG_INF))
    fast_ok = all_settled(count(lambda sc, off: sc >= thr_fast))
    lo, settled = lax.cond(fast_ok == 1, lambda: (lo_fast, fast_ok), slow_search)
    has_thr = lo > KEY_NEG_INF
    thr = key_to_f32(jnp.maximum(lo, KEY_NEG_INF))
    nbits = max(1, (sc_ref.shape[0] - 1).bit_length())
    take_all_ties = jnp.full((1, QB), sc_ref.shape[0], I32)

    def tie_cut():
        cnt_gt = count(lambda sc, off: sc > thr)
        cnt_eq = count(lambda sc, off: sc == thr)
        tied = (cnt_gt + cnt_eq > topk) & has_thr

        def search_cut():
            def idx_step(it, m):
                cand = m + jnp.left_shift(jnp.int32(1), nbits - 1 - it)
                cnt = cnt_gt + count(lambda sc, off: (sc == thr) & ((off + rowh) < cand))
                return jnp.where(cnt < topk, cand, m)
            return lax.fori_loop(0, nbits, idx_step, jnp.zeros((1, QB), I32))

        return lax.cond(jnp.max(jnp.where(tied, 1, 0)) > 0, search_cut, lambda: take_all_ties)

    cut = lax.cond(settled == 1, lambda: take_all_ties, tie_cut)
    cut = jnp.where(has_thr, cut, -1)

    def selection_mask(off, nrows):
        sc = sc_ref[pl.ds(off, nrows), :]
        ri = rowi if nrows == SK else rowh
        sel = (sc > thr) | ((sc == thr) & ((off + ri) <= cut))
        return jnp.where(sel, 0.0, MASK_NEG).astype(F32)

    near_lo = q0 - BQ

    def attend(off, bias_off, first=False, nrows=SK):
        rows = pl.ds(off, nrows)
        s = _dot(ckv_ref[rows, :], qat_ref[...])
        mk = selection_mask(off, nrows)
        m_old = jnp.full((1, GW), MASK_NEG, F32) if first else m_ref[...]
        tmax = []
        for h in range(DS_HEADS):
            cs = slice(h * QB, (h + 1) * QB)
            t = s[:, cs] + mk
            if bias_off is not None:
                t = t + jnp.concatenate(
                    [tab_ref[pl.ds(pl.multiple_of(jnp.maximum(bias_off - j * BQ, 0), BQ), nrows),
                             h * BQ:(h + 1) * BQ] for j in range(QB // BQ)], axis=1)
            lgt_ref[0:nrows, cs] = t
            tmax.append(jnp.max(_fold_rows(t, jnp.maximum), axis=0, keepdims=True))
        m_new = jnp.maximum(m_old, jnp.concatenate(tmax, axis=1))
        m_ref[...] = m_new
        pr = jnp.exp2(lgt_ref[0:nrows, :] - m_new).astype(BF16)
        pv = _dot(ckvt_ref[:, rows], pr)
        acc_ref[...] = pv if first else acc_ref[...] * jnp.exp2(m_old - m_new) + pv

    def far_tile(kt, c):
        attend(pl.multiple_of(kt * SK, SK), None)
        return c

    def edge_tile(kt, c):
        off = pl.multiple_of(kt * SK, SK)
        attend(off, SK + off - near_lo)
        return c

    @pl.when(nfull >= 1)
    def _():
        attend(0, SK - near_lo, first=True)
        lax.fori_loop(1, nfull - 1, far_tile, 0)
        lax.fori_loop(jnp.maximum(nfull - 1, 1), nfull, edge_tile, 0)

        @pl.when(nfull < ntile)
        def _():
            off = pl.multiple_of(nfull * SK, SK)
            attend(off, SK + off - near_lo, nrows=QB)

    @pl.when(nfull == 0)
    def _():
        attend(0, SK - near_lo, first=True, nrows=QB)

    o_lat = acc_ref[0:R, :] * (1.0 / acc_ref[R:R + 1, :])
    for h in range(DS_HEADS):
        out_ref[h * DS_HEAD:(h + 1) * DS_HEAD, :] = _dot(
            wuvt_ref[h], o_lat[:, h * QB:(h + 1) * QB].astype(BF16))

    g = gds_ref[...]
    o_ref[...] = (out_ref[...].T * (g * _sigmoid(g))).astype(BF16)


def _dsa(z, ckv, kid, B, T, q_norm_g, w_uq, w_uk, w_uv, iw_q, tab):
    nq = T // QB
    topk = min(TOPK_MAX, T // 4)
    R = DS_KV_RANK
    wq = jnp.concatenate([w_uq, iw_q], axis=1).astype(BF16)
    wuk_h = jnp.transpose(w_uk, (1, 0, 2)).astype(BF16)
    wuv_t = jnp.transpose(w_uv, (1, 2, 0)).astype(BF16)
    ckv_t = jnp.concatenate([jnp.swapaxes(ckv.reshape(B, T, R), 1, 2),
                             jnp.ones((B, ONES_ROWS, T), BF16)], axis=1).reshape(B * (R + ONES_ROWS), T)
    const2 = lambda b, i: (0, 0)
    const3 = lambda b, i: (0, 0, 0)
    resident = dict(pipeline_mode=pl.Buffered(1))
    return pl.pallas_call(
        functools.partial(_dsa_kernel, topk=topk),
        grid=(B, nq),
        in_specs=[pl.BlockSpec((QB, DS_Q_RANK), lambda b, i: (b * nq + i, COL_Q // DS_Q_RANK)),
                  pl.BlockSpec((QB, LANES), lambda b, i: (b * nq + i, COL_KX // LANES)),
                  pl.BlockSpec((QB, DS_WIDTH), lambda b, i: (b * nq + i, COL_GDS // DS_WIDTH)),
                  pl.BlockSpec((T, IDX_DIM), lambda b, i: (b, 0), **resident),
                  pl.BlockSpec((T, R), lambda b, i: (b, 0), **resident),
                  pl.BlockSpec((R + ONES_ROWS, T), lambda b, i: (b, 0), **resident),
                  pl.BlockSpec((1, DS_Q_RANK), const2),
                  pl.BlockSpec((DS_Q_RANK, 2 * DS_WIDTH), const2, **resident),
                  pl.BlockSpec((DS_HEADS, R, DS_HEAD), const3, **resident),
                  pl.BlockSpec((DS_HEADS, DS_HEAD, R), const3, **resident),
                  pl.BlockSpec((BIAS_ROWS, DS_HEADS * BQ), const2, **resident)],
        out_specs=pl.BlockSpec((QB, DS_WIDTH), lambda b, i: (b * nq + i, 0)),
        out_shape=jax.ShapeDtypeStruct((B * T, DS_WIDTH), BF16),
        scratch_shapes=[pltpu.VMEM((T, QB), F32),
                        pltpu.VMEM((T, QB), I32),
                        pltpu.VMEM((T // PLANE_ROWS * SUBLANES, QB), I32),
                        pltpu.VMEM((SK, DS_HEADS * QB), F32),
                        pltpu.VMEM((R + ONES_ROWS, DS_HEADS * QB), F32),
                        pltpu.VMEM((1, DS_HEADS * QB), F32),
                        pltpu.VMEM((R, DS_HEADS * QB), BF16),
                        pltpu.VMEM((IDX_HEADS // 2, IDX_DIM, 2 * QB), BF16),
                        pltpu.VMEM((IDX_HEADS, QB), F32),
                        pltpu.VMEM((DS_WIDTH, QB), F32)],
        compiler_params=pltpu.CompilerParams(dimension_semantics=("parallel", "arbitrary"),
                                             vmem_limit_bytes=VMEM_LIMIT),
        name="dsa",
    )(z, z, z, kid, ckv, ckv_t, q_norm_g.reshape(1, -1).astype(F32), wq, wuk_h, wuv_t, tab)


def _post_kernel(x_ref, a1_ref, a2_ref, p_ref, w1_ref, w2_ref, pw_ref, gw_ref, fg_ref, o_ref):
    h = x_ref[...] + _dot(a1_ref[...], w1_ref[...]) + _dot(a2_ref[...], w2_ref[...])
    e = _dot(p_ref[...].astype(BF16), pw_ref[...])
    gate = _sigmoid(_dot(h.astype(BF16), gw_ref[...]))
    h2 = h + e * gate
    ms = jnp.mean(h2 * h2, axis=-1, keepdims=True)
    o_ref[...] = h2 * lax.rsqrt(ms + NORM_EPS) * fg_ref[...]


def _post(x2, o_rw, o_ds, p2, w_out, ple_w, gate_w, final_g, tm=512):
    m, d = x2.shape
    kh = o_rw.shape[1]
    pd = p2.shape[1]
    resident = dict(pipeline_mode=pl.Buffered(1))
    return pl.pallas_call(
        _post_kernel,
        grid=(m // tm,),
        in_specs=[pl.BlockSpec((tm, d), lambda i: (i, 0)),
                  pl.BlockSpec((tm, kh), lambda i: (i, 0)),
                  pl.BlockSpec((tm, kh), lambda i: (i, 0)),
                  pl.BlockSpec((tm, pd), lambda i: (i, 0)),
                  pl.BlockSpec((kh, d), lambda i: (0, 0), **resident),
                  pl.BlockSpec((kh, d), lambda i: (1, 0), **resident),
                  pl.BlockSpec((pd, d), lambda i: (0, 0), **resident),
                  pl.BlockSpec((d, d), lambda i: (0, 0), **resident),
                  pl.BlockSpec((1, d), lambda i: (0, 0))],
        out_specs=pl.BlockSpec((tm, d), lambda i: (i, 0)),
        out_shape=jax.ShapeDtypeStruct((m, d), F32),
        compiler_params=pltpu.CompilerParams(dimension_semantics=("parallel",),
                                             vmem_limit_bytes=VMEM_LIMIT),
        name="post",
    )(x2, o_rw, o_ds, p2, w_out, w_out, ple_w, gate_w, final_g)


def _split_w_in_kernel(w_ref, o_ref, og_ref):
    n = w_ref.shape[1]
    o_ref[...] = w_ref[...].astype(BF16)
    og_ref[...] = w_ref[:, n - DS_WIDTH:].astype(BF16)


def _split_w_in(w_in, layer, tr=256):
    _, d, n = w_in.shape
    return pl.pallas_call(
        _split_w_in_kernel,
        grid=(d // tr,),
        in_specs=[pl.BlockSpec((None, tr, n), lambda r: (layer, r, 0))],
        out_specs=[pl.BlockSpec((tr, n), lambda r: (r, 0)),
                   pl.BlockSpec((tr, DS_WIDTH), lambda r: (r, 0))],
        out_shape=[jax.ShapeDtypeStruct((d, n), BF16),
                   jax.ShapeDtypeStruct((d, DS_WIDTH), BF16)],
        compiler_params=pltpu.CompilerParams(dimension_semantics=("parallel",),
                                             vmem_limit_bytes=VMEM_LIMIT),
        name="castw",
    )(w_in)


def kernel(x, p, w_in, norm_g, rw_mu, rw_w0, rw_w_up, rw_a0, rw_a_up, rw_k_k, rw_k_a, rw_r_k, rw_ln_g, rw_ln_b, ds_q_norm_g, ds_kv_norm_g, idx_k_norm_g, ds_w_uq, ds_w_uk, ds_w_uv, idx_w_q, rel_bias, w_out, ple_w, ple_gate_w, final_g):
    B, T, D = x.shape
    depth = w_in.shape[0]
    assert depth == 1 and T % SK == 0 and SK == 2 * QB and T % RW_CHUNK == 0 and (B * T) % 512 == 0 and D % 256 == 0
    h = x.reshape(B * T, D)
    tab = _biastab(rel_bias)
    for i in range(depth):
        z, ckv, kid = _inproj(h, norm_g[i].reshape(1, D), *_split_w_in(w_in, i),
                              ds_kv_norm_g[i].reshape(1, -1), idx_k_norm_g[i].reshape(1, -1))
        o_rw = _rwkv(z, B, T, rw_mu[i], rw_w0[i], rw_a0[i], rw_k_k[i], rw_k_a[i],
                     rw_r_k[i].reshape(-1), rw_ln_g[i], rw_ln_b[i], rw_w_up[i], rw_a_up[i])
        o_ds = _dsa(z, ckv, kid, B, T, ds_q_norm_g[i], ds_w_uq[i], ds_w_uk[i], ds_w_uv[i],
                    idx_w_q[i], tab)
        h = _post(h, o_rw, o_ds, p[i].reshape(B * T, -1), w_out[i].astype(BF16),
                  ple_w[i].astype(BF16), ple_gate_w[i].astype(BF16), final_g.reshape(1, D))
    return h.reshape(B, T, D)
```

```python
import functools
import math

import jax
import jax.numpy as jnp
from jax import lax
from jax.experimental import pallas as pl
from jax.experimental.pallas import tpu as pltpu

F32 = jnp.float32
BF16 = jnp.bfloat16
I32 = jnp.int32

RW_WIDTH = 1024
RW_HEAD = 64
RW_HEADS = 16
RW_LORA = 64
DS_WIDTH = 1024
DS_HEAD = 64
DS_HEADS = 16
DS_Q_RANK = 384
DS_KV_RANK = 256
IDX_HEADS = 16
IDX_DIM = 64
TOPK_MAX = 256
CHUNK = 64
NUM_BUCKETS = 32
MAX_DISTANCE = 128
NORM_EPS = 1e-6
GN_EPS = 64e-5

COL_R, COL_K, COL_V, COL_GRW, COL_GDS = 0, 1024, 2048, 3072, 4096
COL_KV = 5120
COL_Q = 5376
COL_WA = 5760
COL_KX = 5888
Z_WIDTH = 6016

LANES = 128
SUBLANES = 8
QB = 256
BQ = 128
SK = 512
BIAS_ROWS = 2 * SK + 2 * BQ
RW_CHUNK = 64
INT_MIN = -2 ** 31
KEY_NEG_INF = -2139095041
PLANE_ROWS = 32 * SUBLANES
SEARCH_MIN_BITS = 23
SEARCH_STEP_BITS = 3
assert (32 - SEARCH_MIN_BITS) % SEARCH_STEP_BITS == 0
MASK_NEG = -1e30
LOG2E = 1.4426950408889634
ONES_ROWS = 16
VMEM_LIMIT = 52 * 1024 * 1024


def _sigmoid(x):
    return 1.0 / (1.0 + jnp.exp(-x))


def _dot(a, b):
    return jnp.dot(a, b, preferred_element_type=F32)


def _inproj_plan():
    s_wa = 3 * RW_WIDTH
    s_grw = s_wa + 2 * RW_LORA
    s_q = s_grw + RW_WIDTH
    s_kv = s_q + DS_Q_RANK
    s_kx = s_kv + DS_KV_RANK
    return [(COL_R, 0, 3 * RW_WIDTH), (COL_GRW, s_grw, RW_WIDTH), (COL_GDS, None, DS_WIDTH),
            (COL_KV, s_kv, DS_KV_RANK), (COL_Q, s_q, DS_Q_RANK), (COL_WA, s_wa, LANES),
            (COL_KX, s_kx, LANES)]


def _inproj_kernel(x_ref, g_ref, w_ref, wg_ref, gkv_ref, gik_ref, o_ref, ckv_ref, kid_ref, *, tn):
    x = x_ref[...]
    ms = jnp.mean(x * x, axis=-1, keepdims=True)
    xn = (x * lax.rsqrt(ms + NORM_EPS) * g_ref[...]).astype(BF16)
    for dst, src, width in _inproj_plan():
        for j in range(0, width, tn):
            w = min(tn, width - j)
            wt = wg_ref[:, j:j + w] if src is None else w_ref[:, src + j:src + j + w]
            o_ref[:, dst + j:dst + j + w] = _dot(xn, wt)
    kv = o_ref[:, COL_KV:COL_KV + DS_KV_RANK]
    ckv_ref[...] = (kv * lax.rsqrt(jnp.mean(kv * kv, axis=-1, keepdims=True) + NORM_EPS)
                    * gkv_ref[...]).astype(BF16)
    ki = o_ref[:, COL_KX:COL_KX + IDX_DIM]
    kid_ref[...] = (ki * lax.rsqrt(jnp.mean(ki * ki, axis=-1, keepdims=True) + NORM_EPS)
                    * gik_ref[...]).astype(BF16)


def _inproj(x2, g, w, w_gds, gkv, gik, tm=256, tn=768):
    m, d = x2.shape
    resident = dict(pipeline_mode=pl.Buffered(1))
    return pl.pallas_call(
        functools.partial(_inproj_kernel, tn=tn),
        grid=(m // tm,),
        in_specs=[pl.BlockSpec((tm, d), lambda i: (i, 0)),
                  pl.BlockSpec((1, d), lambda i: (0, 0)),
                  pl.BlockSpec(w.shape, lambda i: (0, 0), **resident),
                  pl.BlockSpec(w_gds.shape, lambda i: (0, 0), **resident),
                  pl.BlockSpec((1, DS_KV_RANK), lambda i: (0, 0)),
                  pl.BlockSpec((1, IDX_DIM), lambda i: (0, 0))],
        out_specs=[pl.BlockSpec((tm, Z_WIDTH), lambda i: (i, 0)),
                   pl.BlockSpec((tm, DS_KV_RANK), lambda i: (i, 0)),
                   pl.BlockSpec((tm, IDX_DIM), lambda i: (i, 0))],
        out_shape=[jax.ShapeDtypeStruct((m, Z_WIDTH), F32),
                   jax.ShapeDtypeStruct((m, DS_KV_RANK), BF16),
                   jax.ShapeDtypeStruct((m, IDX_DIM), BF16)],
        compiler_params=pltpu.CompilerParams(dimension_semantics=("parallel",),
                                             vmem_limit_bytes=VMEM_LIMIT),
        name="inproj",
    )(x2, g, w, w_gds, gkv, gik)


def _rwkv_kernel(r_ref, k_ref, v_ref, g_ref, wa_ref,
                 mur_ref, muk_ref, muv_ref, muwa_ref,
                 w0_ref, a0_ref, kk_ref, ka_ref, rk_ref, lng_ref, lnb_ref,
                 wup_ref, aup_ref,
                 o_ref,
                 pr_ref, pk_ref, pv_ref, pwa_ref, st_ref, *, nb):
    C = RW_CHUNK
    N = RW_HEAD

    @pl.when(pl.program_id(0) == 0)
    def _():
        pr_ref[...] = jnp.zeros_like(pr_ref)
        pk_ref[...] = jnp.zeros_like(pk_ref)
        pv_ref[...] = jnp.zeros_like(pv_ref)
        pwa_ref[...] = jnp.zeros_like(pwa_ref)
        st_ref[...] = jnp.zeros_like(st_ref)

    row = lax.broadcasted_iota(I32, (SUBLANES, 1), 0)

    def shift(ref, prev_ref, mu_ref):
        z = ref[...].reshape(nb * C, ref.shape[2])
        zp = pltpu.roll(z, 1, 0)
        parts = []
        for b in range(nb):
            parts.append(jnp.where(row == 0, prev_ref[b], zp[b * C:b * C + SUBLANES]))
            parts.append(zp[b * C + SUBLANES:(b + 1) * C])
            prev_ref[b] = z[(b + 1) * C - 1:(b + 1) * C, :]
        zp = jnp.concatenate(parts, axis=0)
        return z + mu_ref[...] * (zp - z)

    r = shift(r_ref, pr_ref, mur_ref)
    k = shift(k_ref, pk_ref, muk_ref)
    v = shift(v_ref, pv_ref, muv_ref)
    wa = shift(wa_ref, pwa_ref, muwa_ref)
    wd = wa[:, 0:RW_LORA]
    ad = wa[:, RW_LORA:2 * RW_LORA]

    wl = w0_ref[...] + _dot(jnp.tanh(wd).astype(BF16), wup_ref[...])
    lw = -math.exp(-0.5) * _sigmoid(wl)
    a = _sigmoid(a0_ref[...] + _dot(ad.astype(BF16), aup_ref[...]))
    kk = k * kk_ref[...]
    k2 = k * (1.0 + (a - 1.0) * ka_ref[...])

    ti = lax.broadcasted_iota(I32, (C, C), 0)
    tj = lax.broadcasted_iota(I32, (C, C), 1)
    tri = jnp.where(ti >= tj, 1.0, 0.0).astype(F32)

    def per_row(mat, x):
        hi = x.astype(BF16)
        r1 = x - hi.astype(F32)
        mid = r1.astype(BF16)
        lo = (r1 - mid.astype(F32)).astype(BF16)
        mb = mat.astype(BF16)
        return jnp.concatenate(
            [_dot(mb, hi[b * C:(b + 1) * C]) + _dot(mb, mid[b * C:(b + 1) * C]) + _dot(mb, lo[b * C:(b + 1) * C])
             for b in range(nb)], axis=0)

    cum = per_row(tri, lw)
    p = jnp.exp(cum)
    pinv = jnp.exp(-cum)
    pprev = jnp.exp(cum - lw)
    tot = per_row(jnp.ones((C, C), F32), lw)
    pend = jnp.exp(tot)

    g = g_ref[...].reshape(nb * C, RW_WIDTH)
    gate = g * _sigmoid(g)

    HP = RW_HEADS // 2
    NP = nb * HP

    def pairs(x):
        return jnp.stack([x[b * C:(b + 1) * C, j * LANES:(j + 1) * LANES]
                          for b in range(nb) for j in range(HP)], axis=0)

    def per_pair(ref):
        return jnp.concatenate([ref[...]] * nb, axis=0)

    lane = lax.broadcasted_iota(I32, (1, 1, LANES), 2)
    m_lo = jnp.where(lane < N, 1.0, 0.0).astype(BF16)
    m_hi = jnp.where(lane < N, 0.0, 1.0).astype(BF16)
    bi = lax.broadcasted_iota(I32, (LANES, LANES), 0)
    bj = lax.broadcasted_iota(I32, (LANES, LANES), 1)
    same_head = (bi < N) == (bj < N)
    ones_bd = jnp.where(same_head, 1.0, 0.0).astype(BF16)

    def head_sum(x):
        return _dot(x.reshape(NP * C, LANES).astype(BF16), ones_bd).reshape(NP, C, LANES)

    def halves(x):
        xb = x.astype(BF16)
        return jnp.concatenate([xb * m_lo, xb * m_hi], axis=1)

    def bmm(x, y):
        return lax.dot_general(x, y, (((2,), (1,)), ((0,), (0,))), preferred_element_type=F32)

    def bmm_nt(x, y):
        return lax.dot_general(x, y, (((2,), (2,)), ((0,), (0,))), preferred_element_type=F32)

    def block_mask(nblk, cmp):
        wi = lax.broadcasted_iota(I32, (C, nblk * C), 0)
        wj = lax.broadcasted_iota(I32, (C, nblk * C), 1) & (C - 1)
        return cmp(wi, wj)

    r_p, k2_p, v_p, a_p = pairs(r), pairs(k2), pairs(v), pairs(a)
    p_p, pinv_p, pprev_p = pairs(p), pairs(pinv), pairs(pprev)
    kk_p = pairs(kk)
    kkn = kk_p * lax.rsqrt(jnp.maximum(head_sum(kk_p * kk_p), 1e-24))
    at = (-kkn) * pprev_p
    bt = (kkn * a_p) * pinv_p
    kt = k2_p * pinv_p
    rt = r_p * p_p
    pend_p = pairs(pend)
    pend2 = jnp.concatenate([pend_p, pend_p], axis=1)

    lhs2 = jnp.concatenate([at, rt], axis=1).astype(BF16)
    rhs4 = jnp.concatenate([halves(kt), halves(bt)], axis=1)
    gc = bmm_nt(lhs2, rhs4)
    strict2 = block_mask(2, lambda i_, j_: i_ > j_)
    incl4 = block_mask(4, lambda i_, j_: i_ >= j_)
    a_ak = jnp.where(strict2, gc[:, 0:C, 0:2 * C], 0.0)
    nmat = jnp.where(strict2, gc[:, 0:C, 2 * C:4 * C], 0.0)
    a_rkb = jnp.where(incl4, gc[:, C:2 * C, :], 0.0)

    g0 = st_ref[...]
    sg = bmm_nt(lhs2, g0.astype(BF16))
    vm2 = halves(v_p)
    u = sg[:, 0:C] + bmm(a_ak.astype(BF16), vm2)
    pw = nmat.astype(BF16)
    u = u + bmm(pw, halves(u))
    n = 1
    while 2 * n < C:
        pw = bmm(pw, halves(pw)).astype(BF16)
        u = u + bmm(pw, halves(u))
        n *= 2
    um2b = halves(u)
    y = sg[:, C:2 * C] + bmm(a_rkb.astype(BF16), jnp.concatenate([vm2, um2b], axis=1))
    uv = jnp.concatenate([u, v_p], axis=1).astype(BF16)
    bkh = (jnp.concatenate([bt, kt], axis=1) * pend2).astype(BF16)
    upd = lax.dot_general(uv, bkh, (((1,), (1,)), ((0,), (0,))), preferred_element_type=F32)
    st_ref[...] = g0 * pend2 + jnp.where(same_head, upd, 0.0)

    inv_n = 1.0 / N
    yc = y - head_sum(y) * inv_n
    var = head_sum(yc * yc) * inv_n
    yn = yc * lax.rsqrt(var + GN_EPS) * per_pair(lng_ref) + per_pair(lnb_ref)
    bonus = head_sum(r_p * k2_p * per_pair(rk_ref)) * v_p
    out = (yn + bonus) * pairs(gate)
    for b in range(nb):
        for j in range(HP):
            o_ref[b, :, j * LANES:(j + 1) * LANES] = out[b * HP + j].astype(BF16)


def _rwkv(z, B, T, mu, w0, a0, k_k, k_a, r_k, ln_g, ln_b, w_up, a_up):
    C = RW_CHUNK
    W = RW_WIDTH
    z3 = z.reshape(B, T, z.shape[1])
    row = lambda a: a.reshape(1, -1).astype(F32)
    mu_r, mu_k, mu_v = mu[0:W], mu[W:2 * W], mu[2 * W:3 * W]
    mu_wa = mu[3 * W:3 * W + 2 * RW_LORA]
    zspec = lambda col: pl.BlockSpec((B, C, W), lambda c: (0, c, col // W))
    pspec = lambda width: pl.BlockSpec((1, width), lambda c: (0, 0))
    npairs = RW_HEADS // 2
    prow = lambda a: a.reshape(npairs, 1, LANES).astype(F32)
    ppspec = pl.BlockSpec((npairs, 1, LANES), lambda c: (0, 0, 0))
    wspec = pl.BlockSpec((RW_LORA, W), lambda c: (0, 0))
    out = pl.pallas_call(
        functools.partial(_rwkv_kernel, nb=B),
        grid=(T // C,),
        in_specs=[zspec(COL_R), zspec(COL_K), zspec(COL_V), zspec(COL_GRW),
                  pl.BlockSpec((B, C, LANES), lambda c: (0, c, COL_WA // LANES)),
                  pspec(W), pspec(W), pspec(W), pspec(LANES),
                  pspec(W), pspec(W), pspec(W), pspec(W), ppspec, ppspec, ppspec,
                  wspec, wspec],
        out_specs=pl.BlockSpec((B, C, W), lambda c: (0, c, 0)),
        out_shape=jax.ShapeDtypeStruct((B, T, W), BF16),
        scratch_shapes=[pltpu.VMEM((B, 1, W), F32), pltpu.VMEM((B, 1, W), F32), pltpu.VMEM((B, 1, W), F32),
                        pltpu.VMEM((B, 1, LANES), F32),
                        pltpu.VMEM((B * npairs, LANES, LANES), F32)],
        compiler_params=pltpu.CompilerParams(dimension_semantics=("arbitrary",),
                                             vmem_limit_bytes=VMEM_LIMIT),
        name="rwkv",
    )(z3, z3, z3, z3, z3,
      row(mu_r), row(mu_k), row(mu_v), row(mu_wa),
      row(w0), row(a0), row(k_k), row(k_a), prow(r_k), prow(ln_g), prow(ln_b),
      w_up.astype(BF16), a_up.astype(BF16))
    return out.reshape(B * T, W)


def _biastab_kernel(rb_ref, o_ref):
    o_ref[...] = jnp.zeros_like(o_ref)
    nb = NUM_BUCKETS // 2
    max_exact = nb // 2
    c = lax.broadcasted_iota(I32, (2 * BQ, BQ), 0)
    r = lax.broadcasted_iota(I32, (2 * BQ, BQ), 1)
    rel = c - BQ - r
    ret = jnp.where(rel > 0, nb, 0)
    n = jnp.abs(rel)
    nf = jnp.maximum(n, 1).astype(F32)
    large = max_exact + (jnp.log(nf / max_exact) / math.log(MAX_DISTANCE / max_exact)
                         * (nb - max_exact)).astype(I32)
    large = jnp.minimum(large, nb - 1) & (NUM_BUCKETS - 1)
    bucket = ret + jnp.where(n < max_exact, n, large)
    for h in range(DS_HEADS):
        far = rb_ref[nb - 1, h]
        acc = jnp.zeros((2 * BQ, BQ), F32)
        for b in range(NUM_BUCKETS):
            acc = jnp.where(bucket == b, rb_ref[b, h] - far, acc)
        o_ref[SK:SK + 2 * BQ, h * BQ:(h + 1) * BQ] = acc * LOG2E


def _biastab(rel_bias):
    return pl.pallas_call(
        _biastab_kernel,
        in_specs=[pl.BlockSpec(memory_space=pltpu.SMEM)],
        out_specs=pl.BlockSpec(memory_space=pltpu.VMEM),
        out_shape=jax.ShapeDtypeStruct((BIAS_ROWS, DS_HEADS * BQ), F32),
        compiler_params=pltpu.CompilerParams(vmem_limit_bytes=VMEM_LIMIT),
        name="biastab",
    )(rel_bias.astype(F32))


def _fold_rows(x, op):
    n = x.shape[0] // SUBLANES
    accs = [x[j * SUBLANES:(j + 1) * SUBLANES] for j in range(min(4, n))]
    for j in range(4, n):
        accs[j % 4] = op(accs[j % 4], x[j * SUBLANES:(j + 1) * SUBLANES])
    while len(accs) > 1:
        accs = [op(accs[j], accs[j + 1]) for j in range(0, len(accs) - 1, 2)] + (
            [accs[-1]] if len(accs) % 2 else [])
    return accs[0]


def _dsa_kernel(ql_ref, kx_ref, gds_ref, kid_ref, ckv_ref, ckvt_ref, qg_ref, wq_ref, wuk_ref, wuvt_ref,
                tab_ref, o_ref,
                sc_ref, pl_ref, cand_ref, lgt_ref, acc_ref, m_ref, qat_ref, qit_ref, w_ref, out_ref, *, topk):
    i = pl.program_id(1)
    q0 = i * QB
    ntile = jnp.right_shift(q0 + (QB + SK - 1), SK.bit_length() - 1)
    R = DS_KV_RANK
    GW = DS_HEADS * QB

    ql = ql_ref[...]
    ms = jnp.mean(ql * ql, axis=-1, keepdims=True)
    qn = (ql * lax.rsqrt(ms + NORM_EPS) * qg_ref[...]).astype(BF16)
    qt = _dot(qn, wq_ref[...]).T
    for h in range(DS_HEADS):
        qh = qt[h * DS_HEAD:(h + 1) * DS_HEAD, :].astype(BF16)
        qat_ref[:, h * QB:(h + 1) * QB] = (_dot(wuk_ref[h], qh) * (DS_HEAD ** -0.5 * LOG2E)).astype(BF16)
    for pr in range(IDX_HEADS // 2):
        base = DS_WIDTH + 2 * pr * IDX_DIM
        qit_ref[pr] = jnp.concatenate([qt[base:base + IDX_DIM, :],
                                       qt[base + IDX_DIM:base + 2 * IDX_DIM, :]], axis=1).astype(BF16)
    w_ref[...] = kx_ref[...].T[IDX_DIM:IDX_DIM + IDX_HEADS, :] * (IDX_HEADS ** -0.5 * IDX_DIM ** -0.5)

    lanei = lax.broadcasted_iota(I32, (1, QB), 1)
    csh = CHUNK.bit_length() - 1
    limit = jnp.left_shift(jnp.right_shift(q0 + lanei, csh) + 1, csh)
    rowi = lax.broadcasted_iota(I32, (SK, QB), 0)
    rowh = lax.broadcasted_iota(I32, (QB, QB), 0)
    nfull = jnp.right_shift(q0 + QB, SK.bit_length() - 1)
    nhalf = jnp.right_shift(q0 + QB, QB.bit_length() - 1)

    def score_rows(off, nrows):
        kid = kid_ref[pl.ds(off, nrows), :]
        s = jnp.zeros((nrows, QB), F32)
        for pr in range(IDX_HEADS // 2):
            lg = _dot(kid, qit_ref[pr])
            s = s + w_ref[2 * pr:2 * pr + 1, :] * jnp.maximum(lg[:, 0:QB], 0.0)
            s = s + w_ref[2 * pr + 1:2 * pr + 2, :] * jnp.maximum(lg[:, QB:2 * QB], 0.0)
        adm = (off + (rowi if nrows == SK else rowh)) < limit
        sc_ref[pl.ds(off, nrows), :] = jnp.where(adm, s, -jnp.inf)

    def score_tile(kt, carry):
        score_rows(pl.multiple_of(kt * SK, SK), SK)
        return carry

    lax.fori_loop(0, nfull, score_tile, 0)

    @pl.when(nfull < ntile)
    def _():
        score_rows(pl.multiple_of(nfull * SK, SK), QB)

    def key_to_f32(key):
        return pltpu.bitcast(jnp.where(key < 0, key ^ 0x7FFFFFFF, key), F32)

    def count(pred):
        def body(kt, acc):
            off = pl.multiple_of(kt * QB, QB)
            sc = sc_ref[pl.ds(off, QB), :]
            return acc + _fold_rows(jnp.where(pred(sc, off), 1, 0).astype(I32), jnp.add)
        acc = lax.fori_loop(0, nhalf, body, jnp.zeros((SUBLANES, QB), I32))
        return jnp.sum(acc, axis=0, keepdims=True)

    def bit_step(it, carry):
        lo, cnt_lo = carry
        cand = lo + jnp.left_shift(jnp.int32(1), 31 - it)
        cf = key_to_f32(cand)
        cnt = count(lambda sc, off: sc >= cf)
        take = cnt >= topk
        return jnp.where(take, cand, lo), jnp.where(take, cnt, cnt_lo)

    def all_settled(cnt_lo):
        return jnp.min(jnp.where((cnt_lo == topk) | (limit < topk), 1, 0))

    def slow_search():
        lo, cnt_lo = lax.fori_loop(0, SEARCH_MIN_BITS, bit_step,
                                   (jnp.full((1, QB), INT_MIN, I32), jnp.full((1, QB), 2 ** 30, I32)))

        def more_bits(c):
            it, lo, cnt_lo, _ = c
            lo, cnt_lo = lax.fori_loop(it, it + SEARCH_STEP_BITS, bit_step, (lo, cnt_lo))
            return it + SEARCH_STEP_BITS, lo, cnt_lo, all_settled(cnt_lo)

        _, lo, _, settled = lax.while_loop(lambda c: (c[0] < 32) & (c[3] == 0), more_bits,
                                           (jnp.int32(SEARCH_MIN_BITS), lo, cnt_lo, all_settled(cnt_lo)))
        return lo, settled

    ngrp_max = sc_ref.shape[0] // PLANE_ROWS
    ngrp = jnp.right_shift(q0 + QB, PLANE_ROWS.bit_length() - 1)

    @pl.when(i == 0)
    def _():
        pl_ref[...] = jnp.zeros_like(pl_ref)

    cand_ref[...] = jnp.zeros_like(cand_ref)

    def to_planes(g, c):
        base = pl.multiple_of(g * PLANE_ROWS, PLANE_ROWS)
        words = []
        for j in range(32):
            bits = pltpu.bitcast(sc_ref[pl.ds(base + SUBLANES * j, SUBLANES), :], I32)
            words.append(jnp.where(bits < 0, ~bits, bits | INT_MIN))
        j, m = 16, 0x0000FFFF
        while j:
            k = 0
            while k < 32:
                t = (words[k] ^ lax.shift_right_logical(words[k + j], j)) & m
                words[k] = words[k] ^ t
                words[k + j] = words[k + j] ^ lax.shift_left(t, j)
                k = (k + j + 1) & ~j
            j >>= 1
            m = (m ^ (m << j)) & 0xFFFFFFFF
        for b in range(32):
            pl_ref[pl.ds(base + SUBLANES * b, SUBLANES), :] = words[b]
        cand_ref[pl.ds(pl.multiple_of(g * SUBLANES, SUBLANES), SUBLANES), :] = jnp.full((SUBLANES, QB), -1, I32)
        return c

    lax.fori_loop(0, ngrp, to_planes, 0)

    def radix_select(ng):
        def radix_step(it, c):
            above, tb = c
            prow = pl.multiple_of(it * SUBLANES, SUBLANES)
            acc = jnp.zeros((SUBLANES, QB), I32)
            for g in range(ng):
                plane = pl_ref[pl.ds(g * PLANE_ROWS + prow, SUBLANES), :]
                acc = acc + lax.population_count(cand_ref[g * SUBLANES:(g + 1) * SUBLANES, :] & plane)
            ones = jnp.sum(acc, axis=0, keepdims=True)
            take = above + ones >= topk
            for g in range(ng):
                plane = pl_ref[pl.ds(g * PLANE_ROWS + prow, SUBLANES), :]
                cand = cand_ref[g * SUBLANES:(g + 1) * SUBLANES, :]
                cand_ref[g * SUBLANES:(g + 1) * SUBLANES, :] = jnp.where(take, cand & plane, cand & ~plane)
            return (jnp.where(take, above, above + ones),
                    jnp.where(take, tb | lax.shift_left(jnp.int32(1), 31 - it), tb))

        return lambda: lax.fori_loop(0, 32, radix_step,
                                     (jnp.zeros((1, QB), I32), jnp.zeros((1, QB), I32)))[1]

    tbits = lax.cond(ngrp <= ngrp_max // 2, radix_select(ngrp_max // 2), radix_select(ngrp_max))
    lo_fast = tbits ^ INT_MIN
    thr_fast = key_to_f32(jnp.maximum(lo_fast, KEY_NEG_INF))
    fast_ok = all_settled(count(lambda sc, off: sc >= thr_fast))
    lo, settled = lax.cond(fast_ok == 1, lambda: (lo_fast, fast_ok), slow_search)
    has_thr = lo > KEY_NEG_INF
    thr = key_to_f32(jnp.maximum(lo, KEY_NEG_INF))
    nbits = max(1, (sc_ref.shape[0] - 1).bit_length())
    take_all_ties = jnp.full((1, QB), sc_ref.shape[0], I32)

    def tie_cut():
        cnt_gt = count(lambda sc, off: sc > thr)
        cnt_eq = count(lambda sc, off: sc == thr)
        tied = (cnt_gt + cnt_eq > topk) & has_thr

        def search_cut():
            def idx_step(it, m):
                cand = m + jnp.left_shift(jnp.int32(1), nbits - 1 - it)
                cnt = cnt_gt + count(lambda sc, off: (sc == thr) & ((off + rowh) < cand))
                return jnp.where(cnt < topk, cand, m)
            return lax.fori_loop(0, nbits, idx_step, jnp.zeros((1, QB), I32))

        return lax.cond(jnp.max(jnp.where(tied, 1, 0)) > 0, search_cut, lambda: take_all_ties)

    cut = lax.cond(settled == 1, lambda: take_all_ties, tie_cut)
    cut = jnp.where(has_thr, cut, -1)

    def selection_mask(off, nrows):
        sc = sc_ref[pl.ds(off, nrows), :]
        ri = rowi if nrows == SK else rowh
        sel = (sc > thr) | ((sc == thr) & ((off + ri) <= cut))
        return jnp.where(sel, 0.0, MASK_NEG).astype(F32)

    near_lo = q0 - BQ

    def attend(off, bias_off, first=False, nrows=SK):
        rows = pl.ds(off, nrows)
        s = _dot(ckv_ref[rows, :], qat_ref[...])
        mk = selection_mask(off, nrows)
        m_old = jnp.full((1, GW), MASK_NEG, F32) if first else m_ref[...]
        tmax = []
        for h in range(DS_HEADS):
            cs = slice(h * QB, (h + 1) * QB)
            t = s[:, cs] + mk
            if bias_off is not None:
                t = t + jnp.concatenate(
                    [tab_ref[pl.ds(pl.multiple_of(jnp.maximum(bias_off - j * BQ, 0), BQ), nrows),
                             h * BQ:(h + 1) * BQ] for j in range(QB // BQ)], axis=1)
            lgt_ref[0:nrows, cs] = t
            tmax.append(jnp.max(_fold_rows(t, jnp.maximum), axis=0, keepdims=True))
        m_new = jnp.maximum(m_old, jnp.concatenate(tmax, axis=1))
        m_ref[...] = m_new
        pr = jnp.exp2(lgt_ref[0:nrows, :] - m_new).astype(BF16)
        pv = _dot(ckvt_ref[:, rows], pr)
        acc_ref[...] = pv if first else acc_ref[...] * jnp.exp2(m_old - m_new) + pv

    def far_tile(kt, c):
        attend(pl.multiple_of(kt * SK, SK), None)
        return c

    def edge_tile(kt, c):
        off = pl.multiple_of(kt * SK, SK)
        attend(off, SK + off - near_lo)
        return c

    @pl.when(nfull >= 1)
    def _():
        attend(0, SK - near_lo, first=True)
        lax.fori_loop(1, nfull - 1, far_tile, 0)
        lax.fori_loop(jnp.maximum(nfull - 1, 1), nfull, edge_tile, 0)

        @pl.when(nfull < ntile)
        def _():
            off = pl.multiple_of(nfull * SK, SK)
            attend(off, SK + off - near_lo, nrows=QB)

    @pl.when(nfull == 0)
    def _():
        attend(0, SK - near_lo, first=True, nrows=QB)

    o_lat = acc_ref[0:R, :] * (1.0 / acc_ref[R:R + 1, :])
    for h in range(DS_HEADS):
        out_ref[h * DS_HEAD:(h + 1) * DS_HEAD, :] = _dot(
            wuvt_ref[h], o_lat[:, h * QB:(h + 1) * QB].astype(BF16))

    g = gds_ref[...]
    o_ref[...] = (out_ref[...].T * (g * _sigmoid(g))).astype(BF16)


def _dsa(z, ckv, kid, B, T, q_norm_g, w_uq, w_uk, w_uv, iw_q, tab):
    nq = T // QB
    topk = min(TOPK_MAX, T // 4)
    R = DS_KV_RANK
    wq = jnp.concatenate([w_uq, iw_q], axis=1).astype(BF16)
    wuk_h = jnp.transpose(w_uk, (1, 0, 2)).astype(BF16)
    wuv_t = jnp.transpose(w_uv, (1, 2, 0)).astype(BF16)
    ckv_t = jnp.concatenate([jnp.swapaxes(ckv.reshape(B, T, R), 1, 2),
                             jnp.ones((B, ONES_ROWS, T), BF16)], axis=1).reshape(B * (R + ONES_ROWS), T)
    const2 = lambda b, i: (0, 0)
    const3 = lambda b, i: (0, 0, 0)
    resident = dict(pipeline_mode=pl.Buffered(1))
    return pl.pallas_call(
        functools.partial(_dsa_kernel, topk=topk),
        grid=(B, nq),
        in_specs=[pl.BlockSpec((QB, DS_Q_RANK), lambda b, i: (b * nq + i, COL_Q // DS_Q_RANK)),
                  pl.BlockSpec((QB, LANES), lambda b, i: (b * nq + i, COL_KX // LANES)),
                  pl.BlockSpec((QB, DS_WIDTH), lambda b, i: (b * nq + i, COL_GDS // DS_WIDTH)),
                  pl.BlockSpec((T, IDX_DIM), lambda b, i: (b, 0), **resident),
                  pl.BlockSpec((T, R), lambda b, i: (b, 0), **resident),
                  pl.BlockSpec((R + ONES_ROWS, T), lambda b, i: (b, 0), **resident),
                  pl.BlockSpec((1, DS_Q_RANK), const2),
                  pl.BlockSpec((DS_Q_RANK, 2 * DS_WIDTH), const2, **resident),
                  pl.BlockSpec((DS_HEADS, R, DS_HEAD), const3, **resident),
                  pl.BlockSpec((DS_HEADS, DS_HEAD, R), const3, **resident),
                  pl.BlockSpec((BIAS_ROWS, DS_HEADS * BQ), const2, **resident)],
        out_specs=pl.BlockSpec((QB, DS_WIDTH), lambda b, i: (b * nq + i, 0)),
        out_shape=jax.ShapeDtypeStruct((B * T, DS_WIDTH), BF16),
        scratch_shapes=[pltpu.VMEM((T, QB), F32),
                        pltpu.VMEM((T, QB), I32),
                        pltpu.VMEM((T // PLANE_ROWS * SUBLANES, QB), I32),
                        pltpu.VMEM((SK, DS_HEADS * QB), F32),
                        pltpu.VMEM((R + ONES_ROWS, DS_HEADS * QB), F32),
                        pltpu.VMEM((1, DS_HEADS * QB), F32),
                        pltpu.VMEM((R, DS_HEADS * QB), BF16),
                        pltpu.VMEM((IDX_HEADS // 2, IDX_DIM, 2 * QB), BF16),
                        pltpu.VMEM((IDX_HEADS, QB), F32),
                        pltpu.VMEM((DS_WIDTH, QB), F32)],
        compiler_params=pltpu.CompilerParams(dimension_semantics=("parallel", "arbitrary"),
                                             vmem_limit_bytes=VMEM_LIMIT),
        name="dsa",
    )(z, z, z, kid, ckv, ckv_t, q_norm_g.reshape(1, -1).astype(F32), wq, wuk_h, wuv_t, tab)


def _post_kernel(x_ref, a1_ref, a2_ref, p_ref, w1_ref, w2_ref, pw_ref, gw_ref, fg_ref, o_ref):
    h = x_ref[...] + _dot(a1_ref[...], w1_ref[...]) + _dot(a2_ref[...], w2_ref[...])
    e = _dot(p_ref[...].astype(BF16), pw_ref[...])
    gate = _sigmoid(_dot(h.astype(BF16), gw_ref[...]))
    h2 = h + e * gate
    ms = jnp.mean(h2 * h2, axis=-1, keepdims=True)
    o_ref[...] = h2 * lax.rsqrt(ms + NORM_EPS) * fg_ref[...]


def _post(x2, o_rw, o_ds, p2, w_out, ple_w, gate_w, final_g, tm=512):
    m, d = x2.shape
    kh = o_rw.shape[1]
    pd = p2.shape[1]
    resident = dict(pipeline_mode=pl.Buffered(1))
    return pl.pallas_call(
        _post_kernel,
        grid=(m // tm,),
        in_specs=[pl.BlockSpec((tm, d), lambda i: (i, 0)),
                  pl.BlockSpec((tm, kh), lambda i: (i, 0)),
                  pl.BlockSpec((tm, kh), lambda i: (i, 0)),
                  pl.BlockSpec((tm, pd), lambda i: (i, 0)),
                  pl.BlockSpec((kh, d), lambda i: (0, 0), **resident),
                  pl.BlockSpec((kh, d), lambda i: (1, 0), **resident),
                  pl.BlockSpec((pd, d), lambda i: (0, 0), **resident),
                  pl.BlockSpec((d, d), lambda i: (0, 0), **resident),
                  pl.BlockSpec((1, d), lambda i: (0, 0))],
        out_specs=pl.BlockSpec((tm, d), lambda i: (i, 0)),
        out_shape=jax.ShapeDtypeStruct((m, d), F32),
        compiler_params=pltpu.CompilerParams(dimension_semantics=("parallel",),
                                             vmem_limit_bytes=VMEM_LIMIT),
        name="post",
    )(x2, o_rw, o_ds, p2, w_out, w_out, ple_w, gate_w, final_g)


def _split_w_in_kernel(w_ref, o_ref, og_ref):
    n = w_ref.shape[1]
    o_ref[...] = w_ref[...].astype(BF16)
    og_ref[...] = w_ref[:, n - DS_WIDTH:].astype(BF16)


def _split_w_in(w_in, layer, tr=256):
    depth, d, n = w_in.shape
    steps = d // tr
    return pl.pallas_call(
        _split_w_in_kernel,
        grid=(steps,),
        in_specs=[pl.BlockSpec((tr, n), lambda r: (layer * steps + r, 0))],
        out_specs=[pl.BlockSpec((tr, n), lambda r: (r, 0)),
                   pl.BlockSpec((tr, DS_WIDTH), lambda r: (r, 0))],
        out_shape=[jax.ShapeDtypeStruct((d, n), BF16),
                   jax.ShapeDtypeStruct((d, DS_WIDTH), BF16)],
        compiler_params=pltpu.CompilerParams(dimension_semantics=("parallel",),
                                             vmem_limit_bytes=VMEM_LIMIT),
        name="castw",
    )(w_in.reshape(depth * d, n))


def kernel(x, p, w_in, norm_g, rw_mu, rw_w0, rw_w_up, rw_a0, rw_a_up, rw_k_k, rw_k_a, rw_r_k, rw_ln_g, rw_ln_b, ds_q_norm_g, ds_kv_norm_g, idx_k_norm_g, ds_w_uq, ds_w_uk, ds_w_uv, idx_w_q, rel_bias, w_out, ple_w, ple_gate_w, final_g):
    B, T, D = x.shape
    depth = w_in.shape[0]
    assert depth == 1 and T % SK == 0 and SK == 2 * QB and T % RW_CHUNK == 0 and (B * T) % 512 == 0 and D % 256 == 0
    h = x.reshape(B * T, D)
    tab = _biastab(rel_bias)
    for i in range(depth):
        z, ckv, kid = _inproj(h, norm_g[i].reshape(1, D), *_split_w_in(w_in, i),
                              ds_kv_norm_g[i].reshape(1, -1), idx_k_norm_g[i].reshape(1, -1))
        o_rw = _rwkv(z, B, T, rw_mu[i], rw_w0[i], rw_a0[i], rw_k_k[i], rw_k_a[i],
                     rw_r_k[i].reshape(-1), rw_ln_g[i], rw_ln_b[i], rw_w_up[i], rw_a_up[i])
        o_ds = _dsa(z, ckv, kid, B, T, ds_q_norm_g[i], ds_w_uq[i], ds_w_uk[i], ds_w_uv[i],
                    idx_w_q[i], tab)
        h = _post(h, o_rw, o_ds, p[i].reshape(B * T, -1), w_out[i].astype(BF16),
                  ple_w[i].astype(BF16), ple_gate_w[i].astype(BF16), final_g.reshape(1, D))
    return h.reshape(B, T, D)
```
